```python
import jax, jax.numpy as jnp
from jax import lax
import numpy as np

D_MODEL = 2048
BATCH = 8
SEQ = 8192
DEPTH = 1

GRID_W = 64
CTX_LEN = 256
NH_A = 8
DK_A = 128
DV_A = 256
QK_A = NH_A * DK_A
V_A = NH_A * DV_A
CONV_W = 3
CHUNK = 64
M_INIT = -1e30
NH_B = 16
NKV_B = 4
HD_B = 128
Q_B = NH_B * HD_B
KV_B = NKV_B * HD_B
ROT_HALF = HD_B // 2
ROPE_THETA = 10000.0
Q_BLOCK = 128
EPS = 1e-6
ALPHA = (2 * DEPTH) ** 0.25
BETA = (8 * DEPTH) ** -0.25
KV_WIDTHS = (2 * QK_A, V_A, 4 * NH_A, KV_B, KV_B)
N_KV = 2 * QK_A + V_A + 4 * NH_A + 2 * KV_B
OUT_WIDTHS = (V_A, V_A, Q_B, Q_B, 2 * D_MODEL)
N_IN = N_KV + 2 * V_A + 2 * Q_B + 2 * D_MODEL

kernel_name = "hybrid_mlstm_gqa_dit_block"


def _split(p, widths, start=0):
    outs = []
    off = start
    for w in widths:
        outs.append(p[..., off:off + w])
        off += w
    return outs


def layer_norm(x, w=None, b=None):
    xf = x.astype(jnp.float32)
    mu = xf.mean(-1, keepdims=True)
    var = jnp.mean(jnp.square(xf - mu), -1, keepdims=True)
    y = (xf - mu) * lax.rsqrt(var + EPS)
    if w is not None:
        y = y * w.astype(jnp.float32) + b.astype(jnp.float32)
    return y.astype(x.dtype)


def rms_norm(x, w):
    xf = x.astype(jnp.float32)
    y = xf * lax.rsqrt(jnp.mean(jnp.square(xf), -1, keepdims=True) + EPS)
    return (y * w.astype(jnp.float32)).astype(x.dtype)


def dwconv_centred(x, w, b):
    T = x.shape[1]
    pad = CONV_W // 2
    xp = jnp.pad(x, ((0, 0), (pad, CONV_W - 1 - pad), (0, 0)))
    y = b
    for j in range(CONV_W):
        y = y + xp[:, j:j + T] * w[j]
    return y


def rope_tables(n):
    rows_n = n // GRID_W
    row = jnp.repeat(jnp.arange(rows_n), GRID_W).astype(jnp.float32)
    col = jnp.tile(jnp.arange(GRID_W), rows_n).astype(jnp.float32)
    inv = ROPE_THETA ** (-jnp.arange(0, ROT_HALF, 2, dtype=jnp.float32) / ROT_HALF)
    ang_r = row[:, None] * inv[None]
    ang_c = col[:, None] * inv[None]
    return (jnp.cos(ang_r), jnp.sin(ang_r), jnp.cos(ang_c), jnp.sin(ang_c))


def _rot(xh, cos, sin):
    h = xh.shape[-1] // 2
    x1, x2 = xh[..., :h], xh[..., h:]
    cos = cos[None, :, None, :]
    sin = sin[None, :, None, :]
    return jnp.concatenate([x1 * cos - x2 * sin, x1 * sin + x2 * cos], axis=-1)


def apply_rope_2d(x, rope):
    cr, sr, cc, sc = rope
    xf = x.astype(jnp.float32)
    y = jnp.concatenate([_rot(xf[..., :ROT_HALF], cr, sr),
                         _rot(xf[..., ROT_HALF:], cc, sc)], axis=-1)
    return y.astype(x.dtype)


def zero_state(b):
    return (jnp.zeros((b, NH_A, DV_A, DK_A), jnp.float32),
            jnp.zeros((b, NH_A, DK_A), jnp.float32),
            jnp.full((b, NH_A), M_INIT, jnp.float32))


def mlstm_chunked(q, k, v, log_i, log_f, state):
    B, T, H, _ = q.shape
    nc = T // CHUNK

    def to_chunks(a):
        a = a.reshape((B, nc, CHUNK, H) + a.shape[3:])
        return jnp.moveaxis(a, (1, 3), (0, 2))

    tril = jnp.tril(jnp.ones((CHUNK, CHUNK), bool))

    def step(carry, xs):
        C0, n0, m0 = carry
        qc, kc, vc, ic, fc = xs
        b = jnp.cumsum(fc, axis=-1)
        d = jnp.where(tril, b[..., :, None] - b[..., None, :] + ic[..., None, :], -jnp.inf)
        m_inter = b + m0[..., None]
        m = jnp.maximum(m_inter, d.max(-1))
        w = jnp.exp(d - m[..., None])
        a = jnp.exp(m_inter - m)
        s = jnp.einsum('bhjd,bhsd->bhjs', qc, kc) * w
        num = (a[..., None] * jnp.einsum('bhvd,bhjd->bhjv', C0, qc)
               + jnp.einsum('bhjs,bhsv->bhjv', s, vc))
        den = a * jnp.einsum('bhd,bhjd->bhj', n0, qc) + s.sum(-1)
        h = num / jnp.maximum(jnp.abs(den), jnp.exp(-m))[..., None]
        m_end = m[..., -1]
        w_end = jnp.exp(b[..., -1:] - b + ic - m_end[..., None])
        a_end = a[..., -1]
        C = a_end[..., None, None] * C0 + jnp.einsum('bhs,bhsv,bhsd->bhvd', w_end, vc, kc)
        n = a_end[..., None] * n0 + jnp.einsum('bhs,bhsd->bhd', w_end, kc)
        return (C, n, m_end), h

    xs = tuple(to_chunks(a) for a in (q, k, v, log_i, log_f))
    state, h = lax.scan(step, state, xs)
    h = jnp.moveaxis(h, (0, 2), (1, 3)).reshape(B, T, H, v.shape[-1])
    return h, state


def mlstm_final_state(k, v, log_i, log_f):
    b = jnp.cumsum(log_f, axis=1)
    g = b[:, -1:] - b + log_i
    m = g.max(axis=1)
    w = jnp.exp(g - m[:, None])
    C = jnp.einsum('bth,bthv,bthd->bhvd', w, v, k)
    n = jnp.einsum('bth,bthd->bhd', w, k)
    return (C, n, m)


def _flip(*arrs):
    return [jnp.flip(a, axis=1) for a in arrs]


def mlstm_inputs(qk_pre, v_a, if_a, conv_w, conv_b, b_if):
    B, T = v_a.shape[:2]
    qk = jax.nn.silu(dwconv_centred(qk_pre, conv_w, conv_b)).astype(jnp.float32)
    q = qk[..., :QK_A].reshape(B, T, NH_A, DK_A)
    k = qk[..., QK_A:].reshape(B, T, NH_A, DK_A) * (DK_A ** -0.5)
    v = v_a.astype(jnp.float32).reshape(B, T, NH_A, DV_A)
    gt = (if_a + b_if).astype(jnp.float32).reshape(B, T, 4, NH_A)
    fwd = (gt[:, :, 0], jax.nn.log_sigmoid(gt[:, :, 1]))
    bwd = (gt[:, :, 2], jax.nn.log_sigmoid(gt[:, :, 3]))
    return q, k, v, fwd, bwd


def attn_kv(k_b, v_b, k_norm_w, rope):
    B, T = k_b.shape[:2]
    k = rms_norm(k_b.reshape(B, T, NKV_B, HD_B), k_norm_w)
    if rope is not None:
        k = apply_rope_2d(k, rope)
    return k, v_b.reshape(B, T, NKV_B, HD_B)


def attend_blocks(q, k_all, v_all):
    B, S = q.shape[:2]
    G = NH_B // NKV_B
    nb = S // Q_BLOCK
    qb = q.reshape(B, nb, Q_BLOCK, NKV_B, G, HD_B).transpose(1, 0, 2, 3, 4, 5)
    scale = HD_B ** -0.5

    def one(qblk):
        s = jnp.einsum('bqkgd,btkd->bkgqt', qblk, k_all).astype(jnp.float32) * scale
        p = jax.nn.softmax(s, axis=-1).astype(v_all.dtype)
        return jnp.einsum('bkgqt,btkd->bqkgd', p, v_all)

    o = lax.map(one, qb)
    return o.transpose(1, 0, 2, 3, 4, 5).reshape(B, S, Q_B)


def branch_merge(h_a, o_attn, o_a, z_a, z_b, g_logits, mh_norm_w, w_ba, w_bb, w_out):
    B, T = h_a.shape[:2]
    h = rms_norm(h_a, mh_norm_w.reshape(NH_A, DV_A)).reshape(B, T, V_A)
    y_a = (jax.nn.sigmoid(o_a) * h * jax.nn.silu(z_a)) @ w_ba
    y_b = (o_attn * jax.nn.silu(z_b)) @ w_bb
    g_a, g_b = jnp.split(g_logits, 2, axis=-1)
    return (jax.nn.sigmoid(g_a) * y_a + jax.nn.sigmoid(g_b) * y_b) @ w_out


def trunk_layer(x, ctx, c, c_ctx, rope, w_mod, b_mod, w_in, b_if, conv_w, conv_b,
                mh_norm_w, q_norm_w, k_norm_w, w_ba, w_bb, w_out, ln_w, ln_b, update_ctx):
    B, S = x.shape[:2]
    T_c = ctx.shape[1]
    shift, scale, gate = [m[:, None, :] for m in jnp.split(jax.nn.silu(c) @ w_mod + b_mod, 3, axis=-1)]
    shift_c, scale_c, gate_c = jnp.split(jax.nn.silu(c_ctx) @ w_mod + b_mod, 3, axis=-1)
    u = layer_norm(x) * (1 + scale) + shift
    u_c = layer_norm(ctx) * (1 + scale_c) + shift_c
    p = u @ w_in
    p_c = u_c @ (w_in if update_ctx else w_in[:, :N_KV])

    qk_c, va_c, if_c, kb_c, vb_c = _split(p_c, KV_WIDTHS)
    qc, kc, vc, gf_c, gb_c = mlstm_inputs(qk_c, va_c, if_c, conv_w, conv_b, b_if)
    k_bc, v_bc = attn_kv(kb_c, vb_c, k_norm_w, None)
    if update_ctx:
        h_cf, st_f = mlstm_chunked(qc, kc, vc, *gf_c, zero_state(B))
        h_cb, st_b = mlstm_chunked(*_flip(qc, kc, vc, *gb_c), zero_state(B))
        h_c = (h_cf + jnp.flip(h_cb, axis=1)).astype(ctx.dtype)
        o_ac, z_ac, q_bc, z_bc, g_c = _split(p_c, OUT_WIDTHS, N_KV)
        q_bc = rms_norm(q_bc.reshape(B, T_c, NH_B, HD_B), q_norm_w)
        o_attn_c = attend_blocks(q_bc, k_bc, v_bc)
        out_c = branch_merge(h_c, o_attn_c, o_ac, z_ac, z_bc, g_c, mh_norm_w, w_ba, w_bb, w_out)
        ctx_new = layer_norm(ALPHA * ctx + gate_c * out_c, ln_w, ln_b)
    else:
        st_f = mlstm_final_state(kc, vc, *gf_c)
        st_b = mlstm_final_state(*_flip(kc, vc, *gb_c))
        ctx_new = ctx

    qk_l, va_l, if_l, kb_l, vb_l = _split(p, KV_WIDTHS)
    o_a, z_a, q_b, z_b, g_l = _split(p, OUT_WIDTHS, N_KV)
    ql, kl, vl, gf_l, gb_l = mlstm_inputs(qk_l, va_l, if_l, conv_w, conv_b, b_if)
    h_f, _ = mlstm_chunked(ql, kl, vl, *gf_l, st_f)
    h_b, _ = mlstm_chunked(*_flip(ql, kl, vl, *gb_l), st_b)
    h_l = (h_f + jnp.flip(h_b, axis=1)).astype(x.dtype)

    q_l = apply_rope_2d(rms_norm(q_b.reshape(B, S, NH_B, HD_B), q_norm_w), rope)
    k_bl, v_bl = attn_kv(kb_l, vb_l, k_norm_w, rope)
    k_all = jnp.concatenate([k_bc, k_bl], axis=1)
    v_all = jnp.concatenate([v_bc, v_bl], axis=1)
    o_attn = attend_blocks(q_l, k_all, v_all)

    out = branch_merge(h_l, o_attn, o_a, z_a, z_b, g_l, mh_norm_w, w_ba, w_bb, w_out)
    x_new = layer_norm(ALPHA * x + gate * out, ln_w, ln_b)
    return x_new, ctx_new


def _fwd_setup_inputs(seed: int = 0) -> dict:
    key = jax.random.key(seed)
    ks = jax.random.split(key, 20)
    D = D_MODEL
    nrm = jax.random.normal
    b_if_i = 0.1 * nrm(ks[8], (DEPTH, 2, 1, NH_A))
    b_if_f = 3.0 + 0.5 * nrm(ks[9], (DEPTH, 2, 1, NH_A))
    b_if = jnp.concatenate([b_if_i, b_if_f], axis=2).reshape(DEPTH, 4 * NH_A)
    return {
        "x": nrm(ks[0], (BATCH, SEQ, D), jnp.float32),
        "c": nrm(ks[1], (BATCH, D), jnp.float32),
        "ctx": nrm(ks[2], (BATCH, CTX_LEN, D), jnp.float32),
        "c_ctx": nrm(ks[3], (D,), jnp.float32),
        "w_mod": 0.5 * D ** -0.5 * nrm(ks[4], (DEPTH, D, 3 * D), jnp.float32),
        "b_mod": 0.01 * nrm(ks[5], (DEPTH, 3 * D), jnp.float32),
        "w_in": D ** -0.5 * nrm(ks[6], (DEPTH, D, N_IN), jnp.float32),
        "b_if": b_if.astype(jnp.float32),
        "conv_w": CONV_W ** -0.5 * nrm(ks[7], (DEPTH, CONV_W, 2 * QK_A), jnp.float32),
        "conv_b": 0.01 * nrm(ks[10], (DEPTH, 2 * QK_A), jnp.float32),
        "mh_norm_w": 1.0 + 0.02 * nrm(ks[11], (DEPTH, V_A), jnp.float32),
        "q_norm_w": 1.0 + 0.02 * nrm(ks[12], (DEPTH, HD_B), jnp.float32),
        "k_norm_w": 1.0 + 0.02 * nrm(ks[13], (DEPTH, HD_B), jnp.float32),
        "w_branch_a": BETA * V_A ** -0.5 * nrm(ks[14], (DEPTH, V_A, D), jnp.float32),
        "w_branch_b": BETA * Q_B ** -0.5 * nrm(ks[15], (DEPTH, Q_B, D), jnp.float32),
        "w_out": BETA * D ** -0.5 * nrm(ks[16], (DEPTH, D, D), jnp.float32),
        "ln_w": 1.0 + 0.02 * nrm(ks[17], (DEPTH, D), jnp.float32),
        "ln_b": 0.01 * nrm(ks[18], (DEPTH, D), jnp.float32),
    }


def _fwd_reference(x, c, ctx, c_ctx, w_mod, b_mod, w_in, b_if, conv_w, conv_b, mh_norm_w,
              q_norm_w, k_norm_w, w_branch_a, w_branch_b, w_out, ln_w, ln_b):
    rope = rope_tables(x.shape[1])
    for layer in range(DEPTH):
        x, ctx = trunk_layer(x, ctx, c, c_ctx, rope, w_mod[layer], b_mod[layer], w_in[layer],
                             b_if[layer], conv_w[layer], conv_b[layer], mh_norm_w[layer],
                             q_norm_w[layer], k_norm_w[layer], w_branch_a[layer],
                             w_branch_b[layer], w_out[layer], ln_w[layer], ln_b[layer],
                             layer < DEPTH - 1)
    return x


import jax as _jax
import jax.numpy as _jnp

TWIN_FORMAT = 'train_step'
FWD_PARAMS = ['x', 'c', 'ctx', 'c_ctx', 'w_mod', 'b_mod', 'w_in', 'b_if', 'conv_w', 'conv_b', 'mh_norm_w', 'q_norm_w', 'k_norm_w', 'w_branch_a', 'w_branch_b', 'w_out', 'ln_w', 'ln_b']
TWIN_WEIGHTS = ['c_ctx', 'w_mod', 'b_mod', 'w_in', 'b_if', 'conv_w', 'conv_b', 'mh_norm_w', 'q_norm_w', 'k_norm_w', 'w_branch_a', 'w_branch_b', 'w_out', 'ln_w', 'ln_b']
TWIN_DIFF_INPUT = 'x'
TWIN_INPUTS = ['x', 'c', 'ctx', 'c_ctx', 'w_mod', 'b_mod', 'w_in', 'b_if', 'conv_w', 'conv_b', 'mh_norm_w', 'q_norm_w', 'k_norm_w', 'w_branch_a', 'w_branch_b', 'w_out', 'ln_w', 'ln_b', 'loss_target', 'm_c_ctx', 'm_w_mod', 'm_b_mod', 'm_w_in', 'm_b_if', 'm_conv_w', 'm_conv_b', 'm_mh_norm_w', 'm_q_norm_w', 'm_k_norm_w', 'm_w_branch_a', 'm_w_branch_b', 'm_w_out', 'm_ln_w', 'm_ln_b', 'v_c_ctx', 'v_w_mod', 'v_b_mod', 'v_w_in', 'v_b_if', 'v_conv_w', 'v_conv_b', 'v_mh_norm_w', 'v_q_norm_w', 'v_k_norm_w', 'v_w_branch_a', 'v_w_branch_b', 'v_w_out', 'v_ln_w', 'v_ln_b']
TWIN_OUTPUTS = ['loss', 'grad_x', 'grad_c_ctx', 'grad_w_mod', 'grad_b_mod', 'grad_w_in', 'grad_b_if', 'grad_conv_w', 'grad_conv_b', 'grad_mh_norm_w', 'grad_q_norm_w', 'grad_k_norm_w', 'grad_w_branch_a', 'grad_w_branch_b', 'grad_w_out', 'grad_ln_w', 'grad_ln_b', 'delta_c_ctx', 'delta_w_mod', 'delta_b_mod', 'delta_w_in', 'delta_b_if', 'delta_conv_w', 'delta_conv_b', 'delta_mh_norm_w', 'delta_q_norm_w', 'delta_k_norm_w', 'delta_w_branch_a', 'delta_w_branch_b', 'delta_w_out', 'delta_ln_w', 'delta_ln_b', 'new_m_c_ctx', 'new_m_w_mod', 'new_m_b_mod', 'new_m_w_in', 'new_m_b_if', 'new_m_conv_w', 'new_m_conv_b', 'new_m_mh_norm_w', 'new_m_q_norm_w', 'new_m_k_norm_w', 'new_m_w_branch_a', 'new_m_w_branch_b', 'new_m_w_out', 'new_m_ln_w', 'new_m_ln_b', 'new_v_c_ctx', 'new_v_w_mod', 'new_v_b_mod', 'new_v_w_in', 'new_v_b_if', 'new_v_conv_w', 'new_v_conv_b', 'new_v_mh_norm_w', 'new_v_q_norm_w', 'new_v_k_norm_w', 'new_v_w_branch_a', 'new_v_w_branch_b', 'new_v_w_out', 'new_v_ln_w', 'new_v_ln_b']
TWIN_LEAF_KINDS = {'loss': 'loss', 'grad_x': 'grad_x', 'grad_c_ctx': 'grad_w', 'grad_w_mod': 'grad_w', 'grad_b_mod': 'grad_w', 'grad_w_in': 'grad_w', 'grad_b_if': 'grad_w', 'grad_conv_w': 'grad_w', 'grad_conv_b': 'grad_w', 'grad_mh_norm_w': 'grad_w', 'grad_q_norm_w': 'grad_w', 'grad_k_norm_w': 'grad_w', 'grad_w_branch_a': 'grad_w', 'grad_w_branch_b': 'grad_w', 'grad_w_out': 'grad_w', 'grad_ln_w': 'grad_w', 'grad_ln_b': 'grad_w', 'delta_c_ctx': 'delta_w', 'delta_w_mod': 'delta_w', 'delta_b_mod': 'delta_w', 'delta_w_in': 'delta_w', 'delta_b_if': 'delta_w', 'delta_conv_w': 'delta_w', 'delta_conv_b': 'delta_w', 'delta_mh_norm_w': 'delta_w', 'delta_q_norm_w': 'delta_w', 'delta_k_norm_w': 'delta_w', 'delta_w_branch_a': 'delta_w', 'delta_w_branch_b': 'delta_w', 'delta_w_out': 'delta_w', 'delta_ln_w': 'delta_w', 'delta_ln_b': 'delta_w', 'new_m_c_ctx': 'new_m', 'new_m_w_mod': 'new_m', 'new_m_b_mod': 'new_m', 'new_m_w_in': 'new_m', 'new_m_b_if': 'new_m', 'new_m_conv_w': 'new_m', 'new_m_conv_b': 'new_m', 'new_m_mh_norm_w': 'new_m', 'new_m_q_norm_w': 'new_m', 'new_m_k_norm_w': 'new_m', 'new_m_w_branch_a': 'new_m', 'new_m_w_branch_b': 'new_m', 'new_m_w_out': 'new_m', 'new_m_ln_w': 'new_m', 'new_m_ln_b': 'new_m', 'new_v_c_ctx': 'new_v', 'new_v_w_mod': 'new_v', 'new_v_b_mod': 'new_v', 'new_v_w_in': 'new_v', 'new_v_b_if': 'new_v', 'new_v_conv_w': 'new_v', 'new_v_conv_b': 'new_v', 'new_v_mh_norm_w': 'new_v', 'new_v_q_norm_w': 'new_v', 'new_v_k_norm_w': 'new_v', 'new_v_w_branch_a': 'new_v', 'new_v_w_branch_b': 'new_v', 'new_v_w_out': 'new_v', 'new_v_ln_w': 'new_v', 'new_v_ln_b': 'new_v'}


def _forward(args):
    return _fwd_reference(*[args[k] for k in FWD_PARAMS])


def _output_shape():
    def fwd():
        inp = _fwd_setup_inputs(0)
        return _fwd_reference(*[inp[k] for k in FWD_PARAMS])
    out = _jax.eval_shape(fwd)
    return out.shape, out.dtype

N_MICROBATCH = 1
ADAM_LR = 0.001
ADAM_B1 = 0.9
ADAM_B2 = 0.999
ADAM_EPS = 1e-08
ADAM_WD = 0.01
ADAM_STEP = 10
PER_EXAMPLE_BATCH_AXIS = {'x': 0, 'c': 0, 'ctx': 0, 'loss_target': 0}
SHARED_INPUTS = []
_WEIGHT_DTYPES = {'c_ctx': _jnp.float32, 'w_mod': _jnp.float32, 'b_mod': _jnp.float32, 'w_in': _jnp.float32, 'b_if': _jnp.float32, 'conv_w': _jnp.float32, 'conv_b': _jnp.float32, 'mh_norm_w': _jnp.float32, 'q_norm_w': _jnp.float32, 'k_norm_w': _jnp.float32, 'w_branch_a': _jnp.float32, 'w_branch_b': _jnp.float32, 'w_out': _jnp.float32, 'ln_w': _jnp.float32, 'ln_b': _jnp.float32}
MOMENT_SCALE = {'c_ctx': 6.895274e-04, 'w_mod': 4.183066e-03, 'b_mod': 7.195213e-03, 'w_in': 1.485221e-03, 'b_if': 8.873419e-03, 'conv_w': 1.702238e-03, 'conv_b': 1.593202e-03, 'mh_norm_w': 2.241169e-03, 'q_norm_w': 1.374475e-03, 'k_norm_w': 1.460393e-03, 'w_branch_a': 3.743327e-03, 'w_branch_b': 2.085408e-03, 'w_out': 4.267478e-03, 'ln_w': 3.196856e+01, 'ln_b': 2.592968e-01}


def _to_microbatches(a, axis):
    t = _jnp.moveaxis(a, axis, 0)
    t = t.reshape((N_MICROBATCH, t.shape[0] // N_MICROBATCH) + t.shape[1:])
    return _jnp.moveaxis(t, 1, axis + 1)


def setup_inputs(seed: int = 0) -> dict:
    inp = _fwd_setup_inputs(seed)
    key = _jax.random.fold_in(_jax.random.key(seed), 7919)
    shape, _ = _output_shape()
    out = dict(inp)
    out["loss_target"] = _jax.random.normal(_jax.random.fold_in(key, 0), shape, _jnp.float32)
    for i, name in enumerate(TWIN_WEIGHTS):
        w = inp[name].astype(_jnp.float32)
        if MOMENT_SCALE is None:
            s = _jnp.sqrt(_jnp.mean(_jnp.square(w)) + 1e-30)
        else:
            s = MOMENT_SCALE[name]
        km, kv = _jax.random.split(_jax.random.fold_in(key, i + 1))
        out[name] = w
        out["m_" + name] = s * _jax.random.normal(km, w.shape, _jnp.float32)
        out["v_" + name] = (s * s) * _jax.random.uniform(kv, w.shape, _jnp.float32, 0.5, 1.5)
    if N_MICROBATCH > 1:
        for name, axis in PER_EXAMPLE_BATCH_AXIS.items():
            out[name] = _to_microbatches(out[name], axis)
    return {'x': out['x'], 'c': out['c'], 'ctx': out['ctx'], 'c_ctx': out['c_ctx'], 'w_mod': out['w_mod'], 'b_mod': out['b_mod'], 'w_in': out['w_in'], 'b_if': out['b_if'], 'conv_w': out['conv_w'], 'conv_b': out['conv_b'], 'mh_norm_w': out['mh_norm_w'], 'q_norm_w': out['q_norm_w'], 'k_norm_w': out['k_norm_w'], 'w_branch_a': out['w_branch_a'], 'w_branch_b': out['w_branch_b'], 'w_out': out['w_out'], 'ln_w': out['ln_w'], 'ln_b': out['ln_b'], 'loss_target': out['loss_target'], 'm_c_ctx': out['m_c_ctx'], 'm_w_mod': out['m_w_mod'], 'm_b_mod': out['m_b_mod'], 'm_w_in': out['m_w_in'], 'm_b_if': out['m_b_if'], 'm_conv_w': out['m_conv_w'], 'm_conv_b': out['m_conv_b'], 'm_mh_norm_w': out['m_mh_norm_w'], 'm_q_norm_w': out['m_q_norm_w'], 'm_k_norm_w': out['m_k_norm_w'], 'm_w_branch_a': out['m_w_branch_a'], 'm_w_branch_b': out['m_w_branch_b'], 'm_w_out': out['m_w_out'], 'm_ln_w': out['m_ln_w'], 'm_ln_b': out['m_ln_b'], 'v_c_ctx': out['v_c_ctx'], 'v_w_mod': out['v_w_mod'], 'v_b_mod': out['v_b_mod'], 'v_w_in': out['v_w_in'], 'v_b_if': out['v_b_if'], 'v_conv_w': out['v_conv_w'], 'v_conv_b': out['v_conv_b'], 'v_mh_norm_w': out['v_mh_norm_w'], 'v_q_norm_w': out['v_q_norm_w'], 'v_k_norm_w': out['v_k_norm_w'], 'v_w_branch_a': out['v_w_branch_a'], 'v_w_branch_b': out['v_w_branch_b'], 'v_w_out': out['v_w_out'], 'v_ln_w': out['v_ln_w'], 'v_ln_b': out['v_ln_b']}


def _loss(weights, diff, rest, loss_target):
    with _jax.named_scope("forward"):
        args = {**rest, TWIN_DIFF_INPUT: diff, **{k: w.astype(_WEIGHT_DTYPES[k]) for k, w in weights.items()}}
        y = _forward(args)
    with _jax.named_scope("loss_head"):
        err = _jnp.square(y.astype(_jnp.float32) - loss_target)
        return 0.5 * _jnp.sum(_jnp.mean(err, axis=-1)) if err.ndim else 0.5 * err


def _adamw(w, g, m, v):
    m = ADAM_B1 * m + (1.0 - ADAM_B1) * g
    v = ADAM_B2 * v + (1.0 - ADAM_B2) * _jnp.square(g)
    m_hat = m / (1.0 - ADAM_B1 ** ADAM_STEP)
    v_hat = v / (1.0 - ADAM_B2 ** ADAM_STEP)
    delta = -ADAM_LR * (m_hat / (_jnp.sqrt(v_hat) + ADAM_EPS) + ADAM_WD * w)
    return delta, m, v


def reference(x, c, ctx, c_ctx, w_mod, b_mod, w_in, b_if, conv_w, conv_b, mh_norm_w, q_norm_w, k_norm_w, w_branch_a, w_branch_b, w_out, ln_w, ln_b, loss_target, m_c_ctx, m_w_mod, m_b_mod, m_w_in, m_b_if, m_conv_w, m_conv_b, m_mh_norm_w, m_q_norm_w, m_k_norm_w, m_w_branch_a, m_w_branch_b, m_w_out, m_ln_w, m_ln_b, v_c_ctx, v_w_mod, v_b_mod, v_w_in, v_b_if, v_conv_w, v_conv_b, v_mh_norm_w, v_q_norm_w, v_k_norm_w, v_w_branch_a, v_w_branch_b, v_w_out, v_ln_w, v_ln_b):
    given = dict(x=x, c=c, ctx=ctx, c_ctx=c_ctx, w_mod=w_mod, b_mod=b_mod, w_in=w_in, b_if=b_if, conv_w=conv_w, conv_b=conv_b, mh_norm_w=mh_norm_w, q_norm_w=q_norm_w, k_norm_w=k_norm_w, w_branch_a=w_branch_a, w_branch_b=w_branch_b, w_out=w_out, ln_w=ln_w, ln_b=ln_b, loss_target=loss_target, m_c_ctx=m_c_ctx, m_w_mod=m_w_mod, m_b_mod=m_b_mod, m_w_in=m_w_in, m_b_if=m_b_if, m_conv_w=m_conv_w, m_conv_b=m_conv_b, m_mh_norm_w=m_mh_norm_w, m_q_norm_w=m_q_norm_w, m_k_norm_w=m_k_norm_w, m_w_branch_a=m_w_branch_a, m_w_branch_b=m_w_branch_b, m_w_out=m_w_out, m_ln_w=m_ln_w, m_ln_b=m_ln_b, v_c_ctx=v_c_ctx, v_w_mod=v_w_mod, v_b_mod=v_b_mod, v_w_in=v_w_in, v_b_if=v_b_if, v_conv_w=v_conv_w, v_conv_b=v_conv_b, v_mh_norm_w=v_mh_norm_w, v_q_norm_w=v_q_norm_w, v_k_norm_w=v_k_norm_w, v_w_branch_a=v_w_branch_a, v_w_branch_b=v_w_branch_b, v_w_out=v_w_out, v_ln_w=v_ln_w, v_ln_b=v_ln_b)
    weights = {n: given[n] for n in TWIN_WEIGHTS}
    shared = {n: given[n] for n in SHARED_INPUTS}
    per_example = {n: given[n] for n in ['x', 'c', 'ctx']}
    grad_fn = _jax.value_and_grad(_loss, argnums=(0, 1))

    def one_microbatch(ex, loss_target):
        ex = dict(ex)
        diff = ex.pop(TWIN_DIFF_INPUT)
        return grad_fn(weights, diff, {**shared, **ex}, loss_target)

    if N_MICROBATCH == 1:
        loss, (grad_w, grad_x) = one_microbatch(per_example, given["loss_target"])
    else:
        def body(carry, xs):
            loss_sum, grad_sum = carry
            l_k, (gw_k, gx_k) = one_microbatch(xs[0], xs[1])
            with _jax.named_scope("update"):
                return (loss_sum + l_k, _jax.tree.map(_jnp.add, grad_sum, gw_k)), gx_k

        init = (_jnp.zeros((), _jnp.float32), _jax.tree.map(_jnp.zeros_like, weights))
        (loss, grad_w), grad_x = _jax.lax.scan(body, init, (per_example, given["loss_target"]))
    with _jax.named_scope("update"):
        delta_w, new_m, new_v = {}, {}, {}
        for n in TWIN_WEIGHTS:
            delta_w[n], new_m[n], new_v[n] = _adamw(weights[n], grad_w[n], given["m_" + n], given["v_" + n])
    return (loss, grad_x, *[grad_w[n] for n in TWIN_WEIGHTS], *[delta_w[n] for n in TWIN_WEIGHTS],
            *[new_m[n] for n in TWIN_WEIGHTS], *[new_v[n] for n in TWIN_WEIGHTS])
```

```python
import functools

import jax
import jax.numpy as jnp
from jax import lax
from jax.experimental import pallas as pl
from jax.experimental.pallas import tpu as pltpu

F32 = jnp.float32
BF16 = jnp.bfloat16
MESH = pl.DeviceIdType.MESH

D_MODEL = 2048
NH_A, DK_A, DV_A = 8, 128, 256
QK_A, V_A = NH_A * DK_A, NH_A * DV_A
NH_B, NKV_B, HD_B = 16, 4, 128
Q_B, KV_B = NH_B * HD_B, NKV_B * HD_B
GRID_W = 64
ROT_HALF = HD_B // 2
ROPE_THETA = 10000.0
M_INIT = -1e30
EPS = 1e-6
ALPHA = 2.0 ** 0.25
N_IN = 17440
IF_START, N_IF, IF_PAD = 4096, 32, 128
N_MAIN = N_IN - N_IF
O_QK, O_VA, O_KB, O_VB, O_OA, O_ZA, O_QB, O_ZB, O_GA, O_GB = (
    0, 2048, 4096, 4608, 5120, 7168, 9216, 11264, 13312, 15360)
MLSTM_CHUNK = 256

ADAM_LR, ADAM_B1, ADAM_B2, ADAM_EPS, ADAM_WD, ADAM_STEP = 0.001, 0.9, 0.999, 1e-08, 0.01, 10

VMEM_LIMIT = 48 * 1024 * 1024
N_CHIPS, N_DEV = 4, 8


def _pick(n, cands):
    for c in cands:
        if n % c == 0:
            return c
    raise ValueError(f"no tile for {n} in {cands}")


def _mm_nn(a, b, name):
    m, k = a.shape
    _, n = b.shape
    tm = _pick(m, (512, 256, 128, 64, 32, 16))
    tn = _pick(n, (1024, 512, 256, 128))

    def body(a_ref, b_ref, o_ref):
        o_ref[...] = jnp.dot(a_ref[...], b_ref[...], preferred_element_type=F32)

    return pl.pallas_call(
        body, grid=(m // tm, n // tn),
        in_specs=[pl.BlockSpec((tm, k), lambda i, j: (i, 0)), pl.BlockSpec((k, tn), lambda i, j: (0, j))],
        out_specs=pl.BlockSpec((tm, tn), lambda i, j: (i, j)),
        out_shape=jax.ShapeDtypeStruct((m, n), F32),
        compiler_params=pltpu.CompilerParams(dimension_semantics=("parallel", "parallel"),
                                             vmem_limit_bytes=VMEM_LIMIT),
        name=name)(a, b)


def _mm_nt(g, w, name):
    m, n = g.shape
    k, _ = w.shape
    tm = _pick(m, (512, 256, 128, 64, 32, 16))
    tn = _pick(n, (1024, 512, 256, 128))

    def body(g_ref, w_ref, o_ref):
        part = lax.dot_general(g_ref[...], w_ref[...], (((1,), (1,)), ((), ())), preferred_element_type=F32)

        @pl.when(pl.program_id(1) == 0)
        def _():
            o_ref[...] = part

        @pl.when(pl.program_id(1) > 0)
        def _():
            o_ref[...] += part

    return pl.pallas_call(
        body, grid=(m // tm, n // tn),
        in_specs=[pl.BlockSpec((tm, tn), lambda i, j: (i, j)), pl.BlockSpec((k, tn), lambda i, j: (0, j))],
        out_specs=pl.BlockSpec((tm, k), lambda i, j: (i, 0)),
        out_shape=jax.ShapeDtypeStruct((m, k), F32),
        compiler_params=pltpu.CompilerParams(dimension_semantics=("parallel", "arbitrary"),
                                             vmem_limit_bytes=VMEM_LIMIT),
        name=name)(g, w)


def _mm_tn(a, g, name):
    m, k = a.shape
    _, n = g.shape
    tm = _pick(m, (512, 256, 128, 64, 32, 16))
    tn = _pick(n, (1024, 512, 256, 128))

    def body(a_ref, g_ref, o_ref):
        part = lax.dot_general(a_ref[...], g_ref[...], (((0,), (0,)), ((), ())), preferred_element_type=F32)

        @pl.when(pl.program_id(1) == 0)
        def _():
            o_ref[...] = part

        @pl.when(pl.program_id(1) > 0)
        def _():
            o_ref[...] += part

    return pl.pallas_call(
        body, grid=(n // tn, m // tm),
        in_specs=[pl.BlockSpec((tm, k), lambda j, i: (i, 0)), pl.BlockSpec((tm, tn), lambda j, i: (i, j))],
        out_specs=pl.BlockSpec((k, tn), lambda j, i: (0, j)),
        out_shape=jax.ShapeDtypeStruct((k, n), F32),
        compiler_params=pltpu.CompilerParams(dimension_semantics=("parallel", "arbitrary"),
                                             vmem_limit_bytes=VMEM_LIMIT),
        name=name)(a, g)


def _make_mm(tag):
    @jax.custom_vjp
    def mm(a, w, proxy):
        del proxy
        return _mm_nn(a.astype(BF16), w, f"mm_{tag}_fwd")

    def fwd(a, w, proxy):
        del proxy
        ab = a.astype(BF16)
        return _mm_nn(ab, w, f"mm_{tag}_fwd"), (ab, w)

    def bwd(res, g):
        ab, w = res
        gb = g.astype(BF16)
        da = _mm_nt(gb, w, f"mm_{tag}_da")
        dw = _mm_tn(ab, gb, f"mm_{tag}_dw")
        return da, jnp.zeros_like(w), dw

    mm.defvjp(fwd, bwd)
    return mm


ATT_SCALE = HD_B ** -0.5
GROUP = NH_B // NKV_B


def _dot_nt(a, b):
    return lax.dot_general(a, b, (((1,), (1,)), ((), ())), preferred_element_type=F32)


def _attn_tiles(t, n):
    return _pick(t, (512, 256, 128)), _pick(n, (768, 512, 256, 128))


def _attn_fwd(q, k, v):
    t, n = q.shape[0], k.shape[0]
    tq, tk = _attn_tiles(t, n)
    nk = n // tk

    def body(q_ref, k_ref, v_ref, o_ref, lse_ref, m_sc, l_sc, acc_sc):
        j = pl.program_id(2)

        @pl.when(j == 0)
        def _():
            m_sc[...] = jnp.full(m_sc.shape, -jnp.inf, F32)
            l_sc[...] = jnp.zeros(l_sc.shape, F32)
            acc_sc[...] = jnp.zeros(acc_sc.shape, F32)

        s = _dot_nt(q_ref[...], k_ref[...]) * ATT_SCALE
        m_prev = m_sc[...]
        m_new = jnp.maximum(m_prev, jnp.max(s, axis=-1, keepdims=True))
        alpha = jnp.exp(m_prev - m_new)
        p = jnp.exp(s - m_new)
        l_sc[...] = alpha * l_sc[...] + jnp.sum(p, axis=-1, keepdims=True)
        acc_sc[...] = alpha * acc_sc[...] + jnp.dot(p.astype(BF16), v_ref[...], preferred_element_type=F32)
        m_sc[...] = m_new

        @pl.when(j == nk - 1)
        def _():
            o_ref[...] = acc_sc[...] / l_sc[...]
            lse_ref[...] = jnp.broadcast_to(m_sc[...] + jnp.log(l_sc[...]), lse_ref.shape)

    return pl.pallas_call(
        body, grid=(NH_B, t // tq, nk),
        in_specs=[pl.BlockSpec((tq, HD_B), lambda h, i, j: (i, h)),
                  pl.BlockSpec((tk, HD_B), lambda h, i, j: (j, h // GROUP)),
                  pl.BlockSpec((tk, HD_B), lambda h, i, j: (j, h // GROUP))],
        out_specs=[pl.BlockSpec((tq, HD_B), lambda h, i, j: (i, h)),
                   pl.BlockSpec((tq, HD_B), lambda h, i, j: (i, h))],
        out_shape=[jax.ShapeDtypeStruct((t, Q_B), F32), jax.ShapeDtypeStruct((t, Q_B), F32)],
        scratch_shapes=[pltpu.VMEM((tq, 1), F32), pltpu.VMEM((tq, 1), F32), pltpu.VMEM((tq, HD_B), F32)],
        compiler_params=pltpu.CompilerParams(dimension_semantics=("parallel", "parallel", "arbitrary"),
                                             vmem_limit_bytes=VMEM_LIMIT),
        name="attn_fwd")(q, k, v)


def _attn_dq(q, k, v, do, lse, delta):
    t, n = q.shape[0], k.shape[0]
    tq, tk = _attn_tiles(t, n)
    nk = n // tk

    def body(q_ref, k_ref, v_ref, do_ref, lse_ref, dl_ref, dq_ref):
        j = pl.program_id(2)
        s = _dot_nt(q_ref[...], k_ref[...]) * ATT_SCALE
        p = jnp.exp(s - lse_ref[:, 0:1])
        dp = _dot_nt(do_ref[...], v_ref[...])
        ds = p * (dp - dl_ref[:, 0:1]) * ATT_SCALE
        part = jnp.dot(ds.astype(BF16), k_ref[...], preferred_element_type=F32)

        @pl.when(j == 0)
        def _():
            dq_ref[...] = part

        @pl.when(j > 0)
        def _():
            dq_ref[...] += part

    qspec = pl.BlockSpec((tq, HD_B), lambda h, i, j: (i, h))
    kspec = pl.BlockSpec((tk, HD_B), lambda h, i, j: (j, h // GROUP))
    return pl.pallas_call(
        body, grid=(NH_B, t // tq, nk),
        in_specs=[qspec, kspec, kspec, qspec, qspec, qspec],
        out_specs=qspec,
        out_shape=jax.ShapeDtypeStruct((t, Q_B), F32),
        compiler_params=pltpu.CompilerParams(dimension_semantics=("parallel", "parallel", "arbitrary"),
                                             vmem_limit_bytes=VMEM_LIMIT),
        name="attn_dq")(q, k, v, do, lse, delta)


def _attn_dkv(q, k, v, do, lse_t, delta_t):
    t, n = q.shape[0], k.shape[0]
    tq, tk = _attn_tiles(t, n)
    nq = t // tq

    def body(q_ref, k_ref, v_ref, do_ref, lse_ref, dl_ref, dk_ref, dv_ref):
        s = pl.program_id(2)
        st = _dot_nt(k_ref[...], q_ref[...]) * ATT_SCALE
        pt = jnp.exp(st - lse_ref[0:1, :])
        dv_part = jnp.dot(pt.astype(BF16), do_ref[...], preferred_element_type=F32)
        dpt = _dot_nt(v_ref[...], do_ref[...])
        dst = pt * (dpt - dl_ref[0:1, :]) * ATT_SCALE
        dk_part = jnp.dot(dst.astype(BF16), q_ref[...], preferred_element_type=F32)

        @pl.when(s == 0)
        def _():
            dk_ref[...] = dk_part
            dv_ref[...] = dv_part

        @pl.when(s > 0)
        def _():
            dk_ref[...] += dk_part
            dv_ref[...] += dv_part

    qspec = pl.BlockSpec((tq, HD_B), lambda kh, j, s: (s % nq, kh * GROUP + s // nq))
    tspec = pl.BlockSpec((8, tq), lambda kh, j, s: (kh * GROUP + s // nq, s % nq))
    kspec = pl.BlockSpec((tk, HD_B), lambda kh, j, s: (j, kh))
    return pl.pallas_call(
        body, grid=(NKV_B, n // tk, GROUP * nq),
        in_specs=[qspec, kspec, kspec, qspec, tspec, tspec],
        out_specs=[kspec, kspec],
        out_shape=[jax.ShapeDtypeStruct((n, KV_B), F32), jax.ShapeDtypeStruct((n, KV_B), F32)],
        compiler_params=pltpu.CompilerParams(dimension_semantics=("parallel", "parallel", "arbitrary"),
                                             vmem_limit_bytes=VMEM_LIMIT),
        name="attn_dkv")(q, k, v, do, lse_t, delta_t)


@jax.custom_vjp
def _attention(q, k, v):
    return _attn_fwd(q.astype(BF16), k.astype(BF16), v.astype(BF16))[0]


def _attention_fwd(q, k, v):
    qb, kb, vb = q.astype(BF16), k.astype(BF16), v.astype(BF16)
    o, lse = _attn_fwd(qb, kb, vb)
    return o, (qb, kb, vb, o, lse)


def _attention_bwd(res, do):
    qb, kb, vb, o, lse = res
    t = qb.shape[0]
    delta = jnp.sum((do * o).reshape(t, NH_B, HD_B), axis=-1)
    lse_h = lse.reshape(t, NH_B, HD_B)[:, :, 0]
    delta_b = jnp.broadcast_to(delta[:, :, None], (t, NH_B, HD_B)).reshape(t, Q_B)
    lse_t = jnp.broadcast_to(lse_h.T[:, None, :], (NH_B, 8, t)).reshape(NH_B * 8, t)
    delta_t = jnp.broadcast_to(delta.T[:, None, :], (NH_B, 8, t)).reshape(NH_B * 8, t)
    dob = do.astype(BF16)
    dq = _attn_dq(qb, kb, vb, dob, lse, delta_b)
    dk, dv = _attn_dkv(qb, kb, vb, dob, lse_t, delta_t)
    return dq, dk, dv


_attention.defvjp(_attention_fwd, _attention_bwd)


def _silu(x):
    return x * jax.nn.sigmoid(x)


def _ln_plain(x):
    mu = jnp.mean(x, -1, keepdims=True)
    var = jnp.mean(jnp.square(x - mu), -1, keepdims=True)
    return (x - mu) * lax.rsqrt(var + EPS)


def _rms(x, w):
    return x * lax.rsqrt(jnp.mean(jnp.square(x), -1, keepdims=True) + EPS) * w


def _rope_tables(t):
    pos = jnp.arange(t)
    row = (pos // GRID_W).astype(F32)
    col = (pos % GRID_W).astype(F32)
    inv = ROPE_THETA ** (-jnp.arange(0, ROT_HALF, 2, dtype=F32) / ROT_HALF)
    ar, ac = row[:, None] * inv[None], col[:, None] * inv[None]
    cos = jnp.concatenate([jnp.cos(ar), jnp.cos(ar), jnp.cos(ac), jnp.cos(ac)], -1)
    sin = jnp.concatenate([-jnp.sin(ar), jnp.sin(ar), -jnp.sin(ac), jnp.sin(ac)], -1)
    return cos, sin


def _rope(x, cos, sin):
    sw = jnp.concatenate([x[..., 32:64], x[..., 0:32], x[..., 96:128], x[..., 64:96]], axis=-1)
    return x * cos[:, None, :] + sw * sin[:, None, :]


def _mlstm_seq(q, k, v, li, lf):
    n = q.shape[0]
    nc = n // MLSTM_CHUNK

    def chunks(a):
        return jnp.moveaxis(a.reshape((nc, MLSTM_CHUNK) + a.shape[1:]), 2, 1)

    tril = jnp.tril(jnp.ones((MLSTM_CHUNK, MLSTM_CHUNK), bool))

    def step(carry, xs):
        c0, n0, m0 = carry
        qc, kc, vc, ic, fc = xs
        b = jnp.cumsum(fc, axis=-1)
        d = jnp.where(tril, b[:, :, None] - b[:, None, :] + ic[:, None, :], -jnp.inf)
        m_inter = b + m0[:, None]
        m = jnp.maximum(m_inter, d.max(-1))
        w = jnp.exp(d - m[:, :, None])
        a = jnp.exp(m_inter - m)
        s = jnp.einsum('hjd,hsd->hjs', qc, kc) * w
        num = a[:, :, None] * jnp.einsum('hvd,hjd->hjv', c0, qc) + jnp.einsum('hjs,hsv->hjv', s, vc)
        den = a * jnp.einsum('hd,hjd->hj', n0, qc) + s.sum(-1)
        h = num / jnp.maximum(jnp.abs(den), jnp.exp(-m))[:, :, None]
        m_end = m[:, -1]
        w_end = jnp.exp(b[:, -1:] - b + ic - m_end[:, None])
        a_end = a[:, -1]
        c1 = a_end[:, None, None] * c0 + jnp.einsum('hs,hsv,hsd->hvd', w_end, vc, kc)
        n1 = a_end[:, None] * n0 + jnp.einsum('hs,hsd->hd', w_end, kc)
        return (c1, n1, m_end), h

    init = (jnp.zeros((NH_A, DV_A, DK_A), F32), jnp.zeros((NH_A, DK_A), F32), jnp.full((NH_A,), M_INIT, F32))
    _, h = lax.scan(step, init, tuple(chunks(a) for a in (q, k, v, li, lf)))
    return jnp.moveaxis(h, 1, 2).reshape(n, NH_A, DV_A)


_mm_mod, _mm_main, _mm_if, _mm_ba, _mm_bb, _mm_out = (_make_mm(t) for t in ("mod", "main", "if", "ba", "bb", "out"))


def _local_loss(diff, const):
    x, c, ctx, target = diff["x"], const["c"], const["ctx"], const["target"]
    t, tc = x.shape[0], ctx.shape[0]
    n, r = tc + t, t + 2 * tc

    sc = jnp.concatenate([_silu(c), _silu(diff["c_ctx"])[None], jnp.zeros((14, D_MODEL), F32)], 0)
    mod = _mm_mod(sc, const["w_mod"], diff["p_mod"])[:2] + diff["b_mod"]
    shift, scale, gate = mod[0, :D_MODEL], mod[0, D_MODEL:2 * D_MODEL], mod[0, 2 * D_MODEL:]
    shift_c, scale_c = mod[1, :D_MODEL], mod[1, D_MODEL:2 * D_MODEL]
    u = _ln_plain(x) * (1 + scale) + shift
    u_c = _ln_plain(ctx) * (1 + scale_c) + shift_c
    u_all = jnp.concatenate([u_c, u, u_c], 0)

    p = _mm_main(u_all, const["w_main"], diff["p_main"])
    gt = _mm_if(u_all, const["w_if"], diff["p_if"])[:, :N_IF] + diff["b_if"]

    qk_pre = p[:, O_QK:O_QK + 2 * QK_A]
    rows = jnp.arange(r)
    seg_start = ((rows == 0) | (rows == tc) | (rows == n))[:, None]
    seg_end = ((rows == tc - 1) | (rows == n - 1) | (rows == r - 1))[:, None]
    prev = jnp.where(seg_start, 0.0, jnp.roll(qk_pre, 1, axis=0))
    nxt = jnp.where(seg_end, 0.0, jnp.roll(qk_pre, -1, axis=0))
    cw, cb = diff["conv_w"], diff["conv_b"]
    qk = _silu(cb + prev * cw[0] + qk_pre * cw[1] + nxt * cw[2])
    q_a = qk[:, :QK_A].reshape(r, NH_A, DK_A)
    k_a = qk[:, QK_A:].reshape(r, NH_A, DK_A) * (DK_A ** -0.5)
    v_a = p[:, O_VA:O_VA + V_A].reshape(r, NH_A, DV_A)
    li_f, lf_f = gt[:, 0:8], jax.nn.log_sigmoid(gt[:, 8:16])
    li_b, lf_b = gt[:, 16:24], jax.nn.log_sigmoid(gt[:, 24:32])

    h_f = _mlstm_seq(q_a[:n], k_a[:n], v_a[:n], li_f[:n], lf_f[:n])[tc:]
    fl = lambda a: jnp.flip(a[tc:], axis=0)
    h_b = jnp.flip(_mlstm_seq(fl(q_a), fl(k_a), fl(v_a), fl(li_b), fl(lf_b)), axis=0)[:t]
    h_l = h_f + h_b

    cos, sin = _rope_tables(t)
    lat = slice(tc, n)
    q_l = _rope(_rms(p[lat, O_QB:O_QB + Q_B].reshape(t, NH_B, HD_B), diff["q_norm_w"]), cos, sin)
    cos_k = jnp.concatenate([jnp.ones((tc, HD_B), F32), cos], 0)
    sin_k = jnp.concatenate([jnp.zeros((tc, HD_B), F32), sin], 0)
    k_all = _rope(_rms(p[:n, O_KB:O_KB + KV_B].reshape(n, NKV_B, HD_B), diff["k_norm_w"]), cos_k, sin_k)
    v_all = p[:n, O_VB:O_VB + KV_B]
    o_attn = _attention(q_l.reshape(t, Q_B), k_all.reshape(n, KV_B), v_all)

    hn = _rms(h_l, diff["mh_norm_w"].reshape(NH_A, DV_A)).reshape(t, V_A)
    a_in = jax.nn.sigmoid(p[lat, O_OA:O_OA + V_A]) * hn * _silu(p[lat, O_ZA:O_ZA + V_A])
    y_a = _mm_ba(a_in, const["w_ba"], diff["p_ba"])
    b_in = o_attn * _silu(p[lat, O_ZB:O_ZB + Q_B])
    y_b = _mm_bb(b_in, const["w_bb"], diff["p_bb"])
    m_in = jax.nn.sigmoid(p[lat, O_GA:O_GA + D_MODEL]) * y_a + jax.nn.sigmoid(p[lat, O_GB:O_GB + D_MODEL]) * y_b
    out = _mm_out(m_in, const["w_out"], diff["p_out"])

    y = _ln_plain(ALPHA * x + gate * out) * diff["ln_w"] + diff["ln_b"]
    return 0.5 * jnp.sum(jnp.mean(jnp.square(y - target), axis=-1))


def _flip(v, bit):
    return 1 - v if bit else v


def _gather_chips(shard, name):
    other = [(1, 0), (0, 1), (1, 1)]

    def body(x_ref, o_ref, send_sems, recv_sems, local_sem):
        x, y, c = lax.axis_index("x"), lax.axis_index("y"), lax.axis_index("c")
        mine = pltpu.make_async_copy(x_ref, o_ref.at[2 * x + y], local_sem)
        mine.start()

        def copy(r, slot):
            dx, dy = other[r]
            return pltpu.make_async_remote_copy(
                src_ref=x_ref, dst_ref=o_ref.at[slot], send_sem=send_sems.at[r], recv_sem=recv_sems.at[r],
                device_id=(_flip(x, dx), _flip(y, dy), c), device_id_type=MESH)

        sends = [copy(r, 2 * x + y) for r in range(3)]
        for cp in sends:
            cp.start()
        for r, (dx, dy) in enumerate(other):
            copy(r, 2 * _flip(x, dx) + _flip(y, dy)).wait_recv()
        for cp in sends:
            cp.wait_send()
        mine.wait()

    return pl.pallas_call(
        body, out_shape=jax.ShapeDtypeStruct((N_CHIPS,) + shard.shape, shard.dtype),
        in_specs=[pl.BlockSpec(memory_space=pl.ANY)], out_specs=pl.BlockSpec(memory_space=pl.ANY),
        scratch_shapes=[pltpu.SemaphoreType.DMA((3,)), pltpu.SemaphoreType.DMA((3,)), pltpu.SemaphoreType.DMA],
        name=name)(shard)


def _scatter_grads(slots, name):
    def body(g_ref, o_ref, send_sems, recv_sems, local_sem):
        x, y, c = lax.axis_index("x"), lax.axis_index("y"), lax.axis_index("c")
        me = 4 * x + 2 * y + c
        mine = pltpu.make_async_copy(g_ref.at[2 * x + y], o_ref.at[me], local_sem)
        mine.start()

        def peer(r):
            return _flip(x, (r >> 2) & 1), _flip(y, (r >> 1) & 1), _flip(c, r & 1)

        def copy(r, src_slot, dst_slot):
            return pltpu.make_async_remote_copy(
                src_ref=g_ref.at[src_slot], dst_ref=o_ref.at[dst_slot], send_sem=send_sems.at[r - 1],
                recv_sem=recv_sems.at[r - 1], device_id=peer(r), device_id_type=MESH)

        sends = []
        for r in range(1, N_DEV):
            px, py, _ = peer(r)
            sends.append(copy(r, 2 * px + py, me))
            sends[-1].start()
        for r in range(1, N_DEV):
            px, py, pc = peer(r)
            copy(r, 2 * x + y, 4 * px + 2 * py + pc).wait_recv()
        for cp in sends:
            cp.wait_send()
        mine.wait()

    return pl.pallas_call(
        body, out_shape=jax.ShapeDtypeStruct((N_DEV,) + slots.shape[1:], slots.dtype),
        in_specs=[pl.BlockSpec(memory_space=pl.ANY)], out_specs=pl.BlockSpec(memory_space=pl.ANY),
        scratch_shapes=[pltpu.SemaphoreType.DMA((N_DEV - 1,)), pltpu.SemaphoreType.DMA((N_DEV - 1,)),
                        pltpu.SemaphoreType.DMA],
        name=name)(slots)


def _allreduce_small(v, name):
    def body(v_ref, o_ref, buf, send_sems, recv_sems):
        x, y, c = lax.axis_index("x"), lax.axis_index("y"), lax.axis_index("c")
        me = 4 * x + 2 * y + c
        buf[me] = v_ref[...]

        def peer(r):
            return _flip(x, (r >> 2) & 1), _flip(y, (r >> 1) & 1), _flip(c, r & 1)

        def copy(r, dst_slot):
            return pltpu.make_async_remote_copy(
                src_ref=v_ref, dst_ref=buf.at[dst_slot], send_sem=send_sems.at[r - 1],
                recv_sem=recv_sems.at[r - 1], device_id=peer(r), device_id_type=MESH)

        sends = [copy(r, me) for r in range(1, N_DEV)]
        for cp in sends:
            cp.start()
        for r in range(1, N_DEV):
            px, py, pc = peer(r)
            copy(r, 4 * px + 2 * py + pc).wait_recv()
        for cp in sends:
            cp.wait_send()
        acc = buf[0]
        for d in range(1, N_DEV):
            acc = acc + buf[d]
        o_ref[...] = acc

    return pl.pallas_call(
        body, out_shape=jax.ShapeDtypeStruct(v.shape, v.dtype),
        in_specs=[pl.BlockSpec(memory_space=pltpu.VMEM)], out_specs=pl.BlockSpec(memory_space=pltpu.VMEM),
        scratch_shapes=[pltpu.VMEM((N_DEV,) + v.shape, v.dtype), pltpu.SemaphoreType.DMA((N_DEV - 1,)),
                        pltpu.SemaphoreType.DMA((N_DEV - 1,))],
        name=name)(v)


def _adamw_math(w, g, m, v):
    m = ADAM_B1 * m + (1.0 - ADAM_B1) * g
    v = ADAM_B2 * v + (1.0 - ADAM_B2) * jnp.square(g)
    m_hat = m / (1.0 - ADAM_B1 ** ADAM_STEP)
    v_hat = v / (1.0 - ADAM_B2 ** ADAM_STEP)
    delta = -ADAM_LR * (m_hat / (jnp.sqrt(v_hat) + ADAM_EPS) + ADAM_WD * w)
    return delta, m, v


def _adamw_sum(parts, w, m, v, name):
    npart, rows, cols = parts.shape
    tr = _pick(rows, (64, 32, 16, 8)) if rows >= 8 else rows

    def body(p_ref, w_ref, m_ref, v_ref, g_out, d_out, m_out, v_out):
        g = p_ref[0].astype(F32)
        for k in range(1, npart):
            g = g + p_ref[k].astype(F32)
        d, m2, v2 = _adamw_math(w_ref[...], g, m_ref[...], v_ref[...])
        g_out[...] = g
        d_out[...] = d
        m_out[...] = m2
        v_out[...] = v2

    spec = pl.BlockSpec((tr, cols), lambda i: (i, 0))
    shp = jax.ShapeDtypeStruct((rows, cols), F32)
    return pl.pallas_call(
        body, grid=(rows // tr,),
        in_specs=[pl.BlockSpec((npart, tr, cols), lambda i: (0, i, 0)), spec, spec, spec],
        out_specs=[spec, spec, spec, spec], out_shape=[shp, shp, shp, shp],
        compiler_params=pltpu.CompilerParams(dimension_semantics=("parallel",), vmem_limit_bytes=VMEM_LIMIT),
        name=name)(parts, w, m, v)


SMALL_ROWS = 16


def _pack_small(c_ctx, b_mod, conv_b, mh, ln_w, ln_b, conv_w_rows, b_if, qn, kn):
    last = jnp.concatenate([b_if.reshape(-1), qn.reshape(-1), kn.reshape(-1),
                            jnp.zeros((D_MODEL - N_IF - 2 * HD_B,), F32)])
    rows = [c_ctx.reshape(1, D_MODEL), b_mod.reshape(3, D_MODEL), conv_b.reshape(1, D_MODEL),
            mh.reshape(1, D_MODEL), ln_w.reshape(1, D_MODEL), ln_b.reshape(1, D_MODEL),
            conv_w_rows.reshape(3, D_MODEL), last[None], jnp.zeros((SMALL_ROWS - 12, D_MODEL), F32)]
    return jnp.concatenate(rows, 0)


def _unpack_small(pk, conv_cols):
    return dict(c_ctx=pk[0], b_mod=pk[1:4].reshape(1, 3 * D_MODEL), conv_b=pk[4:5], mh_norm_w=pk[5:6],
                ln_w=pk[6:7], ln_b=pk[7:8], conv_w=pk[8:11, :conv_cols][None], b_if=pk[11:12, :N_IF],
                q_norm_w=pk[11:12, N_IF:N_IF + HD_B], k_norm_w=pk[11:12, N_IF + HD_B:N_IF + 2 * HD_B])


def kernel(x, c, ctx, c_ctx, w_mod, b_mod, w_in, b_if, conv_w, conv_b, mh_norm_w, q_norm_w, k_norm_w, w_branch_a, w_branch_b, w_out, ln_w, ln_b, loss_target, m_c_ctx, m_w_mod, m_b_mod, m_w_in, m_b_if, m_conv_w, m_conv_b, m_mh_norm_w, m_q_norm_w, m_k_norm_w, m_w_branch_a, m_w_branch_b, m_w_out, m_ln_w, m_ln_b, v_c_ctx, v_w_mod, v_b_mod, v_w_in, v_b_if, v_conv_w, v_conv_b, v_mh_norm_w, v_q_norm_w, v_k_norm_w, v_w_branch_a, v_w_branch_b, v_w_out, v_ln_w, v_ln_b):
    chip = 2 * lax.axis_index("x") + lax.axis_index("y")
    mod_cols, in_cols, conv_cols = w_mod.shape[2], w_in.shape[2], conv_w.shape[2]
    br_rows = w_out.shape[1]

    g_mod = _gather_chips(w_mod[0].astype(BF16), "gather_w_mod")
    g_in = _gather_chips(w_in[0].astype(BF16), "gather_w_in")
    g_ba = _gather_chips(w_branch_a[0].astype(BF16), "gather_w_ba")
    g_bb = _gather_chips(w_branch_b[0].astype(BF16), "gather_w_bb")
    g_out = _gather_chips(w_out[0].astype(BF16), "gather_w_out")
    g_conv = _gather_chips(conv_w[0], "gather_conv_w")
    w_mod_full = jnp.moveaxis(g_mod, 0, 1).reshape(D_MODEL, N_CHIPS * mod_cols)
    w_in_full = jnp.moveaxis(g_in, 0, 1).reshape(D_MODEL, N_CHIPS * in_cols)
    w_main = jnp.concatenate([w_in_full[:, :IF_START], w_in_full[:, IF_START + N_IF:]], 1)
    w_if = jnp.pad(w_in_full[:, IF_START:IF_START + N_IF], ((0, 0), (0, IF_PAD - N_IF)))
    conv_w_full = jnp.moveaxis(g_conv, 0, 1).reshape(3, N_CHIPS * conv_cols)

    const = dict(c=c, ctx=ctx[0], target=loss_target[0], w_mod=w_mod_full, w_main=w_main, w_if=w_if,
                 w_ba=g_ba.reshape(D_MODEL, D_MODEL), w_bb=g_bb.reshape(D_MODEL, D_MODEL),
                 w_out=g_out.reshape(D_MODEL, D_MODEL))
    diff = dict(x=x[0], c_ctx=c_ctx, b_mod=b_mod[0], b_if=b_if[0], conv_w=conv_w_full, conv_b=conv_b[0],
                mh_norm_w=mh_norm_w[0], q_norm_w=q_norm_w[0], k_norm_w=k_norm_w[0], ln_w=ln_w[0], ln_b=ln_b[0],
                p_mod=jnp.zeros(w_mod_full.shape, F32), p_main=jnp.zeros(w_main.shape, F32),
                p_if=jnp.zeros(w_if.shape, F32), p_ba=jnp.zeros((D_MODEL, D_MODEL), F32),
                p_bb=jnp.zeros((D_MODEL, D_MODEL), F32), p_out=jnp.zeros((D_MODEL, D_MODEL), F32))
    loss_local, g = jax.value_and_grad(_local_loss)(diff, const)
    loss = lax.psum(loss_local, ("x", "y", "c"))

    g_small = _allreduce_small(
        _pack_small(g["c_ctx"], g["b_mod"], g["conv_b"], g["mh_norm_w"], g["ln_w"], g["ln_b"], g["conv_w"],
                    g["b_if"], g["q_norm_w"], g["k_norm_w"]), "allreduce_small")
    conv_g = lax.dynamic_slice(g_small[8:11], (0, chip * conv_cols), (3, conv_cols))
    g_small = g_small.at[8:11].set(jnp.pad(conv_g, ((0, 0), (0, D_MODEL - conv_cols))))
    pad_conv = lambda a: jnp.pad(a[0], ((0, 0), (0, D_MODEL - conv_cols)))
    packed = [_pack_small(cc, bm[0], cb[0], mh[0], lw[0], lb[0], pad_conv(cw), bi[0], qn[0], kn[0])
              for cc, bm, cb, mh, lw, lb, cw, bi, qn, kn in (
                  (c_ctx, b_mod, conv_b, mh_norm_w, ln_w, ln_b, conv_w, b_if, q_norm_w, k_norm_w),
                  (m_c_ctx, m_b_mod, m_conv_b, m_mh_norm_w, m_ln_w, m_ln_b, m_conv_w, m_b_if, m_q_norm_w, m_k_norm_w),
                  (v_c_ctx, v_b_mod, v_conv_b, v_mh_norm_w, v_ln_w, v_ln_b, v_conv_w, v_b_if, v_q_norm_w, v_k_norm_w))]
    small = [_unpack_small(a, conv_cols)
             for a in _adamw_sum(g_small[None], packed[0], packed[1], packed[2], "adamw_small")]

    def col_slots(gfull, cols):
        return jnp.moveaxis(gfull.reshape(D_MODEL, N_CHIPS, cols), 1, 0).astype(BF16)

    g_in_full = jnp.concatenate([g["p_main"][:, :IF_START], g["p_if"][:, :N_IF], g["p_main"][:, IF_START:]], 1)
    big = {}
    for nm, slots, w_, m_, v_ in (
            ("w_mod", col_slots(g["p_mod"], mod_cols), w_mod, m_w_mod, v_w_mod),
            ("w_in", col_slots(g_in_full, in_cols), w_in, m_w_in, v_w_in),
            ("w_branch_a", g["p_ba"].reshape(N_CHIPS, br_rows, D_MODEL).astype(BF16), w_branch_a, m_w_branch_a, v_w_branch_a),
            ("w_branch_b", g["p_bb"].reshape(N_CHIPS, br_rows, D_MODEL).astype(BF16), w_branch_b, m_w_branch_b, v_w_branch_b),
            ("w_out", g["p_out"].reshape(N_CHIPS, br_rows, D_MODEL).astype(BF16), w_out, m_w_out, v_w_out)):
        parts = _scatter_grads(slots, "scatter_" + nm)
        big[nm] = [a[None] for a in _adamw_sum(parts, w_[0], m_[0], v_[0], "adamw_" + nm)]

    names = ["c_ctx", "w_mod", "b_mod", "w_in", "b_if", "conv_w", "conv_b", "mh_norm_w", "q_norm_w", "k_norm_w",
             "w_branch_a", "w_branch_b", "w_out", "ln_w", "ln_b"]
    outs = [[big[nm][k] if nm in big else small[k][nm] for nm in names] for k in range(4)]
    return (loss, g["x"][None], *outs[0], *outs[1], *outs[2], *outs[3])
```

```python
import functools

import jax
import jax.numpy as jnp
from jax import lax
from jax.experimental import pallas as pl
from jax.experimental.pallas import tpu as pltpu

F32 = jnp.float32
BF16 = jnp.bfloat16
MESH = pl.DeviceIdType.MESH

D_MODEL = 2048
NH_A, DK_A, DV_A = 8, 128, 256
QK_A, V_A = NH_A * DK_A, NH_A * DV_A
NH_B, NKV_B, HD_B = 16, 4, 128
Q_B, KV_B = NH_B * HD_B, NKV_B * HD_B
GRID_W = 64
ROT_HALF = HD_B // 2
ROPE_THETA = 10000.0
M_INIT = -1e30
EPS = 1e-6
ALPHA = 2.0 ** 0.25
N_IN = 17440
IF_START, N_IF, IF_PAD = 4096, 32, 128
N_MAIN = N_IN - N_IF
O_QK, O_VA, O_KB, O_VB, O_OA, O_ZA, O_QB, O_ZB, O_GA, O_GB = (
    0, 2048, 4096, 4608, 5120, 7168, 9216, 11264, 13312, 15360)
MLSTM_CHUNK = 256

ADAM_LR, ADAM_B1, ADAM_B2, ADAM_EPS, ADAM_WD, ADAM_STEP = 0.001, 0.9, 0.999, 1e-08, 0.01, 10

VMEM_LIMIT = 48 * 1024 * 1024
N_CHIPS, N_DEV = 4, 8
MX = BF16


def _pick(n, cands):
    for c in cands:
        if n % c == 0:
            return c
    raise ValueError(f"no tile for {n} in {cands}")


def _dot(a, b):
    return jnp.dot(a, b, preferred_element_type=F32)


def _dot_nt(a, b):
    return lax.dot_general(a, b, (((1,), (1,)), ((), ())), preferred_element_type=F32)


def _dot_tn(a, b):
    return lax.dot_general(a, b, (((0,), (0,)), ((), ())), preferred_element_type=F32)


def _mm_nn(a, b, name):
    m, k = a.shape
    _, n = b.shape
    tm = _pick(m, (512, 256, 128, 64, 32, 16))
    tn = _pick(n, (1024, 512, 256, 128))

    def body(a_ref, b_ref, o_ref):
        o_ref[...] = _dot(a_ref[...], b_ref[...])

    return pl.pallas_call(
        body, grid=(m // tm, n // tn),
        in_specs=[pl.BlockSpec((tm, k), lambda i, j: (i, 0)), pl.BlockSpec((k, tn), lambda i, j: (0, j))],
        out_specs=pl.BlockSpec((tm, tn), lambda i, j: (i, j)),
        out_shape=jax.ShapeDtypeStruct((m, n), F32),
        compiler_params=pltpu.CompilerParams(dimension_semantics=("parallel", "parallel"),
                                             vmem_limit_bytes=VMEM_LIMIT),
        name=name)(a, b)


def _mm_nt(g, w, name):
    m, n = g.shape
    k, _ = w.shape
    tm = _pick(m, (512, 256, 128, 64, 32, 16))
    tn = _pick(n, (1024, 512, 256, 128))

    def body(g_ref, w_ref, o_ref):
        part = _dot_nt(g_ref[...], w_ref[...])

        @pl.when(pl.program_id(1) == 0)
        def _():
            o_ref[...] = part

        @pl.when(pl.program_id(1) > 0)
        def _():
            o_ref[...] += part

    return pl.pallas_call(
        body, grid=(m // tm, n // tn),
        in_specs=[pl.BlockSpec((tm, tn), lambda i, j: (i, j)), pl.BlockSpec((k, tn), lambda i, j: (0, j))],
        out_specs=pl.BlockSpec((tm, k), lambda i, j: (i, 0)),
        out_shape=jax.ShapeDtypeStruct((m, k), F32),
        compiler_params=pltpu.CompilerParams(dimension_semantics=("parallel", "arbitrary"),
                                             vmem_limit_bytes=VMEM_LIMIT),
        name=name)(g, w)


def _mm_tn(a, g, name):
    m, k = a.shape
    _, n = g.shape
    tm = _pick(m, (512, 256, 128, 64, 32, 16))
    tn = _pick(n, (1024, 512, 256, 128))

    def body(a_ref, g_ref, o_ref):
        part = _dot_tn(a_ref[...], g_ref[...])

        @pl.when(pl.program_id(1) == 0)
        def _():
            o_ref[...] = part

        @pl.when(pl.program_id(1) > 0)
        def _():
            o_ref[...] += part

    return pl.pallas_call(
        body, grid=(n // tn, m // tm),
        in_specs=[pl.BlockSpec((tm, k), lambda j, i: (i, 0)), pl.BlockSpec((tm, tn), lambda j, i: (i, j))],
        out_specs=pl.BlockSpec((k, tn), lambda j, i: (0, j)),
        out_shape=jax.ShapeDtypeStruct((k, n), F32),
        compiler_params=pltpu.CompilerParams(dimension_semantics=("parallel", "arbitrary"),
                                             vmem_limit_bytes=VMEM_LIMIT),
        name=name)(a, g)


def _make_mm(tag):
    @jax.custom_vjp
    def mm(a, w, proxy):
        del proxy
        return _mm_nn(a.astype(BF16), w, f"mm_{tag}_fwd")

    def fwd(a, w, proxy):
        del proxy
        ab = a.astype(BF16)
        return _mm_nn(ab, w, f"mm_{tag}_fwd"), (ab, w)

    def bwd(res, g):
        ab, w = res
        gb = g.astype(BF16)
        da = _mm_nt(gb, w, f"mm_{tag}_da")
        dw = _mm_tn(ab, gb, f"mm_{tag}_dw")
        return da, jnp.zeros_like(w), dw

    mm.defvjp(fwd, bwd)
    return mm


ATT_SCALE = HD_B ** -0.5
GROUP = NH_B // NKV_B


def _attn_tiles(t, n):
    return _pick(t, (512, 256, 128)), _pick(n, (768, 512, 256, 128))


def _attn_fwd(q, k, v):
    t, n = q.shape[0], k.shape[0]
    tq, tk = _attn_tiles(t, n)
    nk = n // tk

    def body(q_ref, k_ref, v_ref, o_ref, lse_ref, m_sc, l_sc, acc_sc):
        j = pl.program_id(2)

        @pl.when(j == 0)
        def _():
            m_sc[...] = jnp.full(m_sc.shape, -jnp.inf, F32)
            l_sc[...] = jnp.zeros(l_sc.shape, F32)
            acc_sc[...] = jnp.zeros(acc_sc.shape, F32)

        kb, vb = k_ref[...], v_ref[...]
        for g in range(GROUP):
            cols = slice(g * HD_B, (g + 1) * HD_B)
            s = _dot_nt(q_ref[:, cols], kb) * ATT_SCALE
            m_prev = m_sc[g]
            m_new = jnp.maximum(m_prev, jnp.max(s, axis=-1, keepdims=True))
            alpha = jnp.exp(m_prev - m_new)
            p = jnp.exp(s - m_new)
            l_sc[g] = alpha * l_sc[g] + jnp.sum(p, axis=-1, keepdims=True)
            acc_sc[:, cols] = alpha * acc_sc[:, cols] + _dot(p.astype(BF16), vb)
            m_sc[g] = m_new

        @pl.when(j == nk - 1)
        def _():
            for g in range(GROUP):
                cols = slice(g * HD_B, (g + 1) * HD_B)
                o_ref[:, cols] = acc_sc[:, cols] / l_sc[g]
                lse_ref[:, cols] = jnp.broadcast_to(m_sc[g] + jnp.log(l_sc[g]), (tq, HD_B))

    qspec = pl.BlockSpec((tq, GROUP * HD_B), lambda kh, i, j: (i, kh))
    kspec = pl.BlockSpec((tk, HD_B), lambda kh, i, j: (j, kh))
    return pl.pallas_call(
        body, grid=(NKV_B, t // tq, nk),
        in_specs=[qspec, kspec, kspec], out_specs=[qspec, qspec],
        out_shape=[jax.ShapeDtypeStruct((t, Q_B), F32), jax.ShapeDtypeStruct((t, Q_B), F32)],
        scratch_shapes=[pltpu.VMEM((GROUP, tq, 1), F32), pltpu.VMEM((GROUP, tq, 1), F32),
                        pltpu.VMEM((tq, GROUP * HD_B), F32)],
        compiler_params=pltpu.CompilerParams(dimension_semantics=("parallel", "parallel", "arbitrary"),
                                             vmem_limit_bytes=VMEM_LIMIT),
        name="attn_fwd")(q, k, v)


def _attn_dq(q, k, v, do, lse, delta):
    t, n = q.shape[0], k.shape[0]
    tq, tk = _attn_tiles(t, n)

    def body(q_ref, k_ref, v_ref, do_ref, lse_ref, dl_ref, dq_ref):
        j = pl.program_id(2)
        kb, vb = k_ref[...], v_ref[...]
        parts = []
        for g in range(GROUP):
            cols = slice(g * HD_B, (g + 1) * HD_B)
            s = _dot_nt(q_ref[:, cols], kb) * ATT_SCALE
            p = jnp.exp(s - lse_ref[:, g * HD_B:g * HD_B + 1])
            dp = _dot_nt(do_ref[:, cols], vb)
            ds = p * (dp - dl_ref[:, g * HD_B:g * HD_B + 1]) * ATT_SCALE
            parts.append(_dot(ds.astype(BF16), kb))

        @pl.when(j == 0)
        def _():
            for g in range(GROUP):
                dq_ref[:, g * HD_B:(g + 1) * HD_B] = parts[g]

        @pl.when(j > 0)
        def _():
            for g in range(GROUP):
                dq_ref[:, g * HD_B:(g + 1) * HD_B] += parts[g]

    qspec = pl.BlockSpec((tq, GROUP * HD_B), lambda kh, i, j: (i, kh))
    kspec = pl.BlockSpec((tk, HD_B), lambda kh, i, j: (j, kh))
    return pl.pallas_call(
        body, grid=(NKV_B, t // tq, n // tk),
        in_specs=[qspec, kspec, kspec, qspec, qspec, qspec],
        out_specs=qspec,
        out_shape=jax.ShapeDtypeStruct((t, Q_B), F32),
        compiler_params=pltpu.CompilerParams(dimension_semantics=("parallel", "parallel", "arbitrary"),
                                             vmem_limit_bytes=VMEM_LIMIT),
        name="attn_dq")(q, k, v, do, lse, delta)


def _attn_dkv(q, k, v, do, lse_t, delta_t):
    t, n = q.shape[0], k.shape[0]
    tq, tk = _attn_tiles(t, n)

    def body(q_ref, k_ref, v_ref, do_ref, lse_ref, dl_ref, dk_ref, dv_ref):
        i = pl.program_id(2)
        kb, vb = k_ref[...], v_ref[...]
        dk_part = dv_part = None
        for g in range(GROUP):
            cols = slice(g * HD_B, (g + 1) * HD_B)
            qg, dog = q_ref[:, cols], do_ref[:, cols]
            st = _dot_nt(kb, qg) * ATT_SCALE
            pt = jnp.exp(st - lse_ref[8 * g:8 * g + 1, :])
            dvg = _dot(pt.astype(BF16), dog)
            dpt = _dot_nt(vb, dog)
            dst = pt * (dpt - dl_ref[8 * g:8 * g + 1, :]) * ATT_SCALE
            dkg = _dot(dst.astype(BF16), qg)
            dk_part = dkg if dk_part is None else dk_part + dkg
            dv_part = dvg if dv_part is None else dv_part + dvg

        @pl.when(i == 0)
        def _():
            dk_ref[...] = dk_part
            dv_ref[...] = dv_part

        @pl.when(i > 0)
        def _():
            dk_ref[...] += dk_part
            dv_ref[...] += dv_part

    qspec = pl.BlockSpec((tq, GROUP * HD_B), lambda kh, j, i: (i, kh))
    tspec = pl.BlockSpec((8 * GROUP, tq), lambda kh, j, i: (kh, i))
    kspec = pl.BlockSpec((tk, HD_B), lambda kh, j, i: (j, kh))
    return pl.pallas_call(
        body, grid=(NKV_B, n // tk, t // tq),
        in_specs=[qspec, kspec, kspec, qspec, tspec, tspec],
        out_specs=[kspec, kspec],
        out_shape=[jax.ShapeDtypeStruct((n, KV_B), F32), jax.ShapeDtypeStruct((n, KV_B), F32)],
        compiler_params=pltpu.CompilerParams(dimension_semantics=("parallel", "parallel", "arbitrary"),
                                             vmem_limit_bytes=VMEM_LIMIT),
        name="attn_dkv")(q, k, v, do, lse_t, delta_t)


@jax.custom_vjp
def _attention(q, k, v):
    return _attn_fwd(q.astype(BF16), k.astype(BF16), v.astype(BF16))[0]


def _attention_fwd(q, k, v):
    qb, kb, vb = q.astype(BF16), k.astype(BF16), v.astype(BF16)
    o, lse = _attn_fwd(qb, kb, vb)
    return o, (qb, kb, vb, o, lse)


def _attention_bwd(res, do):
    qb, kb, vb, o, lse = res
    t = qb.shape[0]
    delta = jnp.sum((do * o).reshape(t, NH_B, HD_B), axis=-1)
    lse_h = lse.reshape(t, NH_B, HD_B)[:, :, 0]
    delta_b = jnp.broadcast_to(delta[:, :, None], (t, NH_B, HD_B)).reshape(t, Q_B)
    lse_t = jnp.broadcast_to(lse_h.T[:, None, :], (NH_B, 8, t)).reshape(NH_B * 8, t)
    delta_t = jnp.broadcast_to(delta.T[:, None, :], (NH_B, 8, t)).reshape(NH_B * 8, t)
    dob = do.astype(BF16)
    dq = _attn_dq(qb, kb, vb, dob, lse, delta_b)
    dk, dv = _attn_dkv(qb, kb, vb, dob, lse_t, delta_t)
    return dq, dk, dv


_attention.defvjp(_attention_fwd, _attention_bwd)


def _mlstm_chunk_forward(q, k, v, lic, lfc, lir, lfr, s0, n0, m0):
    L = q.shape[0]
    ti = lax.broadcasted_iota(jnp.int32, (L, L), 0)
    si = lax.broadcasted_iota(jnp.int32, (L, L), 1)
    tril = si <= ti
    b_col = jnp.sum(jnp.where(tril, lfr, 0.0), axis=1, keepdims=True)
    b_row = jnp.sum(jnp.where(ti <= si, lfc, 0.0), axis=0, keepdims=True)
    d = jnp.where(tril, b_col - b_row + lir, -jnp.inf)
    m = jnp.maximum(b_col + m0, jnp.max(d, axis=1, keepdims=True))
    w = jnp.exp(d - m)
    a = jnp.exp(b_col + m0 - m)
    qm, km, vm = q.astype(MX), k.astype(MX), v.astype(MX)
    s = _dot_nt(qm, km) * w
    qs = _dot(qm, s0.astype(MX))
    num = a * qs + _dot(s.astype(MX), vm)
    qn = jnp.sum(q * n0, axis=1, keepdims=True)
    den = a * qn + jnp.sum(s, axis=1, keepdims=True)
    floor = jnp.exp(-m)
    dd = jnp.maximum(jnp.abs(den), floor)
    b_last = jnp.sum(lfr, axis=1, keepdims=True)
    m_end = jnp.maximum(b_last + m0, jnp.max(b_last - b_row + lir, axis=1, keepdims=True))
    w_end = jnp.exp(b_last - b_col + lic - m_end)
    a_end = jnp.exp(b_last + m0 - m_end)
    return dict(ti=ti, si=si, w=w, a=a, s=s, qs=qs, num=num, qn=qn, den=den, floor=floor, dd=dd,
                m_end=m_end, w_end=w_end, a_end=a_end, qm=qm, km=km, vm=vm)


def _mlstm_fwd_call(q, k, v, gc, gr):
    n = q.shape[0]
    L = MLSTM_CHUNK
    nc = n // L

    def body(q_ref, k_ref, v_ref, gc_ref, gr_ref, h_ref, s0_ref, n0_ref, m0_ref, s_sc, n_sc, m_sc):
        @pl.when(pl.program_id(1) == 0)
        def _():
            s_sc[...] = jnp.zeros(s_sc.shape, F32)
            n_sc[...] = jnp.zeros(n_sc.shape, F32)
            m_sc[...] = jnp.full(m_sc.shape, M_INIT, F32)

        s0, n0, m0 = s_sc[...], n_sc[...], m_sc[...]
        s0_ref[0, 0] = s0
        n0_ref[0, 0] = n0
        m0_ref[0, 0] = jnp.broadcast_to(m0, (1, DK_A))
        k, v = k_ref[...], v_ref[...]
        f = _mlstm_chunk_forward(q_ref[...], k, v, gc_ref[0, 0], gc_ref[1, 0], gr_ref[0, 0], gr_ref[1, 0], s0, n0, m0)
        h_ref[...] = f["num"] / f["dd"]
        s_sc[...] = f["a_end"] * s0 + _dot_tn(f["km"], (f["w_end"] * v).astype(MX))
        n_sc[...] = f["a_end"] * n0 + jnp.sum(f["w_end"] * k, axis=0, keepdims=True)
        m_sc[...] = f["m_end"]

    qk_spec = pl.BlockSpec((L, DK_A), lambda h, i: (i, h))
    v_spec = pl.BlockSpec((L, DV_A), lambda h, i: (i, h))
    gc_spec = pl.BlockSpec((2, 1, L, 1), lambda h, i: (0, h, i, 0))
    gr_spec = pl.BlockSpec((2, 1, 1, L), lambda h, i: (0, h, 0, i))
    st_spec = pl.BlockSpec((1, 1, DK_A, DV_A), lambda h, i: (h, i, 0, 0))
    vec_spec = pl.BlockSpec((1, 1, 1, DK_A), lambda h, i: (h, i, 0, 0))
    return pl.pallas_call(
        body, grid=(NH_A, nc),
        in_specs=[qk_spec, qk_spec, v_spec, gc_spec, gr_spec],
        out_specs=[v_spec, st_spec, vec_spec, vec_spec],
        out_shape=[jax.ShapeDtypeStruct((n, V_A), F32), jax.ShapeDtypeStruct((NH_A, nc, DK_A, DV_A), F32),
                   jax.ShapeDtypeStruct((NH_A, nc, 1, DK_A), F32), jax.ShapeDtypeStruct((NH_A, nc, 1, DK_A), F32)],
        scratch_shapes=[pltpu.VMEM((DK_A, DV_A), F32), pltpu.VMEM((1, DK_A), F32), pltpu.VMEM((1, 1), F32)],
        compiler_params=pltpu.CompilerParams(dimension_semantics=("parallel", "arbitrary"),
                                             vmem_limit_bytes=VMEM_LIMIT),
        name="mlstm_fwd")(q, k, v, gc, gr)


def _mlstm_bwd_call(q, k, v, gc, gr, s0_all, n0_all, m0_all, dh):
    n = q.shape[0]
    L = MLSTM_CHUNK
    nc = n // L

    def body(q_ref, k_ref, v_ref, gc_ref, gr_ref, s0_ref, n0_ref, m0_ref, dh_ref,
             dq_ref, dk_ref, dv_ref, dg_ref, ds_sc, dn_sc):
        @pl.when(pl.program_id(1) == 0)
        def _():
            ds_sc[...] = jnp.zeros(ds_sc.shape, F32)
            dn_sc[...] = jnp.zeros(dn_sc.shape, F32)

        q, k, v = q_ref[...], k_ref[...], v_ref[...]
        s0, n0, m0 = s0_ref[0, 0], n0_ref[0, 0], m0_ref[0, 0][:, 0:1]
        f = _mlstm_chunk_forward(q, k, v, gc_ref[0, 0], gc_ref[1, 0], gr_ref[0, 0], gr_ref[1, 0], s0, n0, m0)
        ti, si, w, a, s = f["ti"], f["si"], f["w"], f["a"], f["s"]
        qm, km, vm, w_end, a_end = f["qm"], f["km"], f["vm"], f["w_end"], f["a_end"]
        ds1, dn1 = ds_sc[...], dn_sc[...]
        ds1m, s0m = ds1.astype(MX), s0.astype(MX)

        inv = 1.0 / f["dd"]
        dh = dh_ref[...]
        dnum = dh * inv
        ddd = -jnp.sum(dh * (f["num"] * inv), axis=1, keepdims=True) * inv
        dden = jnp.where(jnp.abs(f["den"]) > f["floor"], jnp.sign(f["den"]) * ddd, 0.0)
        adn = (a * dnum).astype(MX)
        dnm = dnum.astype(MX)
        ds_tot = _dot_nt(dnm, vm) + dden
        dsr = (ds_tot * w).astype(MX)
        e = ds_tot * s
        wv = (w_end * v).astype(MX)
        kds = _dot(km, ds1m)
        dq_ref[...] = _dot_nt(adn, s0m) + _dot(dsr, km) + (dden * a) * n0
        dk_ref[...] = _dot_tn(dsr, qm) + _dot_nt(wv, ds1m) + w_end * dn1
        dv_ref[...] = _dot_tn(s.astype(MX), dnm) + w_end * kds

        eye = ti == si
        to_col = lambda r: jnp.sum(jnp.where(eye, r, 0.0), axis=1, keepdims=True)
        to_row = lambda c: jnp.sum(jnp.where(eye, c, 0.0), axis=0, keepdims=True)
        g_a = (jnp.sum(dnum * f["qs"], axis=1, keepdims=True) + dden * f["qn"]) * a
        g_w = (jnp.sum(v * kds, axis=1, keepdims=True) + jnp.sum(k * dn1, axis=1, keepdims=True)) * w_end
        g_end = (jnp.sum(jnp.sum(ds1 * s0, axis=1, keepdims=True), axis=0, keepdims=True)
                 + jnp.sum(dn1 * n0, axis=1, keepdims=True)) * a_end
        col_e = to_col(jnp.sum(e, axis=0, keepdims=True))
        db = jnp.sum(e, axis=1, keepdims=True) - col_e + g_a - g_w
        last = lax.broadcasted_iota(jnp.int32, (L, 1), 0) == L - 1
        db = db + jnp.where(last, jnp.sum(g_w, axis=0, keepdims=True) + g_end, 0.0)
        dg_ref[0, 0] = col_e + g_w
        dg_ref[1, 0] = jnp.sum(jnp.where(si >= ti, to_row(db), 0.0), axis=1, keepdims=True)

        ds_sc[...] = a_end * ds1 + _dot_tn(qm, adn)
        dn_sc[...] = a_end * dn1 + jnp.sum((dden * a) * q, axis=0, keepdims=True)

    rev = lambda i: nc - 1 - i
    qk_spec = pl.BlockSpec((L, DK_A), lambda h, i: (rev(i), h))
    v_spec = pl.BlockSpec((L, DV_A), lambda h, i: (rev(i), h))
    gc_spec = pl.BlockSpec((2, 1, L, 1), lambda h, i: (0, h, rev(i), 0))
    gr_spec = pl.BlockSpec((2, 1, 1, L), lambda h, i: (0, h, 0, rev(i)))
    st_spec = pl.BlockSpec((1, 1, DK_A, DV_A), lambda h, i: (h, rev(i), 0, 0))
    vec_spec = pl.BlockSpec((1, 1, 1, DK_A), lambda h, i: (h, rev(i), 0, 0))
    return pl.pallas_call(
        body, grid=(NH_A, nc),
        in_specs=[qk_spec, qk_spec, v_spec, gc_spec, gr_spec, st_spec, vec_spec, vec_spec, v_spec],
        out_specs=[qk_spec, qk_spec, v_spec, gc_spec],
        out_shape=[jax.ShapeDtypeStruct((n, QK_A), F32), jax.ShapeDtypeStruct((n, QK_A), F32),
                   jax.ShapeDtypeStruct((n, V_A), F32), jax.ShapeDtypeStruct((2, NH_A, n, 1), F32)],
        scratch_shapes=[pltpu.VMEM((DK_A, DV_A), F32), pltpu.VMEM((1, DK_A), F32)],
        compiler_params=pltpu.CompilerParams(dimension_semantics=("parallel", "arbitrary"),
                                             vmem_limit_bytes=VMEM_LIMIT),
        name="mlstm_bwd")(q, k, v, gc, gr, s0_all, n0_all, m0_all, dh)


@jax.custom_vjp
def _mlstm_seq(q, k, v, li, lf):
    return _mlstm_seq_fwd(q, k, v, li, lf)[0]


def _mlstm_seq_fwd(q, k, v, li, lf):
    n = q.shape[0]
    q2, k2, v2 = q.reshape(n, QK_A), k.reshape(n, QK_A), v.reshape(n, V_A)
    g = jnp.stack([li, lf]).transpose(0, 2, 1)
    gc, gr = g[:, :, :, None], g[:, :, None, :]
    h, s0, n0, m0 = _mlstm_fwd_call(q2, k2, v2, gc, gr)
    return h.reshape(n, NH_A, DV_A), (q2, k2, v2, gc, gr, s0, n0, m0)


def _mlstm_seq_bwd(res, dh):
    q2, k2, v2, gc, gr, s0, n0, m0 = res
    n = q2.shape[0]
    dq, dk, dv, dg = _mlstm_bwd_call(q2, k2, v2, gc, gr, s0, n0, m0, dh.reshape(n, V_A))
    dg = dg[:, :, :, 0].transpose(0, 2, 1)
    return (dq.reshape(n, NH_A, DK_A), dk.reshape(n, NH_A, DK_A), dv.reshape(n, NH_A, DV_A), dg[0], dg[1])


_mlstm_seq.defvjp(_mlstm_seq_fwd, _mlstm_seq_bwd)


def _silu(x):
    return x * jax.nn.sigmoid(x)


def _ln_plain(x):
    mu = jnp.mean(x, -1, keepdims=True)
    var = jnp.mean(jnp.square(x - mu), -1, keepdims=True)
    return (x - mu) * lax.rsqrt(var + EPS)


def _rms(x, w):
    return x * lax.rsqrt(jnp.mean(jnp.square(x), -1, keepdims=True) + EPS) * w


def _rope_tables(t):
    pos = jnp.arange(t)
    row = (pos // GRID_W).astype(F32)
    col = (pos % GRID_W).astype(F32)
    inv = ROPE_THETA ** (-jnp.arange(0, ROT_HALF, 2, dtype=F32) / ROT_HALF)
    ar, ac = row[:, None] * inv[None], col[:, None] * inv[None]
    cos = jnp.concatenate([jnp.cos(ar), jnp.cos(ar), jnp.cos(ac), jnp.cos(ac)], -1)
    sin = jnp.concatenate([-jnp.sin(ar), jnp.sin(ar), -jnp.sin(ac), jnp.sin(ac)], -1)
    return cos, sin


def _rope(x, cos, sin):
    sw = jnp.concatenate([x[..., 32:64], x[..., 0:32], x[..., 96:128], x[..., 64:96]], axis=-1)
    return x * cos[:, None, :] + sw * sin[:, None, :]


_mm_mod, _mm_main, _mm_if, _mm_ba, _mm_bb, _mm_out = (_make_mm(t) for t in ("mod", "main", "if", "ba", "bb", "out"))


def _local_loss(diff, const):
    x, c, ctx, target = diff["x"], const["c"], const["ctx"], const["target"]
    t, tc = x.shape[0], ctx.shape[0]
    n, r = tc + t, t + 2 * tc

    sc = jnp.concatenate([_silu(c), _silu(diff["c_ctx"])[None], jnp.zeros((14, D_MODEL), F32)], 0)
    mod = _mm_mod(sc, const["w_mod"], diff["p_mod"])[:2] + diff["b_mod"]
    shift, scale, gate = mod[0, :D_MODEL], mod[0, D_MODEL:2 * D_MODEL], mod[0, 2 * D_MODEL:]
    shift_c, scale_c = mod[1, :D_MODEL], mod[1, D_MODEL:2 * D_MODEL]
    u = _ln_plain(x) * (1 + scale) + shift
    u_c = _ln_plain(ctx) * (1 + scale_c) + shift_c
    u_all = jnp.concatenate([u_c, u, u_c], 0)

    p = _mm_main(u_all, const["w_main"], diff["p_main"])
    gt = _mm_if(u_all, const["w_if"], diff["p_if"])[:, :N_IF] + diff["b_if"]

    qk_pre = p[:, O_QK:O_QK + 2 * QK_A]
    rows = jnp.arange(r)
    seg_start = ((rows == 0) | (rows == tc) | (rows == n))[:, None]
    seg_end = ((rows == tc - 1) | (rows == n - 1) | (rows == r - 1))[:, None]
    prev = jnp.where(seg_start, 0.0, jnp.roll(qk_pre, 1, axis=0))
    nxt = jnp.where(seg_end, 0.0, jnp.roll(qk_pre, -1, axis=0))
    cw, cb = diff["conv_w"], diff["conv_b"]
    qk = _silu(cb + prev * cw[0] + qk_pre * cw[1] + nxt * cw[2])
    q_a = qk[:, :QK_A].reshape(r, NH_A, DK_A)
    k_a = qk[:, QK_A:].reshape(r, NH_A, DK_A) * (DK_A ** -0.5)
    v_a = p[:, O_VA:O_VA + V_A].reshape(r, NH_A, DV_A)
    li_f, lf_f = gt[:, 0:8], jax.nn.log_sigmoid(gt[:, 8:16])
    li_b, lf_b = gt[:, 16:24], jax.nn.log_sigmoid(gt[:, 24:32])

    h_f = _mlstm_seq(q_a[:n], k_a[:n], v_a[:n], li_f[:n], lf_f[:n])[tc:]
    fl = lambda a: jnp.flip(a[tc:], axis=0)
    h_b = jnp.flip(_mlstm_seq(fl(q_a), fl(k_a), fl(v_a), fl(li_b), fl(lf_b)), axis=0)[:t]
    h_l = h_f + h_b

    cos, sin = _rope_tables(t)
    lat = slice(tc, n)
    q_l = _rope(_rms(p[lat, O_QB:O_QB + Q_B].reshape(t, NH_B, HD_B), diff["q_norm_w"]), cos, sin)
    cos_k = jnp.concatenate([jnp.ones((tc, HD_B), F32), cos], 0)
    sin_k = jnp.concatenate([jnp.zeros((tc, HD_B), F32), sin], 0)
    k_all = _rope(_rms(p[:n, O_KB:O_KB + KV_B].reshape(n, NKV_B, HD_B), diff["k_norm_w"]), cos_k, sin_k)
    v_all = p[:n, O_VB:O_VB + KV_B]
    o_attn = _attention(q_l.reshape(t, Q_B), k_all.reshape(n, KV_B), v_all)

    hn = _rms(h_l, diff["mh_norm_w"].reshape(NH_A, DV_A)).reshape(t, V_A)
    a_in = jax.nn.sigmoid(p[lat, O_OA:O_OA + V_A]) * hn * _silu(p[lat, O_ZA:O_ZA + V_A])
    y_a = _mm_ba(a_in, const["w_ba"], diff["p_ba"])
    b_in = o_attn * _silu(p[lat, O_ZB:O_ZB + Q_B])
    y_b = _mm_bb(b_in, const["w_bb"], diff["p_bb"])
    m_in = jax.nn.sigmoid(p[lat, O_GA:O_GA + D_MODEL]) * y_a + jax.nn.sigmoid(p[lat, O_GB:O_GB + D_MODEL]) * y_b
    out = _mm_out(m_in, const["w_out"], diff["p_out"])

    y = _ln_plain(ALPHA * x + gate * out) * diff["ln_w"] + diff["ln_b"]
    return 0.5 * jnp.sum(jnp.mean(jnp.square(y - target), axis=-1))


OTHER_CHIPS = [(1, 0), (0, 1), (1, 1)]


def _flip(v, bit):
    return 1 - v if bit else v


def _gather_chips(shard, name):
    def body(x_ref, o_ref, send_sems, recv_sems, local_sem):
        x, y, c = lax.axis_index("x"), lax.axis_index("y"), lax.axis_index("c")
        mine = pltpu.make_async_copy(x_ref, o_ref.at[2 * x + y], local_sem)
        mine.start()

        def copy(r, slot):
            dx, dy = OTHER_CHIPS[r]
            return pltpu.make_async_remote_copy(
                src_ref=x_ref, dst_ref=o_ref.at[slot], send_sem=send_sems.at[r], recv_sem=recv_sems.at[r],
                device_id=(_flip(x, dx), _flip(y, dy), c), device_id_type=MESH)

        sends = [copy(r, 2 * x + y) for r in range(3)]
        for cp in sends:
            cp.start()
        for r, (dx, dy) in enumerate(OTHER_CHIPS):
            copy(r, 2 * _flip(x, dx) + _flip(y, dy)).wait_recv()
        for cp in sends:
            cp.wait_send()
        mine.wait()

    return pl.pallas_call(
        body, out_shape=jax.ShapeDtypeStruct((N_CHIPS,) + shard.shape, shard.dtype),
        in_specs=[pl.BlockSpec(memory_space=pl.ANY)], out_specs=pl.BlockSpec(memory_space=pl.ANY),
        scratch_shapes=[pltpu.SemaphoreType.DMA((3,)), pltpu.SemaphoreType.DMA((3,)), pltpu.SemaphoreType.DMA],
        name=name)(shard)


def _gather_chips_halves(shard, name):
    rows, cols = shard.shape
    halves = shard.reshape(2, rows // 2, cols)

    def body(x_ref, o_ref, send_sems, recv_sems, local_sem):
        x, y, c = lax.axis_index("x"), lax.axis_index("y"), lax.axis_index("c")
        my_chip = 2 * x + y
        mine = pltpu.make_async_copy(x_ref, o_ref.at[my_chip], local_sem)
        mine.start()

        def chip_of(r):
            dx, dy = OTHER_CHIPS[r]
            return _flip(x, dx), _flip(y, dy)

        def copy(k, chip_slot, half, to, src=None):
            dst = o_ref.at[chip_slot, half]
            return pltpu.make_async_remote_copy(
                src_ref=dst if src is None else src, dst_ref=dst, send_sem=send_sems.at[k],
                recv_sem=recv_sems.at[k], device_id=to, device_id_type=MESH)

        first = [copy(r, my_chip, c, (*chip_of(r), c), src=x_ref.at[c]) for r in range(3)]
        for cp in first:
            cp.start()
        passed = []
        for r in range(3):
            px, py = chip_of(r)
            copy(r, 2 * px + py, c, (px, py, c)).wait_recv()
            passed.append(copy(3 + r, 2 * px + py, c, (x, y, 1 - c)))
            passed[-1].start()
        for r in range(3):
            px, py = chip_of(r)
            copy(3 + r, 2 * px + py, 1 - c, (x, y, 1 - c)).wait_recv()
        for cp in first + passed:
            cp.wait_send()
        mine.wait()

    out = pl.pallas_call(
        body, out_shape=jax.ShapeDtypeStruct((N_CHIPS, 2, rows // 2, cols), shard.dtype),
        in_specs=[pl.BlockSpec(memory_space=pl.ANY)], out_specs=pl.BlockSpec(memory_space=pl.ANY),
        scratch_shapes=[pltpu.SemaphoreType.DMA((6,)), pltpu.SemaphoreType.DMA((6,)), pltpu.SemaphoreType.DMA],
        name=name)(halves)
    return out.reshape(N_CHIPS, rows, cols)


def _scatter_grads(slots, name):
    def body(g_ref, o_ref, send_sems, recv_sems, local_sem):
        x, y, c = lax.axis_index("x"), lax.axis_index("y"), lax.axis_index("c")
        me, my_chip, sibling = 4 * x + 2 * y + c, 2 * x + y, (x, y, 1 - c)
        mine = pltpu.make_async_copy(g_ref.at[my_chip], o_ref.at[me], local_sem)
        mine.start()

        def chip_of(r):
            dx, dy = OTHER_CHIPS[r]
            return _flip(x, dx), _flip(y, dy)

        def copy(k, slot, to, src=None):
            dst = o_ref.at[slot]
            return pltpu.make_async_remote_copy(
                src_ref=dst if src is None else src, dst_ref=dst, send_sem=send_sems.at[k],
                recv_sem=recv_sems.at[k], device_id=to, device_id_type=MESH)

        first = [copy(0, me, sibling, src=g_ref.at[my_chip])]
        for r in range(3):
            px, py = chip_of(r)
            first.append(copy(1 + r, me, (px, py, c), src=g_ref.at[2 * px + py]))
        for cp in first:
            cp.start()
        passed = []
        for r in range(3):
            px, py = chip_of(r)
            copy(1 + r, 4 * px + 2 * py + c, (px, py, c)).wait_recv()
            passed.append(copy(4 + r, 4 * px + 2 * py + c, sibling))
            passed[-1].start()
        copy(0, 4 * x + 2 * y + 1 - c, sibling).wait_recv()
        for r in range(3):
            px, py = chip_of(r)
            copy(4 + r, 4 * px + 2 * py + 1 - c, sibling).wait_recv()
        for cp in first + passed:
            cp.wait_send()
        mine.wait()

    return pl.pallas_call(
        body, out_shape=jax.ShapeDtypeStruct((N_DEV,) + slots.shape[1:], slots.dtype),
        in_specs=[pl.BlockSpec(memory_space=pl.ANY)], out_specs=pl.BlockSpec(memory_space=pl.ANY),
        scratch_shapes=[pltpu.SemaphoreType.DMA((N_DEV - 1,)), pltpu.SemaphoreType.DMA((N_DEV - 1,)),
                        pltpu.SemaphoreType.DMA],
        name=name)(slots)


def _allreduce_small(v, name):
    def body(v_ref, o_ref, buf, send_sems, recv_sems):
        x, y, c = lax.axis_index("x"), lax.axis_index("y"), lax.axis_index("c")
        me = 4 * x + 2 * y + c
        buf[me] = v_ref[...]

        def peer(r):
            return _flip(x, (r >> 2) & 1), _flip(y, (r >> 1) & 1), _flip(c, r & 1)

        def copy(r, dst_slot):
            return pltpu.make_async_remote_copy(
                src_ref=v_ref, dst_ref=buf.at[dst_slot], send_sem=send_sems.at[r - 1],
                recv_sem=recv_sems.at[r - 1], device_id=peer(r), device_id_type=MESH)

        sends = [copy(r, me) for r in range(1, N_DEV)]
        for cp in sends:
            cp.start()
        for r in range(1, N_DEV):
            px, py, pc = peer(r)
            copy(r, 4 * px + 2 * py + pc).wait_recv()
        for cp in sends:
            cp.wait_send()
        acc = buf[0]
        for d in range(1, N_DEV):
            acc = acc + buf[d]
        o_ref[...] = acc

    return pl.pallas_call(
        body, out_shape=jax.ShapeDtypeStruct(v.shape, v.dtype),
        in_specs=[pl.BlockSpec(memory_space=pltpu.VMEM)], out_specs=pl.BlockSpec(memory_space=pltpu.VMEM),
        scratch_shapes=[pltpu.VMEM((N_DEV,) + v.shape, v.dtype), pltpu.SemaphoreType.DMA((N_DEV - 1,)),
                        pltpu.SemaphoreType.DMA((N_DEV - 1,))],
        name=name)(v)


def _adamw_math(w, g, m, v):
    m = ADAM_B1 * m + (1.0 - ADAM_B1) * g
    v = ADAM_B2 * v + (1.0 - ADAM_B2) * jnp.square(g)
    m_hat = m / (1.0 - ADAM_B1 ** ADAM_STEP)
    v_hat = v / (1.0 - ADAM_B2 ** ADAM_STEP)
    delta = -ADAM_LR * (m_hat / (jnp.sqrt(v_hat) + ADAM_EPS) + ADAM_WD * w)
    return delta, m, v


def _adamw_sum(parts, w, m, v, name):
    npart, rows, cols = parts.shape
    tr = _pick(rows, (64, 32, 16, 8)) if rows >= 8 else rows

    def body(p_ref, w_ref, m_ref, v_ref, g_out, d_out, m_out, v_out):
        g = p_ref[0].astype(F32)
        for k in range(1, npart):
            g = g + p_ref[k].astype(F32)
        d, m2, v2 = _adamw_math(w_ref[...], g, m_ref[...], v_ref[...])
        g_out[...] = g
        d_out[...] = d
        m_out[...] = m2
        v_out[...] = v2

    spec = pl.BlockSpec((tr, cols), lambda i: (i, 0))
    shp = jax.ShapeDtypeStruct((rows, cols), F32)
    return pl.pallas_call(
        body, grid=(rows // tr,),
        in_specs=[pl.BlockSpec((npart, tr, cols), lambda i: (0, i, 0)), spec, spec, spec],
        out_specs=[spec, spec, spec, spec], out_shape=[shp, shp, shp, shp],
        compiler_params=pltpu.CompilerParams(dimension_semantics=("parallel",), vmem_limit_bytes=VMEM_LIMIT),
        name=name)(parts, w, m, v)


SMALL_ROWS = 16


def _pack_small(c_ctx, b_mod, conv_b, mh, ln_w, ln_b, conv_w_rows, b_if, qn, kn):
    last = jnp.concatenate([b_if.reshape(-1), qn.reshape(-1), kn.reshape(-1),
                            jnp.zeros((D_MODEL - N_IF - 2 * HD_B,), F32)])
    rows = [c_ctx.reshape(1, D_MODEL), b_mod.reshape(3, D_MODEL), conv_b.reshape(1, D_MODEL),
            mh.reshape(1, D_MODEL), ln_w.reshape(1, D_MODEL), ln_b.reshape(1, D_MODEL),
            conv_w_rows.reshape(3, D_MODEL), last[None], jnp.zeros((SMALL_ROWS - 12, D_MODEL), F32)]
    return jnp.concatenate(rows, 0)


def _unpack_small(pk, conv_cols):
    return dict(c_ctx=pk[0], b_mod=pk[1:4].reshape(1, 3 * D_MODEL), conv_b=pk[4:5], mh_norm_w=pk[5:6],
                ln_w=pk[6:7], ln_b=pk[7:8], conv_w=pk[8:11, :conv_cols][None], b_if=pk[11:12, :N_IF],
                q_norm_w=pk[11:12, N_IF:N_IF + HD_B], k_norm_w=pk[11:12, N_IF + HD_B:N_IF + 2 * HD_B])


def kernel(x, c, ctx, c_ctx, w_mod, b_mod, w_in, b_if, conv_w, conv_b, mh_norm_w, q_norm_w, k_norm_w, w_branch_a, w_branch_b, w_out, ln_w, ln_b, loss_target, m_c_ctx, m_w_mod, m_b_mod, m_w_in, m_b_if, m_conv_w, m_conv_b, m_mh_norm_w, m_q_norm_w, m_k_norm_w, m_w_branch_a, m_w_branch_b, m_w_out, m_ln_w, m_ln_b, v_c_ctx, v_w_mod, v_b_mod, v_w_in, v_b_if, v_conv_w, v_conv_b, v_mh_norm_w, v_q_norm_w, v_k_norm_w, v_w_branch_a, v_w_branch_b, v_w_out, v_ln_w, v_ln_b):
    chip = 2 * lax.axis_index("x") + lax.axis_index("y")
    mod_cols, in_cols, conv_cols = w_mod.shape[2], w_in.shape[2], conv_w.shape[2]
    br_rows = w_out.shape[1]

    g_mod = _gather_chips_halves(w_mod[0].astype(BF16), "gather_w_mod")
    g_in = _gather_chips_halves(w_in[0].astype(BF16), "gather_w_in")
    g_ba = _gather_chips_halves(w_branch_a[0].astype(BF16), "gather_w_ba")
    g_bb = _gather_chips_halves(w_branch_b[0].astype(BF16), "gather_w_bb")
    g_out = _gather_chips_halves(w_out[0].astype(BF16), "gather_w_out")
    g_conv = _gather_chips(conv_w[0], "gather_conv_w")
    w_mod_full = jnp.moveaxis(g_mod, 0, 1).reshape(D_MODEL, N_CHIPS * mod_cols)
    w_in_full = jnp.moveaxis(g_in, 0, 1).reshape(D_MODEL, N_CHIPS * in_cols)
    w_main = jnp.concatenate([w_in_full[:, :IF_START], w_in_full[:, IF_START + N_IF:]], 1)
    w_if = jnp.pad(w_in_full[:, IF_START:IF_START + N_IF], ((0, 0), (0, IF_PAD - N_IF)))
    conv_w_full = jnp.moveaxis(g_conv, 0, 1).reshape(3, N_CHIPS * conv_cols)

    const = dict(c=c, ctx=ctx[0], target=loss_target[0], w_mod=w_mod_full, w_main=w_main, w_if=w_if,
                 w_ba=g_ba.reshape(D_MODEL, D_MODEL), w_bb=g_bb.reshape(D_MODEL, D_MODEL),
                 w_out=g_out.reshape(D_MODEL, D_MODEL))
    diff = dict(x=x[0], c_ctx=c_ctx, b_mod=b_mod[0], b_if=b_if[0], conv_w=conv_w_full, conv_b=conv_b[0],
                mh_norm_w=mh_norm_w[0], q_norm_w=q_norm_w[0], k_norm_w=k_norm_w[0], ln_w=ln_w[0], ln_b=ln_b[0],
                p_mod=jnp.zeros(w_mod_full.shape, F32), p_main=jnp.zeros(w_main.shape, F32),
                p_if=jnp.zeros(w_if.shape, F32), p_ba=jnp.zeros((D_MODEL, D_MODEL), F32),
                p_bb=jnp.zeros((D_MODEL, D_MODEL), F32), p_out=jnp.zeros((D_MODEL, D_MODEL), F32))
    loss_local, g = jax.value_and_grad(_local_loss)(diff, const)
    loss = lax.psum(loss_local, ("x", "y", "c"))

    g_small = _allreduce_small(
        _pack_small(g["c_ctx"], g["b_mod"], g["conv_b"], g["mh_norm_w"], g["ln_w"], g["ln_b"], g["conv_w"],
                    g["b_if"], g["q_norm_w"], g["k_norm_w"]), "allreduce_small")
    conv_g = lax.dynamic_slice(g_small[8:11], (0, chip * conv_cols), (3, conv_cols))
    g_small = g_small.at[8:11].set(jnp.pad(conv_g, ((0, 0), (0, D_MODEL - conv_cols))))
    pad_conv = lambda a: jnp.pad(a[0], ((0, 0), (0, D_MODEL - conv_cols)))
    packed = [_pack_small(cc, bm[0], cb[0], mh[0], lw[0], lb[0], pad_conv(cw), bi[0], qn[0], kn[0])
              for cc, bm, cb, mh, lw, lb, cw, bi, qn, kn in (
                  (c_ctx, b_mod, conv_b, mh_norm_w, ln_w, ln_b, conv_w, b_if, q_norm_w, k_norm_w),
                  (m_c_ctx, m_b_mod, m_conv_b, m_mh_norm_w, m_ln_w, m_ln_b, m_conv_w, m_b_if, m_q_norm_w, m_k_norm_w),
                  (v_c_ctx, v_b_mod, v_conv_b, v_mh_norm_w, v_ln_w, v_ln_b, v_conv_w, v_b_if, v_q_norm_w, v_k_norm_w))]
    small = [_unpack_small(a, conv_cols)
             for a in _adamw_sum(g_small[None], packed[0], packed[1], packed[2], "adamw_small")]

    def col_slots(gfull, cols):
        return jnp.moveaxis(gfull.reshape(D_MODEL, N_CHIPS, cols), 1, 0).astype(BF16)

    g_in_full = jnp.concatenate([g["p_main"][:, :IF_START], g["p_if"][:, :N_IF], g["p_main"][:, IF_START:]], 1)
    big = {}
    for nm, slots, w_, m_, v_ in (
            ("w_mod", col_slots(g["p_mod"], mod_cols), w_mod, m_w_mod, v_w_mod),
            ("w_in", col_slots(g_in_full, in_cols), w_in, m_w_in, v_w_in),
            ("w_branch_a", g["p_ba"].reshape(N_CHIPS, br_rows, D_MODEL).astype(BF16), w_branch_a, m_w_branch_a, v_w_branch_a),
            ("w_branch_b", g["p_bb"].reshape(N_CHIPS, br_rows, D_MODEL).astype(BF16), w_branch_b, m_w_branch_b, v_w_branch_b),
            ("w_out", g["p_out"].reshape(N_CHIPS, br_rows, D_MODEL).astype(BF16), w_out, m_w_out, v_w_out)):
        parts = _scatter_grads(slots, "scatter_" + nm)
        big[nm] = [a[None] for a in _adamw_sum(parts, w_[0], m_[0], v_[0], "adamw_" + nm)]

    names = ["c_ctx", "w_mod", "b_mod", "w_in", "b_if", "conv_w", "conv_b", "mh_norm_w", "q_norm_w", "k_norm_w",
             "w_branch_a", "w_branch_b", "w_out", "ln_w", "ln_b"]
    outs = [[big[nm][k] if nm in big else small[k][nm] for nm in names] for k in range(4)]
    return (loss, g["x"][None], *outs[0], *outs[1], *outs[2], *outs[3])
```

```python
import functools

import jax
import jax.numpy as jnp
from jax import lax
from jax.experimental import pallas as pl
from jax.experimental.pallas import tpu as pltpu

F32 = jnp.float32
BF16 = jnp.bfloat16
MESH = pl.DeviceIdType.MESH

D_MODEL = 2048
NH_A, DK_A, DV_A = 8, 128, 256
QK_A, V_A = NH_A * DK_A, NH_A * DV_A
NH_B, NKV_B, HD_B = 16, 4, 128
Q_B, KV_B = NH_B * HD_B, NKV_B * HD_B
GRID_W = 64
ROT_HALF = HD_B // 2
ROPE_THETA = 10000.0
M_INIT = -1e30
EPS = 1e-6
ALPHA = 2.0 ** 0.25
N_IN = 17440
IF_START, N_IF, IF_PAD = 4096, 32, 128
N_MAIN = N_IN - N_IF
O_QK, O_VA, O_KB, O_VB, O_OA, O_ZA, O_QB, O_ZB, O_GA, O_GB = (
    0, 2048, 4096, 4608, 5120, 7168, 9216, 11264, 13312, 15360)
MLSTM_CHUNK = 256

ADAM_LR, ADAM_B1, ADAM_B2, ADAM_EPS, ADAM_WD, ADAM_STEP = 0.001, 0.9, 0.999, 1e-08, 0.01, 10

VMEM_LIMIT = 48 * 1024 * 1024
N_CHIPS, N_DEV = 4, 8
MX = BF16


def _pick(n, cands):
    for c in cands:
        if n % c == 0:
            return c
    raise ValueError(f"no tile for {n} in {cands}")


def _dot(a, b):
    return jnp.dot(a, b, preferred_element_type=F32)


def _dot_nt(a, b):
    return lax.dot_general(a, b, (((1,), (1,)), ((), ())), preferred_element_type=F32)


def _dot_tn(a, b):
    return lax.dot_general(a, b, (((0,), (0,)), ((), ())), preferred_element_type=F32)


def _mm_nn(a, b, name):
    m, k = a.shape
    _, n = b.shape
    tm = _pick(m, (512, 256, 128, 64, 32, 16))
    tn = _pick(n, (1024, 512, 256, 128))

    def body(a_ref, b_ref, o_ref):
        o_ref[...] = _dot(a_ref[...], b_ref[...])

    return pl.pallas_call(
        body, grid=(m // tm, n // tn),
        in_specs=[pl.BlockSpec((tm, k), lambda i, j: (i, 0)), pl.BlockSpec((k, tn), lambda i, j: (0, j))],
        out_specs=pl.BlockSpec((tm, tn), lambda i, j: (i, j)),
        out_shape=jax.ShapeDtypeStruct((m, n), F32),
        compiler_params=pltpu.CompilerParams(dimension_semantics=("parallel", "parallel"),
                                             vmem_limit_bytes=VMEM_LIMIT),
        name=name)(a, b)


def _mm_nt(g, w, name):
    m, n = g.shape
    k, _ = w.shape
    tm = _pick(m, (512, 256, 128, 64, 32, 16))
    tn = _pick(n, (1024, 512, 256, 128))

    def body(g_ref, w_ref, o_ref):
        part = _dot_nt(g_ref[...], w_ref[...])

        @pl.when(pl.program_id(1) == 0)
        def _():
            o_ref[...] = part

        @pl.when(pl.program_id(1) > 0)
        def _():
            o_ref[...] += part

    return pl.pallas_call(
        body, grid=(m // tm, n // tn),
        in_specs=[pl.BlockSpec((tm, tn), lambda i, j: (i, j)), pl.BlockSpec((k, tn), lambda i, j: (0, j))],
        out_specs=pl.BlockSpec((tm, k), lambda i, j: (i, 0)),
        out_shape=jax.ShapeDtypeStruct((m, k), F32),
        compiler_params=pltpu.CompilerParams(dimension_semantics=("parallel", "arbitrary"),
                                             vmem_limit_bytes=VMEM_LIMIT),
        name=name)(g, w)


def _mm_tn(a, g, name):
    m, k = a.shape
    _, n = g.shape
    tm = _pick(m, (512, 256, 128, 64, 32, 16))
    tn = _pick(n, (1024, 512, 256, 128))

    def body(a_ref, g_ref, o_ref):
        part = _dot_tn(a_ref[...], g_ref[...])

        @pl.when(pl.program_id(1) == 0)
        def _():
            o_ref[...] = part

        @pl.when(pl.program_id(1) > 0)
        def _():
            o_ref[...] += part

    return pl.pallas_call(
        body, grid=(n // tn, m // tm),
        in_specs=[pl.BlockSpec((tm, k), lambda j, i: (i, 0)), pl.BlockSpec((tm, tn), lambda j, i: (i, j))],
        out_specs=pl.BlockSpec((k, tn), lambda j, i: (0, j)),
        out_shape=jax.ShapeDtypeStruct((k, n), F32),
        compiler_params=pltpu.CompilerParams(dimension_semantics=("parallel", "arbitrary"),
                                             vmem_limit_bytes=VMEM_LIMIT),
        name=name)(a, g)


def _make_mm(tag):
    @jax.custom_vjp
    def mm(a, w, proxy):
        del proxy
        return _mm_nn(a.astype(BF16), w, f"mm_{tag}_fwd")

    def fwd(a, w, proxy):
        del proxy
        ab = a.astype(BF16)
        return _mm_nn(ab, w, f"mm_{tag}_fwd"), (ab, w)

    def bwd(res, g):
        ab, w = res
        gb = g.astype(BF16)
        da = _mm_nt(gb, w, f"mm_{tag}_da")
        dw = _mm_tn(ab, gb, f"mm_{tag}_dw")
        return da, jnp.zeros_like(w), dw

    mm.defvjp(fwd, bwd)
    return mm


ATT_SCALE = HD_B ** -0.5
GROUP = NH_B // NKV_B


STRIP_Q, STRIP_K = 128, 256


def _attn_tiles(t, n):
    return _pick(t, (512, 256, 128)), _pick(n, (768, 512, 256))


def _attn_fwd(q, k, v):
    t, n = q.shape[0], k.shape[0]
    tq, tk = _attn_tiles(t, n)
    nk = n // tk

    def body(q_ref, k_ref, v_ref, o_ref, lse_ref, m_sc, l_sc, acc_sc):
        j = pl.program_id(2)

        @pl.when(j == 0)
        def _():
            m_sc[...] = jnp.full(m_sc.shape, -jnp.inf, F32)
            l_sc[...] = jnp.zeros(l_sc.shape, F32)
            acc_sc[...] = jnp.zeros(acc_sc.shape, F32)

        kb, vb = k_ref[...], v_ref[...]
        for g in range(GROUP):
            cols = slice(g * HD_B, (g + 1) * HD_B)
            s = _dot_nt(q_ref[:, cols], kb) * ATT_SCALE
            m_prev = m_sc[g]
            m_new = jnp.maximum(m_prev, jnp.max(s, axis=-1, keepdims=True))
            alpha = jnp.exp(m_prev - m_new)
            p = jnp.exp(s - m_new)
            l_sc[g] = alpha * l_sc[g] + jnp.sum(p, axis=-1, keepdims=True)
            acc_sc[:, cols] = alpha * acc_sc[:, cols] + _dot(p.astype(BF16), vb)
            m_sc[g] = m_new

        @pl.when(j == nk - 1)
        def _():
            for g in range(GROUP):
                cols = slice(g * HD_B, (g + 1) * HD_B)
                o_ref[:, cols] = acc_sc[:, cols] / l_sc[g]
                lse_ref[:, cols] = jnp.broadcast_to(m_sc[g] + jnp.log(l_sc[g]), (tq, HD_B))

    qspec = pl.BlockSpec((tq, GROUP * HD_B), lambda kh, i, j: (i, kh))
    kspec = pl.BlockSpec((tk, HD_B), lambda kh, i, j: (j, kh))
    return pl.pallas_call(
        body, grid=(NKV_B, t // tq, nk),
        in_specs=[qspec, kspec, kspec], out_specs=[qspec, qspec],
        out_shape=[jax.ShapeDtypeStruct((t, Q_B), F32), jax.ShapeDtypeStruct((t, Q_B), F32)],
        scratch_shapes=[pltpu.VMEM((GROUP, tq, 1), F32), pltpu.VMEM((GROUP, tq, 1), F32),
                        pltpu.VMEM((tq, GROUP * HD_B), F32)],
        compiler_params=pltpu.CompilerParams(dimension_semantics=("parallel", "parallel", "arbitrary"),
                                             vmem_limit_bytes=VMEM_LIMIT),
        name="attn_fwd")(q, k, v)


def _attn_dq(q, k, v, do, lse, delta):
    t, n = q.shape[0], k.shape[0]
    tq, tk = _attn_tiles(t, n)

    def body(q_ref, k_ref, v_ref, do_ref, lse_ref, dl_ref, dq_ref):
        j = pl.program_id(2)
        kb, vb = k_ref[...], v_ref[...]
        parts = []
        for g in range(GROUP):
            cols = slice(g * HD_B, (g + 1) * HD_B)
            s = _dot_nt(q_ref[:, cols], kb) * ATT_SCALE
            p = jnp.exp(s - lse_ref[:, g * HD_B:g * HD_B + 1])
            dp = _dot_nt(do_ref[:, cols], vb)
            ds = p * (dp - dl_ref[:, g * HD_B:g * HD_B + 1]) * ATT_SCALE
            parts.append(_dot(ds.astype(BF16), kb))

        @pl.when(j == 0)
        def _():
            for g in range(GROUP):
                dq_ref[:, g * HD_B:(g + 1) * HD_B] = parts[g]

        @pl.when(j > 0)
        def _():
            for g in range(GROUP):
                dq_ref[:, g * HD_B:(g + 1) * HD_B] += parts[g]

    qspec = pl.BlockSpec((tq, GROUP * HD_B), lambda kh, i, j: (i, kh))
    kspec = pl.BlockSpec((tk, HD_B), lambda kh, i, j: (j, kh))
    return pl.pallas_call(
        body, grid=(NKV_B, t // tq, n // tk),
        in_specs=[qspec, kspec, kspec, qspec, qspec, qspec],
        out_specs=qspec,
        out_shape=jax.ShapeDtypeStruct((t, Q_B), F32),
        compiler_params=pltpu.CompilerParams(dimension_semantics=("parallel", "parallel", "arbitrary"),
                                             vmem_limit_bytes=VMEM_LIMIT),
        name="attn_dq")(q, k, v, do, lse, delta)


def _attn_dkv(q, k, v, do, lse_t, delta_t):
    t, n = q.shape[0], k.shape[0]
    tq, tk = _attn_tiles(t, n)
    nq = t // tq
    n_r, n_c = tq // STRIP_Q, tk // STRIP_K

    def body(q_ref, k_ref, v_ref, do_ref, lse_ref, dl_ref, dk_ref, dv_ref, dk_sc, dv_sc):
        i = pl.program_id(2)

        @pl.when(i == 0)
        def _():
            dk_sc[...] = jnp.zeros(dk_sc.shape, F32)
            dv_sc[...] = jnp.zeros(dv_sc.shape, F32)

        for r in range(n_r):
            rows = slice(r * STRIP_Q, (r + 1) * STRIP_Q)
            for c in range(n_c):
                kv = slice(c * STRIP_K, (c + 1) * STRIP_K)
                kc, vc = k_ref[kv, :], v_ref[kv, :]
                dk_part = dv_part = None
                for g in range(GROUP):
                    cols = slice(g * HD_B, (g + 1) * HD_B)
                    qg, dog = q_ref[rows, cols], do_ref[rows, cols]
                    st = _dot_nt(kc, qg)
                    pt = jnp.exp(st * ATT_SCALE - lse_ref[8 * g:8 * g + 1, rows])
                    dvg = _dot(pt.astype(BF16), dog)
                    dpt = _dot_nt(vc, dog)
                    dst = pt * (dpt - dl_ref[8 * g:8 * g + 1, rows])
                    dkg = _dot(dst.astype(BF16), qg)
                    dk_part = dkg if dk_part is None else dk_part + dkg
                    dv_part = dvg if dv_part is None else dv_part + dvg
                dk_sc[kv, :] += dk_part
                dv_sc[kv, :] += dv_part

        @pl.when(i == nq - 1)
        def _():
            dk_ref[...] = dk_sc[...] * ATT_SCALE
            dv_ref[...] = dv_sc[...]

    qspec = pl.BlockSpec((tq, GROUP * HD_B), lambda kh, j, i: (i, kh))
    tspec = pl.BlockSpec((8 * GROUP, tq), lambda kh, j, i: (kh, i))
    kspec = pl.BlockSpec((tk, HD_B), lambda kh, j, i: (j, kh))
    return pl.pallas_call(
        body, grid=(NKV_B, n // tk, nq),
        in_specs=[qspec, kspec, kspec, qspec, tspec, tspec],
        out_specs=[kspec, kspec],
        out_shape=[jax.ShapeDtypeStruct((n, KV_B), F32), jax.ShapeDtypeStruct((n, KV_B), F32)],
        scratch_shapes=[pltpu.VMEM((tk, HD_B), F32), pltpu.VMEM((tk, HD_B), F32)],
        compiler_params=pltpu.CompilerParams(dimension_semantics=("parallel", "parallel", "arbitrary"),
                                             vmem_limit_bytes=VMEM_LIMIT),
        name="attn_dkv")(q, k, v, do, lse_t, delta_t)


def _attention_bwd(res, do):
    qb, kb, vb, o, lse = res
    t = qb.shape[0]
    delta = jnp.sum((do * o).reshape(t, NH_B, HD_B), axis=-1)
    lse_h = lse.reshape(t, NH_B, HD_B)[:, :, 0]
    delta_b = jnp.broadcast_to(delta[:, :, None], (t, NH_B, HD_B)).reshape(t, Q_B)
    lse_t = jnp.broadcast_to(lse_h.T[:, None, :], (NH_B, 8, t)).reshape(NH_B * 8, t)
    delta_t = jnp.broadcast_to(delta.T[:, None, :], (NH_B, 8, t)).reshape(NH_B * 8, t)
    dob = do.astype(BF16)
    dq = _attn_dq(qb, kb, vb, dob, lse, delta_b)
    dk, dv = _attn_dkv(qb, kb, vb, dob, lse_t, delta_t)
    return dq, dk, dv


def _swap32(y):
    lane = lax.broadcasted_iota(jnp.int32, y.shape, 1)
    return jnp.where((lane // 32) % 2 == 0, pltpu.roll(y, 96, 1), pltpu.roll(y, 32, 1))


def _norm_rope_fwd(x, w, cos, sin, name):
    r, width = x.shape
    heads = width // HD_B
    tr = _pick(r, (256, 128))

    def body(x_ref, w_ref, c_ref, s_ref, o_ref):
        w, c, s = w_ref[...], c_ref[...], s_ref[...]
        for h in range(heads):
            cols = slice(h * HD_B, (h + 1) * HD_B)
            xh = x_ref[:, cols]
            y = xh * lax.rsqrt(jnp.mean(xh * xh, axis=-1, keepdims=True) + EPS) * w
            o_ref[:, cols] = (y * c + _swap32(y) * s).astype(o_ref.dtype)

    row = pl.BlockSpec((tr, width), lambda i: (i, 0))
    tab = pl.BlockSpec((tr, HD_B), lambda i: (i, 0))
    return pl.pallas_call(
        body, grid=(r // tr,),
        in_specs=[row, pl.BlockSpec((1, HD_B), lambda i: (0, 0)), tab, tab], out_specs=row,
        out_shape=jax.ShapeDtypeStruct((r, width), BF16),
        compiler_params=pltpu.CompilerParams(dimension_semantics=("parallel",), vmem_limit_bytes=VMEM_LIMIT),
        name=name)(x, w, cos, sin)


def _norm_rope_bwd(x, w, cos, sin, dy, name):
    r, width = x.shape
    heads = width // HD_B
    tr = _pick(r, (256, 128))

    def body(x_ref, w_ref, c_ref, s_ref, dy_ref, dx_ref, dw_ref):
        @pl.when(pl.program_id(0) == 0)
        def _():
            dw_ref[...] = jnp.zeros(dw_ref.shape, F32)

        w, c, s = w_ref[...], c_ref[...], s_ref[...]
        dw = jnp.zeros((1, HD_B), F32)
        for h in range(heads):
            cols = slice(h * HD_B, (h + 1) * HD_B)
            xh, dyh = x_ref[:, cols], dy_ref[:, cols]
            rs = lax.rsqrt(jnp.mean(xh * xh, axis=-1, keepdims=True) + EPS)
            dn = dyh * c + _swap32(dyh * s)
            dw = dw + jnp.sum(dn * (xh * rs), axis=0, keepdims=True)
            dxn = dn * w
            dx_ref[:, cols] = rs * dxn - xh * (rs * rs * rs * jnp.mean(dxn * xh, axis=-1, keepdims=True))
        dw_ref[...] += dw

    row = pl.BlockSpec((tr, width), lambda i: (i, 0))
    tab = pl.BlockSpec((tr, HD_B), lambda i: (i, 0))
    vec = pl.BlockSpec((1, HD_B), lambda i: (0, 0))
    return pl.pallas_call(
        body, grid=(r // tr,),
        in_specs=[row, vec, tab, tab, row], out_specs=[row, vec],
        out_shape=[jax.ShapeDtypeStruct((r, width), F32), jax.ShapeDtypeStruct((1, HD_B), F32)],
        compiler_params=pltpu.CompilerParams(dimension_semantics=("arbitrary",), vmem_limit_bytes=VMEM_LIMIT),
        name=name)(x, w, cos, sin, dy)


def _rope_tables(t):
    pos = jnp.arange(t)
    row = (pos // GRID_W).astype(F32)
    col = (pos % GRID_W).astype(F32)
    inv = ROPE_THETA ** (-jnp.arange(0, ROT_HALF, 2, dtype=F32) / ROT_HALF)
    ar, ac = row[:, None] * inv[None], col[:, None] * inv[None]
    cos = jnp.concatenate([jnp.cos(ar), jnp.cos(ar), jnp.cos(ac), jnp.cos(ac)], -1)
    sin = jnp.concatenate([-jnp.sin(ar), jnp.sin(ar), -jnp.sin(ac), jnp.sin(ac)], -1)
    return cos, sin


def _gqa_tables(t, n):
    cos, sin = _rope_tables(t)
    cos_k = jnp.concatenate([jnp.ones((n - t, HD_B), F32), cos], 0)
    sin_k = jnp.concatenate([jnp.zeros((n - t, HD_B), F32), sin], 0)
    return cos, sin, cos_k, sin_k


@jax.custom_vjp
def _gqa(pq, pk, pv, qw, kw):
    return _gqa_fwd(pq, pk, pv, qw, kw)[0]


def _gqa_fwd(pq, pk, pv, qw, kw):
    cos, sin, cos_k, sin_k = _gqa_tables(pq.shape[0], pk.shape[0])
    q = _norm_rope_fwd(pq, qw[None], cos, sin, "q_norm_rope")
    k = _norm_rope_fwd(pk, kw[None], cos_k, sin_k, "k_norm_rope")
    vb = pv.astype(BF16)
    o, lse = _attn_fwd(q, k, vb)
    return o, (pq, pk, qw, kw, q, k, vb, o, lse)


def _gqa_bwd(res, do):
    pq, pk, qw, kw, q, k, vb, o, lse = res
    cos, sin, cos_k, sin_k = _gqa_tables(pq.shape[0], pk.shape[0])
    dq, dk, dv = _attention_bwd((q, k, vb, o, lse), do)
    dpq, dqw = _norm_rope_bwd(pq, qw[None], cos, sin, dq, "q_norm_rope_bwd")
    dpk, dkw = _norm_rope_bwd(pk, kw[None], cos_k, sin_k, dk, "k_norm_rope_bwd")
    return dpq, dpk, dv, dqw[0], dkw[0]


_gqa.defvjp(_gqa_fwd, _gqa_bwd)


def _mlstm_chunk_forward(q, k, v, lir, lfr, s0, n0, m0, reverse):
    L = q.shape[0]
    ti = lax.broadcasted_iota(jnp.int32, (L, L), 0)
    si = lax.broadcasted_iota(jnp.int32, (L, L), 1)
    seen = (si >= ti) if reverse else (si <= ti)
    seen_t = (ti >= si) if reverse else (ti <= si)
    eye = ti == si
    lic = jnp.sum(jnp.where(eye, lir, 0.0), axis=1, keepdims=True)
    lfc = jnp.sum(jnp.where(eye, lfr, 0.0), axis=1, keepdims=True)
    b_col = jnp.sum(jnp.where(seen, lfr, 0.0), axis=1, keepdims=True)
    b_row = jnp.sum(jnp.where(seen_t, lfc, 0.0), axis=0, keepdims=True)
    d = jnp.where(seen, b_col - b_row + lir, -jnp.inf)
    m = jnp.maximum(b_col + m0, jnp.max(d, axis=1, keepdims=True))
    w = jnp.exp(d - m)
    a = jnp.exp(b_col + m0 - m)
    qm, km, vm = q.astype(MX), k.astype(MX), v.astype(MX)
    s = _dot_nt(qm, km) * w
    qs = _dot(qm, s0.astype(MX))
    num = a * qs + _dot(s.astype(MX), vm)
    qn = jnp.sum(q * n0, axis=1, keepdims=True)
    den = a * qn + jnp.sum(s, axis=1, keepdims=True)
    floor = jnp.exp(-m)
    dd = jnp.maximum(jnp.abs(den), floor)
    b_last = jnp.sum(lfr, axis=1, keepdims=True)
    m_end = jnp.maximum(b_last + m0, jnp.max(b_last - b_row + lir, axis=1, keepdims=True))
    w_end = jnp.exp(b_last - b_col + lic - m_end)
    a_end = jnp.exp(b_last + m0 - m_end)
    return dict(eye=eye, seen=seen, w=w, a=a, s=s, qs=qs, num=num, qn=qn, den=den, floor=floor, dd=dd,
                m_end=m_end, w_end=w_end, a_end=a_end, qm=qm, km=km, vm=vm)


def _mlstm_fwd_call(q, k, v, gr, n, row_off, reverse):
    L = MLSTM_CHUNK
    nc, off = n // L, row_off // L
    pos = (lambda i: nc - 1 - i) if reverse else (lambda i: i)

    def body(q_ref, k_ref, v_ref, gr_ref, h_ref, s0_ref, n0_ref, m0_ref, s_sc, n_sc, m_sc):
        @pl.when(pl.program_id(1) == 0)
        def _():
            s_sc[...] = jnp.zeros(s_sc.shape, F32)
            n_sc[...] = jnp.zeros(n_sc.shape, F32)
            m_sc[...] = jnp.full(m_sc.shape, M_INIT, F32)

        s0, n0, m0 = s_sc[...], n_sc[...], m_sc[...]
        s0_ref[0, 0] = s0
        n0_ref[0, 0] = n0
        m0_ref[0, 0] = jnp.broadcast_to(m0, (1, DK_A))
        k, v = k_ref[...], v_ref[...]
        f = _mlstm_chunk_forward(q_ref[...], k, v, gr_ref[0, 0], gr_ref[1, 0], s0, n0, m0, reverse)
        h_ref[...] = f["num"] / f["dd"]
        s_sc[...] = f["a_end"] * s0 + _dot_tn(f["km"], (f["w_end"] * v).astype(MX))
        n_sc[...] = f["a_end"] * n0 + jnp.sum(f["w_end"] * k, axis=0, keepdims=True)
        m_sc[...] = f["m_end"]

    qk_spec = pl.BlockSpec((L, DK_A), lambda h, i: (off + pos(i), h))
    v_spec = pl.BlockSpec((L, DV_A), lambda h, i: (off + pos(i), h))
    gr_spec = pl.BlockSpec((2, 1, 1, L), lambda h, i: (0, h, 0, off + pos(i)))
    h_spec = pl.BlockSpec((L, DV_A), lambda h, i: (pos(i), h))
    st_spec = pl.BlockSpec((1, 1, DK_A, DV_A), lambda h, i: (h, pos(i), 0, 0))
    vec_spec = pl.BlockSpec((1, 1, 1, DK_A), lambda h, i: (h, pos(i), 0, 0))
    return pl.pallas_call(
        body, grid=(NH_A, nc),
        in_specs=[qk_spec, qk_spec, v_spec, gr_spec],
        out_specs=[h_spec, st_spec, vec_spec, vec_spec],
        out_shape=[jax.ShapeDtypeStruct((n, V_A), F32), jax.ShapeDtypeStruct((NH_A, nc, DK_A, DV_A), F32),
                   jax.ShapeDtypeStruct((NH_A, nc, 1, DK_A), F32), jax.ShapeDtypeStruct((NH_A, nc, 1, DK_A), F32)],
        scratch_shapes=[pltpu.VMEM((DK_A, DV_A), F32), pltpu.VMEM((1, DK_A), F32), pltpu.VMEM((1, 1), F32)],
        compiler_params=pltpu.CompilerParams(dimension_semantics=("parallel", "arbitrary"),
                                             vmem_limit_bytes=VMEM_LIMIT),
        name="mlstm_fwd")(q, k, v, gr)


def _mlstm_bwd_call(q, k, v, gr, s0_all, n0_all, m0_all, dh, n, row_off, reverse):
    L = MLSTM_CHUNK
    nc, off = n // L, row_off // L
    pos = (lambda i: i) if reverse else (lambda i: nc - 1 - i)

    def body(q_ref, k_ref, v_ref, gr_ref, s0_ref, n0_ref, m0_ref, dh_ref,
             dq_ref, dk_ref, dv_ref, dg_ref, ds_sc, dn_sc):
        @pl.when(pl.program_id(1) == 0)
        def _():
            ds_sc[...] = jnp.zeros(ds_sc.shape, F32)
            dn_sc[...] = jnp.zeros(dn_sc.shape, F32)

        q, k, v = q_ref[...], k_ref[...], v_ref[...]
        s0, n0, m0 = s0_ref[0, 0], n0_ref[0, 0], m0_ref[0, 0][:, 0:1]
        f = _mlstm_chunk_forward(q, k, v, gr_ref[0, 0], gr_ref[1, 0], s0, n0, m0, reverse)
        w, a, s = f["w"], f["a"], f["s"]
        qm, km, vm, w_end, a_end = f["qm"], f["km"], f["vm"], f["w_end"], f["a_end"]
        ds1, dn1 = ds_sc[...], dn_sc[...]
        ds1m, s0m = ds1.astype(MX), s0.astype(MX)

        inv = 1.0 / f["dd"]
        dh = dh_ref[...]
        dnum = dh * inv
        ddd = -jnp.sum(dh * (f["num"] * inv), axis=1, keepdims=True) * inv
        dden = jnp.where(jnp.abs(f["den"]) > f["floor"], jnp.sign(f["den"]) * ddd, 0.0)
        adn = (a * dnum).astype(MX)
        dnm = dnum.astype(MX)
        ds_tot = _dot_nt(dnm, vm) + dden
        dsr = (ds_tot * w).astype(MX)
        e = ds_tot * s
        wv = (w_end * v).astype(MX)
        kds = _dot(km, ds1m)
        dq_ref[...] = _dot_nt(adn, s0m) + _dot(dsr, km) + (dden * a) * n0
        dk_ref[...] = _dot_tn(dsr, qm) + _dot_nt(wv, ds1m) + w_end * dn1
        dv_ref[...] = _dot_tn(s.astype(MX), dnm) + w_end * kds

        eye = f["eye"]
        to_col = lambda r: jnp.sum(jnp.where(eye, r, 0.0), axis=1, keepdims=True)
        to_row = lambda c: jnp.sum(jnp.where(eye, c, 0.0), axis=0, keepdims=True)
        g_a = (jnp.sum(dnum * f["qs"], axis=1, keepdims=True) + dden * f["qn"]) * a
        g_w = (jnp.sum(v * kds, axis=1, keepdims=True) + jnp.sum(k * dn1, axis=1, keepdims=True)) * w_end
        g_end = (jnp.sum(jnp.sum(ds1 * s0, axis=1, keepdims=True), axis=0, keepdims=True)
                 + jnp.sum(dn1 * n0, axis=1, keepdims=True)) * a_end
        col_e = jnp.sum(e, axis=0, keepdims=True)
        db = jnp.sum(e, axis=1, keepdims=True) - to_col(col_e) + g_a - g_w
        last = lax.broadcasted_iota(jnp.int32, (L, 1), 0) == (0 if reverse else L - 1)
        db = db + jnp.where(last, jnp.sum(g_w, axis=0, keepdims=True) + g_end, 0.0)
        dg_ref[0, 0] = col_e + to_row(g_w)
        dg_ref[1, 0] = jnp.sum(jnp.where(f["seen"], db, 0.0), axis=0, keepdims=True)

        ds_sc[...] = a_end * ds1 + _dot_tn(qm, adn)
        dn_sc[...] = a_end * dn1 + jnp.sum((dden * a) * q, axis=0, keepdims=True)

    qk_spec = pl.BlockSpec((L, DK_A), lambda h, i: (off + pos(i), h))
    v_spec = pl.BlockSpec((L, DV_A), lambda h, i: (off + pos(i), h))
    gr_spec = pl.BlockSpec((2, 1, 1, L), lambda h, i: (0, h, 0, off + pos(i)))
    st_spec = pl.BlockSpec((1, 1, DK_A, DV_A), lambda h, i: (h, pos(i), 0, 0))
    vec_spec = pl.BlockSpec((1, 1, 1, DK_A), lambda h, i: (h, pos(i), 0, 0))
    oqk_spec = pl.BlockSpec((L, DK_A), lambda h, i: (pos(i), h))
    ov_spec = pl.BlockSpec((L, DV_A), lambda h, i: (pos(i), h))
    og_spec = pl.BlockSpec((2, 1, 1, L), lambda h, i: (0, h, 0, pos(i)))
    return pl.pallas_call(
        body, grid=(NH_A, nc),
        in_specs=[qk_spec, qk_spec, v_spec, gr_spec, st_spec, vec_spec, vec_spec, ov_spec],
        out_specs=[oqk_spec, oqk_spec, ov_spec, og_spec],
        out_shape=[jax.ShapeDtypeStruct((n, QK_A), F32), jax.ShapeDtypeStruct((n, QK_A), F32),
                   jax.ShapeDtypeStruct((n, V_A), F32), jax.ShapeDtypeStruct((2, NH_A, 1, n), F32)],
        scratch_shapes=[pltpu.VMEM((DK_A, DV_A), F32), pltpu.VMEM((1, DK_A), F32)],
        compiler_params=pltpu.CompilerParams(dimension_semantics=("parallel", "arbitrary"),
                                             vmem_limit_bytes=VMEM_LIMIT),
        name="mlstm_bwd")(q, k, v, gr, s0_all, n0_all, m0_all, dh)


def _make_mlstm(n, row_off, reverse):
    def gate_rows(li, lf):
        return jnp.stack([li, lf]).transpose(0, 2, 1)[:, :, None, :]

    @jax.custom_vjp
    def op(q, k, v, li, lf):
        return _mlstm_fwd_call(q, k, v, gate_rows(li, lf), n, row_off, reverse)[0]

    def fwd(q, k, v, li, lf):
        gr = gate_rows(li, lf)
        h, s0, n0, m0 = _mlstm_fwd_call(q, k, v, gr, n, row_off, reverse)
        return h, (q, k, v, gr, s0, n0, m0)

    def bwd(res, dh):
        q, k, v, gr, s0, n0, m0 = res
        dq, dk, dv, dg = _mlstm_bwd_call(q, k, v, gr, s0, n0, m0, dh, n, row_off, reverse)
        rows = ((row_off, q.shape[0] - row_off - n), (0, 0))
        dg = jnp.pad(dg[:, :, 0, :].transpose(0, 2, 1), ((0, 0),) + rows)
        return jnp.pad(dq, rows), jnp.pad(dk, rows), jnp.pad(dv, rows), dg[0], dg[1]

    op.defvjp(fwd, bwd)
    return op


def _silu(x):
    return x * jax.nn.sigmoid(x)


def _ln_plain(x):
    mu = jnp.mean(x, -1, keepdims=True)
    var = jnp.mean(jnp.square(x - mu), -1, keepdims=True)
    return (x - mu) * lax.rsqrt(var + EPS)


def _rms(x, w):
    return x * lax.rsqrt(jnp.mean(jnp.square(x), -1, keepdims=True) + EPS) * w


_mm_mod, _mm_main, _mm_if, _mm_ba, _mm_bb, _mm_out = (_make_mm(t) for t in ("mod", "main", "if", "ba", "bb", "out"))


def _local_loss(diff, const):
    x, c, ctx, target = diff["x"], const["c"], const["ctx"], const["target"]
    t, tc = x.shape[0], ctx.shape[0]
    n, r = tc + t, t + 2 * tc

    sc = jnp.concatenate([_silu(c), _silu(diff["c_ctx"])[None], jnp.zeros((14, D_MODEL), F32)], 0)
    mod = _mm_mod(sc, const["w_mod"], diff["p_mod"])[:2] + diff["b_mod"]
    shift, scale, gate = mod[0, :D_MODEL], mod[0, D_MODEL:2 * D_MODEL], mod[0, 2 * D_MODEL:]
    shift_c, scale_c = mod[1, :D_MODEL], mod[1, D_MODEL:2 * D_MODEL]
    u = _ln_plain(x) * (1 + scale) + shift
    u_c = _ln_plain(ctx) * (1 + scale_c) + shift_c
    u_all = jnp.concatenate([u_c, u, u_c], 0)

    p = _mm_main(u_all, const["w_main"], diff["p_main"])
    gt = _mm_if(u_all, const["w_if"], diff["p_if"])[:, :N_IF] + diff["b_if"]

    qk_pre = p[:, O_QK:O_QK + 2 * QK_A]
    rows = jnp.arange(r)
    seg_start = ((rows == 0) | (rows == tc) | (rows == n))[:, None]
    seg_end = ((rows == tc - 1) | (rows == n - 1) | (rows == r - 1))[:, None]
    prev = jnp.where(seg_start, 0.0, jnp.roll(qk_pre, 1, axis=0))
    nxt = jnp.where(seg_end, 0.0, jnp.roll(qk_pre, -1, axis=0))
    cw, cb = diff["conv_w"], diff["conv_b"]
    qk = _silu(cb + prev * cw[0] + qk_pre * cw[1] + nxt * cw[2])
    q_a, k_a = qk[:, :QK_A], qk[:, QK_A:] * (DK_A ** -0.5)
    v_a = p[:, O_VA:O_VA + V_A]
    li_f, lf_f = gt[:, 0:8], jax.nn.log_sigmoid(gt[:, 8:16])
    li_b, lf_b = gt[:, 16:24], jax.nn.log_sigmoid(gt[:, 24:32])

    h_f = _make_mlstm(n, 0, False)(q_a, k_a, v_a, li_f, lf_f)[tc:]
    h_b = _make_mlstm(n, tc, True)(q_a, k_a, v_a, li_b, lf_b)[:t]
    h_l = (h_f + h_b).reshape(t, NH_A, DV_A)

    lat = slice(tc, n)
    o_attn = _gqa(p[lat, O_QB:O_QB + Q_B], p[:n, O_KB:O_KB + KV_B], p[:n, O_VB:O_VB + KV_B],
                  diff["q_norm_w"], diff["k_norm_w"])

    hn = _rms(h_l, diff["mh_norm_w"].reshape(NH_A, DV_A)).reshape(t, V_A)
    a_in = jax.nn.sigmoid(p[lat, O_OA:O_OA + V_A]) * hn * _silu(p[lat, O_ZA:O_ZA + V_A])
    y_a = _mm_ba(a_in, const["w_ba"], diff["p_ba"])
    b_in = o_attn * _silu(p[lat, O_ZB:O_ZB + Q_B])
    y_b = _mm_bb(b_in, const["w_bb"], diff["p_bb"])
    m_in = jax.nn.sigmoid(p[lat, O_GA:O_GA + D_MODEL]) * y_a + jax.nn.sigmoid(p[lat, O_GB:O_GB + D_MODEL]) * y_b
    out = _mm_out(m_in, const["w_out"], diff["p_out"])

    y = _ln_plain(ALPHA * x + gate * out) * diff["ln_w"] + diff["ln_b"]
    return 0.5 * jnp.sum(jnp.mean(jnp.square(y - target), axis=-1))


OTHER_CHIPS = [(1, 0), (0, 1), (1, 1)]


def _flip(v, bit):
    return 1 - v if bit else v


def _gather_chips(shard, name):
    def body(x_ref, o_ref, send_sems, recv_sems, local_sem):
        x, y, c = lax.axis_index("x"), lax.axis_index("y"), lax.axis_index("c")
        mine = pltpu.make_async_copy(x_ref, o_ref.at[2 * x + y], local_sem)
        mine.start()

        def copy(r, slot):
            dx, dy = OTHER_CHIPS[r]
            return pltpu.make_async_remote_copy(
                src_ref=x_ref, dst_ref=o_ref.at[slot], send_sem=send_sems.at[r], recv_sem=recv_sems.at[r],
                device_id=(_flip(x, dx), _flip(y, dy), c), device_id_type=MESH)

        sends = [copy(r, 2 * x + y) for r in range(3)]
        for cp in sends:
            cp.start()
        for r, (dx, dy) in enumerate(OTHER_CHIPS):
            copy(r, 2 * _flip(x, dx) + _flip(y, dy)).wait_recv()
        for cp in sends:
            cp.wait_send()
        mine.wait()

    return pl.pallas_call(
        body, out_shape=jax.ShapeDtypeStruct((N_CHIPS,) + shard.shape, shard.dtype),
        in_specs=[pl.BlockSpec(memory_space=pl.ANY)], out_specs=pl.BlockSpec(memory_space=pl.ANY),
        scratch_shapes=[pltpu.SemaphoreType.DMA((3,)), pltpu.SemaphoreType.DMA((3,)), pltpu.SemaphoreType.DMA],
        name=name)(shard)


def _gather_chips_halves(shard, name):
    rows, cols = shard.shape
    halves = shard.reshape(2, rows // 2, cols)

    def body(x_ref, o_ref, send_sems, recv_sems, local_sem):
        x, y, c = lax.axis_index("x"), lax.axis_index("y"), lax.axis_index("c")
        my_chip = 2 * x + y
        mine = pltpu.make_async_copy(x_ref, o_ref.at[my_chip], local_sem)
        mine.start()

        def chip_of(r):
            dx, dy = OTHER_CHIPS[r]
            return _flip(x, dx), _flip(y, dy)

        def copy(k, chip_slot, half, to, src=None):
            dst = o_ref.at[chip_slot, half]
            return pltpu.make_async_remote_copy(
                src_ref=dst if src is None else src, dst_ref=dst, send_sem=send_sems.at[k],
                recv_sem=recv_sems.at[k], device_id=to, device_id_type=MESH)

        first = [copy(r, my_chip, c, (*chip_of(r), c), src=x_ref.at[c]) for r in range(3)]
        for cp in first:
            cp.start()
        passed = []
        for r in range(3):
            px, py = chip_of(r)
            copy(r, 2 * px + py, c, (px, py, c)).wait_recv()
            passed.append(copy(3 + r, 2 * px + py, c, (x, y, 1 - c)))
            passed[-1].start()
        for r in range(3):
            px, py = chip_of(r)
            copy(3 + r, 2 * px + py, 1 - c, (x, y, 1 - c)).wait_recv()
        for cp in first + passed:
            cp.wait_send()
        mine.wait()

    out = pl.pallas_call(
        body, out_shape=jax.ShapeDtypeStruct((N_CHIPS, 2, rows // 2, cols), shard.dtype),
        in_specs=[pl.BlockSpec(memory_space=pl.ANY)], out_specs=pl.BlockSpec(memory_space=pl.ANY),
        scratch_shapes=[pltpu.SemaphoreType.DMA((6,)), pltpu.SemaphoreType.DMA((6,)), pltpu.SemaphoreType.DMA],
        name=name)(halves)
    return out.reshape(N_CHIPS, rows, cols)


def _scatter_grads(slots, name):
    def body(g_ref, o_ref, send_sems, recv_sems, local_sem):
        x, y, c = lax.axis_index("x"), lax.axis_index("y"), lax.axis_index("c")
        me, my_chip, sibling = 4 * x + 2 * y + c, 2 * x + y, (x, y, 1 - c)
        mine = pltpu.make_async_copy(g_ref.at[my_chip], o_ref.at[me], local_sem)
        mine.start()

        def chip_of(r):
            dx, dy = OTHER_CHIPS[r]
            return _flip(x, dx), _flip(y, dy)

        def copy(k, slot, to, src=None):
            dst = o_ref.at[slot]
            return pltpu.make_async_remote_copy(
                src_ref=dst if src is None else src, dst_ref=dst, send_sem=send_sems.at[k],
                recv_sem=recv_sems.at[k], device_id=to, device_id_type=MESH)

        first = [copy(0, me, sibling, src=g_ref.at[my_chip])]
        for r in range(3):
            px, py = chip_of(r)
            first.append(copy(1 + r, me, (px, py, c), src=g_ref.at[2 * px + py]))
        for cp in first:
            cp.start()
        passed = []
        for r in range(3):
            px, py = chip_of(r)
            copy(1 + r, 4 * px + 2 * py + c, (px, py, c)).wait_recv()
            passed.append(copy(4 + r, 4 * px + 2 * py + c, sibling))
            passed[-1].start()
        copy(0, 4 * x + 2 * y + 1 - c, sibling).wait_recv()
        for r in range(3):
            px, py = chip_of(r)
            copy(4 + r, 4 * px + 2 * py + 1 - c, sibling).wait_recv()
        for cp in first + passed:
            cp.wait_send()
        mine.wait()

    return pl.pallas_call(
        body, out_shape=jax.ShapeDtypeStruct((N_DEV,) + slots.shape[1:], slots.dtype),
        in_specs=[pl.BlockSpec(memory_space=pl.ANY)], out_specs=pl.BlockSpec(memory_space=pl.ANY),
        scratch_shapes=[pltpu.SemaphoreType.DMA((N_DEV - 1,)), pltpu.SemaphoreType.DMA((N_DEV - 1,)),
                        pltpu.SemaphoreType.DMA],
        name=name)(slots)


def _allreduce_small(v, name):
    def body(v_ref, o_ref, buf, send_sems, recv_sems):
        x, y, c = lax.axis_index("x"), lax.axis_index("y"), lax.axis_index("c")
        me = 4 * x + 2 * y + c
        buf[me] = v_ref[...]

        def peer(r):
            return _flip(x, (r >> 2) & 1), _flip(y, (r >> 1) & 1), _flip(c, r & 1)

        def copy(r, dst_slot):
            return pltpu.make_async_remote_copy(
                src_ref=v_ref, dst_ref=buf.at[dst_slot], send_sem=send_sems.at[r - 1],
                recv_sem=recv_sems.at[r - 1], device_id=peer(r), device_id_type=MESH)

        sends = [copy(r, me) for r in range(1, N_DEV)]
        for cp in sends:
            cp.start()
        for r in range(1, N_DEV):
            px, py, pc = peer(r)
            copy(r, 4 * px + 2 * py + pc).wait_recv()
        for cp in sends:
            cp.wait_send()
        acc = buf[0]
        for d in range(1, N_DEV):
            acc = acc + buf[d]
        o_ref[...] = acc

    return pl.pallas_call(
        body, out_shape=jax.ShapeDtypeStruct(v.shape, v.dtype),
        in_specs=[pl.BlockSpec(memory_space=pltpu.VMEM)], out_specs=pl.BlockSpec(memory_space=pltpu.VMEM),
        scratch_shapes=[pltpu.VMEM((N_DEV,) + v.shape, v.dtype), pltpu.SemaphoreType.DMA((N_DEV - 1,)),
                        pltpu.SemaphoreType.DMA((N_DEV - 1,))],
        name=name)(v)


def _adamw_math(w, g, m, v):
    m = ADAM_B1 * m + (1.0 - ADAM_B1) * g
    v = ADAM_B2 * v + (1.0 - ADAM_B2) * jnp.square(g)
    m_hat = m / (1.0 - ADAM_B1 ** ADAM_STEP)
    v_hat = v / (1.0 - ADAM_B2 ** ADAM_STEP)
    delta = -ADAM_LR * (m_hat / (jnp.sqrt(v_hat) + ADAM_EPS) + ADAM_WD * w)
    return delta, m, v


def _adamw_sum(parts, w, m, v, name):
    npart, rows, cols = parts.shape
    tr = _pick(rows, (64, 32, 16, 8)) if rows >= 8 else rows

    def body(p_ref, w_ref, m_ref, v_ref, g_out, d_out, m_out, v_out):
        g = p_ref[0].astype(F32)
        for k in range(1, npart):
            g = g + p_ref[k].astype(F32)
        d, m2, v2 = _adamw_math(w_ref[...], g, m_ref[...], v_ref[...])
        g_out[...] = g
        d_out[...] = d
        m_out[...] = m2
        v_out[...] = v2

    spec = pl.BlockSpec((tr, cols), lambda i: (i, 0))
    shp = jax.ShapeDtypeStruct((rows, cols), F32)
    return pl.pallas_call(
        body, grid=(rows // tr,),
        in_specs=[pl.BlockSpec((npart, tr, cols), lambda i: (0, i, 0)), spec, spec, spec],
        out_specs=[spec, spec, spec, spec], out_shape=[shp, shp, shp, shp],
        compiler_params=pltpu.CompilerParams(dimension_semantics=("parallel",), vmem_limit_bytes=VMEM_LIMIT),
        name=name)(parts, w, m, v)


SMALL_ROWS = 16


def _pack_small(c_ctx, b_mod, conv_b, mh, ln_w, ln_b, conv_w_rows, b_if, qn, kn):
    last = jnp.concatenate([b_if.reshape(-1), qn.reshape(-1), kn.reshape(-1),
                            jnp.zeros((D_MODEL - N_IF - 2 * HD_B,), F32)])
    rows = [c_ctx.reshape(1, D_MODEL), b_mod.reshape(3, D_MODEL), conv_b.reshape(1, D_MODEL),
            mh.reshape(1, D_MODEL), ln_w.reshape(1, D_MODEL), ln_b.reshape(1, D_MODEL),
            conv_w_rows.reshape(3, D_MODEL), last[None], jnp.zeros((SMALL_ROWS - 12, D_MODEL), F32)]
    return jnp.concatenate(rows, 0)


def _unpack_small(pk, conv_cols):
    return dict(c_ctx=pk[0], b_mod=pk[1:4].reshape(1, 3 * D_MODEL), conv_b=pk[4:5], mh_norm_w=pk[5:6],
                ln_w=pk[6:7], ln_b=pk[7:8], conv_w=pk[8:11, :conv_cols][None], b_if=pk[11:12, :N_IF],
                q_norm_w=pk[11:12, N_IF:N_IF + HD_B], k_norm_w=pk[11:12, N_IF + HD_B:N_IF + 2 * HD_B])


def kernel(x, c, ctx, c_ctx, w_mod, b_mod, w_in, b_if, conv_w, conv_b, mh_norm_w, q_norm_w, k_norm_w, w_branch_a, w_branch_b, w_out, ln_w, ln_b, loss_target, m_c_ctx, m_w_mod, m_b_mod, m_w_in, m_b_if, m_conv_w, m_conv_b, m_mh_norm_w, m_q_norm_w, m_k_norm_w, m_w_branch_a, m_w_branch_b, m_w_out, m_ln_w, m_ln_b, v_c_ctx, v_w_mod, v_b_mod, v_w_in, v_b_if, v_conv_w, v_conv_b, v_mh_norm_w, v_q_norm_w, v_k_norm_w, v_w_branch_a, v_w_branch_b, v_w_out, v_ln_w, v_ln_b):
    chip = 2 * lax.axis_index("x") + lax.axis_index("y")
    mod_cols, in_cols, conv_cols = w_mod.shape[2], w_in.shape[2], conv_w.shape[2]
    br_rows = w_out.shape[1]

    g_mod = _gather_chips_halves(w_mod[0].astype(BF16), "gather_w_mod")
    g_in = _gather_chips_halves(w_in[0].astype(BF16), "gather_w_in")
    g_ba = _gather_chips_halves(w_branch_a[0].astype(BF16), "gather_w_ba")
    g_bb = _gather_chips_halves(w_branch_b[0].astype(BF16), "gather_w_bb")
    g_out = _gather_chips_halves(w_out[0].astype(BF16), "gather_w_out")
    g_conv = _gather_chips(conv_w[0], "gather_conv_w")
    w_mod_full = jnp.moveaxis(g_mod, 0, 1).reshape(D_MODEL, N_CHIPS * mod_cols)
    w_in_full = jnp.moveaxis(g_in, 0, 1).reshape(D_MODEL, N_CHIPS * in_cols)
    w_main = jnp.concatenate([w_in_full[:, :IF_START], w_in_full[:, IF_START + N_IF:]], 1)
    w_if = jnp.pad(w_in_full[:, IF_START:IF_START + N_IF], ((0, 0), (0, IF_PAD - N_IF)))
    conv_w_full = jnp.moveaxis(g_conv, 0, 1).reshape(3, N_CHIPS * conv_cols)

    const = dict(c=c, ctx=ctx[0], target=loss_target[0], w_mod=w_mod_full, w_main=w_main, w_if=w_if,
                 w_ba=g_ba.reshape(D_MODEL, D_MODEL), w_bb=g_bb.reshape(D_MODEL, D_MODEL),
                 w_out=g_out.reshape(D_MODEL, D_MODEL))
    diff = dict(x=x[0], c_ctx=c_ctx, b_mod=b_mod[0], b_if=b_if[0], conv_w=conv_w_full, conv_b=conv_b[0],
                mh_norm_w=mh_norm_w[0], q_norm_w=q_norm_w[0], k_norm_w=k_norm_w[0], ln_w=ln_w[0], ln_b=ln_b[0],
                p_mod=jnp.zeros(w_mod_full.shape, F32), p_main=jnp.zeros(w_main.shape, F32),
                p_if=jnp.zeros(w_if.shape, F32), p_ba=jnp.zeros((D_MODEL, D_MODEL), F32),
                p_bb=jnp.zeros((D_MODEL, D_MODEL), F32), p_out=jnp.zeros((D_MODEL, D_MODEL), F32))
    loss_local, g = jax.value_and_grad(_local_loss)(diff, const)
    loss = lax.psum(loss_local, ("x", "y", "c"))

    g_small = _allreduce_small(
        _pack_small(g["c_ctx"], g["b_mod"], g["conv_b"], g["mh_norm_w"], g["ln_w"], g["ln_b"], g["conv_w"],
                    g["b_if"], g["q_norm_w"], g["k_norm_w"]), "allreduce_small")
    conv_g = lax.dynamic_slice(g_small[8:11], (0, chip * conv_cols), (3, conv_cols))
    g_small = g_small.at[8:11].set(jnp.pad(conv_g, ((0, 0), (0, D_MODEL - conv_cols))))
    pad_conv = lambda a: jnp.pad(a[0], ((0, 0), (0, D_MODEL - conv_cols)))
    packed = [_pack_small(cc, bm[0], cb[0], mh[0], lw[0], lb[0], pad_conv(cw), bi[0], qn[0], kn[0])
              for cc, bm, cb, mh, lw, lb, cw, bi, qn, kn in (
                  (c_ctx, b_mod, conv_b, mh_norm_w, ln_w, ln_b, conv_w, b_if, q_norm_w, k_norm_w),
                  (m_c_ctx, m_b_mod, m_conv_b, m_mh_norm_w, m_ln_w, m_ln_b, m_conv_w, m_b_if, m_q_norm_w, m_k_norm_w),
                  (v_c_ctx, v_b_mod, v_conv_b, v_mh_norm_w, v_ln_w, v_ln_b, v_conv_w, v_b_if, v_q_norm_w, v_k_norm_w))]
    small = [_unpack_small(a, conv_cols)
             for a in _adamw_sum(g_small[None], packed[0], packed[1], packed[2], "adamw_small")]

    def col_slots(gfull, cols):
        return jnp.moveaxis(gfull.reshape(D_MODEL, N_CHIPS, cols), 1, 0).astype(BF16)

    g_in_full = jnp.concatenate([g["p_main"][:, :IF_START], g["p_if"][:, :N_IF], g["p_main"][:, IF_START:]], 1)
    big = {}
    for nm, slots, w_, m_, v_ in (
            ("w_mod", col_slots(g["p_mod"], mod_cols), w_mod, m_w_mod, v_w_mod),
            ("w_in", col_slots(g_in_full, in_cols), w_in, m_w_in, v_w_in),
            ("w_branch_a", g["p_ba"].reshape(N_CHIPS, br_rows, D_MODEL).astype(BF16), w_branch_a, m_w_branch_a, v_w_branch_a),
            ("w_branch_b", g["p_bb"].reshape(N_CHIPS, br_rows, D_MODEL).astype(BF16), w_branch_b, m_w_branch_b, v_w_branch_b),
            ("w_out", g["p_out"].reshape(N_CHIPS, br_rows, D_MODEL).astype(BF16), w_out, m_w_out, v_w_out)):
        parts = _scatter_grads(slots, "scatter_" + nm)
        big[nm] = [a[None] for a in _adamw_sum(parts, w_[0], m_[0], v_[0], "adamw_" + nm)]

    names = ["c_ctx", "w_mod", "b_mod", "w_in", "b_if", "conv_w", "conv_b", "mh_norm_w", "q_norm_w", "k_norm_w",
             "w_branch_a", "w_branch_b", "w_out", "ln_w", "ln_b"]
    outs = [[big[nm][k] if nm in big else small[k][nm] for nm in names] for k in range(4)]
    return (loss, g["x"][None], *outs[0], *outs[1], *outs[2], *outs[3])
```

```python
import functools

import jax
import jax.numpy as jnp
from jax import lax
from jax.experimental import pallas as pl
from jax.experimental.pallas import tpu as pltpu

F32 = jnp.float32
BF16 = jnp.bfloat16
MESH = pl.DeviceIdType.MESH

D_MODEL = 2048
NH_A, DK_A, DV_A = 8, 128, 256
QK_A, V_A = NH_A * DK_A, NH_A * DV_A
NH_B, NKV_B, HD_B = 16, 4, 128
Q_B, KV_B = NH_B * HD_B, NKV_B * HD_B
GRID_W = 64
ROT_HALF = HD_B // 2
ROPE_THETA = 10000.0
M_INIT = -1e30
EPS = 1e-6
ALPHA = 2.0 ** 0.25
N_IN = 17440
IF_START, N_IF, IF_PAD = 4096, 32, 128
N_MAIN = N_IN - N_IF
O_QK, O_VA, O_KB, O_VB, O_OA, O_ZA, O_QB, O_ZB, O_GA, O_GB = (
    0, 2048, 4096, 4608, 5120, 7168, 9216, 11264, 13312, 15360)
MLSTM_CHUNK = 256

ADAM_LR, ADAM_B1, ADAM_B2, ADAM_EPS, ADAM_WD, ADAM_STEP = 0.001, 0.9, 0.999, 1e-08, 0.01, 10

VMEM_LIMIT = 48 * 1024 * 1024
N_CHIPS, N_DEV = 4, 8
MX = BF16


def _pick(n, cands):
    for c in cands:
        if n % c == 0:
            return c
    raise ValueError(f"no tile for {n} in {cands}")


def _dot(a, b):
    return jnp.dot(a, b, preferred_element_type=F32)


def _dot_nt(a, b):
    return lax.dot_general(a, b, (((1,), (1,)), ((), ())), preferred_element_type=F32)


def _dot_tn(a, b):
    return lax.dot_general(a, b, (((0,), (0,)), ((), ())), preferred_element_type=F32)


def _mm_nn(a, b, name):
    m, k = a.shape
    _, n = b.shape
    tm = _pick(m, (512, 256, 128, 64, 32, 16))
    tn = _pick(n, (1024, 512, 256, 128))

    def body(a_ref, b_ref, o_ref):
        o_ref[...] = _dot(a_ref[...], b_ref[...])

    return pl.pallas_call(
        body, grid=(m // tm, n // tn),
        in_specs=[pl.BlockSpec((tm, k), lambda i, j: (i, 0)), pl.BlockSpec((k, tn), lambda i, j: (0, j))],
        out_specs=pl.BlockSpec((tm, tn), lambda i, j: (i, j)),
        out_shape=jax.ShapeDtypeStruct((m, n), F32),
        compiler_params=pltpu.CompilerParams(dimension_semantics=("parallel", "parallel"),
                                             vmem_limit_bytes=VMEM_LIMIT),
        name=name)(a, b)


def _mm_nt(g, w, name):
    m, n = g.shape
    k, _ = w.shape
    tm = _pick(m, (512, 256, 128, 64, 32, 16))
    tn = _pick(n, (1024, 512, 256, 128))

    def body(g_ref, w_ref, o_ref):
        part = _dot_nt(g_ref[...], w_ref[...])

        @pl.when(pl.program_id(1) == 0)
        def _():
            o_ref[...] = part

        @pl.when(pl.program_id(1) > 0)
        def _():
            o_ref[...] += part

    return pl.pallas_call(
        body, grid=(m // tm, n // tn),
        in_specs=[pl.BlockSpec((tm, tn), lambda i, j: (i, j)), pl.BlockSpec((k, tn), lambda i, j: (0, j))],
        out_specs=pl.BlockSpec((tm, k), lambda i, j: (i, 0)),
        out_shape=jax.ShapeDtypeStruct((m, k), F32),
        compiler_params=pltpu.CompilerParams(dimension_semantics=("parallel", "arbitrary"),
                                             vmem_limit_bytes=VMEM_LIMIT),
        name=name)(g, w)


def _mm_tn(a, g, name):
    m, k = a.shape
    _, n = g.shape
    tm = _pick(m, (512, 256, 128, 64, 32, 16))
    tn = _pick(n, (1024, 512, 256, 128))

    def body(a_ref, g_ref, o_ref):
        part = _dot_tn(a_ref[...], g_ref[...])

        @pl.when(pl.program_id(1) == 0)
        def _():
            o_ref[...] = part

        @pl.when(pl.program_id(1) > 0)
        def _():
            o_ref[...] += part

    return pl.pallas_call(
        body, grid=(n // tn, m // tm),
        in_specs=[pl.BlockSpec((tm, k), lambda j, i: (i, 0)), pl.BlockSpec((tm, tn), lambda j, i: (i, j))],
        out_specs=pl.BlockSpec((k, tn), lambda j, i: (0, j)),
        out_shape=jax.ShapeDtypeStruct((k, n), F32),
        compiler_params=pltpu.CompilerParams(dimension_semantics=("parallel", "arbitrary"),
                                             vmem_limit_bytes=VMEM_LIMIT),
        name=name)(a, g)


def _make_mm(tag):
    @jax.custom_vjp
    def mm(a, w, proxy):
        del proxy
        return _mm_nn(a.astype(BF16), w, f"mm_{tag}_fwd")

    def fwd(a, w, proxy):
        del proxy
        ab = a.astype(BF16)
        return _mm_nn(ab, w, f"mm_{tag}_fwd"), (ab, w)

    def bwd(res, g):
        ab, w = res
        gb = g.astype(BF16)
        da = _mm_nt(gb, w, f"mm_{tag}_da")
        dw = _mm_tn(ab, gb, f"mm_{tag}_dw")
        return da, jnp.zeros_like(w), dw

    mm.defvjp(fwd, bwd)
    return mm


SLABS = (("qk", O_QK, 2 * QK_A), ("va", O_VA, V_A), ("kv", O_KB, 2 * KV_B), ("oa", O_OA, V_A), ("za", O_ZA, V_A),
         ("qb", O_QB, Q_B), ("zb", O_ZB, Q_B), ("ga", O_GA, D_MODEL), ("gb", O_GB, D_MODEL))
SLAB_TN = 1024


def _slab_blocks():
    return [(off // SLAB_TN, (off + width) // SLAB_TN) for _, off, width in SLABS]


def _proj_fwd_slab(a, w, off, width, name):
    m, k = a.shape
    tm = _pick(m, (512, 256, 128))
    tn = min(SLAB_TN, width)

    def body(a_ref, b_ref, o_ref):
        o_ref[...] = _dot(a_ref[...], b_ref[...])

    return pl.pallas_call(
        body, grid=(m // tm, width // tn),
        in_specs=[pl.BlockSpec((tm, k), lambda i, j: (i, 0)), pl.BlockSpec((k, tn), lambda i, j: (0, j + off // tn))],
        out_specs=pl.BlockSpec((tm, tn), lambda i, j: (i, j)),
        out_shape=jax.ShapeDtypeStruct((m, width), F32),
        compiler_params=pltpu.CompilerParams(dimension_semantics=("parallel", "parallel"),
                                             vmem_limit_bytes=VMEM_LIMIT),
        name=name)(a, w)


def _slab_spec(tm, blocks, rows_inner):
    b, e = blocks

    def index(r, c):
        inside = (c >= b) & (c < e)
        return jnp.where(inside, r, 0), jnp.clip(c - b, 0, e - b - 1)

    if rows_inner:
        return pl.BlockSpec((tm, SLAB_TN), lambda c, r: index(r, c))
    return pl.BlockSpec((tm, SLAB_TN), lambda r, c: index(r, c))


def _proj_da(gs, w, name):
    m = gs[0].shape[0]
    k, n = w.shape
    tm = _pick(m, (256, 128))
    blocks = _slab_blocks()

    def body(*refs):
        g_refs, w_ref, o_ref = refs[:len(blocks)], refs[len(blocks)], refs[len(blocks) + 1]
        c = pl.program_id(1)

        @pl.when(c == 0)
        def _():
            o_ref[...] = jnp.zeros(o_ref.shape, F32)

        for g_ref, (b, e) in zip(g_refs, blocks):
            @pl.when((c >= b) & (c < e))
            def _(g_ref=g_ref):
                o_ref[...] += _dot_nt(g_ref[...].astype(BF16), w_ref[...])

    return pl.pallas_call(
        body, grid=(m // tm, n // SLAB_TN),
        in_specs=[_slab_spec(tm, blk, False) for blk in blocks] + [pl.BlockSpec((k, SLAB_TN), lambda r, c: (0, c))],
        out_specs=pl.BlockSpec((tm, k), lambda r, c: (r, 0)),
        out_shape=jax.ShapeDtypeStruct((m, k), F32),
        compiler_params=pltpu.CompilerParams(dimension_semantics=("parallel", "arbitrary"),
                                             vmem_limit_bytes=VMEM_LIMIT),
        name=name)(*gs, w)


def _proj_dw(a, gs, n, name):
    m, k = a.shape
    tm = _pick(m, (256, 128))
    blocks = _slab_blocks()

    def body(*refs):
        a_ref, g_refs, o_ref = refs[0], refs[1:1 + len(blocks)], refs[1 + len(blocks)]
        c, r = pl.program_id(0), pl.program_id(1)

        @pl.when(r == 0)
        def _():
            o_ref[...] = jnp.zeros(o_ref.shape, F32)

        for g_ref, (b, e) in zip(g_refs, blocks):
            @pl.when((c >= b) & (c < e))
            def _(g_ref=g_ref):
                o_ref[...] += _dot_tn(a_ref[...], g_ref[...].astype(BF16))

    return pl.pallas_call(
        body, grid=(n // SLAB_TN, m // tm),
        in_specs=[pl.BlockSpec((tm, k), lambda c, r: (r, 0))] + [_slab_spec(tm, blk, True) for blk in blocks],
        out_specs=pl.BlockSpec((k, SLAB_TN), lambda c, r: (0, c)),
        out_shape=jax.ShapeDtypeStruct((k, n), F32),
        compiler_params=pltpu.CompilerParams(dimension_semantics=("parallel", "arbitrary"),
                                             vmem_limit_bytes=VMEM_LIMIT),
        name=name)(a, *gs)


@jax.custom_vjp
def _project(u, w, proxy):
    return _project_fwd(u, w, proxy)[0]


def _project_fwd(u, w, proxy):
    del proxy
    ub = u.astype(BF16)
    return tuple(_proj_fwd_slab(ub, w, off, width, "proj_" + nm) for nm, off, width in SLABS), (ub, w)


def _project_bwd(res, gs):
    ub, w = res
    return _proj_da(gs, w, "proj_da"), jnp.zeros_like(w), _proj_dw(ub, gs, w.shape[1], "proj_dw")


_project.defvjp(_project_fwd, _project_bwd)


ATT_SCALE = HD_B ** -0.5
GROUP = NH_B // NKV_B


LOG2E, LN2 = 1.4426950408889634, 0.6931471805599453
ATT_C = ATT_SCALE * LOG2E
STRIP_Q, STRIP_K = 128, 256


def _attn_tiles(t, n):
    return _pick(t, (512, 256, 128)), _pick(n, (768, 512, 256))


def _attn_fwd(q, k, v):
    t, n = q.shape[0], k.shape[0]
    tq, tk = _attn_tiles(t, n)
    nk = n // tk

    def body(q_ref, k_ref, v_ref, o_ref, lse_ref, m_sc, acc_sc):
        j = pl.program_id(2)

        @pl.when(j == 0)
        def _():
            m_sc[...] = jnp.full(m_sc.shape, -jnp.inf, F32)
            acc_sc[...] = jnp.zeros(acc_sc.shape, F32)

        kb = k_ref[...]
        v_ones = jnp.concatenate([v_ref[...], jnp.ones((tk, HD_B), BF16)], axis=1)
        for g in range(GROUP):
            s2 = _dot_nt(q_ref[:, g * HD_B:(g + 1) * HD_B], kb) * ATT_C
            m_prev = m_sc[g]
            m_new = jnp.maximum(m_prev, jnp.max(s2, axis=-1, keepdims=True))
            p = jnp.exp2(s2 - m_new).astype(BF16)
            acc_sc[g] = jnp.exp2(m_prev - m_new) * acc_sc[g] + _dot(p, v_ones)
            m_sc[g] = m_new

        @pl.when(j == nk - 1)
        def _():
            for g in range(GROUP):
                cols = slice(g * HD_B, (g + 1) * HD_B)
                l = acc_sc[g, :, HD_B:]
                o_ref[:, cols] = acc_sc[g, :, :HD_B] / l
                lse_ref[:, cols] = m_sc[g] + jnp.log(l) * LOG2E

    qspec = pl.BlockSpec((tq, GROUP * HD_B), lambda kh, i, j: (i, kh))
    kspec = pl.BlockSpec((tk, HD_B), lambda kh, i, j: (j, kh))
    return pl.pallas_call(
        body, grid=(NKV_B, t // tq, nk),
        in_specs=[qspec, kspec, kspec], out_specs=[qspec, qspec],
        out_shape=[jax.ShapeDtypeStruct((t, Q_B), F32), jax.ShapeDtypeStruct((t, Q_B), F32)],
        scratch_shapes=[pltpu.VMEM((GROUP, tq, 1), F32), pltpu.VMEM((GROUP, tq, 2 * HD_B), F32)],
        compiler_params=pltpu.CompilerParams(dimension_semantics=("parallel", "parallel", "arbitrary"),
                                             vmem_limit_bytes=VMEM_LIMIT),
        name="attn_fwd")(q, k, v)


def _attn_dq(q, k, v, do, lse, delta):
    t, n = q.shape[0], k.shape[0]
    tq, tk = _attn_tiles(t, n)
    nk = n // tk

    def body(q_ref, k_ref, v_ref, do_ref, lse_ref, dl_ref, dq_ref):
        j = pl.program_id(2)
        kb, vb = k_ref[...], v_ref[...]
        parts = []
        for g in range(GROUP):
            cols = slice(g * HD_B, (g + 1) * HD_B)
            p = jnp.exp2(_dot_nt(q_ref[:, cols], kb) * ATT_C - lse_ref[:, g * HD_B:g * HD_B + 1])
            dp = _dot_nt(do_ref[:, cols], vb)
            ds = p * (dp - dl_ref[:, g * HD_B:g * HD_B + 1])
            parts.append(_dot(ds.astype(BF16), kb))

        @pl.when(j == 0)
        def _():
            for g in range(GROUP):
                dq_ref[:, g * HD_B:(g + 1) * HD_B] = parts[g]

        @pl.when(j > 0)
        def _():
            for g in range(GROUP):
                dq_ref[:, g * HD_B:(g + 1) * HD_B] += parts[g]

        @pl.when(j == nk - 1)
        def _():
            dq_ref[...] = dq_ref[...] * ATT_SCALE

    qspec = pl.BlockSpec((tq, GROUP * HD_B), lambda kh, i, j: (i, kh))
    kspec = pl.BlockSpec((tk, HD_B), lambda kh, i, j: (j, kh))
    return pl.pallas_call(
        body, grid=(NKV_B, t // tq, n // tk),
        in_specs=[qspec, kspec, kspec, qspec, qspec, qspec],
        out_specs=qspec,
        out_shape=jax.ShapeDtypeStruct((t, Q_B), F32),
        compiler_params=pltpu.CompilerParams(dimension_semantics=("parallel", "parallel", "arbitrary"),
                                             vmem_limit_bytes=VMEM_LIMIT),
        name="attn_dq")(q, k, v, do, lse, delta)


def _attn_dkv(q, k, v, do, lse_t, delta_t):
    t, n = q.shape[0], k.shape[0]
    tq, tk = _attn_tiles(t, n)
    nq = t // tq
    n_r, n_c = tq // STRIP_Q, tk // STRIP_K

    def body(q_ref, k_ref, v_ref, do_ref, lse_ref, dl_ref, dk_ref, dv_ref, dk_sc, dv_sc):
        i = pl.program_id(2)

        @pl.when(i == 0)
        def _():
            dk_sc[...] = jnp.zeros(dk_sc.shape, F32)
            dv_sc[...] = jnp.zeros(dv_sc.shape, F32)

        for r in range(n_r):
            rows = slice(r * STRIP_Q, (r + 1) * STRIP_Q)
            for c in range(n_c):
                kv = slice(c * STRIP_K, (c + 1) * STRIP_K)
                kc, vc = k_ref[kv, :], v_ref[kv, :]
                dk_part = dv_part = None
                for g in range(GROUP):
                    cols = slice(g * HD_B, (g + 1) * HD_B)
                    qg, dog = q_ref[rows, cols], do_ref[rows, cols]
                    st = _dot_nt(kc, qg)
                    pt = jnp.exp2(st * ATT_C - lse_ref[8 * g:8 * g + 1, rows])
                    dvg = _dot(pt.astype(BF16), dog)
                    dpt = _dot_nt(vc, dog)
                    dst = pt * (dpt - dl_ref[8 * g:8 * g + 1, rows])
                    dkg = _dot(dst.astype(BF16), qg)
                    dk_part = dkg if dk_part is None else dk_part + dkg
                    dv_part = dvg if dv_part is None else dv_part + dvg
                dk_sc[kv, :] += dk_part
                dv_sc[kv, :] += dv_part

        @pl.when(i == nq - 1)
        def _():
            dk_ref[...] = dk_sc[...] * ATT_SCALE
            dv_ref[...] = dv_sc[...]

    qspec = pl.BlockSpec((tq, GROUP * HD_B), lambda kh, j, i: (i, kh))
    tspec = pl.BlockSpec((8 * GROUP, tq), lambda kh, j, i: (kh, i))
    kspec = pl.BlockSpec((tk, HD_B), lambda kh, j, i: (j, kh))
    return pl.pallas_call(
        body, grid=(NKV_B, n // tk, nq),
        in_specs=[qspec, kspec, kspec, qspec, tspec, tspec],
        out_specs=[kspec, kspec],
        out_shape=[jax.ShapeDtypeStruct((n, KV_B), F32), jax.ShapeDtypeStruct((n, KV_B), F32)],
        scratch_shapes=[pltpu.VMEM((tk, HD_B), F32), pltpu.VMEM((tk, HD_B), F32)],
        compiler_params=pltpu.CompilerParams(dimension_semantics=("parallel", "parallel", "arbitrary"),
                                             vmem_limit_bytes=VMEM_LIMIT),
        name="attn_dkv")(q, k, v, do, lse_t, delta_t)


def _attention_bwd(res, do):
    qb, kb, vb, o, lse = res
    t = qb.shape[0]
    delta = jnp.sum((do * o).reshape(t, NH_B, HD_B), axis=-1)
    lse_h = lse.reshape(t, NH_B, HD_B)[:, :, 0]
    delta_b = jnp.broadcast_to(delta[:, :, None], (t, NH_B, HD_B)).reshape(t, Q_B)
    lse_t = jnp.broadcast_to(lse_h.T[:, None, :], (NH_B, 8, t)).reshape(NH_B * 8, t)
    delta_t = jnp.broadcast_to(delta.T[:, None, :], (NH_B, 8, t)).reshape(NH_B * 8, t)
    dob = do.astype(BF16)
    dq = _attn_dq(qb, kb, vb, dob, lse, delta_b)
    dk, dv = _attn_dkv(qb, kb, vb, dob, lse_t, delta_t)
    return dq, dk, dv


def _swap32(y):
    lane = lax.broadcasted_iota(jnp.int32, y.shape, 1)
    return jnp.where((lane // 32) % 2 == 0, pltpu.roll(y, 96, 1), pltpu.roll(y, 32, 1))


def _norm_rope_fwd(x, w, cos, sin, name):
    r, width = x.shape
    heads = width // HD_B
    tr = _pick(r, (256, 128))

    def body(x_ref, w_ref, c_ref, s_ref, o_ref):
        w, c, s = w_ref[...], c_ref[...], s_ref[...]
        for h in range(heads):
            cols = slice(h * HD_B, (h + 1) * HD_B)
            xh = x_ref[:, cols]
            y = xh * lax.rsqrt(jnp.mean(xh * xh, axis=-1, keepdims=True) + EPS) * w
            o_ref[:, cols] = (y * c + _swap32(y) * s).astype(o_ref.dtype)

    row = pl.BlockSpec((tr, width), lambda i: (i, 0))
    tab = pl.BlockSpec((tr, HD_B), lambda i: (i, 0))
    return pl.pallas_call(
        body, grid=(r // tr,),
        in_specs=[row, pl.BlockSpec((1, HD_B), lambda i: (0, 0)), tab, tab], out_specs=row,
        out_shape=jax.ShapeDtypeStruct((r, width), BF16),
        compiler_params=pltpu.CompilerParams(dimension_semantics=("parallel",), vmem_limit_bytes=VMEM_LIMIT),
        name=name)(x, w, cos, sin)


def _norm_rope_bwd(x, w, cos, sin, dy, name):
    r, width = x.shape
    heads = width // HD_B
    tr = _pick(r, (256, 128))

    def body(x_ref, w_ref, c_ref, s_ref, dy_ref, dx_ref, dw_ref):
        @pl.when(pl.program_id(0) == 0)
        def _():
            dw_ref[...] = jnp.zeros(dw_ref.shape, F32)

        w, c, s = w_ref[...], c_ref[...], s_ref[...]
        dw = jnp.zeros((1, HD_B), F32)
        for h in range(heads):
            cols = slice(h * HD_B, (h + 1) * HD_B)
            xh, dyh = x_ref[:, cols], dy_ref[:, cols]
            rs = lax.rsqrt(jnp.mean(xh * xh, axis=-1, keepdims=True) + EPS)
            dn = dyh * c + _swap32(dyh * s)
            dw = dw + jnp.sum(dn * (xh * rs), axis=0, keepdims=True)
            dxn = dn * w
            dx_ref[:, cols] = rs * dxn - xh * (rs * rs * rs * jnp.mean(dxn * xh, axis=-1, keepdims=True))
        dw_ref[...] += dw

    row = pl.BlockSpec((tr, width), lambda i: (i, 0))
    tab = pl.BlockSpec((tr, HD_B), lambda i: (i, 0))
    vec = pl.BlockSpec((1, HD_B), lambda i: (0, 0))
    return pl.pallas_call(
        body, grid=(r // tr,),
        in_specs=[row, vec, tab, tab, row], out_specs=[row, vec],
        out_shape=[jax.ShapeDtypeStruct((r, width), F32), jax.ShapeDtypeStruct((1, HD_B), F32)],
        compiler_params=pltpu.CompilerParams(dimension_semantics=("arbitrary",), vmem_limit_bytes=VMEM_LIMIT),
        name=name)(x, w, cos, sin, dy)


def _rope_tables(t):
    pos = jnp.arange(t)
    row = (pos // GRID_W).astype(F32)
    col = (pos % GRID_W).astype(F32)
    inv = ROPE_THETA ** (-jnp.arange(0, ROT_HALF, 2, dtype=F32) / ROT_HALF)
    ar, ac = row[:, None] * inv[None], col[:, None] * inv[None]
    cos = jnp.concatenate([jnp.cos(ar), jnp.cos(ar), jnp.cos(ac), jnp.cos(ac)], -1)
    sin = jnp.concatenate([-jnp.sin(ar), jnp.sin(ar), -jnp.sin(ac), jnp.sin(ac)], -1)
    return cos, sin


def _gqa_tables(t, n):
    cos, sin = _rope_tables(t)
    cos_k = jnp.concatenate([jnp.ones((n - t, HD_B), F32), cos], 0)
    sin_k = jnp.concatenate([jnp.zeros((n - t, HD_B), F32), sin], 0)
    return cos, sin, cos_k, sin_k


@jax.custom_vjp
def _gqa(pq, pk, pv, qw, kw):
    return _gqa_fwd(pq, pk, pv, qw, kw)[0]


def _gqa_fwd(pq, pk, pv, qw, kw):
    cos, sin, cos_k, sin_k = _gqa_tables(pq.shape[0], pk.shape[0])
    q = _norm_rope_fwd(pq, qw[None], cos, sin, "q_norm_rope")
    k = _norm_rope_fwd(pk, kw[None], cos_k, sin_k, "k_norm_rope")
    vb = pv.astype(BF16)
    o, lse = _attn_fwd(q, k, vb)
    return o, (pq, pk, qw, kw, q, k, vb, o, lse)


def _gqa_bwd(res, do):
    pq, pk, qw, kw, q, k, vb, o, lse = res
    cos, sin, cos_k, sin_k = _gqa_tables(pq.shape[0], pk.shape[0])
    dq, dk, dv = _attention_bwd((q, k, vb, o, lse), do)
    dpq, dqw = _norm_rope_bwd(pq, qw[None], cos, sin, dq, "q_norm_rope_bwd")
    dpk, dkw = _norm_rope_bwd(pk, kw[None], cos_k, sin_k, dk, "k_norm_rope_bwd")
    return dpq, dpk, dv, dqw[0], dkw[0]


_gqa.defvjp(_gqa_fwd, _gqa_bwd)


def _mlstm_chunk_forward(q, k, v, lir, lfr, s0, n0, m0, reverse):
    L = q.shape[0]
    ti = lax.broadcasted_iota(jnp.int32, (L, L), 0)
    si = lax.broadcasted_iota(jnp.int32, (L, L), 1)
    seen = (si >= ti) if reverse else (si <= ti)
    seen_t = (ti >= si) if reverse else (ti <= si)
    eye = ti == si
    lic = jnp.sum(jnp.where(eye, lir, 0.0), axis=1, keepdims=True)
    lfc = jnp.sum(jnp.where(eye, lfr, 0.0), axis=1, keepdims=True)
    b_col = jnp.sum(jnp.where(seen, lfr, 0.0), axis=1, keepdims=True)
    b_row = jnp.sum(jnp.where(seen_t, lfc, 0.0), axis=0, keepdims=True)
    d = jnp.where(seen, b_col - b_row + lir, -jnp.inf)
    m = jnp.maximum(b_col + m0, jnp.max(d, axis=1, keepdims=True))
    w = jnp.exp(d - m)
    a = jnp.exp(b_col + m0 - m)
    qm, km, vm = q.astype(MX), k.astype(MX), v.astype(MX)
    s = _dot_nt(qm, km) * w
    qs = _dot(qm, s0.astype(MX))
    num = a * qs + _dot(s.astype(MX), vm)
    qn = jnp.sum(q * n0, axis=1, keepdims=True)
    den = a * qn + jnp.sum(s, axis=1, keepdims=True)
    floor = jnp.exp(-m)
    dd = jnp.maximum(jnp.abs(den), floor)
    b_last = jnp.sum(lfr, axis=1, keepdims=True)
    m_end = jnp.maximum(b_last + m0, jnp.max(b_last - b_row + lir, axis=1, keepdims=True))
    w_end = jnp.exp(b_last - b_col + lic - m_end)
    a_end = jnp.exp(b_last + m0 - m_end)
    return dict(eye=eye, seen=seen, w=w, a=a, s=s, qs=qs, num=num, qn=qn, den=den, floor=floor, dd=dd,
                m_end=m_end, w_end=w_end, a_end=a_end, qm=qm, km=km, vm=vm)


def _mlstm_fwd_call(q, k, v, gr, n, row_off, reverse):
    L = MLSTM_CHUNK
    nc, off = n // L, row_off // L
    pos = (lambda i: nc - 1 - i) if reverse else (lambda i: i)

    def body(q_ref, k_ref, v_ref, gr_ref, h_ref, s0_ref, n0_ref, m0_ref, s_sc, n_sc, m_sc):
        @pl.when(pl.program_id(1) == 0)
        def _():
            s_sc[...] = jnp.zeros(s_sc.shape, F32)
            n_sc[...] = jnp.zeros(n_sc.shape, F32)
            m_sc[...] = jnp.full(m_sc.shape, M_INIT, F32)

        s0, n0, m0 = s_sc[...], n_sc[...], m_sc[...]
        s0_ref[0, 0] = s0
        n0_ref[0, 0] = n0
        m0_ref[0, 0] = jnp.broadcast_to(m0, (1, DK_A))
        k, v = k_ref[...], v_ref[...]
        f = _mlstm_chunk_forward(q_ref[...], k, v, gr_ref[0, 0], gr_ref[1, 0], s0, n0, m0, reverse)
        h_ref[...] = f["num"] / f["dd"]
        s_sc[...] = f["a_end"] * s0 + _dot_tn(f["km"], (f["w_end"] * v).astype(MX))
        n_sc[...] = f["a_end"] * n0 + jnp.sum(f["w_end"] * k, axis=0, keepdims=True)
        m_sc[...] = f["m_end"]

    qk_spec = pl.BlockSpec((L, DK_A), lambda h, i: (off + pos(i), h))
    v_spec = pl.BlockSpec((L, DV_A), lambda h, i: (off + pos(i), h))
    gr_spec = pl.BlockSpec((2, 1, 1, L), lambda h, i: (0, h, 0, off + pos(i)))
    h_spec = pl.BlockSpec((L, DV_A), lambda h, i: (pos(i), h))
    st_spec = pl.BlockSpec((1, 1, DK_A, DV_A), lambda h, i: (h, pos(i), 0, 0))
    vec_spec = pl.BlockSpec((1, 1, 1, DK_A), lambda h, i: (h, pos(i), 0, 0))
    return pl.pallas_call(
        body, grid=(NH_A, nc),
        in_specs=[qk_spec, qk_spec, v_spec, gr_spec],
        out_specs=[h_spec, st_spec, vec_spec, vec_spec],
        out_shape=[jax.ShapeDtypeStruct((n, V_A), F32), jax.ShapeDtypeStruct((NH_A, nc, DK_A, DV_A), F32),
                   jax.ShapeDtypeStruct((NH_A, nc, 1, DK_A), F32), jax.ShapeDtypeStruct((NH_A, nc, 1, DK_A), F32)],
        scratch_shapes=[pltpu.VMEM((DK_A, DV_A), F32), pltpu.VMEM((1, DK_A), F32), pltpu.VMEM((1, 1), F32)],
        compiler_params=pltpu.CompilerParams(dimension_semantics=("parallel", "arbitrary"),
                                             vmem_limit_bytes=VMEM_LIMIT),
        name="mlstm_fwd")(q, k, v, gr)


def _mlstm_bwd_call(q, k, v, gr, s0_all, n0_all, m0_all, dh, n, row_off, reverse):
    L = MLSTM_CHUNK
    nc, off = n // L, row_off // L
    pos = (lambda i: i) if reverse else (lambda i: nc - 1 - i)

    def body(q_ref, k_ref, v_ref, gr_ref, s0_ref, n0_ref, m0_ref, dh_ref,
             dq_ref, dk_ref, dv_ref, dg_ref, ds_sc, dn_sc):
        @pl.when(pl.program_id(1) == 0)
        def _():
            ds_sc[...] = jnp.zeros(ds_sc.shape, F32)
            dn_sc[...] = jnp.zeros(dn_sc.shape, F32)

        q, k, v = q_ref[...], k_ref[...], v_ref[...]
        s0, n0, m0 = s0_ref[0, 0], n0_ref[0, 0], m0_ref[0, 0][:, 0:1]
        f = _mlstm_chunk_forward(q, k, v, gr_ref[0, 0], gr_ref[1, 0], s0, n0, m0, reverse)
        w, a, s = f["w"], f["a"], f["s"]
        qm, km, vm, w_end, a_end = f["qm"], f["km"], f["vm"], f["w_end"], f["a_end"]
        ds1, dn1 = ds_sc[...], dn_sc[...]
        ds1m, s0m = ds1.astype(MX), s0.astype(MX)

        inv = 1.0 / f["dd"]
        dh = dh_ref[...]
        dnum = dh * inv
        ddd = -jnp.sum(dh * (f["num"] * inv), axis=1, keepdims=True) * inv
        dden = jnp.where(jnp.abs(f["den"]) > f["floor"], jnp.sign(f["den"]) * ddd, 0.0)
        adn = (a * dnum).astype(MX)
        dnm = dnum.astype(MX)
        ds_tot = _dot_nt(dnm, vm) + dden
        dsr = (ds_tot * w).astype(MX)
        e = ds_tot * s
        wv = (w_end * v).astype(MX)
        kds = _dot(km, ds1m)
        dq_ref[...] = _dot_nt(adn, s0m) + _dot(dsr, km) + (dden * a) * n0
        dk_ref[...] = _dot_tn(dsr, qm) + _dot_nt(wv, ds1m) + w_end * dn1
        dv_ref[...] = _dot_tn(s.astype(MX), dnm) + w_end * kds

        eye = f["eye"]
        to_col = lambda r: jnp.sum(jnp.where(eye, r, 0.0), axis=1, keepdims=True)
        to_row = lambda c: jnp.sum(jnp.where(eye, c, 0.0), axis=0, keepdims=True)
        g_a = (jnp.sum(dnum * f["qs"], axis=1, keepdims=True) + dden * f["qn"]) * a
        g_w = (jnp.sum(v * kds, axis=1, keepdims=True) + jnp.sum(k * dn1, axis=1, keepdims=True)) * w_end
        g_end = (jnp.sum(jnp.sum(ds1 * s0, axis=1, keepdims=True), axis=0, keepdims=True)
                 + jnp.sum(dn1 * n0, axis=1, keepdims=True)) * a_end
        col_e = jnp.sum(e, axis=0, keepdims=True)
        db = jnp.sum(e, axis=1, keepdims=True) - to_col(col_e) + g_a - g_w
        last = lax.broadcasted_iota(jnp.int32, (L, 1), 0) == (0 if reverse else L - 1)
        db = db + jnp.where(last, jnp.sum(g_w, axis=0, keepdims=True) + g_end, 0.0)
        dg_ref[0, 0] = col_e + to_row(g_w)
        dg_ref[1, 0] = jnp.sum(jnp.where(f["seen"], db, 0.0), axis=0, keepdims=True)

        ds_sc[...] = a_end * ds1 + _dot_tn(qm, adn)
        dn_sc[...] = a_end * dn1 + jnp.sum((dden * a) * q, axis=0, keepdims=True)

    qk_spec = pl.BlockSpec((L, DK_A), lambda h, i: (off + pos(i), h))
    v_spec = pl.BlockSpec((L, DV_A), lambda h, i: (off + pos(i), h))
    gr_spec = pl.BlockSpec((2, 1, 1, L), lambda h, i: (0, h, 0, off + pos(i)))
    st_spec = pl.BlockSpec((1, 1, DK_A, DV_A), lambda h, i: (h, pos(i), 0, 0))
    vec_spec = pl.BlockSpec((1, 1, 1, DK_A), lambda h, i: (h, pos(i), 0, 0))
    oqk_spec = pl.BlockSpec((L, DK_A), lambda h, i: (pos(i), h))
    ov_spec = pl.BlockSpec((L, DV_A), lambda h, i: (pos(i), h))
    og_spec = pl.BlockSpec((2, 1, 1, L), lambda h, i: (0, h, 0, pos(i)))
    return pl.pallas_call(
        body, grid=(NH_A, nc),
        in_specs=[qk_spec, qk_spec, v_spec, gr_spec, st_spec, vec_spec, vec_spec, ov_spec],
        out_specs=[oqk_spec, oqk_spec, ov_spec, og_spec],
        out_shape=[jax.ShapeDtypeStruct((n, QK_A), F32), jax.ShapeDtypeStruct((n, QK_A), F32),
                   jax.ShapeDtypeStruct((n, V_A), F32), jax.ShapeDtypeStruct((2, NH_A, 1, n), F32)],
        scratch_shapes=[pltpu.VMEM((DK_A, DV_A), F32), pltpu.VMEM((1, DK_A), F32)],
        compiler_params=pltpu.CompilerParams(dimension_semantics=("parallel", "arbitrary"),
                                             vmem_limit_bytes=VMEM_LIMIT),
        name="mlstm_bwd")(q, k, v, gr, s0_all, n0_all, m0_all, dh)


def _make_mlstm(n, row_off, reverse):
    def gate_rows(li, lf):
        return jnp.stack([li, lf]).transpose(0, 2, 1)[:, :, None, :]

    @jax.custom_vjp
    def op(q, k, v, li, lf):
        return _mlstm_fwd_call(q, k, v, gate_rows(li, lf), n, row_off, reverse)[0]

    def fwd(q, k, v, li, lf):
        gr = gate_rows(li, lf)
        h, s0, n0, m0 = _mlstm_fwd_call(q, k, v, gr, n, row_off, reverse)
        return h, (q, k, v, gr, s0, n0, m0)

    def bwd(res, dh):
        q, k, v, gr, s0, n0, m0 = res
        dq, dk, dv, dg = _mlstm_bwd_call(q, k, v, gr, s0, n0, m0, dh, n, row_off, reverse)
        rows = ((row_off, q.shape[0] - row_off - n), (0, 0))
        dg = jnp.pad(dg[:, :, 0, :].transpose(0, 2, 1), ((0, 0),) + rows)
        return jnp.pad(dq, rows), jnp.pad(dk, rows), jnp.pad(dv, rows), dg[0], dg[1]

    op.defvjp(fwd, bwd)
    return op


def _silu(x):
    return x * jax.nn.sigmoid(x)


def _ln_plain(x):
    mu = jnp.mean(x, -1, keepdims=True)
    var = jnp.mean(jnp.square(x - mu), -1, keepdims=True)
    return (x - mu) * lax.rsqrt(var + EPS)


def _rms(x, w):
    return x * lax.rsqrt(jnp.mean(jnp.square(x), -1, keepdims=True) + EPS) * w


_mm_mod, _mm_if, _mm_ba, _mm_bb, _mm_out = (_make_mm(t) for t in ("mod", "if", "ba", "bb", "out"))


def _local_loss(diff, const):
    x, c, ctx, target = diff["x"], const["c"], const["ctx"], const["target"]
    t, tc = x.shape[0], ctx.shape[0]
    n, r = tc + t, t + 2 * tc

    sc = jnp.concatenate([_silu(c), _silu(diff["c_ctx"])[None], jnp.zeros((14, D_MODEL), F32)], 0)
    mod = _mm_mod(sc, const["w_mod"], diff["p_mod"])[:2] + diff["b_mod"]
    shift, scale, gate = mod[0, :D_MODEL], mod[0, D_MODEL:2 * D_MODEL], mod[0, 2 * D_MODEL:]
    shift_c, scale_c = mod[1, :D_MODEL], mod[1, D_MODEL:2 * D_MODEL]
    u = _ln_plain(x) * (1 + scale) + shift
    u_c = _ln_plain(ctx) * (1 + scale_c) + shift_c
    u_all = jnp.concatenate([u_c, u, u_c], 0)

    p_qk, p_va, p_kv, p_oa, p_za, p_qb, p_zb, p_ga, p_gb = _project(u_all, const["w_main"], diff["p_main"])
    gt = _mm_if(u_all, const["w_if"], diff["p_if"])[:, :N_IF] + diff["b_if"]

    qk_pre = p_qk
    rows = jnp.arange(r)
    seg_start = ((rows == 0) | (rows == tc) | (rows == n))[:, None]
    seg_end = ((rows == tc - 1) | (rows == n - 1) | (rows == r - 1))[:, None]
    prev = jnp.where(seg_start, 0.0, jnp.roll(qk_pre, 1, axis=0))
    nxt = jnp.where(seg_end, 0.0, jnp.roll(qk_pre, -1, axis=0))
    cw, cb = diff["conv_w"], diff["conv_b"]
    qk = _silu(cb + prev * cw[0] + qk_pre * cw[1] + nxt * cw[2])
    q_a, k_a = qk[:, :QK_A], qk[:, QK_A:] * (DK_A ** -0.5)
    v_a = p_va
    li_f, lf_f = gt[:, 0:8], jax.nn.log_sigmoid(gt[:, 8:16])
    li_b, lf_b = gt[:, 16:24], jax.nn.log_sigmoid(gt[:, 24:32])

    h_f = _make_mlstm(n, 0, False)(q_a, k_a, v_a, li_f, lf_f)[tc:]
    h_b = _make_mlstm(n, tc, True)(q_a, k_a, v_a, li_b, lf_b)[:t]
    h_l = (h_f + h_b).reshape(t, NH_A, DV_A)

    lat = slice(tc, n)
    o_attn = _gqa(p_qb[lat], p_kv[:n, :KV_B], p_kv[:n, KV_B:],
                  diff["q_norm_w"], diff["k_norm_w"])

    hn = _rms(h_l, diff["mh_norm_w"].reshape(NH_A, DV_A)).reshape(t, V_A)
    a_in = jax.nn.sigmoid(p_oa[lat]) * hn * _silu(p_za[lat])
    y_a = _mm_ba(a_in, const["w_ba"], diff["p_ba"])
    b_in = o_attn * _silu(p_zb[lat])
    y_b = _mm_bb(b_in, const["w_bb"], diff["p_bb"])
    m_in = jax.nn.sigmoid(p_ga[lat]) * y_a + jax.nn.sigmoid(p_gb[lat]) * y_b
    out = _mm_out(m_in, const["w_out"], diff["p_out"])

    y = _ln_plain(ALPHA * x + gate * out) * diff["ln_w"] + diff["ln_b"]
    return 0.5 * jnp.sum(jnp.mean(jnp.square(y - target), axis=-1))


OTHER_CHIPS = [(1, 0), (0, 1), (1, 1)]


def _flip(v, bit):
    return 1 - v if bit else v


def _gather_chips(shard, name):
    def body(x_ref, o_ref, send_sems, recv_sems, local_sem):
        x, y, c = lax.axis_index("x"), lax.axis_index("y"), lax.axis_index("c")
        mine = pltpu.make_async_copy(x_ref, o_ref.at[2 * x + y], local_sem)
        mine.start()

        def copy(r, slot):
            dx, dy = OTHER_CHIPS[r]
            return pltpu.make_async_remote_copy(
                src_ref=x_ref, dst_ref=o_ref.at[slot], send_sem=send_sems.at[r], recv_sem=recv_sems.at[r],
                device_id=(_flip(x, dx), _flip(y, dy), c), device_id_type=MESH)

        sends = [copy(r, 2 * x + y) for r in range(3)]
        for cp in sends:
            cp.start()
        for r, (dx, dy) in enumerate(OTHER_CHIPS):
            copy(r, 2 * _flip(x, dx) + _flip(y, dy)).wait_recv()
        for cp in sends:
            cp.wait_send()
        mine.wait()

    return pl.pallas_call(
        body, out_shape=jax.ShapeDtypeStruct((N_CHIPS,) + shard.shape, shard.dtype),
        in_specs=[pl.BlockSpec(memory_space=pl.ANY)], out_specs=pl.BlockSpec(memory_space=pl.ANY),
        scratch_shapes=[pltpu.SemaphoreType.DMA((3,)), pltpu.SemaphoreType.DMA((3,)), pltpu.SemaphoreType.DMA],
        name=name)(shard)


def _gather_chips_halves(shard, name):
    rows, cols = shard.shape
    halves = shard.reshape(2, rows // 2, cols)

    def body(x_ref, o_ref, send_sems, recv_sems, local_sem):
        x, y, c = lax.axis_index("x"), lax.axis_index("y"), lax.axis_index("c")
        my_chip = 2 * x + y
        mine = pltpu.make_async_copy(x_ref, o_ref.at[my_chip], local_sem)
        mine.start()

        def chip_of(r):
            dx, dy = OTHER_CHIPS[r]
            return _flip(x, dx), _flip(y, dy)

        def copy(k, chip_slot, half, to, src=None):
            dst = o_ref.at[chip_slot, half]
            return pltpu.make_async_remote_copy(
                src_ref=dst if src is None else src, dst_ref=dst, send_sem=send_sems.at[k],
                recv_sem=recv_sems.at[k], device_id=to, device_id_type=MESH)

        first = [copy(r, my_chip, c, (*chip_of(r), c), src=x_ref.at[c]) for r in range(3)]
        for cp in first:
            cp.start()
        passed = []
        for r in range(3):
            px, py = chip_of(r)
            copy(r, 2 * px + py, c, (px, py, c)).wait_recv()
            passed.append(copy(3 + r, 2 * px + py, c, (x, y, 1 - c)))
            passed[-1].start()
        for r in range(3):
            px, py = chip_of(r)
            copy(3 + r, 2 * px + py, 1 - c, (x, y, 1 - c)).wait_recv()
        for cp in first + passed:
            cp.wait_send()
        mine.wait()

    out = pl.pallas_call(
        body, out_shape=jax.ShapeDtypeStruct((N_CHIPS, 2, rows // 2, cols), shard.dtype),
        in_specs=[pl.BlockSpec(memory_space=pl.ANY)], out_specs=pl.BlockSpec(memory_space=pl.ANY),
        scratch_shapes=[pltpu.SemaphoreType.DMA((6,)), pltpu.SemaphoreType.DMA((6,)), pltpu.SemaphoreType.DMA],
        name=name)(halves)
    return out.reshape(N_CHIPS, rows, cols)


def _scatter_grads(slots, name):
    def body(g_ref, o_ref, send_sems, recv_sems, local_sem):
        x, y, c = lax.axis_index("x"), lax.axis_index("y"), lax.axis_index("c")
        me, my_chip, sibling = 4 * x + 2 * y + c, 2 * x + y, (x, y, 1 - c)
        mine = pltpu.make_async_copy(g_ref.at[my_chip], o_ref.at[me], local_sem)
        mine.start()

        def chip_of(r):
            dx, dy = OTHER_CHIPS[r]
            return _flip(x, dx), _flip(y, dy)

        def copy(k, slot, to, src=None):
            dst = o_ref.at[slot]
            return pltpu.make_async_remote_copy(
                src_ref=dst if src is None else src, dst_ref=dst, send_sem=send_sems.at[k],
                recv_sem=recv_sems.at[k], device_id=to, device_id_type=MESH)

        first = [copy(0, me, sibling, src=g_ref.at[my_chip])]
        for r in range(3):
            px, py = chip_of(r)
            first.append(copy(1 + r, me, (px, py, c), src=g_ref.at[2 * px + py]))
        for cp in first:
            cp.start()
        passed = []
        for r in range(3):
            px, py = chip_of(r)
            copy(1 + r, 4 * px + 2 * py + c, (px, py, c)).wait_recv()
            passed.append(copy(4 + r, 4 * px + 2 * py + c, sibling))
            passed[-1].start()
        copy(0, 4 * x + 2 * y + 1 - c, sibling).wait_recv()
        for r in range(3):
            px, py = chip_of(r)
            copy(4 + r, 4 * px + 2 * py + 1 - c, sibling).wait_recv()
        for cp in first + passed:
            cp.wait_send()
        mine.wait()

    return pl.pallas_call(
        body, out_shape=jax.ShapeDtypeStruct((N_DEV,) + slots.shape[1:], slots.dtype),
        in_specs=[pl.BlockSpec(memory_space=pl.ANY)], out_specs=pl.BlockSpec(memory_space=pl.ANY),
        scratch_shapes=[pltpu.SemaphoreType.DMA((N_DEV - 1,)), pltpu.SemaphoreType.DMA((N_DEV - 1,)),
                        pltpu.SemaphoreType.DMA],
        name=name)(slots)


def _allreduce_small(v, name):
    def body(v_ref, o_ref, buf, send_sems, recv_sems):
        x, y, c = lax.axis_index("x"), lax.axis_index("y"), lax.axis_index("c")
        me = 4 * x + 2 * y + c
        buf[me] = v_ref[...]

        def peer(r):
            return _flip(x, (r >> 2) & 1), _flip(y, (r >> 1) & 1), _flip(c, r & 1)

        def copy(r, dst_slot):
            return pltpu.make_async_remote_copy(
                src_ref=v_ref, dst_ref=buf.at[dst_slot], send_sem=send_sems.at[r - 1],
                recv_sem=recv_sems.at[r - 1], device_id=peer(r), device_id_type=MESH)

        sends = [copy(r, me) for r in range(1, N_DEV)]
        for cp in sends:
            cp.start()
        for r in range(1, N_DEV):
            px, py, pc = peer(r)
            copy(r, 4 * px + 2 * py + pc).wait_recv()
        for cp in sends:
            cp.wait_send()
        acc = buf[0]
        for d in range(1, N_DEV):
            acc = acc + buf[d]
        o_ref[...] = acc

    return pl.pallas_call(
        body, out_shape=jax.ShapeDtypeStruct(v.shape, v.dtype),
        in_specs=[pl.BlockSpec(memory_space=pltpu.VMEM)], out_specs=pl.BlockSpec(memory_space=pltpu.VMEM),
        scratch_shapes=[pltpu.VMEM((N_DEV,) + v.shape, v.dtype), pltpu.SemaphoreType.DMA((N_DEV - 1,)),
                        pltpu.SemaphoreType.DMA((N_DEV - 1,))],
        name=name)(v)


def _adamw_math(w, g, m, v):
    m = ADAM_B1 * m + (1.0 - ADAM_B1) * g
    v = ADAM_B2 * v + (1.0 - ADAM_B2) * jnp.square(g)
    m_hat = m / (1.0 - ADAM_B1 ** ADAM_STEP)
    v_hat = v / (1.0 - ADAM_B2 ** ADAM_STEP)
    delta = -ADAM_LR * (m_hat / (jnp.sqrt(v_hat) + ADAM_EPS) + ADAM_WD * w)
    return delta, m, v


def _adamw_sum(parts, w, m, v, name):
    npart, rows, cols = parts.shape
    tr = _pick(rows, (64, 32, 16, 8)) if rows >= 8 else rows

    def body(p_ref, w_ref, m_ref, v_ref, g_out, d_out, m_out, v_out):
        g = p_ref[0].astype(F32)
        for k in range(1, npart):
            g = g + p_ref[k].astype(F32)
        d, m2, v2 = _adamw_math(w_ref[...], g, m_ref[...], v_ref[...])
        g_out[...] = g
        d_out[...] = d
        m_out[...] = m2
        v_out[...] = v2

    spec = pl.BlockSpec((tr, cols), lambda i: (i, 0))
    shp = jax.ShapeDtypeStruct((rows, cols), F32)
    return pl.pallas_call(
        body, grid=(rows // tr,),
        in_specs=[pl.BlockSpec((npart, tr, cols), lambda i: (0, i, 0)), spec, spec, spec],
        out_specs=[spec, spec, spec, spec], out_shape=[shp, shp, shp, shp],
        compiler_params=pltpu.CompilerParams(dimension_semantics=("parallel",), vmem_limit_bytes=VMEM_LIMIT),
        name=name)(parts, w, m, v)


SMALL_ROWS = 16


def _pack_small(c_ctx, b_mod, conv_b, mh, ln_w, ln_b, conv_w_rows, b_if, qn, kn):
    last = jnp.concatenate([b_if.reshape(-1), qn.reshape(-1), kn.reshape(-1),
                            jnp.zeros((D_MODEL - N_IF - 2 * HD_B,), F32)])
    rows = [c_ctx.reshape(1, D_MODEL), b_mod.reshape(3, D_MODEL), conv_b.reshape(1, D_MODEL),
            mh.reshape(1, D_MODEL), ln_w.reshape(1, D_MODEL), ln_b.reshape(1, D_MODEL),
            conv_w_rows.reshape(3, D_MODEL), last[None], jnp.zeros((SMALL_ROWS - 12, D_MODEL), F32)]
    return jnp.concatenate(rows, 0)


def _unpack_small(pk, conv_cols):
    return dict(c_ctx=pk[0], b_mod=pk[1:4].reshape(1, 3 * D_MODEL), conv_b=pk[4:5], mh_norm_w=pk[5:6],
                ln_w=pk[6:7], ln_b=pk[7:8], conv_w=pk[8:11, :conv_cols][None], b_if=pk[11:12, :N_IF],
                q_norm_w=pk[11:12, N_IF:N_IF + HD_B], k_norm_w=pk[11:12, N_IF + HD_B:N_IF + 2 * HD_B])


def kernel(x, c, ctx, c_ctx, w_mod, b_mod, w_in, b_if, conv_w, conv_b, mh_norm_w, q_norm_w, k_norm_w, w_branch_a, w_branch_b, w_out, ln_w, ln_b, loss_target, m_c_ctx, m_w_mod, m_b_mod, m_w_in, m_b_if, m_conv_w, m_conv_b, m_mh_norm_w, m_q_norm_w, m_k_norm_w, m_w_branch_a, m_w_branch_b, m_w_out, m_ln_w, m_ln_b, v_c_ctx, v_w_mod, v_b_mod, v_w_in, v_b_if, v_conv_w, v_conv_b, v_mh_norm_w, v_q_norm_w, v_k_norm_w, v_w_branch_a, v_w_branch_b, v_w_out, v_ln_w, v_ln_b):
    chip = 2 * lax.axis_index("x") + lax.axis_index("y")
    mod_cols, in_cols, conv_cols = w_mod.shape[2], w_in.shape[2], conv_w.shape[2]
    br_rows = w_out.shape[1]

    g_mod = _gather_chips_halves(w_mod[0].astype(BF16), "gather_w_mod")
    g_in = _gather_chips_halves(w_in[0].astype(BF16), "gather_w_in")
    g_ba = _gather_chips_halves(w_branch_a[0].astype(BF16), "gather_w_ba")
    g_bb = _gather_chips_halves(w_branch_b[0].astype(BF16), "gather_w_bb")
    g_out = _gather_chips_halves(w_out[0].astype(BF16), "gather_w_out")
    g_conv = _gather_chips(conv_w[0], "gather_conv_w")
    w_mod_full = jnp.moveaxis(g_mod, 0, 1).reshape(D_MODEL, N_CHIPS * mod_cols)
    w_in_full = jnp.moveaxis(g_in, 0, 1).reshape(D_MODEL, N_CHIPS * in_cols)
    w_main = jnp.concatenate([w_in_full[:, :IF_START], w_in_full[:, IF_START + N_IF:]], 1)
    w_if = jnp.pad(w_in_full[:, IF_START:IF_START + N_IF], ((0, 0), (0, IF_PAD - N_IF)))
    conv_w_full = jnp.moveaxis(g_conv, 0, 1).reshape(3, N_CHIPS * conv_cols)

    const = dict(c=c, ctx=ctx[0], target=loss_target[0], w_mod=w_mod_full, w_main=w_main, w_if=w_if,
                 w_ba=g_ba.reshape(D_MODEL, D_MODEL), w_bb=g_bb.reshape(D_MODEL, D_MODEL),
                 w_out=g_out.reshape(D_MODEL, D_MODEL))
    diff = dict(x=x[0], c_ctx=c_ctx, b_mod=b_mod[0], b_if=b_if[0], conv_w=conv_w_full, conv_b=conv_b[0],
                mh_norm_w=mh_norm_w[0], q_norm_w=q_norm_w[0], k_norm_w=k_norm_w[0], ln_w=ln_w[0], ln_b=ln_b[0],
                p_mod=jnp.zeros(w_mod_full.shape, F32), p_main=jnp.zeros(w_main.shape, F32),
                p_if=jnp.zeros(w_if.shape, F32), p_ba=jnp.zeros((D_MODEL, D_MODEL), F32),
                p_bb=jnp.zeros((D_MODEL, D_MODEL), F32), p_out=jnp.zeros((D_MODEL, D_MODEL), F32))
    loss_local, g = jax.value_and_grad(_local_loss)(diff, const)
    loss = lax.psum(loss_local, ("x", "y", "c"))

    g_small = _allreduce_small(
        _pack_small(g["c_ctx"], g["b_mod"], g["conv_b"], g["mh_norm_w"], g["ln_w"], g["ln_b"], g["conv_w"],
                    g["b_if"], g["q_norm_w"], g["k_norm_w"]), "allreduce_small")
    conv_g = lax.dynamic_slice(g_small[8:11], (0, chip * conv_cols), (3, conv_cols))
    g_small = g_small.at[8:11].set(jnp.pad(conv_g, ((0, 0), (0, D_MODEL - conv_cols))))
    pad_conv = lambda a: jnp.pad(a[0], ((0, 0), (0, D_MODEL - conv_cols)))
    packed = [_pack_small(cc, bm[0], cb[0], mh[0], lw[0], lb[0], pad_conv(cw), bi[0], qn[0], kn[0])
              for cc, bm, cb, mh, lw, lb, cw, bi, qn, kn in (
                  (c_ctx, b_mod, conv_b, mh_norm_w, ln_w, ln_b, conv_w, b_if, q_norm_w, k_norm_w),
                  (m_c_ctx, m_b_mod, m_conv_b, m_mh_norm_w, m_ln_w, m_ln_b, m_conv_w, m_b_if, m_q_norm_w, m_k_norm_w),
                  (v_c_ctx, v_b_mod, v_conv_b, v_mh_norm_w, v_ln_w, v_ln_b, v_conv_w, v_b_if, v_q_norm_w, v_k_norm_w))]
    small = [_unpack_small(a, conv_cols)
             for a in _adamw_sum(g_small[None], packed[0], packed[1], packed[2], "adamw_small")]

    def col_slots(gfull, cols):
        return jnp.moveaxis(gfull.reshape(D_MODEL, N_CHIPS, cols), 1, 0).astype(BF16)

    g_in_full = jnp.concatenate([g["p_main"][:, :IF_START], g["p_if"][:, :N_IF], g["p_main"][:, IF_START:]], 1)
    big = {}
    for nm, slots, w_, m_, v_ in (
            ("w_mod", col_slots(g["p_mod"], mod_cols), w_mod, m_w_mod, v_w_mod),
            ("w_in", col_slots(g_in_full, in_cols), w_in, m_w_in, v_w_in),
            ("w_branch_a", g["p_ba"].reshape(N_CHIPS, br_rows, D_MODEL).astype(BF16), w_branch_a, m_w_branch_a, v_w_branch_a),
            ("w_branch_b", g["p_bb"].reshape(N_CHIPS, br_rows, D_MODEL).astype(BF16), w_branch_b, m_w_branch_b, v_w_branch_b),
            ("w_out", g["p_out"].reshape(N_CHIPS, br_rows, D_MODEL).astype(BF16), w_out, m_w_out, v_w_out)):
        parts = _scatter_grads(slots, "scatter_" + nm)
        big[nm] = [a[None] for a in _adamw_sum(parts, w_[0], m_[0], v_[0], "adamw_" + nm)]

    names = ["c_ctx", "w_mod", "b_mod", "w_in", "b_if", "conv_w", "conv_b", "mh_norm_w", "q_norm_w", "k_norm_w",
             "w_branch_a", "w_branch_b", "w_out", "ln_w", "ln_b"]
    outs = [[big[nm][k] if nm in big else small[k][nm] for nm in names] for k in range(4)]
    return (loss, g["x"][None], *outs[0], *outs[1], *outs[2], *outs[3])
```

```python
import functools

import jax
import jax.numpy as jnp
from jax import lax
from jax.experimental import pallas as pl
from jax.experimental.pallas import tpu as pltpu

F32 = jnp.float32
BF16 = jnp.bfloat16
MESH = pl.DeviceIdType.MESH

D_MODEL = 2048
NH_A, DK_A, DV_A = 8, 128, 256
QK_A, V_A = NH_A * DK_A, NH_A * DV_A
NH_B, NKV_B, HD_B = 16, 4, 128
Q_B, KV_B = NH_B * HD_B, NKV_B * HD_B
GRID_W = 64
ROT_HALF = HD_B // 2
ROPE_THETA = 10000.0
M_INIT = -1e30
EPS = 1e-6
ALPHA = 2.0 ** 0.25
N_IN = 17440
IF_START, N_IF, IF_PAD = 4096, 32, 128
N_MAIN = N_IN - N_IF
O_QK, O_VA, O_KB, O_VB, O_OA, O_ZA, O_QB, O_ZB, O_GA, O_GB = (
    0, 2048, 4096, 4608, 5120, 7168, 9216, 11264, 13312, 15360)
MLSTM_CHUNK = 256

ADAM_LR, ADAM_B1, ADAM_B2, ADAM_EPS, ADAM_WD, ADAM_STEP = 0.001, 0.9, 0.999, 1e-08, 0.01, 10

VMEM_LIMIT = 48 * 1024 * 1024
N_CHIPS, N_DEV = 4, 8
MX = BF16


def _pick(n, cands):
    for c in cands:
        if n % c == 0:
            return c
    raise ValueError(f"no tile for {n} in {cands}")


def _dot(a, b):
    return jnp.dot(a, b, preferred_element_type=F32)


def _dot_nt(a, b):
    return lax.dot_general(a, b, (((1,), (1,)), ((), ())), preferred_element_type=F32)


def _dot_tn(a, b):
    return lax.dot_general(a, b, (((0,), (0,)), ((), ())), preferred_element_type=F32)


def _mm_nn(a, b, name):
    m, k = a.shape
    _, n = b.shape
    tm = _pick(m, (512, 256, 128, 64, 32, 16))
    tn = _pick(n, (1024, 512, 256, 128))

    def body(a_ref, b_ref, o_ref):
        o_ref[...] = _dot(a_ref[...], b_ref[...])

    return pl.pallas_call(
        body, grid=(m // tm, n // tn),
        in_specs=[pl.BlockSpec((tm, k), lambda i, j: (i, 0)), pl.BlockSpec((k, tn), lambda i, j: (0, j))],
        out_specs=pl.BlockSpec((tm, tn), lambda i, j: (i, j)),
        out_shape=jax.ShapeDtypeStruct((m, n), F32),
        compiler_params=pltpu.CompilerParams(dimension_semantics=("parallel", "parallel"),
                                             vmem_limit_bytes=VMEM_LIMIT),
        name=name)(a, b)


def _mm_nt(g, w, name):
    m, n = g.shape
    k, _ = w.shape
    tm = _pick(m, (512, 256, 128, 64, 32, 16))
    tn = _pick(n, (1024, 512, 256, 128))

    def body(g_ref, w_ref, o_ref):
        part = _dot_nt(g_ref[...].astype(BF16), w_ref[...])

        @pl.when(pl.program_id(1) == 0)
        def _():
            o_ref[...] = part

        @pl.when(pl.program_id(1) > 0)
        def _():
            o_ref[...] += part

    return pl.pallas_call(
        body, grid=(m // tm, n // tn),
        in_specs=[pl.BlockSpec((tm, tn), lambda i, j: (i, j)), pl.BlockSpec((k, tn), lambda i, j: (0, j))],
        out_specs=pl.BlockSpec((tm, k), lambda i, j: (i, 0)),
        out_shape=jax.ShapeDtypeStruct((m, k), F32),
        compiler_params=pltpu.CompilerParams(dimension_semantics=("parallel", "arbitrary"),
                                             vmem_limit_bytes=VMEM_LIMIT),
        name=name)(g, w)


def _mm_tn(a, g, name):
    m, k = a.shape
    _, n = g.shape
    tm = _pick(m, (512, 256, 128, 64, 32, 16))
    tn = _pick(n, (1024, 512, 256, 128))

    def body(a_ref, g_ref, o_ref):
        part = _dot_tn(a_ref[...], g_ref[...].astype(BF16))

        @pl.when(pl.program_id(1) == 0)
        def _():
            o_ref[...] = part

        @pl.when(pl.program_id(1) > 0)
        def _():
            o_ref[...] += part

    return pl.pallas_call(
        body, grid=(n // tn, m // tm),
        in_specs=[pl.BlockSpec((tm, k), lambda j, i: (i, 0)), pl.BlockSpec((tm, tn), lambda j, i: (i, j))],
        out_specs=pl.BlockSpec((k, tn), lambda j, i: (0, j)),
        out_shape=jax.ShapeDtypeStruct((k, n), F32),
        compiler_params=pltpu.CompilerParams(dimension_semantics=("parallel", "arbitrary"),
                                             vmem_limit_bytes=VMEM_LIMIT),
        name=name)(a, g)


def _make_mm(tag):
    @jax.custom_vjp
    def mm(a, w, proxy):
        del proxy
        return _mm_nn(a.astype(BF16), w, f"mm_{tag}_fwd")

    def fwd(a, w, proxy):
        del proxy
        ab = a.astype(BF16)
        return _mm_nn(ab, w, f"mm_{tag}_fwd"), (ab, w)

    def bwd(res, g):
        ab, w = res
        da = _mm_nt(g, w, f"mm_{tag}_da")
        dw = _mm_tn(ab, g, f"mm_{tag}_dw")
        return da, jnp.zeros_like(w), dw

    mm.defvjp(fwd, bwd)
    return mm


SLABS = (("qk", O_QK, 2 * QK_A), ("va", O_VA, V_A), ("kv", O_KB, 2 * KV_B), ("oa", O_OA, V_A), ("za", O_ZA, V_A),
         ("qb", O_QB, Q_B), ("zb", O_ZB, Q_B), ("ga", O_GA, D_MODEL), ("gb", O_GB, D_MODEL))
SLAB_FWD_TN = 1024
SLAB_TN = 512


def _slab_blocks():
    return [(off // SLAB_TN, (off + width) // SLAB_TN) for _, off, width in SLABS]


def _proj_fwd_slab(a, w, off, width, name):
    m, k = a.shape
    tm = _pick(m, (1088, 512, 256, 128))
    tn = min(SLAB_FWD_TN, width)

    def body(a_ref, b_ref, o_ref):
        o_ref[...] = _dot(a_ref[...], b_ref[...])

    return pl.pallas_call(
        body, grid=(m // tm, width // tn),
        in_specs=[pl.BlockSpec((tm, k), lambda i, j: (i, 0)), pl.BlockSpec((k, tn), lambda i, j: (0, j + off // tn))],
        out_specs=pl.BlockSpec((tm, tn), lambda i, j: (i, j)),
        out_shape=jax.ShapeDtypeStruct((m, width), F32),
        compiler_params=pltpu.CompilerParams(dimension_semantics=("parallel", "parallel"),
                                             vmem_limit_bytes=VMEM_LIMIT),
        name=name)(a, w)


def _slab_spec(tm, blocks, rows_inner):
    b, e = blocks

    def index(r, c):
        inside = (c >= b) & (c < e)
        return jnp.where(inside, r, 0), jnp.clip(c - b, 0, e - b - 1)

    if rows_inner:
        return pl.BlockSpec((tm, SLAB_TN), lambda c, r: index(r, c))
    return pl.BlockSpec((tm, SLAB_TN), lambda r, c: index(r, c))


def _proj_da(gs, w, name):
    m = gs[0].shape[0]
    k, n = w.shape
    tm = _pick(m, (512, 256, 128))
    blocks = _slab_blocks()

    def body(*refs):
        g_refs, w_ref, o_ref = refs[:len(blocks)], refs[len(blocks)], refs[len(blocks) + 1]
        c = pl.program_id(1)

        @pl.when(c == 0)
        def _():
            o_ref[...] = jnp.zeros(o_ref.shape, F32)

        for g_ref, (b, e) in zip(g_refs, blocks):
            @pl.when((c >= b) & (c < e))
            def _(g_ref=g_ref):
                o_ref[...] += _dot_nt(g_ref[...].astype(BF16), w_ref[...])

    return pl.pallas_call(
        body, grid=(m // tm, n // SLAB_TN),
        in_specs=[_slab_spec(tm, blk, False) for blk in blocks] + [pl.BlockSpec((k, SLAB_TN), lambda r, c: (0, c))],
        out_specs=pl.BlockSpec((tm, k), lambda r, c: (r, 0)),
        out_shape=jax.ShapeDtypeStruct((m, k), F32),
        compiler_params=pltpu.CompilerParams(dimension_semantics=("parallel", "arbitrary"),
                                             vmem_limit_bytes=VMEM_LIMIT),
        name=name)(*gs, w)


def _proj_dw(a, gs, n, name):
    m, k = a.shape
    tm = _pick(m, (512, 256, 128))
    blocks = _slab_blocks()

    def body(*refs):
        a_ref, g_refs, o_ref = refs[0], refs[1:1 + len(blocks)], refs[1 + len(blocks)]
        c, r = pl.program_id(0), pl.program_id(1)

        @pl.when(r == 0)
        def _():
            o_ref[...] = jnp.zeros(o_ref.shape, F32)

        for g_ref, (b, e) in zip(g_refs, blocks):
            @pl.when((c >= b) & (c < e))
            def _(g_ref=g_ref):
                o_ref[...] += _dot_tn(a_ref[...], g_ref[...].astype(BF16))

    return pl.pallas_call(
        body, grid=(n // SLAB_TN, m // tm),
        in_specs=[pl.BlockSpec((tm, k), lambda c, r: (r, 0))] + [_slab_spec(tm, blk, True) for blk in blocks],
        out_specs=pl.BlockSpec((k, SLAB_TN), lambda c, r: (0, c)),
        out_shape=jax.ShapeDtypeStruct((k, n), F32),
        compiler_params=pltpu.CompilerParams(dimension_semantics=("parallel", "arbitrary"),
                                             vmem_limit_bytes=VMEM_LIMIT),
        name=name)(a, *gs)


@jax.custom_vjp
def _project(u, w, proxy):
    return _project_fwd(u, w, proxy)[0]


def _project_fwd(u, w, proxy):
    del proxy
    ub = u.astype(BF16)
    return tuple(_proj_fwd_slab(ub, w, off, width, "proj_" + nm) for nm, off, width in SLABS), (ub, w)


def _project_bwd(res, gs):
    ub, w = res
    return _proj_da(gs, w, "proj_da"), jnp.zeros_like(w), _proj_dw(ub, gs, w.shape[1], "proj_dw")


_project.defvjp(_project_fwd, _project_bwd)


ATT_SCALE = HD_B ** -0.5
GROUP = NH_B // NKV_B


LOG2E, LN2 = 1.4426950408889634, 0.6931471805599453
ATT_C = ATT_SCALE * LOG2E
STRIP_Q, STRIP_K = 128, 256


def _attn_tiles(t, n):
    return _pick(t, (512, 256, 128)), _pick(n, (768, 512, 256))


def _attn_fwd(q, k, v):
    t, n = q.shape[0], k.shape[0]
    tq, tk = _attn_tiles(t, n)
    nk = n // tk

    def body(q_ref, k_ref, v_ref, o_ref, lse_ref, m_sc, acc_sc):
        j = pl.program_id(2)

        @pl.when(j == 0)
        def _():
            m_sc[...] = jnp.full(m_sc.shape, -jnp.inf, F32)
            acc_sc[...] = jnp.zeros(acc_sc.shape, F32)

        kb = k_ref[...]
        v_ones = jnp.concatenate([v_ref[...], jnp.ones((tk, HD_B), BF16)], axis=1)
        for g in range(GROUP):
            s2 = _dot_nt(q_ref[:, g * HD_B:(g + 1) * HD_B], kb) * ATT_C
            m_prev = m_sc[g]
            m_new = jnp.maximum(m_prev, jnp.max(s2, axis=-1, keepdims=True))
            p = jnp.exp2(s2 - m_new).astype(BF16)
            acc_sc[g] = jnp.exp2(m_prev - m_new) * acc_sc[g] + _dot(p, v_ones)
            m_sc[g] = m_new

        @pl.when(j == nk - 1)
        def _():
            for g in range(GROUP):
                cols = slice(g * HD_B, (g + 1) * HD_B)
                l = acc_sc[g, :, HD_B:]
                o_ref[:, cols] = acc_sc[g, :, :HD_B] / l
                lse_ref[:, cols] = m_sc[g] + jnp.log(l) * LOG2E

    qspec = pl.BlockSpec((tq, GROUP * HD_B), lambda kh, i, j: (i, kh))
    kspec = pl.BlockSpec((tk, HD_B), lambda kh, i, j: (j, kh))
    return pl.pallas_call(
        body, grid=(NKV_B, t // tq, nk),
        in_specs=[qspec, kspec, kspec], out_specs=[qspec, qspec],
        out_shape=[jax.ShapeDtypeStruct((t, Q_B), F32), jax.ShapeDtypeStruct((t, Q_B), F32)],
        scratch_shapes=[pltpu.VMEM((GROUP, tq, 1), F32), pltpu.VMEM((GROUP, tq, 2 * HD_B), F32)],
        compiler_params=pltpu.CompilerParams(dimension_semantics=("parallel", "parallel", "arbitrary"),
                                             vmem_limit_bytes=VMEM_LIMIT),
        name="attn_fwd")(q, k, v)


def _attn_dq(q, k, v, do, lse, delta):
    t, n = q.shape[0], k.shape[0]
    tq, tk = _attn_tiles(t, n)
    nk = n // tk

    def body(q_ref, k_ref, v_ref, do_ref, lse_ref, dl_ref, dq_ref):
        j = pl.program_id(2)
        kb, vb = k_ref[...], v_ref[...]
        parts = []
        for g in range(GROUP):
            cols = slice(g * HD_B, (g + 1) * HD_B)
            p = jnp.exp2(_dot_nt(q_ref[:, cols], kb) * ATT_C - lse_ref[:, g * HD_B:g * HD_B + 1])
            dp = _dot_nt(do_ref[:, cols], vb)
            ds = p * (dp - dl_ref[:, g * HD_B:g * HD_B + 1])
            parts.append(_dot(ds.astype(BF16), kb))

        @pl.when(j == 0)
        def _():
            for g in range(GROUP):
                dq_ref[:, g * HD_B:(g + 1) * HD_B] = parts[g]

        @pl.when(j > 0)
        def _():
            for g in range(GROUP):
                dq_ref[:, g * HD_B:(g + 1) * HD_B] += parts[g]

        @pl.when(j == nk - 1)
        def _():
            dq_ref[...] = dq_ref[...] * ATT_SCALE

    qspec = pl.BlockSpec((tq, GROUP * HD_B), lambda kh, i, j: (i, kh))
    kspec = pl.BlockSpec((tk, HD_B), lambda kh, i, j: (j, kh))
    return pl.pallas_call(
        body, grid=(NKV_B, t // tq, n // tk),
        in_specs=[qspec, kspec, kspec, qspec, qspec, qspec],
        out_specs=qspec,
        out_shape=jax.ShapeDtypeStruct((t, Q_B), F32),
        compiler_params=pltpu.CompilerParams(dimension_semantics=("parallel", "parallel", "arbitrary"),
                                             vmem_limit_bytes=VMEM_LIMIT),
        name="attn_dq")(q, k, v, do, lse, delta)


def _attn_dkv(q, k, v, do, lse_t, delta_t):
    t, n = q.shape[0], k.shape[0]
    tq, tk = _attn_tiles(t, n)
    nq = t // tq
    n_r, n_c = tq // STRIP_Q, tk // STRIP_K

    def body(q_ref, k_ref, v_ref, do_ref, lse_ref, dl_ref, dk_ref, dv_ref, dk_sc, dv_sc):
        i = pl.program_id(2)

        @pl.when(i == 0)
        def _():
            dk_sc[...] = jnp.zeros(dk_sc.shape, F32)
            dv_sc[...] = jnp.zeros(dv_sc.shape, F32)

        for r in range(n_r):
            rows = slice(r * STRIP_Q, (r + 1) * STRIP_Q)
            for c in range(n_c):
                kv = slice(c * STRIP_K, (c + 1) * STRIP_K)
                kc, vc = k_ref[kv, :], v_ref[kv, :]
                dk_part = dv_part = None
                for g in range(GROUP):
                    cols = slice(g * HD_B, (g + 1) * HD_B)
                    qg, dog = q_ref[rows, cols], do_ref[rows, cols]
                    st = _dot_nt(kc, qg)
                    pt = jnp.exp2(st * ATT_C - lse_ref[8 * g:8 * g + 1, rows])
                    dvg = _dot(pt.astype(BF16), dog)
                    dpt = _dot_nt(vc, dog)
                    dst = pt * (dpt - dl_ref[8 * g:8 * g + 1, rows])
                    dkg = _dot(dst.astype(BF16), qg)
                    dk_part = dkg if dk_part is None else dk_part + dkg
                    dv_part = dvg if dv_part is None else dv_part + dvg
                dk_sc[kv, :] += dk_part
                dv_sc[kv, :] += dv_part

        @pl.when(i == nq - 1)
        def _():
            dk_ref[...] = dk_sc[...] * ATT_SCALE
            dv_ref[...] = dv_sc[...]

    qspec = pl.BlockSpec((tq, GROUP * HD_B), lambda kh, j, i: (i, kh))
    tspec = pl.BlockSpec((8 * GROUP, tq), lambda kh, j, i: (kh, i))
    kspec = pl.BlockSpec((tk, HD_B), lambda kh, j, i: (j, kh))
    return pl.pallas_call(
        body, grid=(NKV_B, n // tk, nq),
        in_specs=[qspec, kspec, kspec, qspec, tspec, tspec],
        out_specs=[kspec, kspec],
        out_shape=[jax.ShapeDtypeStruct((n, KV_B), F32), jax.ShapeDtypeStruct((n, KV_B), F32)],
        scratch_shapes=[pltpu.VMEM((tk, HD_B), F32), pltpu.VMEM((tk, HD_B), F32)],
        compiler_params=pltpu.CompilerParams(dimension_semantics=("parallel", "parallel", "arbitrary"),
                                             vmem_limit_bytes=VMEM_LIMIT),
        name="attn_dkv")(q, k, v, do, lse_t, delta_t)


def _attention_bwd(res, do):
    qb, kb, vb, o, lse = res
    t = qb.shape[0]
    delta = jnp.sum((do * o).reshape(t, NH_B, HD_B), axis=-1)
    lse_h = lse.reshape(t, NH_B, HD_B)[:, :, 0]
    delta_b = jnp.broadcast_to(delta[:, :, None], (t, NH_B, HD_B)).reshape(t, Q_B)
    lse_t = jnp.broadcast_to(lse_h.T[:, None, :], (NH_B, 8, t)).reshape(NH_B * 8, t)
    delta_t = jnp.broadcast_to(delta.T[:, None, :], (NH_B, 8, t)).reshape(NH_B * 8, t)
    dob = do.astype(BF16)
    dq = _attn_dq(qb, kb, vb, dob, lse, delta_b)
    dk, dv = _attn_dkv(qb, kb, vb, dob, lse_t, delta_t)
    return dq, dk, dv


def _swap32(y):
    lane = lax.broadcasted_iota(jnp.int32, y.shape, 1)
    return jnp.where((lane // 32) % 2 == 0, pltpu.roll(y, 96, 1), pltpu.roll(y, 32, 1))


def _norm_rope_fwd(x, w, cos, sin, name):
    r, width = x.shape
    heads = width // HD_B
    tr = _pick(r, (256, 128))

    def body(x_ref, w_ref, c_ref, s_ref, o_ref):
        w, c, s = w_ref[...], c_ref[...], s_ref[...]
        for h in range(heads):
            cols = slice(h * HD_B, (h + 1) * HD_B)
            xh = x_ref[:, cols]
            y = xh * lax.rsqrt(jnp.mean(xh * xh, axis=-1, keepdims=True) + EPS) * w
            o_ref[:, cols] = (y * c + _swap32(y) * s).astype(o_ref.dtype)

    row = pl.BlockSpec((tr, width), lambda i: (i, 0))
    tab = pl.BlockSpec((tr, HD_B), lambda i: (i, 0))
    return pl.pallas_call(
        body, grid=(r // tr,),
        in_specs=[row, pl.BlockSpec((1, HD_B), lambda i: (0, 0)), tab, tab], out_specs=row,
        out_shape=jax.ShapeDtypeStruct((r, width), BF16),
        compiler_params=pltpu.CompilerParams(dimension_semantics=("parallel",), vmem_limit_bytes=VMEM_LIMIT),
        name=name)(x, w, cos, sin)


def _norm_rope_bwd(x, w, cos, sin, dy, name):
    r, width = x.shape
    heads = width // HD_B
    tr = _pick(r, (256, 128))

    def body(x_ref, w_ref, c_ref, s_ref, dy_ref, dx_ref, dw_ref):
        @pl.when(pl.program_id(0) == 0)
        def _():
            dw_ref[...] = jnp.zeros(dw_ref.shape, F32)

        w, c, s = w_ref[...], c_ref[...], s_ref[...]
        dw = jnp.zeros((1, HD_B), F32)
        for h in range(heads):
            cols = slice(h * HD_B, (h + 1) * HD_B)
            xh, dyh = x_ref[:, cols], dy_ref[:, cols]
            rs = lax.rsqrt(jnp.mean(xh * xh, axis=-1, keepdims=True) + EPS)
            dn = dyh * c + _swap32(dyh * s)
            dw = dw + jnp.sum(dn * (xh * rs), axis=0, keepdims=True)
            dxn = dn * w
            dx_ref[:, cols] = rs * dxn - xh * (rs * rs * rs * jnp.mean(dxn * xh, axis=-1, keepdims=True))
        dw_ref[...] += dw

    row = pl.BlockSpec((tr, width), lambda i: (i, 0))
    tab = pl.BlockSpec((tr, HD_B), lambda i: (i, 0))
    vec = pl.BlockSpec((1, HD_B), lambda i: (0, 0))
    return pl.pallas_call(
        body, grid=(r // tr,),
        in_specs=[row, vec, tab, tab, row], out_specs=[row, vec],
        out_shape=[jax.ShapeDtypeStruct((r, width), F32), jax.ShapeDtypeStruct((1, HD_B), F32)],
        compiler_params=pltpu.CompilerParams(dimension_semantics=("arbitrary",), vmem_limit_bytes=VMEM_LIMIT),
        name=name)(x, w, cos, sin, dy)


def _rope_tables(t):
    pos = jnp.arange(t)
    row = (pos // GRID_W).astype(F32)
    col = (pos % GRID_W).astype(F32)
    inv = ROPE_THETA ** (-jnp.arange(0, ROT_HALF, 2, dtype=F32) / ROT_HALF)
    ar, ac = row[:, None] * inv[None], col[:, None] * inv[None]
    cos = jnp.concatenate([jnp.cos(ar), jnp.cos(ar), jnp.cos(ac), jnp.cos(ac)], -1)
    sin = jnp.concatenate([-jnp.sin(ar), jnp.sin(ar), -jnp.sin(ac), jnp.sin(ac)], -1)
    return cos, sin


def _gqa_tables(t, n):
    cos, sin = _rope_tables(t)
    cos_k = jnp.concatenate([jnp.ones((n - t, HD_B), F32), cos], 0)
    sin_k = jnp.concatenate([jnp.zeros((n - t, HD_B), F32), sin], 0)
    return cos, sin, cos_k, sin_k


@jax.custom_vjp
def _gqa(pq, pk, pv, qw, kw):
    return _gqa_fwd(pq, pk, pv, qw, kw)[0]


def _gqa_fwd(pq, pk, pv, qw, kw):
    cos, sin, cos_k, sin_k = _gqa_tables(pq.shape[0], pk.shape[0])
    q = _norm_rope_fwd(pq, qw[None], cos, sin, "q_norm_rope")
    k = _norm_rope_fwd(pk, kw[None], cos_k, sin_k, "k_norm_rope")
    vb = pv.astype(BF16)
    o, lse = _attn_fwd(q, k, vb)
    return o, (pq, pk, qw, kw, q, k, vb, o, lse)


def _gqa_bwd(res, do):
    pq, pk, qw, kw, q, k, vb, o, lse = res
    cos, sin, cos_k, sin_k = _gqa_tables(pq.shape[0], pk.shape[0])
    dq, dk, dv = _attention_bwd((q, k, vb, o, lse), do)
    dpq, dqw = _norm_rope_bwd(pq, qw[None], cos, sin, dq, "q_norm_rope_bwd")
    dpk, dkw = _norm_rope_bwd(pk, kw[None], cos_k, sin_k, dk, "k_norm_rope_bwd")
    return dpq, dpk, dv, dqw[0], dkw[0]


_gqa.defvjp(_gqa_fwd, _gqa_bwd)


def _mlstm_chunk_forward(q, k, v, lir, lfr, s0, n0, m0, reverse):
    L = q.shape[0]
    ti = lax.broadcasted_iota(jnp.int32, (L, L), 0)
    si = lax.broadcasted_iota(jnp.int32, (L, L), 1)
    seen = (si >= ti) if reverse else (si <= ti)
    seen_t = (ti >= si) if reverse else (ti <= si)
    eye = ti == si
    lic = jnp.sum(jnp.where(eye, lir, 0.0), axis=1, keepdims=True)
    lfc = jnp.sum(jnp.where(eye, lfr, 0.0), axis=1, keepdims=True)
    b_col = jnp.sum(jnp.where(seen, lfr, 0.0), axis=1, keepdims=True)
    b_row = jnp.sum(jnp.where(seen_t, lfc, 0.0), axis=0, keepdims=True)
    d = jnp.where(seen, b_col - b_row + lir, -jnp.inf)
    m = jnp.maximum(b_col + m0, jnp.max(d, axis=1, keepdims=True))
    w = jnp.exp(d - m)
    a = jnp.exp(b_col + m0 - m)
    qm, km, vm = q.astype(MX), k.astype(MX), v.astype(MX)
    s = _dot_nt(qm, km) * w
    qs = _dot(qm, s0.astype(MX))
    num = a * qs + _dot(s.astype(MX), vm)
    qn = jnp.sum(q * n0, axis=1, keepdims=True)
    den = a * qn + jnp.sum(s, axis=1, keepdims=True)
    floor = jnp.exp(-m)
    dd = jnp.maximum(jnp.abs(den), floor)
    b_last = jnp.sum(lfr, axis=1, keepdims=True)
    m_end = jnp.maximum(b_last + m0, jnp.max(b_last - b_row + lir, axis=1, keepdims=True))
    w_end = jnp.exp(b_last - b_col + lic - m_end)
    a_end = jnp.exp(b_last + m0 - m_end)
    return dict(eye=eye, seen=seen, w=w, a=a, s=s, qs=qs, num=num, qn=qn, den=den, floor=floor, dd=dd,
                m_end=m_end, w_end=w_end, a_end=a_end, qm=qm, km=km, vm=vm)


def _mlstm_fwd_call(q, k, v, gr, n, row_off, reverse):
    L = MLSTM_CHUNK
    nc, off = n // L, row_off // L
    pos = (lambda i: nc - 1 - i) if reverse else (lambda i: i)

    def body(q_ref, k_ref, v_ref, gr_ref, h_ref, s0_ref, n0_ref, m0_ref, s_sc, n_sc, m_sc):
        @pl.when(pl.program_id(1) == 0)
        def _():
            s_sc[...] = jnp.zeros(s_sc.shape, F32)
            n_sc[...] = jnp.zeros(n_sc.shape, F32)
            m_sc[...] = jnp.full(m_sc.shape, M_INIT, F32)

        s0, n0, m0 = s_sc[...], n_sc[...], m_sc[...]
        s0_ref[0, 0] = s0
        n0_ref[0, 0] = n0
        m0_ref[0, 0] = jnp.broadcast_to(m0, (1, DK_A))
        k, v = k_ref[...], v_ref[...]
        f = _mlstm_chunk_forward(q_ref[...], k, v, gr_ref[0, 0], gr_ref[1, 0], s0, n0, m0, reverse)
        h_ref[...] = f["num"] / f["dd"]
        s_sc[...] = f["a_end"] * s0 + _dot_tn(f["km"], (f["w_end"] * v).astype(MX))
        n_sc[...] = f["a_end"] * n0 + jnp.sum(f["w_end"] * k, axis=0, keepdims=True)
        m_sc[...] = f["m_end"]

    qk_spec = pl.BlockSpec((L, DK_A), lambda h, i: (off + pos(i), h))
    v_spec = pl.BlockSpec((L, DV_A), lambda h, i: (off + pos(i), h))
    gr_spec = pl.BlockSpec((2, 1, 1, L), lambda h, i: (0, h, 0, off + pos(i)))
    h_spec = pl.BlockSpec((L, DV_A), lambda h, i: (pos(i), h))
    st_spec = pl.BlockSpec((1, 1, DK_A, DV_A), lambda h, i: (h, pos(i), 0, 0))
    vec_spec = pl.BlockSpec((1, 1, 1, DK_A), lambda h, i: (h, pos(i), 0, 0))
    return pl.pallas_call(
        body, grid=(NH_A, nc),
        in_specs=[qk_spec, qk_spec, v_spec, gr_spec],
        out_specs=[h_spec, st_spec, vec_spec, vec_spec],
        out_shape=[jax.ShapeDtypeStruct((n, V_A), F32), jax.ShapeDtypeStruct((NH_A, nc, DK_A, DV_A), F32),
                   jax.ShapeDtypeStruct((NH_A, nc, 1, DK_A), F32), jax.ShapeDtypeStruct((NH_A, nc, 1, DK_A), F32)],
        scratch_shapes=[pltpu.VMEM((DK_A, DV_A), F32), pltpu.VMEM((1, DK_A), F32), pltpu.VMEM((1, 1), F32)],
        compiler_params=pltpu.CompilerParams(dimension_semantics=("parallel", "arbitrary"),
                                             vmem_limit_bytes=VMEM_LIMIT),
        name="mlstm_fwd")(q, k, v, gr)


def _mlstm_bwd_call(q, k, v, gr, s0_all, n0_all, m0_all, dh, n, row_off, reverse):
    L = MLSTM_CHUNK
    nc, off = n // L, row_off // L
    pos = (lambda i: i) if reverse else (lambda i: nc - 1 - i)

    def body(q_ref, k_ref, v_ref, gr_ref, s0_ref, n0_ref, m0_ref, dh_ref,
             dq_ref, dk_ref, dv_ref, dg_ref, ds_sc, dn_sc):
        @pl.when(pl.program_id(1) == 0)
        def _():
            ds_sc[...] = jnp.zeros(ds_sc.shape, F32)
            dn_sc[...] = jnp.zeros(dn_sc.shape, F32)

        q, k, v = q_ref[...], k_ref[...], v_ref[...]
        s0, n0, m0 = s0_ref[0, 0], n0_ref[0, 0], m0_ref[0, 0][:, 0:1]
        f = _mlstm_chunk_forward(q, k, v, gr_ref[0, 0], gr_ref[1, 0], s0, n0, m0, reverse)
        w, a, s = f["w"], f["a"], f["s"]
        qm, km, vm, w_end, a_end = f["qm"], f["km"], f["vm"], f["w_end"], f["a_end"]
        ds1, dn1 = ds_sc[...], dn_sc[...]
        ds1m, s0m = ds1.astype(MX), s0.astype(MX)

        inv = 1.0 / f["dd"]
        dh = dh_ref[...]
        dnum = dh * inv
        ddd = -jnp.sum(dh * (f["num"] * inv), axis=1, keepdims=True) * inv
        dden = jnp.where(jnp.abs(f["den"]) > f["floor"], jnp.sign(f["den"]) * ddd, 0.0)
        adn = (a * dnum).astype(MX)
        dnm = dnum.astype(MX)
        ds_tot = _dot_nt(dnm, vm) + dden
        dsr = (ds_tot * w).astype(MX)
        e = ds_tot * s
        wv = (w_end * v).astype(MX)
        kds = _dot(km, ds1m)
        dq_ref[...] = _dot_nt(adn, s0m) + _dot(dsr, km) + (dden * a) * n0
        dk_ref[...] = _dot_tn(dsr, qm) + _dot_nt(wv, ds1m) + w_end * dn1
        dv_ref[...] = _dot_tn(s.astype(MX), dnm) + w_end * kds

        eye = f["eye"]
        to_col = lambda r: jnp.sum(jnp.where(eye, r, 0.0), axis=1, keepdims=True)
        to_row = lambda c: jnp.sum(jnp.where(eye, c, 0.0), axis=0, keepdims=True)
        g_a = (jnp.sum(dnum * f["qs"], axis=1, keepdims=True) + dden * f["qn"]) * a
        g_w = (jnp.sum(v * kds, axis=1, keepdims=True) + jnp.sum(k * dn1, axis=1, keepdims=True)) * w_end
        g_end = (jnp.sum(jnp.sum(ds1 * s0, axis=1, keepdims=True), axis=0, keepdims=True)
                 + jnp.sum(dn1 * n0, axis=1, keepdims=True)) * a_end
        col_e = jnp.sum(e, axis=0, keepdims=True)
        db = jnp.sum(e, axis=1, keepdims=True) - to_col(col_e) + g_a - g_w
        last = lax.broadcasted_iota(jnp.int32, (L, 1), 0) == (0 if reverse else L - 1)
        db = db + jnp.where(last, jnp.sum(g_w, axis=0, keepdims=True) + g_end, 0.0)
        dg_ref[0, 0] = col_e + to_row(g_w)
        dg_ref[1, 0] = jnp.sum(jnp.where(f["seen"], db, 0.0), axis=0, keepdims=True)

        ds_sc[...] = a_end * ds1 + _dot_tn(qm, adn)
        dn_sc[...] = a_end * dn1 + jnp.sum((dden * a) * q, axis=0, keepdims=True)

    qk_spec = pl.BlockSpec((L, DK_A), lambda h, i: (off + pos(i), h))
    v_spec = pl.BlockSpec((L, DV_A), lambda h, i: (off + pos(i), h))
    gr_spec = pl.BlockSpec((2, 1, 1, L), lambda h, i: (0, h, 0, off + pos(i)))
    st_spec = pl.BlockSpec((1, 1, DK_A, DV_A), lambda h, i: (h, pos(i), 0, 0))
    vec_spec = pl.BlockSpec((1, 1, 1, DK_A), lambda h, i: (h, pos(i), 0, 0))
    oqk_spec = pl.BlockSpec((L, DK_A), lambda h, i: (pos(i), h))
    ov_spec = pl.BlockSpec((L, DV_A), lambda h, i: (pos(i), h))
    og_spec = pl.BlockSpec((2, 1, 1, L), lambda h, i: (0, h, 0, pos(i)))
    return pl.pallas_call(
        body, grid=(NH_A, nc),
        in_specs=[qk_spec, qk_spec, v_spec, gr_spec, st_spec, vec_spec, vec_spec, ov_spec],
        out_specs=[oqk_spec, oqk_spec, ov_spec, og_spec],
        out_shape=[jax.ShapeDtypeStruct((n, QK_A), F32), jax.ShapeDtypeStruct((n, QK_A), F32),
                   jax.ShapeDtypeStruct((n, V_A), F32), jax.ShapeDtypeStruct((2, NH_A, 1, n), F32)],
        scratch_shapes=[pltpu.VMEM((DK_A, DV_A), F32), pltpu.VMEM((1, DK_A), F32)],
        compiler_params=pltpu.CompilerParams(dimension_semantics=("parallel", "arbitrary"),
                                             vmem_limit_bytes=VMEM_LIMIT),
        name="mlstm_bwd")(q, k, v, gr, s0_all, n0_all, m0_all, dh)


def _make_mlstm(n, row_off, reverse):
    def gate_rows(li, lf):
        return jnp.stack([li, lf]).transpose(0, 2, 1)[:, :, None, :]

    @jax.custom_vjp
    def op(q, k, v, li, lf):
        return _mlstm_fwd_call(q, k, v, gate_rows(li, lf), n, row_off, reverse)[0]

    def fwd(q, k, v, li, lf):
        gr = gate_rows(li, lf)
        h, s0, n0, m0 = _mlstm_fwd_call(q, k, v, gr, n, row_off, reverse)
        return h, (q, k, v, gr, s0, n0, m0)

    def bwd(res, dh):
        q, k, v, gr, s0, n0, m0 = res
        dq, dk, dv, dg = _mlstm_bwd_call(q, k, v, gr, s0, n0, m0, dh, n, row_off, reverse)
        rows = ((row_off, q.shape[0] - row_off - n), (0, 0))
        dg = jnp.pad(dg[:, :, 0, :].transpose(0, 2, 1), ((0, 0),) + rows)
        return jnp.pad(dq, rows), jnp.pad(dk, rows), jnp.pad(dv, rows), dg[0], dg[1]

    op.defvjp(fwd, bwd)
    return op


MERGE_ROWS = 128


def _sig(x):
    return jax.nn.sigmoid(x)


def _merge_pre_fwd(h_f, h_b, o_attn, p_oa, p_za, p_zb, mh_w, tc):
    t = o_attn.shape[0]
    tr, off = MERGE_ROWS, tc // MERGE_ROWS

    def body(hf_ref, hb_ref, oat_ref, oa_ref, za_ref, zb_ref, w_ref, a_ref, b_ref):
        for hd in range(NH_A):
            cols = slice(hd * DV_A, (hd + 1) * DV_A)
            h = hf_ref[:, cols] + hb_ref[:, cols]
            hn = h * lax.rsqrt(jnp.mean(h * h, axis=-1, keepdims=True) + EPS) * w_ref[:, cols]
            za = za_ref[:, cols]
            a_ref[:, cols] = (_sig(oa_ref[:, cols]) * hn * (za * _sig(za))).astype(BF16)
        zb = zb_ref[...]
        b_ref[...] = (oat_ref[...] * (zb * _sig(zb))).astype(BF16)

    lat = pl.BlockSpec((tr, V_A), lambda i: (i + off, 0))
    row = pl.BlockSpec((tr, V_A), lambda i: (i, 0))
    return pl.pallas_call(
        body, grid=(t // tr,),
        in_specs=[lat, row, row, lat, lat, lat, pl.BlockSpec((1, V_A), lambda i: (0, 0))],
        out_specs=[row, row],
        out_shape=[jax.ShapeDtypeStruct((t, V_A), BF16), jax.ShapeDtypeStruct((t, V_A), BF16)],
        compiler_params=pltpu.CompilerParams(dimension_semantics=("parallel",), vmem_limit_bytes=VMEM_LIMIT),
        name="merge_pre")(h_f, h_b, o_attn, p_oa, p_za, p_zb, mh_w)


def _ctx_block(i, nb, off):
    k = i - nb
    return jnp.where(i < nb, i + off, jnp.where(k < off, k, k + nb))


def _merge_pre_bwd(da, db, h_f, h_b, o_attn, p_oa, p_za, p_zb, mh_w, tc):
    t = o_attn.shape[0]
    n, r = h_f.shape[0], p_oa.shape[0]
    tr, off = MERGE_ROWS, tc // MERGE_ROWS
    nb = t // tr
    n_ctx = r // tr - nb

    def body(da_ref, db_ref, hf_ref, hb_ref, oat_ref, oa_ref, za_ref, zb_ref, w_ref,
             dhf_ref, dhb_ref, doat_ref, doa_ref, dza_ref, dzb_ref, dw_ref):
        i = pl.program_id(0)

        @pl.when(i == 0)
        def _():
            dw_ref[...] = jnp.zeros(dw_ref.shape, F32)

        @pl.when(i < nb)
        def _():
            for hd in range(NH_A):
                cols = slice(hd * DV_A, (hd + 1) * DV_A)
                h = hf_ref[:, cols] + hb_ref[:, cols]
                rs = lax.rsqrt(jnp.mean(h * h, axis=-1, keepdims=True) + EPS)
                w = w_ref[:, cols]
                hn = h * rs * w
                oa, za, g = oa_ref[:, cols], za_ref[:, cols], da_ref[:, cols]
                so, sz = _sig(oa), _sig(za)
                silu_z = za * sz
                doa_ref[:, cols] = g * hn * silu_z * so * (1.0 - so)
                dza_ref[:, cols] = g * so * hn * (sz * (1.0 + za * (1.0 - sz)))
                dhn = g * so * silu_z
                dw_ref[:, cols] += jnp.sum(dhn * (h * rs), axis=0, keepdims=True)
                dxn = dhn * w
                dh = rs * dxn - h * (rs * rs * rs * jnp.mean(dxn * h, axis=-1, keepdims=True))
                dhf_ref[:, cols] = dh
                dhb_ref[:, cols] = dh
            zb, gb, oat = zb_ref[...], db_ref[...], oat_ref[...]
            sb = _sig(zb)
            doat_ref[...] = gb * (zb * sb)
            dzb_ref[...] = gb * oat * (sb * (1.0 + zb * (1.0 - sb)))

        @pl.when(i >= nb)
        def _():
            for ref in (dhf_ref, dhb_ref, doa_ref, dza_ref, dzb_ref):
                ref[...] = jnp.zeros(ref.shape, F32)

    lati = lambda i: jnp.minimum(i, nb - 1)
    lat = pl.BlockSpec((tr, V_A), lambda i: (lati(i) + off, 0))
    row = pl.BlockSpec((tr, V_A), lambda i: (lati(i), 0))
    vec = pl.BlockSpec((1, V_A), lambda i: (0, 0))
    pout = pl.BlockSpec((tr, V_A), lambda i: (_ctx_block(i, nb, off), 0))
    hf_out = pl.BlockSpec((tr, V_A), lambda i: (jnp.where(i < nb, i + off, jnp.minimum(i - nb, off - 1)), 0))
    hb_out = pl.BlockSpec((tr, V_A), lambda i: (jnp.where(i < nb, i, nb + jnp.minimum(i - nb, off - 1)), 0))
    return pl.pallas_call(
        body, grid=(nb + n_ctx,),
        in_specs=[row, row, lat, row, row, lat, lat, lat, vec],
        out_specs=[hf_out, hb_out, row, pout, pout, pout, vec],
        out_shape=[jax.ShapeDtypeStruct((n, V_A), F32), jax.ShapeDtypeStruct((n, V_A), F32),
                   jax.ShapeDtypeStruct((t, V_A), F32), jax.ShapeDtypeStruct((r, V_A), F32),
                   jax.ShapeDtypeStruct((r, V_A), F32), jax.ShapeDtypeStruct((r, V_A), F32),
                   jax.ShapeDtypeStruct((1, V_A), F32)],
        compiler_params=pltpu.CompilerParams(dimension_semantics=("arbitrary",), vmem_limit_bytes=VMEM_LIMIT),
        name="merge_pre_bwd")(da, db, h_f, h_b, o_attn, p_oa, p_za, p_zb, mh_w)


def _merge_gate_fwd(y_a, y_b, p_ga, p_gb, tc):
    t = y_a.shape[0]
    tr, off = MERGE_ROWS, tc // MERGE_ROWS

    def body(ya_ref, yb_ref, ga_ref, gb_ref, m_ref):
        m_ref[...] = (_sig(ga_ref[...]) * ya_ref[...] + _sig(gb_ref[...]) * yb_ref[...]).astype(BF16)

    lat = pl.BlockSpec((tr, D_MODEL), lambda i: (i + off, 0))
    row = pl.BlockSpec((tr, D_MODEL), lambda i: (i, 0))
    return pl.pallas_call(
        body, grid=(t // tr,), in_specs=[row, row, lat, lat], out_specs=row,
        out_shape=jax.ShapeDtypeStruct((t, D_MODEL), BF16),
        compiler_params=pltpu.CompilerParams(dimension_semantics=("parallel",), vmem_limit_bytes=VMEM_LIMIT),
        name="merge_gate")(y_a, y_b, p_ga, p_gb)


def _merge_gate_bwd(dm, y_a, y_b, p_ga, p_gb, tc):
    t, r = y_a.shape[0], p_ga.shape[0]
    tr, off = MERGE_ROWS, tc // MERGE_ROWS
    nb = t // tr
    n_ctx = r // tr - nb

    def body(dm_ref, ya_ref, yb_ref, ga_ref, gb_ref, dya_ref, dyb_ref, dga_ref, dgb_ref):
        i = pl.program_id(0)

        @pl.when(i < nb)
        def _():
            dm = dm_ref[...]
            sa, sb = _sig(ga_ref[...]), _sig(gb_ref[...])
            dya_ref[...] = (dm * sa).astype(BF16)
            dyb_ref[...] = (dm * sb).astype(BF16)
            dga_ref[...] = dm * ya_ref[...] * sa * (1.0 - sa)
            dgb_ref[...] = dm * yb_ref[...] * sb * (1.0 - sb)

        @pl.when(i >= nb)
        def _():
            dga_ref[...] = jnp.zeros(dga_ref.shape, F32)
            dgb_ref[...] = jnp.zeros(dgb_ref.shape, F32)

    lati = lambda i: jnp.minimum(i, nb - 1)
    lat = pl.BlockSpec((tr, D_MODEL), lambda i: (lati(i) + off, 0))
    row = pl.BlockSpec((tr, D_MODEL), lambda i: (lati(i), 0))
    pout = pl.BlockSpec((tr, D_MODEL), lambda i: (_ctx_block(i, nb, off), 0))
    return pl.pallas_call(
        body, grid=(nb + n_ctx,), in_specs=[row, row, row, lat, lat], out_specs=[row, row, pout, pout],
        out_shape=[jax.ShapeDtypeStruct((t, D_MODEL), BF16), jax.ShapeDtypeStruct((t, D_MODEL), BF16),
                   jax.ShapeDtypeStruct((r, D_MODEL), F32), jax.ShapeDtypeStruct((r, D_MODEL), F32)],
        compiler_params=pltpu.CompilerParams(dimension_semantics=("arbitrary",), vmem_limit_bytes=VMEM_LIMIT),
        name="merge_gate_bwd")(dm, y_a, y_b, p_ga, p_gb)


def _make_merge_block(tc):
    @jax.custom_vjp
    def block(h_f, h_b, o_attn, p_oa, p_za, p_zb, p_ga, p_gb, mh_w, w_ba, w_bb, w_out, pr_ba, pr_bb, pr_out):
        return fwd(h_f, h_b, o_attn, p_oa, p_za, p_zb, p_ga, p_gb, mh_w, w_ba, w_bb, w_out, pr_ba, pr_bb, pr_out)[0]

    def fwd(h_f, h_b, o_attn, p_oa, p_za, p_zb, p_ga, p_gb, mh_w, w_ba, w_bb, w_out, pr_ba, pr_bb, pr_out):
        a_in, b_in = _merge_pre_fwd(h_f, h_b, o_attn, p_oa, p_za, p_zb, mh_w[None], tc)
        y_a, y_b = _mm_nn(a_in, w_ba, "merge_ya"), _mm_nn(b_in, w_bb, "merge_yb")
        m_in = _merge_gate_fwd(y_a, y_b, p_ga, p_gb, tc)
        out = _mm_nn(m_in, w_out, "merge_out")
        return out, (h_f, h_b, o_attn, p_oa, p_za, p_zb, p_ga, p_gb, mh_w, w_ba, w_bb, w_out, a_in, b_in, y_a, y_b, m_in)

    def bwd(res, dout):
        h_f, h_b, o_attn, p_oa, p_za, p_zb, p_ga, p_gb, mh_w, w_ba, w_bb, w_out, a_in, b_in, y_a, y_b, m_in = res
        dm = _mm_nt(dout, w_out, "merge_out_da")
        dw_out = _mm_tn(m_in, dout, "merge_out_dw")
        dy_a, dy_b, dga, dgb = _merge_gate_bwd(dm, y_a, y_b, p_ga, p_gb, tc)
        da, db = _mm_nt(dy_a, w_ba, "merge_ya_da"), _mm_nt(dy_b, w_bb, "merge_yb_da")
        dw_ba, dw_bb = _mm_tn(a_in, dy_a, "merge_ya_dw"), _mm_tn(b_in, dy_b, "merge_yb_dw")
        dhf, dhb, doat, doa, dza, dzb, dmh = _merge_pre_bwd(da, db, h_f, h_b, o_attn, p_oa, p_za, p_zb, mh_w[None], tc)
        z = jnp.zeros_like
        return (dhf, dhb, doat, doa, dza, dzb, dga, dgb, dmh[0], z(w_ba), z(w_bb), z(w_out), dw_ba, dw_bb, dw_out)

    block.defvjp(fwd, bwd)
    return block


def _silu(x):
    return x * jax.nn.sigmoid(x)


def _ln_plain(x):
    mu = jnp.mean(x, -1, keepdims=True)
    var = jnp.mean(jnp.square(x - mu), -1, keepdims=True)
    return (x - mu) * lax.rsqrt(var + EPS)


def _rms(x, w):
    return x * lax.rsqrt(jnp.mean(jnp.square(x), -1, keepdims=True) + EPS) * w


_mm_mod, _mm_if = _make_mm("mod"), _make_mm("if")


def _local_loss(diff, const):
    x, c, ctx, target = diff["x"], const["c"], const["ctx"], const["target"]
    t, tc = x.shape[0], ctx.shape[0]
    n, r = tc + t, t + 2 * tc

    sc = jnp.concatenate([_silu(c), _silu(diff["c_ctx"])[None], jnp.zeros((14, D_MODEL), F32)], 0)
    mod = _mm_mod(sc, const["w_mod"], diff["p_mod"])[:2] + diff["b_mod"]
    shift, scale, gate = mod[0, :D_MODEL], mod[0, D_MODEL:2 * D_MODEL], mod[0, 2 * D_MODEL:]
    shift_c, scale_c = mod[1, :D_MODEL], mod[1, D_MODEL:2 * D_MODEL]
    u = _ln_plain(x) * (1 + scale) + shift
    u_c = _ln_plain(ctx) * (1 + scale_c) + shift_c
    u_all = jnp.concatenate([u_c, u, u_c], 0)

    p_qk, p_va, p_kv, p_oa, p_za, p_qb, p_zb, p_ga, p_gb = _project(u_all, const["w_main"], diff["p_main"])
    gt = _mm_if(u_all, const["w_if"], diff["p_if"])[:, :N_IF] + diff["b_if"]

    qk_pre = p_qk
    rows = jnp.arange(r)
    seg_start = ((rows == 0) | (rows == tc) | (rows == n))[:, None]
    seg_end = ((rows == tc - 1) | (rows == n - 1) | (rows == r - 1))[:, None]
    prev = jnp.where(seg_start, 0.0, jnp.roll(qk_pre, 1, axis=0))
    nxt = jnp.where(seg_end, 0.0, jnp.roll(qk_pre, -1, axis=0))
    cw, cb = diff["conv_w"], diff["conv_b"]
    qk = _silu(cb + prev * cw[0] + qk_pre * cw[1] + nxt * cw[2])
    q_a, k_a = qk[:, :QK_A], qk[:, QK_A:] * (DK_A ** -0.5)
    v_a = p_va
    li_f, lf_f = gt[:, 0:8], jax.nn.log_sigmoid(gt[:, 8:16])
    li_b, lf_b = gt[:, 16:24], jax.nn.log_sigmoid(gt[:, 24:32])

    h_f = _make_mlstm(n, 0, False)(q_a, k_a, v_a, li_f, lf_f)
    h_b = _make_mlstm(n, tc, True)(q_a, k_a, v_a, li_b, lf_b)

    lat = slice(tc, n)
    o_attn = _gqa(p_qb[lat], p_kv[:n, :KV_B], p_kv[:n, KV_B:],
                  diff["q_norm_w"], diff["k_norm_w"])

    out = _make_merge_block(tc)(h_f, h_b, o_attn, p_oa, p_za, p_zb, p_ga, p_gb, diff["mh_norm_w"],
                                const["w_ba"], const["w_bb"], const["w_out"], diff["p_ba"], diff["p_bb"], diff["p_out"])

    y = _ln_plain(ALPHA * x + gate * out) * diff["ln_w"] + diff["ln_b"]
    return 0.5 * jnp.sum(jnp.mean(jnp.square(y - target), axis=-1))


OTHER_CHIPS = [(1, 0), (0, 1), (1, 1)]


def _flip(v, bit):
    return 1 - v if bit else v


def _gather_chips(shard, name):
    def body(x_ref, o_ref, send_sems, recv_sems, local_sem):
        x, y, c = lax.axis_index("x"), lax.axis_index("y"), lax.axis_index("c")
        mine = pltpu.make_async_copy(x_ref, o_ref.at[2 * x + y], local_sem)
        mine.start()

        def copy(r, slot):
            dx, dy = OTHER_CHIPS[r]
            return pltpu.make_async_remote_copy(
                src_ref=x_ref, dst_ref=o_ref.at[slot], send_sem=send_sems.at[r], recv_sem=recv_sems.at[r],
                device_id=(_flip(x, dx), _flip(y, dy), c), device_id_type=MESH)

        sends = [copy(r, 2 * x + y) for r in range(3)]
        for cp in sends:
            cp.start()
        for r, (dx, dy) in enumerate(OTHER_CHIPS):
            copy(r, 2 * _flip(x, dx) + _flip(y, dy)).wait_recv()
        for cp in sends:
            cp.wait_send()
        mine.wait()

    return pl.pallas_call(
        body, out_shape=jax.ShapeDtypeStruct((N_CHIPS,) + shard.shape, shard.dtype),
        in_specs=[pl.BlockSpec(memory_space=pl.ANY)], out_specs=pl.BlockSpec(memory_space=pl.ANY),
        scratch_shapes=[pltpu.SemaphoreType.DMA((3,)), pltpu.SemaphoreType.DMA((3,)), pltpu.SemaphoreType.DMA],
        name=name)(shard)


def _gather_chips_halves(shard, name):
    rows, cols = shard.shape
    halves = shard.reshape(2, rows // 2, cols)

    def body(x_ref, o_ref, send_sems, recv_sems, local_sem):
        x, y, c = lax.axis_index("x"), lax.axis_index("y"), lax.axis_index("c")
        my_chip = 2 * x + y
        mine = pltpu.make_async_copy(x_ref, o_ref.at[my_chip], local_sem)
        mine.start()

        def chip_of(r):
            dx, dy = OTHER_CHIPS[r]
            return _flip(x, dx), _flip(y, dy)

        def copy(k, chip_slot, half, to, src=None):
            dst = o_ref.at[chip_slot, half]
            return pltpu.make_async_remote_copy(
                src_ref=dst if src is None else src, dst_ref=dst, send_sem=send_sems.at[k],
                recv_sem=recv_sems.at[k], device_id=to, device_id_type=MESH)

        first = [copy(r, my_chip, c, (*chip_of(r), c), src=x_ref.at[c]) for r in range(3)]
        for cp in first:
            cp.start()
        passed = []
        for r in range(3):
            px, py = chip_of(r)
            copy(r, 2 * px + py, c, (px, py, c)).wait_recv()
            passed.append(copy(3 + r, 2 * px + py, c, (x, y, 1 - c)))
            passed[-1].start()
        for r in range(3):
            px, py = chip_of(r)
            copy(3 + r, 2 * px + py, 1 - c, (x, y, 1 - c)).wait_recv()
        for cp in first + passed:
            cp.wait_send()
        mine.wait()

    out = pl.pallas_call(
        body, out_shape=jax.ShapeDtypeStruct((N_CHIPS, 2, rows // 2, cols), shard.dtype),
        in_specs=[pl.BlockSpec(memory_space=pl.ANY)], out_specs=pl.BlockSpec(memory_space=pl.ANY),
        scratch_shapes=[pltpu.SemaphoreType.DMA((6,)), pltpu.SemaphoreType.DMA((6,)), pltpu.SemaphoreType.DMA],
        name=name)(halves)
    return out.reshape(N_CHIPS, rows, cols)


def _scatter_grads(slots, name):
    def body(g_ref, o_ref, send_sems, recv_sems, local_sem):
        x, y, c = lax.axis_index("x"), lax.axis_index("y"), lax.axis_index("c")
        me, my_chip, sibling = 4 * x + 2 * y + c, 2 * x + y, (x, y, 1 - c)
        mine = pltpu.make_async_copy(g_ref.at[my_chip], o_ref.at[me], local_sem)
        mine.start()

        def chip_of(r):
            dx, dy = OTHER_CHIPS[r]
            return _flip(x, dx), _flip(y, dy)

        def copy(k, slot, to, src=None):
            dst = o_ref.at[slot]
            return pltpu.make_async_remote_copy(
                src_ref=dst if src is None else src, dst_ref=dst, send_sem=send_sems.at[k],
                recv_sem=recv_sems.at[k], device_id=to, device_id_type=MESH)

        first = [copy(0, me, sibling, src=g_ref.at[my_chip])]
        for r in range(3):
            px, py = chip_of(r)
            first.append(copy(1 + r, me, (px, py, c), src=g_ref.at[2 * px + py]))
        for cp in first:
            cp.start()
        passed = []
        for r in range(3):
            px, py = chip_of(r)
            copy(1 + r, 4 * px + 2 * py + c, (px, py, c)).wait_recv()
            passed.append(copy(4 + r, 4 * px + 2 * py + c, sibling))
            passed[-1].start()
        copy(0, 4 * x + 2 * y + 1 - c, sibling).wait_recv()
        for r in range(3):
            px, py = chip_of(r)
            copy(4 + r, 4 * px + 2 * py + 1 - c, sibling).wait_recv()
        for cp in first + passed:
            cp.wait_send()
        mine.wait()

    return pl.pallas_call(
        body, out_shape=jax.ShapeDtypeStruct((N_DEV,) + slots.shape[1:], slots.dtype),
        in_specs=[pl.BlockSpec(memory_space=pl.ANY)], out_specs=pl.BlockSpec(memory_space=pl.ANY),
        scratch_shapes=[pltpu.SemaphoreType.DMA((N_DEV - 1,)), pltpu.SemaphoreType.DMA((N_DEV - 1,)),
                        pltpu.SemaphoreType.DMA],
        name=name)(slots)


def _allreduce_small(v, name):
    def body(v_ref, o_ref, buf, send_sems, recv_sems):
        x, y, c = lax.axis_index("x"), lax.axis_index("y"), lax.axis_index("c")
        me = 4 * x + 2 * y + c
        buf[me] = v_ref[...]

        def peer(r):
            return _flip(x, (r >> 2) & 1), _flip(y, (r >> 1) & 1), _flip(c, r & 1)

        def copy(r, dst_slot):
            return pltpu.make_async_remote_copy(
                src_ref=v_ref, dst_ref=buf.at[dst_slot], send_sem=send_sems.at[r - 1],
                recv_sem=recv_sems.at[r - 1], device_id=peer(r), device_id_type=MESH)

        sends = [copy(r, me) for r in range(1, N_DEV)]
        for cp in sends:
            cp.start()
        for r in range(1, N_DEV):
            px, py, pc = peer(r)
            copy(r, 4 * px + 2 * py + pc).wait_recv()
        for cp in sends:
            cp.wait_send()
        acc = buf[0]
        for d in range(1, N_DEV):
            acc = acc + buf[d]
        o_ref[...] = acc

    return pl.pallas_call(
        body, out_shape=jax.ShapeDtypeStruct(v.shape, v.dtype),
        in_specs=[pl.BlockSpec(memory_space=pltpu.VMEM)], out_specs=pl.BlockSpec(memory_space=pltpu.VMEM),
        scratch_shapes=[pltpu.VMEM((N_DEV,) + v.shape, v.dtype), pltpu.SemaphoreType.DMA((N_DEV - 1,)),
                        pltpu.SemaphoreType.DMA((N_DEV - 1,))],
        name=name)(v)


def _adamw_math(w, g, m, v):
    m = ADAM_B1 * m + (1.0 - ADAM_B1) * g
    v = ADAM_B2 * v + (1.0 - ADAM_B2) * jnp.square(g)
    m_hat = m / (1.0 - ADAM_B1 ** ADAM_STEP)
    v_hat = v / (1.0 - ADAM_B2 ** ADAM_STEP)
    delta = -ADAM_LR * (m_hat / (jnp.sqrt(v_hat) + ADAM_EPS) + ADAM_WD * w)
    return delta, m, v


def _adamw_sum(parts, w, m, v, name):
    npart, rows, cols = parts.shape
    tr = _pick(rows, (64, 32, 16, 8)) if rows >= 8 else rows

    def body(p_ref, w_ref, m_ref, v_ref, g_out, d_out, m_out, v_out):
        g = p_ref[0].astype(F32)
        for k in range(1, npart):
            g = g + p_ref[k].astype(F32)
        d, m2, v2 = _adamw_math(w_ref[...], g, m_ref[...], v_ref[...])
        g_out[...] = g
        d_out[...] = d
        m_out[...] = m2
        v_out[...] = v2

    spec = pl.BlockSpec((tr, cols), lambda i: (i, 0))
    shp = jax.ShapeDtypeStruct((rows, cols), F32)
    return pl.pallas_call(
        body, grid=(rows // tr,),
        in_specs=[pl.BlockSpec((npart, tr, cols), lambda i: (0, i, 0)), spec, spec, spec],
        out_specs=[spec, spec, spec, spec], out_shape=[shp, shp, shp, shp],
        compiler_params=pltpu.CompilerParams(dimension_semantics=("parallel",), vmem_limit_bytes=VMEM_LIMIT),
        name=name)(parts, w, m, v)


SMALL_ROWS = 16


def _pack_small(c_ctx, b_mod, conv_b, mh, ln_w, ln_b, conv_w_rows, b_if, qn, kn):
    last = jnp.concatenate([b_if.reshape(-1), qn.reshape(-1), kn.reshape(-1),
                            jnp.zeros((D_MODEL - N_IF - 2 * HD_B,), F32)])
    rows = [c_ctx.reshape(1, D_MODEL), b_mod.reshape(3, D_MODEL), conv_b.reshape(1, D_MODEL),
            mh.reshape(1, D_MODEL), ln_w.reshape(1, D_MODEL), ln_b.reshape(1, D_MODEL),
            conv_w_rows.reshape(3, D_MODEL), last[None], jnp.zeros((SMALL_ROWS - 12, D_MODEL), F32)]
    return jnp.concatenate(rows, 0)


def _unpack_small(pk, conv_cols):
    return dict(c_ctx=pk[0], b_mod=pk[1:4].reshape(1, 3 * D_MODEL), conv_b=pk[4:5], mh_norm_w=pk[5:6],
                ln_w=pk[6:7], ln_b=pk[7:8], conv_w=pk[8:11, :conv_cols][None], b_if=pk[11:12, :N_IF],
                q_norm_w=pk[11:12, N_IF:N_IF + HD_B], k_norm_w=pk[11:12, N_IF + HD_B:N_IF + 2 * HD_B])


def kernel(x, c, ctx, c_ctx, w_mod, b_mod, w_in, b_if, conv_w, conv_b, mh_norm_w, q_norm_w, k_norm_w, w_branch_a, w_branch_b, w_out, ln_w, ln_b, loss_target, m_c_ctx, m_w_mod, m_b_mod, m_w_in, m_b_if, m_conv_w, m_conv_b, m_mh_norm_w, m_q_norm_w, m_k_norm_w, m_w_branch_a, m_w_branch_b, m_w_out, m_ln_w, m_ln_b, v_c_ctx, v_w_mod, v_b_mod, v_w_in, v_b_if, v_conv_w, v_conv_b, v_mh_norm_w, v_q_norm_w, v_k_norm_w, v_w_branch_a, v_w_branch_b, v_w_out, v_ln_w, v_ln_b):
    chip = 2 * lax.axis_index("x") + lax.axis_index("y")
    mod_cols, in_cols, conv_cols = w_mod.shape[2], w_in.shape[2], conv_w.shape[2]
    br_rows = w_out.shape[1]

    g_mod = _gather_chips_halves(w_mod[0].astype(BF16), "gather_w_mod")
    g_in = _gather_chips_halves(w_in[0].astype(BF16), "gather_w_in")
    g_ba = _gather_chips_halves(w_branch_a[0].astype(BF16), "gather_w_ba")
    g_bb = _gather_chips_halves(w_branch_b[0].astype(BF16), "gather_w_bb")
    g_out = _gather_chips_halves(w_out[0].astype(BF16), "gather_w_out")
    g_conv = _gather_chips(conv_w[0], "gather_conv_w")
    w_mod_full = jnp.moveaxis(g_mod, 0, 1).reshape(D_MODEL, N_CHIPS * mod_cols)
    w_in_full = jnp.moveaxis(g_in, 0, 1).reshape(D_MODEL, N_CHIPS * in_cols)
    w_main = jnp.concatenate([w_in_full[:, :IF_START], w_in_full[:, IF_START + N_IF:]], 1)
    w_if = jnp.pad(w_in_full[:, IF_START:IF_START + N_IF], ((0, 0), (0, IF_PAD - N_IF)))
    conv_w_full = jnp.moveaxis(g_conv, 0, 1).reshape(3, N_CHIPS * conv_cols)

    const = dict(c=c, ctx=ctx[0], target=loss_target[0], w_mod=w_mod_full, w_main=w_main, w_if=w_if,
                 w_ba=g_ba.reshape(D_MODEL, D_MODEL), w_bb=g_bb.reshape(D_MODEL, D_MODEL),
                 w_out=g_out.reshape(D_MODEL, D_MODEL))
    diff = dict(x=x[0], c_ctx=c_ctx, b_mod=b_mod[0], b_if=b_if[0], conv_w=conv_w_full, conv_b=conv_b[0],
                mh_norm_w=mh_norm_w[0], q_norm_w=q_norm_w[0], k_norm_w=k_norm_w[0], ln_w=ln_w[0], ln_b=ln_b[0],
                p_mod=jnp.zeros(w_mod_full.shape, F32), p_main=jnp.zeros(w_main.shape, F32),
                p_if=jnp.zeros(w_if.shape, F32), p_ba=jnp.zeros((D_MODEL, D_MODEL), F32),
                p_bb=jnp.zeros((D_MODEL, D_MODEL), F32), p_out=jnp.zeros((D_MODEL, D_MODEL), F32))
    loss_local, g = jax.value_and_grad(_local_loss)(diff, const)
    loss = lax.psum(loss_local, ("x", "y", "c"))

    g_small = _allreduce_small(
        _pack_small(g["c_ctx"], g["b_mod"], g["conv_b"], g["mh_norm_w"], g["ln_w"], g["ln_b"], g["conv_w"],
                    g["b_if"], g["q_norm_w"], g["k_norm_w"]), "allreduce_small")
    conv_g = lax.dynamic_slice(g_small[8:11], (0, chip * conv_cols), (3, conv_cols))
    g_small = g_small.at[8:11].set(jnp.pad(conv_g, ((0, 0), (0, D_MODEL - conv_cols))))
    pad_conv = lambda a: jnp.pad(a[0], ((0, 0), (0, D_MODEL - conv_cols)))
    packed = [_pack_small(cc, bm[0], cb[0], mh[0], lw[0], lb[0], pad_conv(cw), bi[0], qn[0], kn[0])
              for cc, bm, cb, mh, lw, lb, cw, bi, qn, kn in (
                  (c_ctx, b_mod, conv_b, mh_norm_w, ln_w, ln_b, conv_w, b_if, q_norm_w, k_norm_w),
                  (m_c_ctx, m_b_mod, m_conv_b, m_mh_norm_w, m_ln_w, m_ln_b, m_conv_w, m_b_if, m_q_norm_w, m_k_norm_w),
                  (v_c_ctx, v_b_mod, v_conv_b, v_mh_norm_w, v_ln_w, v_ln_b, v_conv_w, v_b_if, v_q_norm_w, v_k_norm_w))]
    small = [_unpack_small(a, conv_cols)
             for a in _adamw_sum(g_small[None], packed[0], packed[1], packed[2], "adamw_small")]

    def col_slots(gfull, cols):
        return jnp.moveaxis(gfull.reshape(D_MODEL, N_CHIPS, cols), 1, 0).astype(BF16)

    g_in_full = jnp.concatenate([g["p_main"][:, :IF_START], g["p_if"][:, :N_IF], g["p_main"][:, IF_START:]], 1)
    big = {}
    for nm, slots, w_, m_, v_ in (
            ("w_mod", col_slots(g["p_mod"], mod_cols), w_mod, m_w_mod, v_w_mod),
            ("w_in", col_slots(g_in_full, in_cols), w_in, m_w_in, v_w_in),
            ("w_branch_a", g["p_ba"].reshape(N_CHIPS, br_rows, D_MODEL).astype(BF16), w_branch_a, m_w_branch_a, v_w_branch_a),
            ("w_branch_b", g["p_bb"].reshape(N_CHIPS, br_rows, D_MODEL).astype(BF16), w_branch_b, m_w_branch_b, v_w_branch_b),
            ("w_out", g["p_out"].reshape(N_CHIPS, br_rows, D_MODEL).astype(BF16), w_out, m_w_out, v_w_out)):
        parts = _scatter_grads(slots, "scatter_" + nm)
        big[nm] = [a[None] for a in _adamw_sum(parts, w_[0], m_[0], v_[0], "adamw_" + nm)]

    names = ["c_ctx", "w_mod", "b_mod", "w_in", "b_if", "conv_w", "conv_b", "mh_norm_w", "q_norm_w", "k_norm_w",
             "w_branch_a", "w_branch_b", "w_out", "ln_w", "ln_b"]
    outs = [[big[nm][k] if nm in big else small[k][nm] for nm in names] for k in range(4)]
    return (loss, g["x"][None], *outs[0], *outs[1], *outs[2], *outs[3])
```

```python
import functools

import jax
import jax.numpy as jnp
from jax import lax
from jax.experimental import pallas as pl
from jax.experimental.pallas import tpu as pltpu

F32 = jnp.float32
BF16 = jnp.bfloat16
MESH = pl.DeviceIdType.MESH

D_MODEL = 2048
NH_A, DK_A, DV_A = 8, 128, 256
QK_A, V_A = NH_A * DK_A, NH_A * DV_A
NH_B, NKV_B, HD_B = 16, 4, 128
Q_B, KV_B = NH_B * HD_B, NKV_B * HD_B
GRID_W = 64
ROT_HALF = HD_B // 2
ROPE_THETA = 10000.0
M_INIT = -1e30
EPS = 1e-6
ALPHA = 2.0 ** 0.25
N_IN = 17440
IF_START, N_IF, IF_PAD = 4096, 32, 128
N_MAIN = N_IN - N_IF
O_QK, O_VA, O_KB, O_VB, O_OA, O_ZA, O_QB, O_ZB, O_GA, O_GB = (
    0, 2048, 4096, 4608, 5120, 7168, 9216, 11264, 13312, 15360)
MLSTM_CHUNK = 256

ADAM_LR, ADAM_B1, ADAM_B2, ADAM_EPS, ADAM_WD, ADAM_STEP = 0.001, 0.9, 0.999, 1e-08, 0.01, 10

VMEM_LIMIT = 48 * 1024 * 1024
N_CHIPS, N_DEV = 4, 8
MX = BF16


def _pick(n, cands):
    for c in cands:
        if n % c == 0:
            return c
    raise ValueError(f"no tile for {n} in {cands}")


def _dot(a, b):
    return jnp.dot(a, b, preferred_element_type=F32)


def _dot_nt(a, b):
    return lax.dot_general(a, b, (((1,), (1,)), ((), ())), preferred_element_type=F32)


def _dot_tn(a, b):
    return lax.dot_general(a, b, (((0,), (0,)), ((), ())), preferred_element_type=F32)


def _mm_nn(a, b, name):
    m, k = a.shape
    _, n = b.shape
    tm = _pick(m, (512, 256, 128, 64, 32, 16))
    tn = _pick(n, (1024, 512, 256, 128))

    def body(a_ref, b_ref, o_ref):
        o_ref[...] = _dot(a_ref[...], b_ref[...])

    return pl.pallas_call(
        body, grid=(m // tm, n // tn),
        in_specs=[pl.BlockSpec((tm, k), lambda i, j: (i, 0)), pl.BlockSpec((k, tn), lambda i, j: (0, j))],
        out_specs=pl.BlockSpec((tm, tn), lambda i, j: (i, j)),
        out_shape=jax.ShapeDtypeStruct((m, n), F32),
        compiler_params=pltpu.CompilerParams(dimension_semantics=("parallel", "parallel"),
                                             vmem_limit_bytes=VMEM_LIMIT),
        name=name)(a, b)


def _mm_nt(g, w, name):
    m, n = g.shape
    k, _ = w.shape
    tm = _pick(m, (512, 256, 128, 64, 32, 16))
    tn = _pick(n, (1024, 512, 256, 128))

    def body(g_ref, w_ref, o_ref):
        part = _dot_nt(g_ref[...].astype(BF16), w_ref[...])

        @pl.when(pl.program_id(1) == 0)
        def _():
            o_ref[...] = part

        @pl.when(pl.program_id(1) > 0)
        def _():
            o_ref[...] += part

    return pl.pallas_call(
        body, grid=(m // tm, n // tn),
        in_specs=[pl.BlockSpec((tm, tn), lambda i, j: (i, j)), pl.BlockSpec((k, tn), lambda i, j: (0, j))],
        out_specs=pl.BlockSpec((tm, k), lambda i, j: (i, 0)),
        out_shape=jax.ShapeDtypeStruct((m, k), F32),
        compiler_params=pltpu.CompilerParams(dimension_semantics=("parallel", "arbitrary"),
                                             vmem_limit_bytes=VMEM_LIMIT),
        name=name)(g, w)


def _mm_tn(a, g, name):
    m, k = a.shape
    _, n = g.shape
    tm = _pick(m, (512, 256, 128, 64, 32, 16))
    tn = _pick(n, (1024, 512, 256, 128))

    def body(a_ref, g_ref, o_ref):
        part = _dot_tn(a_ref[...], g_ref[...].astype(BF16))

        @pl.when(pl.program_id(1) == 0)
        def _():
            o_ref[...] = part

        @pl.when(pl.program_id(1) > 0)
        def _():
            o_ref[...] += part

    return pl.pallas_call(
        body, grid=(n // tn, m // tm),
        in_specs=[pl.BlockSpec((tm, k), lambda j, i: (i, 0)), pl.BlockSpec((tm, tn), lambda j, i: (i, j))],
        out_specs=pl.BlockSpec((k, tn), lambda j, i: (0, j)),
        out_shape=jax.ShapeDtypeStruct((k, n), F32),
        compiler_params=pltpu.CompilerParams(dimension_semantics=("parallel", "arbitrary"),
                                             vmem_limit_bytes=VMEM_LIMIT),
        name=name)(a, g)


def _make_mm(tag):
    @jax.custom_vjp
    def mm(a, w, proxy):
        del proxy
        return _mm_nn(a.astype(BF16), w, f"mm_{tag}_fwd")

    def fwd(a, w, proxy):
        del proxy
        ab = a.astype(BF16)
        return _mm_nn(ab, w, f"mm_{tag}_fwd"), (ab, w)

    def bwd(res, g):
        ab, w = res
        da = _mm_nt(g, w, f"mm_{tag}_da")
        dw = _mm_tn(ab, g, f"mm_{tag}_dw")
        return da, jnp.zeros_like(w), dw

    mm.defvjp(fwd, bwd)
    return mm


SLABS = (("qk", O_QK, 2 * QK_A), ("va", O_VA, V_A), ("kv", O_KB, 2 * KV_B), ("oa", O_OA, V_A), ("za", O_ZA, V_A),
         ("qb", O_QB, Q_B), ("zb", O_ZB, Q_B), ("ga", O_GA, D_MODEL), ("gb", O_GB, D_MODEL))
SLAB_FWD_TN = 1024
SLAB_TN = 512


def _slab_blocks():
    return [(off // SLAB_TN, (off + width) // SLAB_TN) for _, off, width in SLABS]


def _proj_fwd_slab(a, w, off, width, name):
    m, k = a.shape
    tm = _pick(m, (1088, 512, 256, 128))
    tn = min(SLAB_FWD_TN, width)

    def body(a_ref, b_ref, o_ref):
        o_ref[...] = _dot(a_ref[...], b_ref[...])

    return pl.pallas_call(
        body, grid=(m // tm, width // tn),
        in_specs=[pl.BlockSpec((tm, k), lambda i, j: (i, 0)), pl.BlockSpec((k, tn), lambda i, j: (0, j + off // tn))],
        out_specs=pl.BlockSpec((tm, tn), lambda i, j: (i, j)),
        out_shape=jax.ShapeDtypeStruct((m, width), F32),
        compiler_params=pltpu.CompilerParams(dimension_semantics=("parallel", "parallel"),
                                             vmem_limit_bytes=VMEM_LIMIT),
        name=name)(a, w)


def _slab_spec(tm, blocks, rows_inner):
    b, e = blocks

    def index(r, c):
        inside = (c >= b) & (c < e)
        return jnp.where(inside, r, 0), jnp.clip(c - b, 0, e - b - 1)

    if rows_inner:
        return pl.BlockSpec((tm, SLAB_TN), lambda c, r: index(r, c))
    return pl.BlockSpec((tm, SLAB_TN), lambda r, c: index(r, c))


def _proj_da(gs, w, name):
    m = gs[0].shape[0]
    k, n = w.shape
    tm = _pick(m, (512, 256, 128))
    blocks = _slab_blocks()

    def body(*refs):
        g_refs, w_ref, o_ref = refs[:len(blocks)], refs[len(blocks)], refs[len(blocks) + 1]
        c = pl.program_id(1)

        @pl.when(c == 0)
        def _():
            o_ref[...] = jnp.zeros(o_ref.shape, F32)

        for g_ref, (b, e) in zip(g_refs, blocks):
            @pl.when((c >= b) & (c < e))
            def _(g_ref=g_ref):
                o_ref[...] += _dot_nt(g_ref[...].astype(BF16), w_ref[...])

    return pl.pallas_call(
        body, grid=(m // tm, n // SLAB_TN),
        in_specs=[_slab_spec(tm, blk, False) for blk in blocks] + [pl.BlockSpec((k, SLAB_TN), lambda r, c: (0, c))],
        out_specs=pl.BlockSpec((tm, k), lambda r, c: (r, 0)),
        out_shape=jax.ShapeDtypeStruct((m, k), F32),
        compiler_params=pltpu.CompilerParams(dimension_semantics=("parallel", "arbitrary"),
                                             vmem_limit_bytes=VMEM_LIMIT),
        name=name)(*gs, w)


def _proj_dw(a, gs, n, name):
    m, k = a.shape
    tm = _pick(m, (512, 256, 128))
    blocks = _slab_blocks()

    def body(*refs):
        a_ref, g_refs, o_ref = refs[0], refs[1:1 + len(blocks)], refs[1 + len(blocks)]
        c, r = pl.program_id(0), pl.program_id(1)

        @pl.when(r == 0)
        def _():
            o_ref[...] = jnp.zeros(o_ref.shape, F32)

        for g_ref, (b, e) in zip(g_refs, blocks):
            @pl.when((c >= b) & (c < e))
            def _(g_ref=g_ref):
                o_ref[...] += _dot_tn(a_ref[...], g_ref[...].astype(BF16))

    return pl.pallas_call(
        body, grid=(n // SLAB_TN, m // tm),
        in_specs=[pl.BlockSpec((tm, k), lambda c, r: (r, 0))] + [_slab_spec(tm, blk, True) for blk in blocks],
        out_specs=pl.BlockSpec((k, SLAB_TN), lambda c, r: (0, c)),
        out_shape=jax.ShapeDtypeStruct((k, n), F32),
        compiler_params=pltpu.CompilerParams(dimension_semantics=("parallel", "arbitrary"),
                                             vmem_limit_bytes=VMEM_LIMIT),
        name=name)(a, *gs)


LN_ROWS = 256


def _ln_stats(x):
    mu = jnp.mean(x, axis=-1, keepdims=True)
    xc = x - mu
    rstd = lax.rsqrt(jnp.mean(xc * xc, axis=-1, keepdims=True) + EPS)
    return xc * rstd, rstd


def _ln_bwd(dxh, xh, rstd):
    return rstd * (dxh - jnp.mean(dxh, axis=-1, keepdims=True) - xh * jnp.mean(dxh * xh, axis=-1, keepdims=True))


def _seg_maps(t, tc):
    cb, nb = tc // LN_ROWS, t // LN_ROWS
    ctx_blk = lambda i: jnp.where(i < cb, i, jnp.clip(i - cb - nb, 0, cb - 1))
    lat_blk = lambda i: jnp.clip(i - cb, 0, nb - 1)
    return cb, nb, ctx_blk, lat_blk


def _ln_mod_fwd(x, ctx, mod):
    t, tc = x.shape[0], ctx.shape[0]
    cb, nb, ctx_blk, lat_blk = _seg_maps(t, tc)

    def body(x_ref, c_ref, m_ref, o_ref):
        i = pl.program_id(0)
        lat = (i >= cb) & (i < cb + nb)

        @pl.when(lat)
        def _():
            xh, _ = _ln_stats(x_ref[...])
            o_ref[...] = (xh * (1 + m_ref[0:1, :]) + m_ref[1:2, :]).astype(BF16)

        @pl.when(jnp.logical_not(lat))
        def _():
            xh, _ = _ln_stats(c_ref[...])
            o_ref[...] = (xh * (1 + m_ref[2:3, :]) + m_ref[3:4, :]).astype(BF16)

    blk = lambda f: pl.BlockSpec((LN_ROWS, D_MODEL), lambda i: (f(i), 0))
    return pl.pallas_call(
        body, grid=(2 * cb + nb,),
        in_specs=[blk(lat_blk), blk(ctx_blk), pl.BlockSpec((4, D_MODEL), lambda i: (0, 0))],
        out_specs=pl.BlockSpec((LN_ROWS, D_MODEL), lambda i: (i, 0)),
        out_shape=jax.ShapeDtypeStruct((t + 2 * tc, D_MODEL), BF16),
        compiler_params=pltpu.CompilerParams(dimension_semantics=("parallel",), vmem_limit_bytes=VMEM_LIMIT),
        name="ln_mod")(x, ctx, mod)


def _ln_mod_bwd(du_a, du_b, x, ctx, mod):
    t, tc = x.shape[0], ctx.shape[0]
    cb, nb, ctx_blk, lat_blk = _seg_maps(t, tc)

    def body(da_ref, db_ref, x_ref, c_ref, m_ref, dx_ref, dm_ref):
        i = pl.program_id(0)
        lat = (i >= cb) & (i < cb + nb)

        @pl.when(i == 0)
        def _():
            dm_ref[...] = jnp.zeros(dm_ref.shape, F32)

        du = da_ref[...] + db_ref[...]

        @pl.when(lat)
        def _():
            xh, rstd = _ln_stats(x_ref[...])
            dm_ref[0:1, :] += jnp.sum(du * xh, axis=0, keepdims=True)
            dm_ref[1:2, :] += jnp.sum(du, axis=0, keepdims=True)
            dx_ref[...] = _ln_bwd(du * (1 + m_ref[0:1, :]), xh, rstd)

        @pl.when(jnp.logical_not(lat))
        def _():
            xh, _ = _ln_stats(c_ref[...])
            dm_ref[2:3, :] += jnp.sum(du * xh, axis=0, keepdims=True)
            dm_ref[3:4, :] += jnp.sum(du, axis=0, keepdims=True)

    blk = lambda f: pl.BlockSpec((LN_ROWS, D_MODEL), lambda i: (f(i), 0))
    row = pl.BlockSpec((LN_ROWS, D_MODEL), lambda i: (i, 0))
    vec = pl.BlockSpec((4, D_MODEL), lambda i: (0, 0))
    return pl.pallas_call(
        body, grid=(2 * cb + nb,),
        in_specs=[row, row, blk(lat_blk), blk(ctx_blk), vec],
        out_specs=[blk(lat_blk), vec],
        out_shape=[jax.ShapeDtypeStruct((t, D_MODEL), F32), jax.ShapeDtypeStruct((4, D_MODEL), F32)],
        compiler_params=pltpu.CompilerParams(dimension_semantics=("arbitrary",), vmem_limit_bytes=VMEM_LIMIT),
        name="ln_mod_bwd")(du_a, du_b, x, ctx, mod)


@jax.custom_vjp
def _ln_project(x, ctx, mod, w_main, w_if, pr_main, pr_if):
    return _ln_project_fwd(x, ctx, mod, w_main, w_if, pr_main, pr_if)[0]


def _ln_project_fwd(x, ctx, mod, w_main, w_if, pr_main, pr_if):
    del pr_main, pr_if
    ub = _ln_mod_fwd(x, ctx, mod)
    slabs = tuple(_proj_fwd_slab(ub, w_main, off, width, "proj_" + nm) for nm, off, width in SLABS)
    return (slabs, _mm_nn(ub, w_if, "proj_if")), (x, ctx, mod, ub, w_main, w_if)


def _ln_project_bwd(res, cot):
    x, ctx, mod, ub, w_main, w_if = res
    gs, g_if = cot
    dx, dmod = _ln_mod_bwd(_proj_da(gs, w_main, "proj_da"), _mm_nt(g_if, w_if, "proj_if_da"), x, ctx, mod)
    return (dx, jnp.zeros_like(ctx), dmod, jnp.zeros_like(w_main), jnp.zeros_like(w_if),
            _proj_dw(ub, gs, w_main.shape[1], "proj_dw"), _mm_tn(ub, g_if, "proj_if_dw"))


_ln_project.defvjp(_ln_project_fwd, _ln_project_bwd)


def _head_fwd(x, out, target, vecs):
    t = x.shape[0]

    def body(x_ref, o_ref, t_ref, v_ref, l_ref):
        @pl.when(pl.program_id(0) == 0)
        def _():
            l_ref[...] = jnp.zeros(l_ref.shape, F32)

        rh, _ = _ln_stats(ALPHA * x_ref[...] + v_ref[0:1, :] * o_ref[...])
        err = rh * v_ref[1:2, :] + v_ref[2:3, :] - t_ref[...]
        part = jnp.sum(jnp.mean(err * err, axis=-1, keepdims=True), axis=0, keepdims=True)
        l_ref[...] += 0.5 * part

    row = pl.BlockSpec((LN_ROWS, D_MODEL), lambda i: (i, 0))
    return pl.pallas_call(
        body, grid=(t // LN_ROWS,),
        in_specs=[row, row, row, pl.BlockSpec((3, D_MODEL), lambda i: (0, 0))],
        out_specs=pl.BlockSpec((1, 128), lambda i: (0, 0)),
        out_shape=jax.ShapeDtypeStruct((1, 128), F32),
        compiler_params=pltpu.CompilerParams(dimension_semantics=("arbitrary",), vmem_limit_bytes=VMEM_LIMIT),
        name="loss_head")(x, out, target, vecs)


def _head_bwd(g, x, out, target, vecs):
    t = x.shape[0]

    def body(g_ref, x_ref, o_ref, t_ref, v_ref, dx_ref, do_ref, dv_ref):
        @pl.when(pl.program_id(0) == 0)
        def _():
            dv_ref[...] = jnp.zeros(dv_ref.shape, F32)

        o = o_ref[...]
        gate, ln_w = v_ref[0:1, :], v_ref[1:2, :]
        rh, rstd = _ln_stats(ALPHA * x_ref[...] + gate * o)
        dy = (rh * ln_w + v_ref[2:3, :] - t_ref[...]) * (g_ref[0:1, 0:1] * (1.0 / D_MODEL))
        dv_ref[1:2, :] += jnp.sum(dy * rh, axis=0, keepdims=True)
        dv_ref[2:3, :] += jnp.sum(dy, axis=0, keepdims=True)
        dr = _ln_bwd(dy * ln_w, rh, rstd)
        dv_ref[0:1, :] += jnp.sum(dr * o, axis=0, keepdims=True)
        dx_ref[...] = ALPHA * dr
        do_ref[...] = gate * dr

    row = pl.BlockSpec((LN_ROWS, D_MODEL), lambda i: (i, 0))
    vec = pl.BlockSpec((3, D_MODEL), lambda i: (0, 0))
    return pl.pallas_call(
        body, grid=(t // LN_ROWS,),
        in_specs=[pl.BlockSpec((1, 128), lambda i: (0, 0)), row, row, row, vec],
        out_specs=[row, row, vec],
        out_shape=[jax.ShapeDtypeStruct((t, D_MODEL), F32), jax.ShapeDtypeStruct((t, D_MODEL), F32),
                   jax.ShapeDtypeStruct((3, D_MODEL), F32)],
        compiler_params=pltpu.CompilerParams(dimension_semantics=("arbitrary",), vmem_limit_bytes=VMEM_LIMIT),
        name="loss_head_bwd")(g, x, out, target, vecs)


@jax.custom_vjp
def _loss_head(x, out, target, gate, ln_w, ln_b):
    return _head_fwd(x, out, target, jnp.stack([gate, ln_w, ln_b]))[0, 0]


def _loss_head_fwd(x, out, target, gate, ln_w, ln_b):
    vecs = jnp.stack([gate, ln_w, ln_b])
    return _head_fwd(x, out, target, vecs)[0, 0], (x, out, target, vecs)


def _loss_head_bwd(res, g):
    x, out, target, vecs = res
    dx, dout, dv = _head_bwd(jnp.full((1, 128), g, F32), x, out, target, vecs)
    return dx, dout, jnp.zeros_like(target), dv[0], dv[1], dv[2]


_loss_head.defvjp(_loss_head_fwd, _loss_head_bwd)


ATT_SCALE = HD_B ** -0.5
GROUP = NH_B // NKV_B


LOG2E, LN2 = 1.4426950408889634, 0.6931471805599453
ATT_C = ATT_SCALE * LOG2E
STRIP_Q, STRIP_K = 128, 256


def _attn_tiles(t, n):
    return _pick(t, (512, 256, 128)), _pick(n, (768, 512, 256))


def _attn_fwd(q, k, v):
    t, n = q.shape[0], k.shape[0]
    tq, tk = _attn_tiles(t, n)
    nk = n // tk

    def body(q_ref, k_ref, v_ref, o_ref, lse_ref, m_sc, acc_sc):
        j = pl.program_id(2)

        @pl.when(j == 0)
        def _():
            m_sc[...] = jnp.full(m_sc.shape, -jnp.inf, F32)
            acc_sc[...] = jnp.zeros(acc_sc.shape, F32)

        kb = k_ref[...]
        v_ones = jnp.concatenate([v_ref[...], jnp.ones((tk, HD_B), BF16)], axis=1)
        for g in range(GROUP):
            s2 = _dot_nt(q_ref[:, g * HD_B:(g + 1) * HD_B], kb) * ATT_C
            m_prev = m_sc[g]
            m_new = jnp.maximum(m_prev, jnp.max(s2, axis=-1, keepdims=True))
            p = jnp.exp2(s2 - m_new).astype(BF16)
            acc_sc[g] = jnp.exp2(m_prev - m_new) * acc_sc[g] + _dot(p, v_ones)
            m_sc[g] = m_new

        @pl.when(j == nk - 1)
        def _():
            for g in range(GROUP):
                cols = slice(g * HD_B, (g + 1) * HD_B)
                l = acc_sc[g, :, HD_B:]
                o_ref[:, cols] = acc_sc[g, :, :HD_B] / l
                lse_ref[:, cols] = m_sc[g] + jnp.log(l) * LOG2E

    qspec = pl.BlockSpec((tq, GROUP * HD_B), lambda kh, i, j: (i, kh))
    kspec = pl.BlockSpec((tk, HD_B), lambda kh, i, j: (j, kh))
    return pl.pallas_call(
        body, grid=(NKV_B, t // tq, nk),
        in_specs=[qspec, kspec, kspec], out_specs=[qspec, qspec],
        out_shape=[jax.ShapeDtypeStruct((t, Q_B), F32), jax.ShapeDtypeStruct((t, Q_B), F32)],
        scratch_shapes=[pltpu.VMEM((GROUP, tq, 1), F32), pltpu.VMEM((GROUP, tq, 2 * HD_B), F32)],
        compiler_params=pltpu.CompilerParams(dimension_semantics=("parallel", "parallel", "arbitrary"),
                                             vmem_limit_bytes=VMEM_LIMIT),
        name="attn_fwd")(q, k, v)


def _attn_dq(q, k, v, do, lse, delta):
    t, n = q.shape[0], k.shape[0]
    tq, tk = _attn_tiles(t, n)
    nk = n // tk

    def body(q_ref, k_ref, v_ref, do_ref, lse_ref, dl_ref, dq_ref):
        j = pl.program_id(2)
        kb, vb = k_ref[...], v_ref[...]
        parts = []
        for g in range(GROUP):
            cols = slice(g * HD_B, (g + 1) * HD_B)
            p = jnp.exp2(_dot_nt(q_ref[:, cols], kb) * ATT_C - lse_ref[:, g * HD_B:g * HD_B + 1])
            dp = _dot_nt(do_ref[:, cols], vb)
            ds = p * (dp - dl_ref[:, g * HD_B:g * HD_B + 1])
            parts.append(_dot(ds.astype(BF16), kb))

        @pl.when(j == 0)
        def _():
            for g in range(GROUP):
                dq_ref[:, g * HD_B:(g + 1) * HD_B] = parts[g]

        @pl.when(j > 0)
        def _():
            for g in range(GROUP):
                dq_ref[:, g * HD_B:(g + 1) * HD_B] += parts[g]

        @pl.when(j == nk - 1)
        def _():
            dq_ref[...] = dq_ref[...] * ATT_SCALE

    qspec = pl.BlockSpec((tq, GROUP * HD_B), lambda kh, i, j: (i, kh))
    kspec = pl.BlockSpec((tk, HD_B), lambda kh, i, j: (j, kh))
    return pl.pallas_call(
        body, grid=(NKV_B, t // tq, n // tk),
        in_specs=[qspec, kspec, kspec, qspec, qspec, qspec],
        out_specs=qspec,
        out_shape=jax.ShapeDtypeStruct((t, Q_B), F32),
        compiler_params=pltpu.CompilerParams(dimension_semantics=("parallel", "parallel", "arbitrary"),
                                             vmem_limit_bytes=VMEM_LIMIT),
        name="attn_dq")(q, k, v, do, lse, delta)


def _attn_dkv(q, k, v, do, lse_t, delta_t):
    t, n = q.shape[0], k.shape[0]
    tq, tk = _attn_tiles(t, n)
    nq = t // tq
    n_r, n_c = tq // STRIP_Q, tk // STRIP_K

    def body(q_ref, k_ref, v_ref, do_ref, lse_ref, dl_ref, dk_ref, dv_ref, dk_sc, dv_sc):
        i = pl.program_id(2)

        @pl.when(i == 0)
        def _():
            dk_sc[...] = jnp.zeros(dk_sc.shape, F32)
            dv_sc[...] = jnp.zeros(dv_sc.shape, F32)

        for r in range(n_r):
            rows = slice(r * STRIP_Q, (r + 1) * STRIP_Q)
            for c in range(n_c):
                kv = slice(c * STRIP_K, (c + 1) * STRIP_K)
                kc, vc = k_ref[kv, :], v_ref[kv, :]
                dk_part = dv_part = None
                for g in range(GROUP):
                    cols = slice(g * HD_B, (g + 1) * HD_B)
                    qg, dog = q_ref[rows, cols], do_ref[rows, cols]
                    st = _dot_nt(kc, qg)
                    pt = jnp.exp2(st * ATT_C - lse_ref[8 * g:8 * g + 1, rows])
                    dvg = _dot(pt.astype(BF16), dog)
                    dpt = _dot_nt(vc, dog)
                    dst = pt * (dpt - dl_ref[8 * g:8 * g + 1, rows])
                    dkg = _dot(dst.astype(BF16), qg)
                    dk_part = dkg if dk_part is None else dk_part + dkg
                    dv_part = dvg if dv_part is None else dv_part + dvg
                dk_sc[kv, :] += dk_part
                dv_sc[kv, :] += dv_part

        @pl.when(i == nq - 1)
        def _():
            dk_ref[...] = dk_sc[...] * ATT_SCALE
            dv_ref[...] = dv_sc[...]

    qspec = pl.BlockSpec((tq, GROUP * HD_B), lambda kh, j, i: (i, kh))
    tspec = pl.BlockSpec((8 * GROUP, tq), lambda kh, j, i: (kh, i))
    kspec = pl.BlockSpec((tk, HD_B), lambda kh, j, i: (j, kh))
    return pl.pallas_call(
        body, grid=(NKV_B, n // tk, nq),
        in_specs=[qspec, kspec, kspec, qspec, tspec, tspec],
        out_specs=[kspec, kspec],
        out_shape=[jax.ShapeDtypeStruct((n, KV_B), F32), jax.ShapeDtypeStruct((n, KV_B), F32)],
        scratch_shapes=[pltpu.VMEM((tk, HD_B), F32), pltpu.VMEM((tk, HD_B), F32)],
        compiler_params=pltpu.CompilerParams(dimension_semantics=("parallel", "parallel", "arbitrary"),
                                             vmem_limit_bytes=VMEM_LIMIT),
        name="attn_dkv")(q, k, v, do, lse_t, delta_t)


def _attention_bwd(res, do):
    qb, kb, vb, o, lse = res
    t = qb.shape[0]
    delta = jnp.sum((do * o).reshape(t, NH_B, HD_B), axis=-1)
    lse_h = lse.reshape(t, NH_B, HD_B)[:, :, 0]
    delta_b = jnp.broadcast_to(delta[:, :, None], (t, NH_B, HD_B)).reshape(t, Q_B)
    lse_t = jnp.broadcast_to(lse_h.T[:, None, :], (NH_B, 8, t)).reshape(NH_B * 8, t)
    delta_t = jnp.broadcast_to(delta.T[:, None, :], (NH_B, 8, t)).reshape(NH_B * 8, t)
    dob = do.astype(BF16)
    dq = _attn_dq(qb, kb, vb, dob, lse, delta_b)
    dk, dv = _attn_dkv(qb, kb, vb, dob, lse_t, delta_t)
    return dq, dk, dv


def _swap32(y):
    lane = lax.broadcasted_iota(jnp.int32, y.shape, 1)
    return jnp.where((lane // 32) % 2 == 0, pltpu.roll(y, 96, 1), pltpu.roll(y, 32, 1))


def _norm_rope_fwd(x, w, cos, sin, name):
    r, width = x.shape
    heads = width // HD_B
    tr = _pick(r, (256, 128))

    def body(x_ref, w_ref, c_ref, s_ref, o_ref):
        w, c, s = w_ref[...], c_ref[...], s_ref[...]
        for h in range(heads):
            cols = slice(h * HD_B, (h + 1) * HD_B)
            xh = x_ref[:, cols]
            y = xh * lax.rsqrt(jnp.mean(xh * xh, axis=-1, keepdims=True) + EPS) * w
            o_ref[:, cols] = (y * c + _swap32(y) * s).astype(o_ref.dtype)

    row = pl.BlockSpec((tr, width), lambda i: (i, 0))
    tab = pl.BlockSpec((tr, HD_B), lambda i: (i, 0))
    return pl.pallas_call(
        body, grid=(r // tr,),
        in_specs=[row, pl.BlockSpec((1, HD_B), lambda i: (0, 0)), tab, tab], out_specs=row,
        out_shape=jax.ShapeDtypeStruct((r, width), BF16),
        compiler_params=pltpu.CompilerParams(dimension_semantics=("parallel",), vmem_limit_bytes=VMEM_LIMIT),
        name=name)(x, w, cos, sin)


def _norm_rope_bwd(x, w, cos, sin, dy, name):
    r, width = x.shape
    heads = width // HD_B
    tr = _pick(r, (256, 128))

    def body(x_ref, w_ref, c_ref, s_ref, dy_ref, dx_ref, dw_ref):
        @pl.when(pl.program_id(0) == 0)
        def _():
            dw_ref[...] = jnp.zeros(dw_ref.shape, F32)

        w, c, s = w_ref[...], c_ref[...], s_ref[...]
        dw = jnp.zeros((1, HD_B), F32)
        for h in range(heads):
            cols = slice(h * HD_B, (h + 1) * HD_B)
            xh, dyh = x_ref[:, cols], dy_ref[:, cols]
            rs = lax.rsqrt(jnp.mean(xh * xh, axis=-1, keepdims=True) + EPS)
            dn = dyh * c + _swap32(dyh * s)
            dw = dw + jnp.sum(dn * (xh * rs), axis=0, keepdims=True)
            dxn = dn * w
            dx_ref[:, cols] = rs * dxn - xh * (rs * rs * rs * jnp.mean(dxn * xh, axis=-1, keepdims=True))
        dw_ref[...] += dw

    row = pl.BlockSpec((tr, width), lambda i: (i, 0))
    tab = pl.BlockSpec((tr, HD_B), lambda i: (i, 0))
    vec = pl.BlockSpec((1, HD_B), lambda i: (0, 0))
    return pl.pallas_call(
        body, grid=(r // tr,),
        in_specs=[row, vec, tab, tab, row], out_specs=[row, vec],
        out_shape=[jax.ShapeDtypeStruct((r, width), F32), jax.ShapeDtypeStruct((1, HD_B), F32)],
        compiler_params=pltpu.CompilerParams(dimension_semantics=("arbitrary",), vmem_limit_bytes=VMEM_LIMIT),
        name=name)(x, w, cos, sin, dy)


def _rope_tables(t):
    pos = jnp.arange(t)
    row = (pos // GRID_W).astype(F32)
    col = (pos % GRID_W).astype(F32)
    inv = ROPE_THETA ** (-jnp.arange(0, ROT_HALF, 2, dtype=F32) / ROT_HALF)
    ar, ac = row[:, None] * inv[None], col[:, None] * inv[None]
    cos = jnp.concatenate([jnp.cos(ar), jnp.cos(ar), jnp.cos(ac), jnp.cos(ac)], -1)
    sin = jnp.concatenate([-jnp.sin(ar), jnp.sin(ar), -jnp.sin(ac), jnp.sin(ac)], -1)
    return cos, sin


def _gqa_tables(t, n):
    cos, sin = _rope_tables(t)
    cos_k = jnp.concatenate([jnp.ones((n - t, HD_B), F32), cos], 0)
    sin_k = jnp.concatenate([jnp.zeros((n - t, HD_B), F32), sin], 0)
    return cos, sin, cos_k, sin_k


@jax.custom_vjp
def _gqa(pq, pk, pv, qw, kw):
    return _gqa_fwd(pq, pk, pv, qw, kw)[0]


def _gqa_fwd(pq, pk, pv, qw, kw):
    cos, sin, cos_k, sin_k = _gqa_tables(pq.shape[0], pk.shape[0])
    q = _norm_rope_fwd(pq, qw[None], cos, sin, "q_norm_rope")
    k = _norm_rope_fwd(pk, kw[None], cos_k, sin_k, "k_norm_rope")
    vb = pv.astype(BF16)
    o, lse = _attn_fwd(q, k, vb)
    return o, (pq, pk, qw, kw, q, k, vb, o, lse)


def _gqa_bwd(res, do):
    pq, pk, qw, kw, q, k, vb, o, lse = res
    cos, sin, cos_k, sin_k = _gqa_tables(pq.shape[0], pk.shape[0])
    dq, dk, dv = _attention_bwd((q, k, vb, o, lse), do)
    dpq, dqw = _norm_rope_bwd(pq, qw[None], cos, sin, dq, "q_norm_rope_bwd")
    dpk, dkw = _norm_rope_bwd(pk, kw[None], cos_k, sin_k, dk, "k_norm_rope_bwd")
    return dpq, dpk, dv, dqw[0], dkw[0]


_gqa.defvjp(_gqa_fwd, _gqa_bwd)


def _mlstm_chunk_forward(q, k, v, lir, f_pre, s0, n0, m0, reverse):
    L = q.shape[0]
    lfr = jnp.minimum(f_pre, 0.0) - jnp.log1p(jnp.exp(-jnp.abs(f_pre)))
    ti = lax.broadcasted_iota(jnp.int32, (L, L), 0)
    si = lax.broadcasted_iota(jnp.int32, (L, L), 1)
    seen = (si >= ti) if reverse else (si <= ti)
    seen_t = (ti >= si) if reverse else (ti <= si)
    eye = ti == si
    lic = jnp.sum(jnp.where(eye, lir, 0.0), axis=1, keepdims=True)
    lfc = jnp.sum(jnp.where(eye, lfr, 0.0), axis=1, keepdims=True)
    b_col = jnp.sum(jnp.where(seen, lfr, 0.0), axis=1, keepdims=True)
    b_row = jnp.sum(jnp.where(seen_t, lfc, 0.0), axis=0, keepdims=True)
    d = jnp.where(seen, b_col - b_row + lir, -jnp.inf)
    m = jnp.maximum(b_col + m0, jnp.max(d, axis=1, keepdims=True))
    w = jnp.exp(d - m)
    a = jnp.exp(b_col + m0 - m)
    qm, km, vm = q.astype(MX), k.astype(MX), v.astype(MX)
    s = _dot_nt(qm, km) * w
    qs = _dot(qm, s0.astype(MX))
    num = a * qs + _dot(s.astype(MX), vm)
    qn = jnp.sum(q * n0, axis=1, keepdims=True)
    den = a * qn + jnp.sum(s, axis=1, keepdims=True)
    floor = jnp.exp(-m)
    dd = jnp.maximum(jnp.abs(den), floor)
    b_last = jnp.sum(lfr, axis=1, keepdims=True)
    m_end = jnp.maximum(b_last + m0, jnp.max(b_last - b_row + lir, axis=1, keepdims=True))
    w_end = jnp.exp(b_last - b_col + lic - m_end)
    a_end = jnp.exp(b_last + m0 - m_end)
    return dict(eye=eye, seen=seen, w=w, a=a, s=s, qs=qs, num=num, qn=qn, den=den, floor=floor, dd=dd,
                m_end=m_end, w_end=w_end, a_end=a_end, qm=qm, km=km, vm=vm)


def _mlstm_fwd_call(q, k, v, gr, n, row_off, reverse):
    L = MLSTM_CHUNK
    nc, off = n // L, row_off // L
    pos = (lambda i: nc - 1 - i) if reverse else (lambda i: i)

    def body(q_ref, k_ref, v_ref, gr_ref, h_ref, s0_ref, n0_ref, m0_ref, s_sc, n_sc, m_sc):
        @pl.when(pl.program_id(1) == 0)
        def _():
            s_sc[...] = jnp.zeros(s_sc.shape, F32)
            n_sc[...] = jnp.zeros(n_sc.shape, F32)
            m_sc[...] = jnp.full(m_sc.shape, M_INIT, F32)

        s0, n0, m0 = s_sc[...], n_sc[...], m_sc[...]
        s0_ref[0, 0] = s0
        n0_ref[0, 0] = n0
        m0_ref[0, 0] = jnp.broadcast_to(m0, (1, DK_A))
        k, v = k_ref[...], v_ref[...]
        f = _mlstm_chunk_forward(q_ref[...], k, v, gr_ref[0, 0], gr_ref[1, 0], s0, n0, m0, reverse)
        h_ref[...] = f["num"] / f["dd"]
        s_sc[...] = f["a_end"] * s0 + _dot_tn(f["km"], (f["w_end"] * v).astype(MX))
        n_sc[...] = f["a_end"] * n0 + jnp.sum(f["w_end"] * k, axis=0, keepdims=True)
        m_sc[...] = f["m_end"]

    qk_spec = pl.BlockSpec((L, DK_A), lambda h, i: (off + pos(i), h))
    v_spec = pl.BlockSpec((L, DV_A), lambda h, i: (off + pos(i), h))
    gr_spec = pl.BlockSpec((2, 1, 1, L), lambda h, i: (0, h, 0, off + pos(i)))
    h_spec = pl.BlockSpec((L, DV_A), lambda h, i: (pos(i), h))
    st_spec = pl.BlockSpec((1, 1, DK_A, DV_A), lambda h, i: (h, pos(i), 0, 0))
    vec_spec = pl.BlockSpec((1, 1, 1, DK_A), lambda h, i: (h, pos(i), 0, 0))
    return pl.pallas_call(
        body, grid=(NH_A, nc),
        in_specs=[qk_spec, qk_spec, v_spec, gr_spec],
        out_specs=[h_spec, st_spec, vec_spec, vec_spec],
        out_shape=[jax.ShapeDtypeStruct((n, V_A), F32), jax.ShapeDtypeStruct((NH_A, nc, DK_A, DV_A), F32),
                   jax.ShapeDtypeStruct((NH_A, nc, 1, DK_A), F32), jax.ShapeDtypeStruct((NH_A, nc, 1, DK_A), F32)],
        scratch_shapes=[pltpu.VMEM((DK_A, DV_A), F32), pltpu.VMEM((1, DK_A), F32), pltpu.VMEM((1, 1), F32)],
        compiler_params=pltpu.CompilerParams(dimension_semantics=("parallel", "arbitrary"),
                                             vmem_limit_bytes=VMEM_LIMIT),
        name="mlstm_fwd")(q, k, v, gr)


def _mlstm_bwd_call(q, k, v, gr, s0_all, n0_all, m0_all, dh, n, row_off, reverse):
    L = MLSTM_CHUNK
    nc, off = n // L, row_off // L
    pos = (lambda i: i) if reverse else (lambda i: nc - 1 - i)

    def body(q_ref, k_ref, v_ref, gr_ref, s0_ref, n0_ref, m0_ref, dh_ref,
             dq_ref, dk_ref, dv_ref, dg_ref, ds_sc, dn_sc):
        @pl.when(pl.program_id(1) == 0)
        def _():
            ds_sc[...] = jnp.zeros(ds_sc.shape, F32)
            dn_sc[...] = jnp.zeros(dn_sc.shape, F32)

        q, k, v = q_ref[...], k_ref[...], v_ref[...]
        s0, n0, m0 = s0_ref[0, 0], n0_ref[0, 0], m0_ref[0, 0][:, 0:1]
        f = _mlstm_chunk_forward(q, k, v, gr_ref[0, 0], gr_ref[1, 0], s0, n0, m0, reverse)
        w, a, s = f["w"], f["a"], f["s"]
        qm, km, vm, w_end, a_end = f["qm"], f["km"], f["vm"], f["w_end"], f["a_end"]
        ds1, dn1 = ds_sc[...], dn_sc[...]
        ds1m, s0m = ds1.astype(MX), s0.astype(MX)

        inv = 1.0 / f["dd"]
        dh = dh_ref[...]
        dnum = dh * inv
        ddd = -jnp.sum(dh * (f["num"] * inv), axis=1, keepdims=True) * inv
        dden = jnp.where(jnp.abs(f["den"]) > f["floor"], jnp.sign(f["den"]) * ddd, 0.0)
        adn = (a * dnum).astype(MX)
        dnm = dnum.astype(MX)
        ds_tot = _dot_nt(dnm, vm) + dden
        dsr = (ds_tot * w).astype(MX)
        e = ds_tot * s
        wv = (w_end * v).astype(MX)
        kds = _dot(km, ds1m)
        dq_ref[...] = _dot_nt(adn, s0m) + _dot(dsr, km) + (dden * a) * n0
        dk_ref[...] = _dot_tn(dsr, qm) + _dot_nt(wv, ds1m) + w_end * dn1
        dv_ref[...] = _dot_tn(s.astype(MX), dnm) + w_end * kds

        eye = f["eye"]
        to_col = lambda r: jnp.sum(jnp.where(eye, r, 0.0), axis=1, keepdims=True)
        to_row = lambda c: jnp.sum(jnp.where(eye, c, 0.0), axis=0, keepdims=True)
        g_a = (jnp.sum(dnum * f["qs"], axis=1, keepdims=True) + dden * f["qn"]) * a
        g_w = (jnp.sum(v * kds, axis=1, keepdims=True) + jnp.sum(k * dn1, axis=1, keepdims=True)) * w_end
        g_end = (jnp.sum(jnp.sum(ds1 * s0, axis=1, keepdims=True), axis=0, keepdims=True)
                 + jnp.sum(dn1 * n0, axis=1, keepdims=True)) * a_end
        col_e = jnp.sum(e, axis=0, keepdims=True)
        db = jnp.sum(e, axis=1, keepdims=True) - to_col(col_e) + g_a - g_w
        last = lax.broadcasted_iota(jnp.int32, (L, 1), 0) == (0 if reverse else L - 1)
        db = db + jnp.where(last, jnp.sum(g_w, axis=0, keepdims=True) + g_end, 0.0)
        dg_ref[0, 0] = col_e + to_row(g_w)
        dlf = jnp.sum(jnp.where(f["seen"], db, 0.0), axis=0, keepdims=True)
        dg_ref[1, 0] = dlf * jax.nn.sigmoid(-gr_ref[1, 0])

        ds_sc[...] = a_end * ds1 + _dot_tn(qm, adn)
        dn_sc[...] = a_end * dn1 + jnp.sum((dden * a) * q, axis=0, keepdims=True)

    qk_spec = pl.BlockSpec((L, DK_A), lambda h, i: (off + pos(i), h))
    v_spec = pl.BlockSpec((L, DV_A), lambda h, i: (off + pos(i), h))
    gr_spec = pl.BlockSpec((2, 1, 1, L), lambda h, i: (0, h, 0, off + pos(i)))
    st_spec = pl.BlockSpec((1, 1, DK_A, DV_A), lambda h, i: (h, pos(i), 0, 0))
    vec_spec = pl.BlockSpec((1, 1, 1, DK_A), lambda h, i: (h, pos(i), 0, 0))
    oqk_spec = pl.BlockSpec((L, DK_A), lambda h, i: (pos(i), h))
    ov_spec = pl.BlockSpec((L, DV_A), lambda h, i: (pos(i), h))
    og_spec = pl.BlockSpec((2, 1, 1, L), lambda h, i: (0, h, 0, pos(i)))
    return pl.pallas_call(
        body, grid=(NH_A, nc),
        in_specs=[qk_spec, qk_spec, v_spec, gr_spec, st_spec, vec_spec, vec_spec, ov_spec],
        out_specs=[oqk_spec, oqk_spec, ov_spec, og_spec],
        out_shape=[jax.ShapeDtypeStruct((n, QK_A), F32), jax.ShapeDtypeStruct((n, QK_A), F32),
                   jax.ShapeDtypeStruct((n, V_A), F32), jax.ShapeDtypeStruct((2, NH_A, 1, n), F32)],
        scratch_shapes=[pltpu.VMEM((DK_A, DV_A), F32), pltpu.VMEM((1, DK_A), F32)],
        compiler_params=pltpu.CompilerParams(dimension_semantics=("parallel", "arbitrary"),
                                             vmem_limit_bytes=VMEM_LIMIT),
        name="mlstm_bwd")(q, k, v, gr, s0_all, n0_all, m0_all, dh)


def _make_mlstm(n, row_off, reverse):
    def gate_rows(li, lf):
        return jnp.stack([li, lf]).transpose(0, 2, 1)[:, :, None, :]

    @jax.custom_vjp
    def op(q, k, v, li, lf):
        return _mlstm_fwd_call(q, k, v, gate_rows(li, lf), n, row_off, reverse)[0]

    def fwd(q, k, v, li, lf):
        gr = gate_rows(li, lf)
        h, s0, n0, m0 = _mlstm_fwd_call(q, k, v, gr, n, row_off, reverse)
        return h, (q, k, v, gr, s0, n0, m0)

    def bwd(res, dh):
        q, k, v, gr, s0, n0, m0 = res
        dq, dk, dv, dg = _mlstm_bwd_call(q, k, v, gr, s0, n0, m0, dh, n, row_off, reverse)
        rows = ((row_off, q.shape[0] - row_off - n), (0, 0))
        dg = jnp.pad(dg[:, :, 0, :].transpose(0, 2, 1), ((0, 0),) + rows)
        return jnp.pad(dq, rows), jnp.pad(dk, rows), jnp.pad(dv, rows), dg[0], dg[1]

    op.defvjp(fwd, bwd)
    return op


MERGE_ROWS = 128


def _sig(x):
    return jax.nn.sigmoid(x)


def _merge_pre_fwd(h_f, h_b, o_attn, p_oa, p_za, p_zb, mh_w, tc):
    t = o_attn.shape[0]
    tr, off = MERGE_ROWS, tc // MERGE_ROWS

    def body(hf_ref, hb_ref, oat_ref, oa_ref, za_ref, zb_ref, w_ref, a_ref, b_ref):
        for hd in range(NH_A):
            cols = slice(hd * DV_A, (hd + 1) * DV_A)
            h = hf_ref[:, cols] + hb_ref[:, cols]
            hn = h * lax.rsqrt(jnp.mean(h * h, axis=-1, keepdims=True) + EPS) * w_ref[:, cols]
            za = za_ref[:, cols]
            a_ref[:, cols] = (_sig(oa_ref[:, cols]) * hn * (za * _sig(za))).astype(BF16)
        zb = zb_ref[...]
        b_ref[...] = (oat_ref[...] * (zb * _sig(zb))).astype(BF16)

    lat = pl.BlockSpec((tr, V_A), lambda i: (i + off, 0))
    row = pl.BlockSpec((tr, V_A), lambda i: (i, 0))
    return pl.pallas_call(
        body, grid=(t // tr,),
        in_specs=[lat, row, row, lat, lat, lat, pl.BlockSpec((1, V_A), lambda i: (0, 0))],
        out_specs=[row, row],
        out_shape=[jax.ShapeDtypeStruct((t, V_A), BF16), jax.ShapeDtypeStruct((t, V_A), BF16)],
        compiler_params=pltpu.CompilerParams(dimension_semantics=("parallel",), vmem_limit_bytes=VMEM_LIMIT),
        name="merge_pre")(h_f, h_b, o_attn, p_oa, p_za, p_zb, mh_w)


def _ctx_block(i, nb, off):
    k = i - nb
    return jnp.where(i < nb, i + off, jnp.where(k < off, k, k + nb))


def _merge_pre_bwd(da, db, h_f, h_b, o_attn, p_oa, p_za, p_zb, mh_w, tc):
    t = o_attn.shape[0]
    n, r = h_f.shape[0], p_oa.shape[0]
    tr, off = MERGE_ROWS, tc // MERGE_ROWS
    nb = t // tr
    n_ctx = r // tr - nb

    def body(da_ref, db_ref, hf_ref, hb_ref, oat_ref, oa_ref, za_ref, zb_ref, w_ref,
             dhf_ref, dhb_ref, doat_ref, doa_ref, dza_ref, dzb_ref, dw_ref):
        i = pl.program_id(0)

        @pl.when(i == 0)
        def _():
            dw_ref[...] = jnp.zeros(dw_ref.shape, F32)

        @pl.when(i < nb)
        def _():
            for hd in range(NH_A):
                cols = slice(hd * DV_A, (hd + 1) * DV_A)
                h = hf_ref[:, cols] + hb_ref[:, cols]
                rs = lax.rsqrt(jnp.mean(h * h, axis=-1, keepdims=True) + EPS)
                w = w_ref[:, cols]
                hn = h * rs * w
                oa, za, g = oa_ref[:, cols], za_ref[:, cols], da_ref[:, cols]
                so, sz = _sig(oa), _sig(za)
                silu_z = za * sz
                doa_ref[:, cols] = g * hn * silu_z * so * (1.0 - so)
                dza_ref[:, cols] = g * so * hn * (sz * (1.0 + za * (1.0 - sz)))
                dhn = g * so * silu_z
                dw_ref[:, cols] += jnp.sum(dhn * (h * rs), axis=0, keepdims=True)
                dxn = dhn * w
                dh = rs * dxn - h * (rs * rs * rs * jnp.mean(dxn * h, axis=-1, keepdims=True))
                dhf_ref[:, cols] = dh
                dhb_ref[:, cols] = dh
            zb, gb, oat = zb_ref[...], db_ref[...], oat_ref[...]
            sb = _sig(zb)
            doat_ref[...] = gb * (zb * sb)
            dzb_ref[...] = gb * oat * (sb * (1.0 + zb * (1.0 - sb)))

        @pl.when(i >= nb)
        def _():
            for ref in (dhf_ref, dhb_ref, doa_ref, dza_ref, dzb_ref):
                ref[...] = jnp.zeros(ref.shape, F32)

    lati = lambda i: jnp.minimum(i, nb - 1)
    lat = pl.BlockSpec((tr, V_A), lambda i: (lati(i) + off, 0))
    row = pl.BlockSpec((tr, V_A), lambda i: (lati(i), 0))
    vec = pl.BlockSpec((1, V_A), lambda i: (0, 0))
    pout = pl.BlockSpec((tr, V_A), lambda i: (_ctx_block(i, nb, off), 0))
    hf_out = pl.BlockSpec((tr, V_A), lambda i: (jnp.where(i < nb, i + off, jnp.minimum(i - nb, off - 1)), 0))
    hb_out = pl.BlockSpec((tr, V_A), lambda i: (jnp.where(i < nb, i, nb + jnp.minimum(i - nb, off - 1)), 0))
    return pl.pallas_call(
        body, grid=(nb + n_ctx,),
        in_specs=[row, row, lat, row, row, lat, lat, lat, vec],
        out_specs=[hf_out, hb_out, row, pout, pout, pout, vec],
        out_shape=[jax.ShapeDtypeStruct((n, V_A), F32), jax.ShapeDtypeStruct((n, V_A), F32),
                   jax.ShapeDtypeStruct((t, V_A), F32), jax.ShapeDtypeStruct((r, V_A), F32),
                   jax.ShapeDtypeStruct((r, V_A), F32), jax.ShapeDtypeStruct((r, V_A), F32),
                   jax.ShapeDtypeStruct((1, V_A), F32)],
        compiler_params=pltpu.CompilerParams(dimension_semantics=("arbitrary",), vmem_limit_bytes=VMEM_LIMIT),
        name="merge_pre_bwd")(da, db, h_f, h_b, o_attn, p_oa, p_za, p_zb, mh_w)


def _merge_gate_fwd(y_a, y_b, p_ga, p_gb, tc):
    t = y_a.shape[0]
    tr, off = MERGE_ROWS, tc // MERGE_ROWS

    def body(ya_ref, yb_ref, ga_ref, gb_ref, m_ref):
        m_ref[...] = (_sig(ga_ref[...]) * ya_ref[...] + _sig(gb_ref[...]) * yb_ref[...]).astype(BF16)

    lat = pl.BlockSpec((tr, D_MODEL), lambda i: (i + off, 0))
    row = pl.BlockSpec((tr, D_MODEL), lambda i: (i, 0))
    return pl.pallas_call(
        body, grid=(t // tr,), in_specs=[row, row, lat, lat], out_specs=row,
        out_shape=jax.ShapeDtypeStruct((t, D_MODEL), BF16),
        compiler_params=pltpu.CompilerParams(dimension_semantics=("parallel",), vmem_limit_bytes=VMEM_LIMIT),
        name="merge_gate")(y_a, y_b, p_ga, p_gb)


def _merge_gate_bwd(dm, y_a, y_b, p_ga, p_gb, tc):
    t, r = y_a.shape[0], p_ga.shape[0]
    tr, off = MERGE_ROWS, tc // MERGE_ROWS
    nb = t // tr
    n_ctx = r // tr - nb

    def body(dm_ref, ya_ref, yb_ref, ga_ref, gb_ref, dya_ref, dyb_ref, dga_ref, dgb_ref):
        i = pl.program_id(0)

        @pl.when(i < nb)
        def _():
            dm = dm_ref[...]
            sa, sb = _sig(ga_ref[...]), _sig(gb_ref[...])
            dya_ref[...] = (dm * sa).astype(BF16)
            dyb_ref[...] = (dm * sb).astype(BF16)
            dga_ref[...] = dm * ya_ref[...] * sa * (1.0 - sa)
            dgb_ref[...] = dm * yb_ref[...] * sb * (1.0 - sb)

        @pl.when(i >= nb)
        def _():
            dga_ref[...] = jnp.zeros(dga_ref.shape, F32)
            dgb_ref[...] = jnp.zeros(dgb_ref.shape, F32)

    lati = lambda i: jnp.minimum(i, nb - 1)
    lat = pl.BlockSpec((tr, D_MODEL), lambda i: (lati(i) + off, 0))
    row = pl.BlockSpec((tr, D_MODEL), lambda i: (lati(i), 0))
    pout = pl.BlockSpec((tr, D_MODEL), lambda i: (_ctx_block(i, nb, off), 0))
    return pl.pallas_call(
        body, grid=(nb + n_ctx,), in_specs=[row, row, row, lat, lat], out_specs=[row, row, pout, pout],
        out_shape=[jax.ShapeDtypeStruct((t, D_MODEL), BF16), jax.ShapeDtypeStruct((t, D_MODEL), BF16),
                   jax.ShapeDtypeStruct((r, D_MODEL), F32), jax.ShapeDtypeStruct((r, D_MODEL), F32)],
        compiler_params=pltpu.CompilerParams(dimension_semantics=("arbitrary",), vmem_limit_bytes=VMEM_LIMIT),
        name="merge_gate_bwd")(dm, y_a, y_b, p_ga, p_gb)


def _make_merge_block(tc):
    @jax.custom_vjp
    def block(h_f, h_b, o_attn, p_oa, p_za, p_zb, p_ga, p_gb, mh_w, w_ba, w_bb, w_out, pr_ba, pr_bb, pr_out):
        return fwd(h_f, h_b, o_attn, p_oa, p_za, p_zb, p_ga, p_gb, mh_w, w_ba, w_bb, w_out, pr_ba, pr_bb, pr_out)[0]

    def fwd(h_f, h_b, o_attn, p_oa, p_za, p_zb, p_ga, p_gb, mh_w, w_ba, w_bb, w_out, pr_ba, pr_bb, pr_out):
        a_in, b_in = _merge_pre_fwd(h_f, h_b, o_attn, p_oa, p_za, p_zb, mh_w[None], tc)
        y_a, y_b = _mm_nn(a_in, w_ba, "merge_ya"), _mm_nn(b_in, w_bb, "merge_yb")
        m_in = _merge_gate_fwd(y_a, y_b, p_ga, p_gb, tc)
        out = _mm_nn(m_in, w_out, "merge_out")
        return out, (h_f, h_b, o_attn, p_oa, p_za, p_zb, p_ga, p_gb, mh_w, w_ba, w_bb, w_out, a_in, b_in, y_a, y_b, m_in)

    def bwd(res, dout):
        h_f, h_b, o_attn, p_oa, p_za, p_zb, p_ga, p_gb, mh_w, w_ba, w_bb, w_out, a_in, b_in, y_a, y_b, m_in = res
        dm = _mm_nt(dout, w_out, "merge_out_da")
        dw_out = _mm_tn(m_in, dout, "merge_out_dw")
        dy_a, dy_b, dga, dgb = _merge_gate_bwd(dm, y_a, y_b, p_ga, p_gb, tc)
        da, db = _mm_nt(dy_a, w_ba, "merge_ya_da"), _mm_nt(dy_b, w_bb, "merge_yb_da")
        dw_ba, dw_bb = _mm_tn(a_in, dy_a, "merge_ya_dw"), _mm_tn(b_in, dy_b, "merge_yb_dw")
        dhf, dhb, doat, doa, dza, dzb, dmh = _merge_pre_bwd(da, db, h_f, h_b, o_attn, p_oa, p_za, p_zb, mh_w[None], tc)
        z = jnp.zeros_like
        return (dhf, dhb, doat, doa, dza, dzb, dga, dgb, dmh[0], z(w_ba), z(w_bb), z(w_out), dw_ba, dw_bb, dw_out)

    block.defvjp(fwd, bwd)
    return block


def _silu(x):
    return x * jax.nn.sigmoid(x)


CONV_ROWS, CONV_HALO = 256, 8


def _make_conv(t, tc):
    r = t + 2 * tc
    width = 2 * QK_A
    nblk = r // CONV_ROWS
    cb, nb = tc // CONV_ROWS, t // CONV_ROWS
    k_scale = DK_A ** -0.5
    per = CONV_ROWS // CONV_HALO

    def taps(x_ref, prev_ref, next_ref):
        i = pl.program_id(0)
        seg_first = (i == 0) | (i == cb) | (i == cb + nb)
        seg_last = (i == cb - 1) | (i == cb + nb - 1) | (i == nblk - 1)
        x = x_ref[...]
        rows = lax.broadcasted_iota(jnp.int32, (CONV_ROWS, 1), 0)
        before = jnp.where(seg_first, 0.0, prev_ref[CONV_HALO - 1:CONV_HALO, :])
        after = jnp.where(seg_last, 0.0, next_ref[0:1, :])
        xm1 = jnp.where(rows == 0, before, pltpu.roll(x, 1, 0))
        xp1 = jnp.where(rows == CONV_ROWS - 1, after, pltpu.roll(x, CONV_ROWS - 1, 0))
        return xm1, x, xp1

    row = pl.BlockSpec((CONV_ROWS, width), lambda i: (i, 0))
    prev = pl.BlockSpec((CONV_HALO, width), lambda i: (jnp.maximum(i * per - 1, 0), 0))
    nxt = pl.BlockSpec((CONV_HALO, width), lambda i: (jnp.minimum((i + 1) * per, r // CONV_HALO - 1), 0))
    half = pl.BlockSpec((CONV_ROWS, QK_A), lambda i: (i, 0))
    wspec = pl.BlockSpec((3, width), lambda i: (0, 0))
    bspec = pl.BlockSpec((1, width), lambda i: (0, 0))
    par = pltpu.CompilerParams(dimension_semantics=("parallel",), vmem_limit_bytes=VMEM_LIMIT)
    seq = pltpu.CompilerParams(dimension_semantics=("arbitrary",), vmem_limit_bytes=VMEM_LIMIT)

    def fwd_call(x, cw, cb_):
        def body(x_ref, p_ref, n_ref, w_ref, b_ref, q_ref, k_ref):
            xm1, x0, xp1 = taps(x_ref, p_ref, n_ref)
            c = b_ref[...] + xm1 * w_ref[0:1, :] + x0 * w_ref[1:2, :] + xp1 * w_ref[2:3, :]
            y = c * jax.nn.sigmoid(c)
            q_ref[...] = y[:, :QK_A]
            k_ref[...] = y[:, QK_A:] * k_scale

        return pl.pallas_call(
            body, grid=(nblk,), in_specs=[row, prev, nxt, wspec, bspec], out_specs=[half, half],
            out_shape=[jax.ShapeDtypeStruct((r, QK_A), F32), jax.ShapeDtypeStruct((r, QK_A), F32)],
            compiler_params=par, name="conv_silu")(x, x, x, cw, cb_)

    def bwd_pre_call(dq, dk, x, cw, cb_):
        def body(dq_ref, dk_ref, x_ref, p_ref, n_ref, w_ref, b_ref, dc_ref, dw_ref, db_ref):
            @pl.when(pl.program_id(0) == 0)
            def _():
                dw_ref[...] = jnp.zeros(dw_ref.shape, F32)
                db_ref[...] = jnp.zeros(db_ref.shape, F32)

            xm1, x0, xp1 = taps(x_ref, p_ref, n_ref)
            c = b_ref[...] + xm1 * w_ref[0:1, :] + x0 * w_ref[1:2, :] + xp1 * w_ref[2:3, :]
            s = jax.nn.sigmoid(c)
            dy = jnp.concatenate([dq_ref[...], dk_ref[...] * k_scale], axis=1)
            dc = dy * (s * (1.0 + c * (1.0 - s)))
            dc_ref[...] = dc
            db_ref[...] += jnp.sum(dc, axis=0, keepdims=True)
            dw_ref[0:1, :] += jnp.sum(dc * xm1, axis=0, keepdims=True)
            dw_ref[1:2, :] += jnp.sum(dc * x0, axis=0, keepdims=True)
            dw_ref[2:3, :] += jnp.sum(dc * xp1, axis=0, keepdims=True)

        return pl.pallas_call(
            body, grid=(nblk,), in_specs=[half, half, row, prev, nxt, wspec, bspec], out_specs=[row, wspec, bspec],
            out_shape=[jax.ShapeDtypeStruct((r, width), F32), jax.ShapeDtypeStruct((3, width), F32),
                       jax.ShapeDtypeStruct((1, width), F32)],
            compiler_params=seq, name="conv_silu_bwd")(dq, dk, x, x, x, cw, cb_)

    def bwd_x_call(dc, cw):
        def body(d_ref, p_ref, n_ref, w_ref, dx_ref):
            dm1, d0, dp1 = taps(d_ref, p_ref, n_ref)
            dx_ref[...] = dm1 * w_ref[2:3, :] + d0 * w_ref[1:2, :] + dp1 * w_ref[0:1, :]

        return pl.pallas_call(
            body, grid=(nblk,), in_specs=[row, prev, nxt, wspec], out_specs=row,
            out_shape=jax.ShapeDtypeStruct((r, width), F32), compiler_params=par,
            name="conv_silu_bwd_x")(dc, dc, dc, cw)

    @jax.custom_vjp
    def op(x, cw, cb_):
        return tuple(fwd_call(x, cw, cb_[None]))

    def fwd(x, cw, cb_):
        return tuple(fwd_call(x, cw, cb_[None])), (x, cw, cb_)

    def bwd(res, cot):
        x, cw, cb_ = res
        dc, dw, db = bwd_pre_call(cot[0], cot[1], x, cw, cb_[None])
        return bwd_x_call(dc, cw), dw, db[0]

    op.defvjp(fwd, bwd)
    return op


_mm_mod = _make_mm("mod")


def _local_loss(diff, const):
    x, c, ctx, target = diff["x"], const["c"], const["ctx"], const["target"]
    t, tc = x.shape[0], ctx.shape[0]
    n, r = tc + t, t + 2 * tc

    sc = jnp.concatenate([_silu(c), _silu(diff["c_ctx"])[None], jnp.zeros((14, D_MODEL), F32)], 0)
    mod = _mm_mod(sc, const["w_mod"], diff["p_mod"])[:2] + diff["b_mod"]
    shift, scale, gate = mod[0, :D_MODEL], mod[0, D_MODEL:2 * D_MODEL], mod[0, 2 * D_MODEL:]
    shift_c, scale_c = mod[1, :D_MODEL], mod[1, D_MODEL:2 * D_MODEL]
    (p_qk, p_va, p_kv, p_oa, p_za, p_qb, p_zb, p_ga, p_gb), p_if = _ln_project(
        x, ctx, jnp.stack([scale, shift, scale_c, shift_c]), const["w_main"], const["w_if"], diff["p_main"], diff["p_if"])
    gt = p_if[:, :N_IF] + diff["b_if"]

    q_a, k_a = _make_conv(t, tc)(p_qk, diff["conv_w"], diff["conv_b"])
    v_a = p_va
    li_f, lf_f, li_b, lf_b = gt[:, 0:8], gt[:, 8:16], gt[:, 16:24], gt[:, 24:32]

    h_f = _make_mlstm(n, 0, False)(q_a, k_a, v_a, li_f, lf_f)
    h_b = _make_mlstm(n, tc, True)(q_a, k_a, v_a, li_b, lf_b)

    lat = slice(tc, n)
    o_attn = _gqa(p_qb[lat], p_kv[:n, :KV_B], p_kv[:n, KV_B:],
                  diff["q_norm_w"], diff["k_norm_w"])

    out = _make_merge_block(tc)(h_f, h_b, o_attn, p_oa, p_za, p_zb, p_ga, p_gb, diff["mh_norm_w"],
                                const["w_ba"], const["w_bb"], const["w_out"], diff["p_ba"], diff["p_bb"], diff["p_out"])

    return _loss_head(x, out, target, gate, diff["ln_w"], diff["ln_b"])


OTHER_CHIPS = [(1, 0), (0, 1), (1, 1)]


def _flip(v, bit):
    return 1 - v if bit else v


def _gather_chips(shard, name):
    def body(x_ref, o_ref, send_sems, recv_sems, local_sem):
        x, y, c = lax.axis_index("x"), lax.axis_index("y"), lax.axis_index("c")
        mine = pltpu.make_async_copy(x_ref, o_ref.at[2 * x + y], local_sem)
        mine.start()

        def copy(r, slot):
            dx, dy = OTHER_CHIPS[r]
            return pltpu.make_async_remote_copy(
                src_ref=x_ref, dst_ref=o_ref.at[slot], send_sem=send_sems.at[r], recv_sem=recv_sems.at[r],
                device_id=(_flip(x, dx), _flip(y, dy), c), device_id_type=MESH)

        sends = [copy(r, 2 * x + y) for r in range(3)]
        for cp in sends:
            cp.start()
        for r, (dx, dy) in enumerate(OTHER_CHIPS):
            copy(r, 2 * _flip(x, dx) + _flip(y, dy)).wait_recv()
        for cp in sends:
            cp.wait_send()
        mine.wait()

    return pl.pallas_call(
        body, out_shape=jax.ShapeDtypeStruct((N_CHIPS,) + shard.shape, shard.dtype),
        in_specs=[pl.BlockSpec(memory_space=pl.ANY)], out_specs=pl.BlockSpec(memory_space=pl.ANY),
        scratch_shapes=[pltpu.SemaphoreType.DMA((3,)), pltpu.SemaphoreType.DMA((3,)), pltpu.SemaphoreType.DMA],
        name=name)(shard)


def _gather_chips_halves(shard, name):
    rows, cols = shard.shape
    halves = shard.reshape(2, rows // 2, cols)

    def body(x_ref, o_ref, send_sems, recv_sems, local_sem):
        x, y, c = lax.axis_index("x"), lax.axis_index("y"), lax.axis_index("c")
        my_chip = 2 * x + y
        mine = pltpu.make_async_copy(x_ref, o_ref.at[my_chip], local_sem)
        mine.start()

        def chip_of(r):
            dx, dy = OTHER_CHIPS[r]
            return _flip(x, dx), _flip(y, dy)

        def copy(k, chip_slot, half, to, src=None):
            dst = o_ref.at[chip_slot, half]
            return pltpu.make_async_remote_copy(
                src_ref=dst if src is None else src, dst_ref=dst, send_sem=send_sems.at[k],
                recv_sem=recv_sems.at[k], device_id=to, device_id_type=MESH)

        first = [copy(r, my_chip, c, (*chip_of(r), c), src=x_ref.at[c]) for r in range(3)]
        for cp in first:
            cp.start()
        passed = []
        for r in range(3):
            px, py = chip_of(r)
            copy(r, 2 * px + py, c, (px, py, c)).wait_recv()
            passed.append(copy(3 + r, 2 * px + py, c, (x, y, 1 - c)))
            passed[-1].start()
        for r in range(3):
            px, py = chip_of(r)
            copy(3 + r, 2 * px + py, 1 - c, (x, y, 1 - c)).wait_recv()
        for cp in first + passed:
            cp.wait_send()
        mine.wait()

    out = pl.pallas_call(
        body, out_shape=jax.ShapeDtypeStruct((N_CHIPS, 2, rows // 2, cols), shard.dtype),
        in_specs=[pl.BlockSpec(memory_space=pl.ANY)], out_specs=pl.BlockSpec(memory_space=pl.ANY),
        scratch_shapes=[pltpu.SemaphoreType.DMA((6,)), pltpu.SemaphoreType.DMA((6,)), pltpu.SemaphoreType.DMA],
        name=name)(halves)
    return out.reshape(N_CHIPS, rows, cols)


def _scatter_grads(slots, name):
    def body(g_ref, o_ref, send_sems, recv_sems, local_sem):
        x, y, c = lax.axis_index("x"), lax.axis_index("y"), lax.axis_index("c")
        me, my_chip, sibling = 4 * x + 2 * y + c, 2 * x + y, (x, y, 1 - c)
        mine = pltpu.make_async_copy(g_ref.at[my_chip], o_ref.at[me], local_sem)
        mine.start()

        def chip_of(r):
            dx, dy = OTHER_CHIPS[r]
            return _flip(x, dx), _flip(y, dy)

        def copy(k, slot, to, src=None):
            dst = o_ref.at[slot]
            return pltpu.make_async_remote_copy(
                src_ref=dst if src is None else src, dst_ref=dst, send_sem=send_sems.at[k],
                recv_sem=recv_sems.at[k], device_id=to, device_id_type=MESH)

        first = [copy(0, me, sibling, src=g_ref.at[my_chip])]
        for r in range(3):
            px, py = chip_of(r)
            first.append(copy(1 + r, me, (px, py, c), src=g_ref.at[2 * px + py]))
        for cp in first:
            cp.start()
        passed = []
        for r in range(3):
            px, py = chip_of(r)
            copy(1 + r, 4 * px + 2 * py + c, (px, py, c)).wait_recv()
            passed.append(copy(4 + r, 4 * px + 2 * py + c, sibling))
            passed[-1].start()
        copy(0, 4 * x + 2 * y + 1 - c, sibling).wait_recv()
        for r in range(3):
            px, py = chip_of(r)
            copy(4 + r, 4 * px + 2 * py + 1 - c, sibling).wait_recv()
        for cp in first + passed:
            cp.wait_send()
        mine.wait()

    return pl.pallas_call(
        body, out_shape=jax.ShapeDtypeStruct((N_DEV,) + slots.shape[1:], slots.dtype),
        in_specs=[pl.BlockSpec(memory_space=pl.ANY)], out_specs=pl.BlockSpec(memory_space=pl.ANY),
        scratch_shapes=[pltpu.SemaphoreType.DMA((N_DEV - 1,)), pltpu.SemaphoreType.DMA((N_DEV - 1,)),
                        pltpu.SemaphoreType.DMA],
        name=name)(slots)


def _allreduce_small(v, name):
    def body(v_ref, o_ref, buf, send_sems, recv_sems):
        x, y, c = lax.axis_index("x"), lax.axis_index("y"), lax.axis_index("c")
        me = 4 * x + 2 * y + c
        buf[me] = v_ref[...]

        def peer(r):
            return _flip(x, (r >> 2) & 1), _flip(y, (r >> 1) & 1), _flip(c, r & 1)

        def copy(r, dst_slot):
            return pltpu.make_async_remote_copy(
                src_ref=v_ref, dst_ref=buf.at[dst_slot], send_sem=send_sems.at[r - 1],
                recv_sem=recv_sems.at[r - 1], device_id=peer(r), device_id_type=MESH)

        sends = [copy(r, me) for r in range(1, N_DEV)]
        for cp in sends:
            cp.start()
        for r in range(1, N_DEV):
            px, py, pc = peer(r)
            copy(r, 4 * px + 2 * py + pc).wait_recv()
        for cp in sends:
            cp.wait_send()
        acc = buf[0]
        for d in range(1, N_DEV):
            acc = acc + buf[d]
        o_ref[...] = acc

    return pl.pallas_call(
        body, out_shape=jax.ShapeDtypeStruct(v.shape, v.dtype),
        in_specs=[pl.BlockSpec(memory_space=pltpu.VMEM)], out_specs=pl.BlockSpec(memory_space=pltpu.VMEM),
        scratch_shapes=[pltpu.VMEM((N_DEV,) + v.shape, v.dtype), pltpu.SemaphoreType.DMA((N_DEV - 1,)),
                        pltpu.SemaphoreType.DMA((N_DEV - 1,))],
        name=name)(v)


def _adamw_math(w, g, m, v):
    m = ADAM_B1 * m + (1.0 - ADAM_B1) * g
    v = ADAM_B2 * v + (1.0 - ADAM_B2) * jnp.square(g)
    m_hat = m / (1.0 - ADAM_B1 ** ADAM_STEP)
    v_hat = v / (1.0 - ADAM_B2 ** ADAM_STEP)
    delta = -ADAM_LR * (m_hat / (jnp.sqrt(v_hat) + ADAM_EPS) + ADAM_WD * w)
    return delta, m, v


def _adamw_sum(parts, w, m, v, name):
    npart, rows, cols = parts.shape
    tr = _pick(rows, (64, 32, 16, 8)) if rows >= 8 else rows

    def body(p_ref, w_ref, m_ref, v_ref, g_out, d_out, m_out, v_out):
        g = p_ref[0].astype(F32)
        for k in range(1, npart):
            g = g + p_ref[k].astype(F32)
        d, m2, v2 = _adamw_math(w_ref[...], g, m_ref[...], v_ref[...])
        g_out[...] = g
        d_out[...] = d
        m_out[...] = m2
        v_out[...] = v2

    spec = pl.BlockSpec((tr, cols), lambda i: (i, 0))
    shp = jax.ShapeDtypeStruct((rows, cols), F32)
    return pl.pallas_call(
        body, grid=(rows // tr,),
        in_specs=[pl.BlockSpec((npart, tr, cols), lambda i: (0, i, 0)), spec, spec, spec],
        out_specs=[spec, spec, spec, spec], out_shape=[shp, shp, shp, shp],
        compiler_params=pltpu.CompilerParams(dimension_semantics=("parallel",), vmem_limit_bytes=VMEM_LIMIT),
        name=name)(parts, w, m, v)


SMALL_ROWS = 16


def _pack_small(c_ctx, b_mod, conv_b, mh, ln_w, ln_b, conv_w_rows, b_if, qn, kn):
    last = jnp.concatenate([b_if.reshape(-1), qn.reshape(-1), kn.reshape(-1),
                            jnp.zeros((D_MODEL - N_IF - 2 * HD_B,), F32)])
    rows = [c_ctx.reshape(1, D_MODEL), b_mod.reshape(3, D_MODEL), conv_b.reshape(1, D_MODEL),
            mh.reshape(1, D_MODEL), ln_w.reshape(1, D_MODEL), ln_b.reshape(1, D_MODEL),
            conv_w_rows.reshape(3, D_MODEL), last[None], jnp.zeros((SMALL_ROWS - 12, D_MODEL), F32)]
    return jnp.concatenate(rows, 0)


def _unpack_small(pk, conv_cols):
    return dict(c_ctx=pk[0], b_mod=pk[1:4].reshape(1, 3 * D_MODEL), conv_b=pk[4:5], mh_norm_w=pk[5:6],
                ln_w=pk[6:7], ln_b=pk[7:8], conv_w=pk[8:11, :conv_cols][None], b_if=pk[11:12, :N_IF],
                q_norm_w=pk[11:12, N_IF:N_IF + HD_B], k_norm_w=pk[11:12, N_IF + HD_B:N_IF + 2 * HD_B])


def kernel(x, c, ctx, c_ctx, w_mod, b_mod, w_in, b_if, conv_w, conv_b, mh_norm_w, q_norm_w, k_norm_w, w_branch_a, w_branch_b, w_out, ln_w, ln_b, loss_target, m_c_ctx, m_w_mod, m_b_mod, m_w_in, m_b_if, m_conv_w, m_conv_b, m_mh_norm_w, m_q_norm_w, m_k_norm_w, m_w_branch_a, m_w_branch_b, m_w_out, m_ln_w, m_ln_b, v_c_ctx, v_w_mod, v_b_mod, v_w_in, v_b_if, v_conv_w, v_conv_b, v_mh_norm_w, v_q_norm_w, v_k_norm_w, v_w_branch_a, v_w_branch_b, v_w_out, v_ln_w, v_ln_b):
    chip = 2 * lax.axis_index("x") + lax.axis_index("y")
    mod_cols, in_cols, conv_cols = w_mod.shape[2], w_in.shape[2], conv_w.shape[2]
    br_rows = w_out.shape[1]

    g_mod = _gather_chips_halves(w_mod[0].astype(BF16), "gather_w_mod")
    g_in = _gather_chips_halves(w_in[0].astype(BF16), "gather_w_in")
    g_ba = _gather_chips_halves(w_branch_a[0].astype(BF16), "gather_w_ba")
    g_bb = _gather_chips_halves(w_branch_b[0].astype(BF16), "gather_w_bb")
    g_out = _gather_chips_halves(w_out[0].astype(BF16), "gather_w_out")
    g_conv = _gather_chips(conv_w[0], "gather_conv_w")
    w_mod_full = jnp.moveaxis(g_mod, 0, 1).reshape(D_MODEL, N_CHIPS * mod_cols)
    w_in_full = jnp.moveaxis(g_in, 0, 1).reshape(D_MODEL, N_CHIPS * in_cols)
    w_main = jnp.concatenate([w_in_full[:, :IF_START], w_in_full[:, IF_START + N_IF:]], 1)
    w_if = jnp.pad(w_in_full[:, IF_START:IF_START + N_IF], ((0, 0), (0, IF_PAD - N_IF)))
    conv_w_full = jnp.moveaxis(g_conv, 0, 1).reshape(3, N_CHIPS * conv_cols)

    const = dict(c=c, ctx=ctx[0], target=loss_target[0], w_mod=w_mod_full, w_main=w_main, w_if=w_if,
                 w_ba=g_ba.reshape(D_MODEL, D_MODEL), w_bb=g_bb.reshape(D_MODEL, D_MODEL),
                 w_out=g_out.reshape(D_MODEL, D_MODEL))
    diff = dict(x=x[0], c_ctx=c_ctx, b_mod=b_mod[0], b_if=b_if[0], conv_w=conv_w_full, conv_b=conv_b[0],
                mh_norm_w=mh_norm_w[0], q_norm_w=q_norm_w[0], k_norm_w=k_norm_w[0], ln_w=ln_w[0], ln_b=ln_b[0],
                p_mod=jnp.zeros(w_mod_full.shape, F32), p_main=jnp.zeros(w_main.shape, F32),
                p_if=jnp.zeros(w_if.shape, F32), p_ba=jnp.zeros((D_MODEL, D_MODEL), F32),
                p_bb=jnp.zeros((D_MODEL, D_MODEL), F32), p_out=jnp.zeros((D_MODEL, D_MODEL), F32))
    loss_local, g = jax.value_and_grad(_local_loss)(diff, const)
    loss = lax.psum(loss_local, ("x", "y", "c"))

    g_small = _allreduce_small(
        _pack_small(g["c_ctx"], g["b_mod"], g["conv_b"], g["mh_norm_w"], g["ln_w"], g["ln_b"], g["conv_w"],
                    g["b_if"], g["q_norm_w"], g["k_norm_w"]), "allreduce_small")
    conv_g = lax.dynamic_slice(g_small[8:11], (0, chip * conv_cols), (3, conv_cols))
    g_small = g_small.at[8:11].set(jnp.pad(conv_g, ((0, 0), (0, D_MODEL - conv_cols))))
    pad_conv = lambda a: jnp.pad(a[0], ((0, 0), (0, D_MODEL - conv_cols)))
    packed = [_pack_small(cc, bm[0], cb[0], mh[0], lw[0], lb[0], pad_conv(cw), bi[0], qn[0], kn[0])
              for cc, bm, cb, mh, lw, lb, cw, bi, qn, kn in (
                  (c_ctx, b_mod, conv_b, mh_norm_w, ln_w, ln_b, conv_w, b_if, q_norm_w, k_norm_w),
                  (m_c_ctx, m_b_mod, m_conv_b, m_mh_norm_w, m_ln_w, m_ln_b, m_conv_w, m_b_if, m_q_norm_w, m_k_norm_w),
                  (v_c_ctx, v_b_mod, v_conv_b, v_mh_norm_w, v_ln_w, v_ln_b, v_conv_w, v_b_if, v_q_norm_w, v_k_norm_w))]
    small = [_unpack_small(a, conv_cols)
             for a in _adamw_sum(g_small[None], packed[0], packed[1], packed[2], "adamw_small")]

    def col_slots(gfull, cols):
        return jnp.moveaxis(gfull.reshape(D_MODEL, N_CHIPS, cols), 1, 0).astype(BF16)

    g_in_full = jnp.concatenate([g["p_main"][:, :IF_START], g["p_if"][:, :N_IF], g["p_main"][:, IF_START:]], 1)
    big = {}
    for nm, slots, w_, m_, v_ in (
            ("w_mod", col_slots(g["p_mod"], mod_cols), w_mod, m_w_mod, v_w_mod),
            ("w_in", col_slots(g_in_full, in_cols), w_in, m_w_in, v_w_in),
            ("w_branch_a", g["p_ba"].reshape(N_CHIPS, br_rows, D_MODEL).astype(BF16), w_branch_a, m_w_branch_a, v_w_branch_a),
            ("w_branch_b", g["p_bb"].reshape(N_CHIPS, br_rows, D_MODEL).astype(BF16), w_branch_b, m_w_branch_b, v_w_branch_b),
            ("w_out", g["p_out"].reshape(N_CHIPS, br_rows, D_MODEL).astype(BF16), w_out, m_w_out, v_w_out)):
        parts = _scatter_grads(slots, "scatter_" + nm)
        big[nm] = [a[None] for a in _adamw_sum(parts, w_[0], m_[0], v_[0], "adamw_" + nm)]

    names = ["c_ctx", "w_mod", "b_mod", "w_in", "b_if", "conv_w", "conv_b", "mh_norm_w", "q_norm_w", "k_norm_w",
             "w_branch_a", "w_branch_b", "w_out", "ln_w", "ln_b"]
    outs = [[big[nm][k] if nm in big else small[k][nm] for nm in names] for k in range(4)]
    return (loss, g["x"][None], *outs[0], *outs[1], *outs[2], *outs[3])
```

```python
import functools

import jax
import jax.numpy as jnp
from jax import lax
from jax.experimental import pallas as pl
from jax.experimental.pallas import tpu as pltpu

F32 = jnp.float32
BF16 = jnp.bfloat16
MESH = pl.DeviceIdType.MESH

D_MODEL = 2048
NH_A, DK_A, DV_A = 8, 128, 256
QK_A, V_A = NH_A * DK_A, NH_A * DV_A
NH_B, NKV_B, HD_B = 16, 4, 128
Q_B, KV_B = NH_B * HD_B, NKV_B * HD_B
GRID_W = 64
ROT_HALF = HD_B // 2
ROPE_THETA = 10000.0
M_INIT = -1e30
EPS = 1e-6
ALPHA = 2.0 ** 0.25
N_IN = 17440
IF_START, N_IF, IF_PAD = 4096, 32, 128
N_MAIN = N_IN - N_IF
O_QK, O_VA, O_KB, O_VB, O_OA, O_ZA, O_QB, O_ZB, O_GA, O_GB = (
    0, 2048, 4096, 4608, 5120, 7168, 9216, 11264, 13312, 15360)
MLSTM_CHUNK = 256

ADAM_LR, ADAM_B1, ADAM_B2, ADAM_EPS, ADAM_WD, ADAM_STEP = 0.001, 0.9, 0.999, 1e-08, 0.01, 10

VMEM_LIMIT = 48 * 1024 * 1024
N_CHIPS, N_DEV = 4, 8
MX = BF16


def _pick(n, cands):
    for c in cands:
        if n % c == 0:
            return c
    raise ValueError(f"no tile for {n} in {cands}")


def _dot(a, b):
    return jnp.dot(a, b, preferred_element_type=F32)


def _dot_nt(a, b):
    return lax.dot_general(a, b, (((1,), (1,)), ((), ())), preferred_element_type=F32)


def _dot_tn(a, b):
    return lax.dot_general(a, b, (((0,), (0,)), ((), ())), preferred_element_type=F32)


def _mm_nn(a, b, name):
    m, k = a.shape
    _, n = b.shape
    tm = _pick(m, (512, 256, 128, 64, 32, 16))
    tn = _pick(n, (1024, 512, 256, 128))

    def body(a_ref, b_ref, o_ref):
        o_ref[...] = _dot(a_ref[...], b_ref[...])

    return pl.pallas_call(
        body, grid=(m // tm, n // tn),
        in_specs=[pl.BlockSpec((tm, k), lambda i, j: (i, 0)), pl.BlockSpec((k, tn), lambda i, j: (0, j))],
        out_specs=pl.BlockSpec((tm, tn), lambda i, j: (i, j)),
        out_shape=jax.ShapeDtypeStruct((m, n), F32),
        compiler_params=pltpu.CompilerParams(dimension_semantics=("parallel", "parallel"),
                                             vmem_limit_bytes=VMEM_LIMIT),
        name=name)(a, b)


def _mm_nt(g, w, name):
    m, n = g.shape
    k, _ = w.shape
    tm = _pick(m, (512, 256, 128, 64, 32, 16))
    tn = _pick(n, (1024, 512, 256, 128))

    def body(g_ref, w_ref, o_ref):
        part = _dot_nt(g_ref[...].astype(BF16), w_ref[...])

        @pl.when(pl.program_id(1) == 0)
        def _():
            o_ref[...] = part

        @pl.when(pl.program_id(1) > 0)
        def _():
            o_ref[...] += part

    return pl.pallas_call(
        body, grid=(m // tm, n // tn),
        in_specs=[pl.BlockSpec((tm, tn), lambda i, j: (i, j)), pl.BlockSpec((k, tn), lambda i, j: (0, j))],
        out_specs=pl.BlockSpec((tm, k), lambda i, j: (i, 0)),
        out_shape=jax.ShapeDtypeStruct((m, k), F32),
        compiler_params=pltpu.CompilerParams(dimension_semantics=("parallel", "arbitrary"),
                                             vmem_limit_bytes=VMEM_LIMIT),
        name=name)(g, w)


def _mm_tn(a, g, name):
    m, k = a.shape
    _, n = g.shape
    tm = _pick(m, (512, 256, 128, 64, 32, 16))
    tn = _pick(n, (1024, 512, 256, 128))

    def body(a_ref, g_ref, o_ref):
        part = _dot_tn(a_ref[...], g_ref[...].astype(BF16))

        @pl.when(pl.program_id(1) == 0)
        def _():
            o_ref[...] = part

        @pl.when(pl.program_id(1) > 0)
        def _():
            o_ref[...] += part

    return pl.pallas_call(
        body, grid=(n // tn, m // tm),
        in_specs=[pl.BlockSpec((tm, k), lambda j, i: (i, 0)), pl.BlockSpec((tm, tn), lambda j, i: (i, j))],
        out_specs=pl.BlockSpec((k, tn), lambda j, i: (0, j)),
        out_shape=jax.ShapeDtypeStruct((k, n), F32),
        compiler_params=pltpu.CompilerParams(dimension_semantics=("parallel", "arbitrary"),
                                             vmem_limit_bytes=VMEM_LIMIT),
        name=name)(a, g)


SLABS = (("qk", O_QK, 2 * QK_A), ("va", O_VA, V_A), ("kv", O_KB, 2 * KV_B), ("oa", O_OA, V_A), ("za", O_ZA, V_A),
         ("qb", O_QB, Q_B), ("zb", O_ZB, Q_B), ("ga", O_GA, D_MODEL), ("gb", O_GB, D_MODEL))
SLAB_FWD_TN = 1024
SLAB_TN = 512


def _slab_blocks():
    return [(off // SLAB_TN, (off + width) // SLAB_TN) for _, off, width in SLABS]


def _proj_fwd_slab(a, w, off, width, name):
    m, k = a.shape
    tm = _pick(m, (1088, 512, 256, 128))
    tn = min(SLAB_FWD_TN, width)

    def body(a_ref, b_ref, o_ref):
        o_ref[...] = _dot(a_ref[...], b_ref[...])

    return pl.pallas_call(
        body, grid=(m // tm, width // tn),
        in_specs=[pl.BlockSpec((tm, k), lambda i, j: (i, 0)), pl.BlockSpec((k, tn), lambda i, j: (0, j + off // tn))],
        out_specs=pl.BlockSpec((tm, tn), lambda i, j: (i, j)),
        out_shape=jax.ShapeDtypeStruct((m, width), F32),
        compiler_params=pltpu.CompilerParams(dimension_semantics=("parallel", "parallel"),
                                             vmem_limit_bytes=VMEM_LIMIT),
        name=name)(a, w)


def _slab_spec(tm, blocks, rows_inner):
    b, e = blocks

    def index(r, c):
        inside = (c >= b) & (c < e)
        return jnp.where(inside, r, 0), jnp.clip(c - b, 0, e - b - 1)

    if rows_inner:
        return pl.BlockSpec((tm, SLAB_TN), lambda c, r: index(r, c))
    return pl.BlockSpec((tm, SLAB_TN), lambda r, c: index(r, c))


def _proj_da(gs, w, name):
    m = gs[0].shape[0]
    k, n = w.shape
    tm = _pick(m, (512, 256, 128))
    blocks = _slab_blocks()

    def body(*refs):
        g_refs, w_ref, o_ref = refs[:len(blocks)], refs[len(blocks)], refs[len(blocks) + 1]
        c = pl.program_id(1)

        @pl.when(c == 0)
        def _():
            o_ref[...] = jnp.zeros(o_ref.shape, F32)

        for g_ref, (b, e) in zip(g_refs, blocks):
            @pl.when((c >= b) & (c < e))
            def _(g_ref=g_ref):
                o_ref[...] += _dot_nt(g_ref[...].astype(BF16), w_ref[...])

    return pl.pallas_call(
        body, grid=(m // tm, n // SLAB_TN),
        in_specs=[_slab_spec(tm, blk, False) for blk in blocks] + [pl.BlockSpec((k, SLAB_TN), lambda r, c: (0, c))],
        out_specs=pl.BlockSpec((tm, k), lambda r, c: (r, 0)),
        out_shape=jax.ShapeDtypeStruct((m, k), F32),
        compiler_params=pltpu.CompilerParams(dimension_semantics=("parallel", "arbitrary"),
                                             vmem_limit_bytes=VMEM_LIMIT),
        name=name)(*gs, w)


def _proj_dw(a, gs, n, name):
    m, k = a.shape
    tm = _pick(m, (512, 256, 128))
    blocks = _slab_blocks()

    def body(*refs):
        a_ref, g_refs, o_ref = refs[0], refs[1:1 + len(blocks)], refs[1 + len(blocks)]
        c, r = pl.program_id(0), pl.program_id(1)

        @pl.when(r == 0)
        def _():
            o_ref[...] = jnp.zeros(o_ref.shape, F32)

        for g_ref, (b, e) in zip(g_refs, blocks):
            @pl.when((c >= b) & (c < e))
            def _(g_ref=g_ref):
                o_ref[...] += _dot_tn(a_ref[...], g_ref[...].astype(BF16))

    return pl.pallas_call(
        body, grid=(n // SLAB_TN, m // tm),
        in_specs=[pl.BlockSpec((tm, k), lambda c, r: (r, 0))] + [_slab_spec(tm, blk, True) for blk in blocks],
        out_specs=pl.BlockSpec((k, SLAB_TN), lambda c, r: (0, c)),
        out_shape=jax.ShapeDtypeStruct((k, n), F32),
        compiler_params=pltpu.CompilerParams(dimension_semantics=("parallel", "arbitrary"),
                                             vmem_limit_bytes=VMEM_LIMIT),
        name=name)(a, *gs)


LN_ROWS = 256


def _ln_stats(x):
    mu = jnp.mean(x, axis=-1, keepdims=True)
    xc = x - mu
    rstd = lax.rsqrt(jnp.mean(xc * xc, axis=-1, keepdims=True) + EPS)
    return xc * rstd, rstd


def _ln_bwd(dxh, xh, rstd):
    return rstd * (dxh - jnp.mean(dxh, axis=-1, keepdims=True) - xh * jnp.mean(dxh * xh, axis=-1, keepdims=True))


def _seg_maps(t, tc):
    cb, nb = tc // LN_ROWS, t // LN_ROWS
    ctx_blk = lambda i: jnp.where(i < cb, i, jnp.clip(i - cb - nb, 0, cb - 1))
    lat_blk = lambda i: jnp.clip(i - cb, 0, nb - 1)
    return cb, nb, ctx_blk, lat_blk


def _ln_mod_fwd(x, ctx, mod):
    t, tc = x.shape[0], ctx.shape[0]
    cb, nb, ctx_blk, lat_blk = _seg_maps(t, tc)

    def body(x_ref, c_ref, m_ref, o_ref):
        i = pl.program_id(0)
        lat = (i >= cb) & (i < cb + nb)

        @pl.when(lat)
        def _():
            xh, _ = _ln_stats(x_ref[...])
            o_ref[...] = (xh * (1 + m_ref[0:1, :]) + m_ref[1:2, :]).astype(BF16)

        @pl.when(jnp.logical_not(lat))
        def _():
            xh, _ = _ln_stats(c_ref[...])
            o_ref[...] = (xh * (1 + m_ref[2:3, :]) + m_ref[3:4, :]).astype(BF16)

    blk = lambda f: pl.BlockSpec((LN_ROWS, D_MODEL), lambda i: (f(i), 0))
    return pl.pallas_call(
        body, grid=(2 * cb + nb,),
        in_specs=[blk(lat_blk), blk(ctx_blk), pl.BlockSpec((4, D_MODEL), lambda i: (0, 0))],
        out_specs=pl.BlockSpec((LN_ROWS, D_MODEL), lambda i: (i, 0)),
        out_shape=jax.ShapeDtypeStruct((t + 2 * tc, D_MODEL), BF16),
        compiler_params=pltpu.CompilerParams(dimension_semantics=("parallel",), vmem_limit_bytes=VMEM_LIMIT),
        name="ln_mod")(x, ctx, mod)


def _ln_mod_bwd(du_a, du_b, x, ctx, mod):
    t, tc = x.shape[0], ctx.shape[0]
    cb, nb, ctx_blk, lat_blk = _seg_maps(t, tc)

    def body(da_ref, db_ref, x_ref, c_ref, m_ref, dx_ref, dm_ref):
        i = pl.program_id(0)
        lat = (i >= cb) & (i < cb + nb)

        @pl.when(i == 0)
        def _():
            dm_ref[...] = jnp.zeros(dm_ref.shape, F32)

        du = da_ref[...] + db_ref[...]

        @pl.when(lat)
        def _():
            xh, rstd = _ln_stats(x_ref[...])
            dm_ref[0:1, :] += jnp.sum(du * xh, axis=0, keepdims=True)
            dm_ref[1:2, :] += jnp.sum(du, axis=0, keepdims=True)
            dx_ref[...] = _ln_bwd(du * (1 + m_ref[0:1, :]), xh, rstd)

        @pl.when(jnp.logical_not(lat))
        def _():
            xh, _ = _ln_stats(c_ref[...])
            dm_ref[2:3, :] += jnp.sum(du * xh, axis=0, keepdims=True)
            dm_ref[3:4, :] += jnp.sum(du, axis=0, keepdims=True)

    blk = lambda f: pl.BlockSpec((LN_ROWS, D_MODEL), lambda i: (f(i), 0))
    row = pl.BlockSpec((LN_ROWS, D_MODEL), lambda i: (i, 0))
    vec = pl.BlockSpec((4, D_MODEL), lambda i: (0, 0))
    return pl.pallas_call(
        body, grid=(2 * cb + nb,),
        in_specs=[row, row, blk(lat_blk), blk(ctx_blk), vec],
        out_specs=[blk(lat_blk), vec],
        out_shape=[jax.ShapeDtypeStruct((t, D_MODEL), F32), jax.ShapeDtypeStruct((4, D_MODEL), F32)],
        compiler_params=pltpu.CompilerParams(dimension_semantics=("arbitrary",), vmem_limit_bytes=VMEM_LIMIT),
        name="ln_mod_bwd")(du_a, du_b, x, ctx, mod)


@jax.custom_vjp
def _ln_project(x, ctx, mod, w_main, w_if, pr_main, pr_if):
    return _ln_project_fwd(x, ctx, mod, w_main, w_if, pr_main, pr_if)[0]


def _ln_project_fwd(x, ctx, mod, w_main, w_if, pr_main, pr_if):
    del pr_main, pr_if
    ub = _ln_mod_fwd(x, ctx, mod)
    slabs = tuple(_proj_fwd_slab(ub, w_main, off, width, "proj_" + nm) for nm, off, width in SLABS)
    return (slabs, _mm_nn(ub, w_if, "proj_if")), (x, ctx, mod, ub, w_main, w_if)


def _ln_project_bwd(res, cot):
    x, ctx, mod, ub, w_main, w_if = res
    gs, g_if = cot
    dx, dmod = _ln_mod_bwd(_proj_da(gs, w_main, "proj_da"), _mm_nt(g_if, w_if, "proj_if_da"), x, ctx, mod)
    return (dx, jnp.zeros_like(ctx), dmod, jnp.zeros_like(w_main), jnp.zeros_like(w_if),
            _proj_dw(ub, gs, w_main.shape[1], "proj_dw"), _mm_tn(ub, g_if, "proj_if_dw"))


_ln_project.defvjp(_ln_project_fwd, _ln_project_bwd)


def _head_fwd(x, out, target, vecs):
    t = x.shape[0]

    def body(x_ref, o_ref, t_ref, v_ref, l_ref):
        @pl.when(pl.program_id(0) == 0)
        def _():
            l_ref[...] = jnp.zeros(l_ref.shape, F32)

        rh, _ = _ln_stats(ALPHA * x_ref[...] + v_ref[0:1, :] * o_ref[...])
        err = rh * v_ref[1:2, :] + v_ref[2:3, :] - t_ref[...]
        part = jnp.sum(jnp.mean(err * err, axis=-1, keepdims=True), axis=0, keepdims=True)
        l_ref[...] += 0.5 * part

    row = pl.BlockSpec((LN_ROWS, D_MODEL), lambda i: (i, 0))
    return pl.pallas_call(
        body, grid=(t // LN_ROWS,),
        in_specs=[row, row, row, pl.BlockSpec((3, D_MODEL), lambda i: (0, 0))],
        out_specs=pl.BlockSpec((1, 128), lambda i: (0, 0)),
        out_shape=jax.ShapeDtypeStruct((1, 128), F32),
        compiler_params=pltpu.CompilerParams(dimension_semantics=("arbitrary",), vmem_limit_bytes=VMEM_LIMIT),
        name="loss_head")(x, out, target, vecs)


def _head_bwd(g, x, out, target, vecs):
    t = x.shape[0]

    def body(g_ref, x_ref, o_ref, t_ref, v_ref, dx_ref, do_ref, dv_ref):
        @pl.when(pl.program_id(0) == 0)
        def _():
            dv_ref[...] = jnp.zeros(dv_ref.shape, F32)

        o = o_ref[...]
        gate, ln_w = v_ref[0:1, :], v_ref[1:2, :]
        rh, rstd = _ln_stats(ALPHA * x_ref[...] + gate * o)
        dy = (rh * ln_w + v_ref[2:3, :] - t_ref[...]) * (g_ref[0:1, 0:1] * (1.0 / D_MODEL))
        dv_ref[1:2, :] += jnp.sum(dy * rh, axis=0, keepdims=True)
        dv_ref[2:3, :] += jnp.sum(dy, axis=0, keepdims=True)
        dr = _ln_bwd(dy * ln_w, rh, rstd)
        dv_ref[0:1, :] += jnp.sum(dr * o, axis=0, keepdims=True)
        dx_ref[...] = ALPHA * dr
        do_ref[...] = gate * dr

    row = pl.BlockSpec((LN_ROWS, D_MODEL), lambda i: (i, 0))
    vec = pl.BlockSpec((3, D_MODEL), lambda i: (0, 0))
    return pl.pallas_call(
        body, grid=(t // LN_ROWS,),
        in_specs=[pl.BlockSpec((1, 128), lambda i: (0, 0)), row, row, row, vec],
        out_specs=[row, row, vec],
        out_shape=[jax.ShapeDtypeStruct((t, D_MODEL), F32), jax.ShapeDtypeStruct((t, D_MODEL), F32),
                   jax.ShapeDtypeStruct((3, D_MODEL), F32)],
        compiler_params=pltpu.CompilerParams(dimension_semantics=("arbitrary",), vmem_limit_bytes=VMEM_LIMIT),
        name="loss_head_bwd")(g, x, out, target, vecs)


@jax.custom_vjp
def _loss_head(x, out, target, gate, ln_w, ln_b):
    return _head_fwd(x, out, target, jnp.stack([gate, ln_w, ln_b]))[0, 0]


def _loss_head_fwd(x, out, target, gate, ln_w, ln_b):
    vecs = jnp.stack([gate, ln_w, ln_b])
    return _head_fwd(x, out, target, vecs)[0, 0], (x, out, target, vecs)


def _loss_head_bwd(res, g):
    x, out, target, vecs = res
    dx, dout, dv = _head_bwd(jnp.full((1, 128), g, F32), x, out, target, vecs)
    return dx, dout, jnp.zeros_like(target), dv[0], dv[1], dv[2]


_loss_head.defvjp(_loss_head_fwd, _loss_head_bwd)


ATT_SCALE = HD_B ** -0.5
GROUP = NH_B // NKV_B


LOG2E, LN2 = 1.4426950408889634, 0.6931471805599453
ATT_C = ATT_SCALE * LOG2E
STRIP_Q, STRIP_K = 128, 256


def _attn_tiles(t, n):
    return _pick(t, (512, 256, 128)), _pick(n, (768, 512, 256))


def _attn_fwd(q, k, v):
    t, n = q.shape[0], k.shape[0]
    tq, tk = _pick(t, (1024, 512, 256, 128)), _attn_tiles(t, n)[1]
    nk = n // tk

    def body(q_ref, k_ref, v_ref, o_ref, lse_ref, m_sc, acc_sc):
        j = pl.program_id(2)

        @pl.when(j == 0)
        def _():
            m_sc[...] = jnp.full(m_sc.shape, -jnp.inf, F32)
            acc_sc[...] = jnp.zeros(acc_sc.shape, F32)

        kb = k_ref[...]
        v_ones = jnp.concatenate([v_ref[...], jnp.ones((tk, HD_B), BF16)], axis=1)
        for g in range(GROUP):
            s2 = _dot_nt(q_ref[:, g * HD_B:(g + 1) * HD_B], kb) * ATT_C
            m_prev = m_sc[g]
            m_new = jnp.maximum(m_prev, jnp.max(s2, axis=-1, keepdims=True))
            p = jnp.exp2(s2 - m_new).astype(BF16)
            acc_sc[g] = jnp.exp2(m_prev - m_new) * acc_sc[g] + _dot(p, v_ones)
            m_sc[g] = m_new

        @pl.when(j == nk - 1)
        def _():
            for g in range(GROUP):
                cols = slice(g * HD_B, (g + 1) * HD_B)
                l = acc_sc[g, :, HD_B:]
                o_ref[:, cols] = acc_sc[g, :, :HD_B] / l
                lse_ref[:, cols] = m_sc[g] + jnp.log(l) * LOG2E

    qspec = pl.BlockSpec((tq, GROUP * HD_B), lambda kh, i, j: (i, kh))
    kspec = pl.BlockSpec((tk, HD_B), lambda kh, i, j: (j, kh))
    return pl.pallas_call(
        body, grid=(NKV_B, t // tq, nk),
        in_specs=[qspec, kspec, kspec], out_specs=[qspec, qspec],
        out_shape=[jax.ShapeDtypeStruct((t, Q_B), F32), jax.ShapeDtypeStruct((t, Q_B), F32)],
        scratch_shapes=[pltpu.VMEM((GROUP, tq, 1), F32), pltpu.VMEM((GROUP, tq, 2 * HD_B), F32)],
        compiler_params=pltpu.CompilerParams(dimension_semantics=("parallel", "parallel", "arbitrary"),
                                             vmem_limit_bytes=VMEM_LIMIT),
        name="attn_fwd")(q, k, v)


def _attn_dq(q, k, v, do, lse, delta):
    t, n = q.shape[0], k.shape[0]
    tq, tk = _pick(t, (1024, 512, 256, 128)), _attn_tiles(t, n)[1]
    nk = n // tk

    def body(q_ref, k_ref, v_ref, do_ref, lse_ref, dl_ref, dq_ref):
        j = pl.program_id(2)
        kb, vb = k_ref[...], v_ref[...]
        parts = []
        for g in range(GROUP):
            cols = slice(g * HD_B, (g + 1) * HD_B)
            p = jnp.exp2(_dot_nt(q_ref[:, cols], kb) * ATT_C - lse_ref[:, g * HD_B:g * HD_B + 1])
            dp = _dot_nt(do_ref[:, cols], vb)
            ds = p * (dp - dl_ref[:, g * HD_B:g * HD_B + 1])
            parts.append(_dot(ds.astype(BF16), kb))

        @pl.when(j == 0)
        def _():
            for g in range(GROUP):
                dq_ref[:, g * HD_B:(g + 1) * HD_B] = parts[g]

        @pl.when(j > 0)
        def _():
            for g in range(GROUP):
                dq_ref[:, g * HD_B:(g + 1) * HD_B] += parts[g]

        @pl.when(j == nk - 1)
        def _():
            dq_ref[...] = dq_ref[...] * ATT_SCALE

    qspec = pl.BlockSpec((tq, GROUP * HD_B), lambda kh, i, j: (i, kh))
    kspec = pl.BlockSpec((tk, HD_B), lambda kh, i, j: (j, kh))
    return pl.pallas_call(
        body, grid=(NKV_B, t // tq, n // tk),
        in_specs=[qspec, kspec, kspec, qspec, qspec, qspec],
        out_specs=qspec,
        out_shape=jax.ShapeDtypeStruct((t, Q_B), F32),
        compiler_params=pltpu.CompilerParams(dimension_semantics=("parallel", "parallel", "arbitrary"),
                                             vmem_limit_bytes=VMEM_LIMIT),
        name="attn_dq")(q, k, v, do, lse, delta)


def _attn_dkv(q, k, v, do, lse_t, delta_t):
    t, n = q.shape[0], k.shape[0]
    tq, tk = _attn_tiles(t, n)
    nq = t // tq
    n_r, n_c = tq // STRIP_Q, tk // STRIP_K

    def body(q_ref, k_ref, v_ref, do_ref, lse_ref, dl_ref, dk_ref, dv_ref, dk_sc, dv_sc):
        i = pl.program_id(2)

        @pl.when(i == 0)
        def _():
            dk_sc[...] = jnp.zeros(dk_sc.shape, F32)
            dv_sc[...] = jnp.zeros(dv_sc.shape, F32)

        for r in range(n_r):
            rows = slice(r * STRIP_Q, (r + 1) * STRIP_Q)
            for c in range(n_c):
                kv = slice(c * STRIP_K, (c + 1) * STRIP_K)
                kc, vc = k_ref[kv, :], v_ref[kv, :]
                dk_part = dv_part = None
                for g in range(GROUP):
                    cols = slice(g * HD_B, (g + 1) * HD_B)
                    qg, dog = q_ref[rows, cols], do_ref[rows, cols]
                    st = _dot_nt(kc, qg)
                    pt = jnp.exp2(st * ATT_C - lse_ref[8 * g:8 * g + 1, rows])
                    dvg = _dot(pt.astype(BF16), dog)
                    dpt = _dot_nt(vc, dog)
                    dst = pt * (dpt - dl_ref[8 * g:8 * g + 1, rows])
                    dkg = _dot(dst.astype(BF16), qg)
                    dk_part = dkg if dk_part is None else dk_part + dkg
                    dv_part = dvg if dv_part is None else dv_part + dvg
                dk_sc[kv, :] += dk_part
                dv_sc[kv, :] += dv_part

        @pl.when(i == nq - 1)
        def _():
            dk_ref[...] = dk_sc[...] * ATT_SCALE
            dv_ref[...] = dv_sc[...]

    qspec = pl.BlockSpec((tq, GROUP * HD_B), lambda kh, j, i: (i, kh))
    tspec = pl.BlockSpec((8 * GROUP, tq), lambda kh, j, i: (kh, i))
    kspec = pl.BlockSpec((tk, HD_B), lambda kh, j, i: (j, kh))
    return pl.pallas_call(
        body, grid=(NKV_B, n // tk, nq),
        in_specs=[qspec, kspec, kspec, qspec, tspec, tspec],
        out_specs=[kspec, kspec],
        out_shape=[jax.ShapeDtypeStruct((n, KV_B), F32), jax.ShapeDtypeStruct((n, KV_B), F32)],
        scratch_shapes=[pltpu.VMEM((tk, HD_B), F32), pltpu.VMEM((tk, HD_B), F32)],
        compiler_params=pltpu.CompilerParams(dimension_semantics=("parallel", "parallel", "arbitrary"),
                                             vmem_limit_bytes=VMEM_LIMIT),
        name="attn_dkv")(q, k, v, do, lse_t, delta_t)


def _attention_bwd(res, do):
    qb, kb, vb, o, lse = res
    t = qb.shape[0]
    delta = jnp.sum((do * o).reshape(t, NH_B, HD_B), axis=-1)
    lse_h = lse.reshape(t, NH_B, HD_B)[:, :, 0]
    delta_b = jnp.broadcast_to(delta[:, :, None], (t, NH_B, HD_B)).reshape(t, Q_B)
    lse_t = jnp.broadcast_to(lse_h.T[:, None, :], (NH_B, 8, t)).reshape(NH_B * 8, t)
    delta_t = jnp.broadcast_to(delta.T[:, None, :], (NH_B, 8, t)).reshape(NH_B * 8, t)
    dob = do.astype(BF16)
    dq = _attn_dq(qb, kb, vb, dob, lse, delta_b)
    dk, dv = _attn_dkv(qb, kb, vb, dob, lse_t, delta_t)
    return dq, dk, dv


def _swap32(y):
    lane = lax.broadcasted_iota(jnp.int32, y.shape, 1)
    return jnp.where((lane // 32) % 2 == 0, pltpu.roll(y, 96, 1), pltpu.roll(y, 32, 1))


def _norm_rope_fwd(x, w, cos, sin, name):
    r, width = x.shape
    heads = width // HD_B
    tr = _pick(r, (256, 128))

    def body(x_ref, w_ref, c_ref, s_ref, o_ref):
        w, c, s = w_ref[...], c_ref[...], s_ref[...]
        for h in range(heads):
            cols = slice(h * HD_B, (h + 1) * HD_B)
            xh = x_ref[:, cols]
            y = xh * lax.rsqrt(jnp.mean(xh * xh, axis=-1, keepdims=True) + EPS) * w
            o_ref[:, cols] = (y * c + _swap32(y) * s).astype(o_ref.dtype)

    row = pl.BlockSpec((tr, width), lambda i: (i, 0))
    tab = pl.BlockSpec((tr, HD_B), lambda i: (i, 0))
    return pl.pallas_call(
        body, grid=(r // tr,),
        in_specs=[row, pl.BlockSpec((1, HD_B), lambda i: (0, 0)), tab, tab], out_specs=row,
        out_shape=jax.ShapeDtypeStruct((r, width), BF16),
        compiler_params=pltpu.CompilerParams(dimension_semantics=("parallel",), vmem_limit_bytes=VMEM_LIMIT),
        name=name)(x, w, cos, sin)


def _norm_rope_bwd(x, w, cos, sin, dy, name):
    r, width = x.shape
    heads = width // HD_B
    tr = _pick(r, (256, 128))

    def body(x_ref, w_ref, c_ref, s_ref, dy_ref, dx_ref, dw_ref):
        @pl.when(pl.program_id(0) == 0)
        def _():
            dw_ref[...] = jnp.zeros(dw_ref.shape, F32)

        w, c, s = w_ref[...], c_ref[...], s_ref[...]
        dw = jnp.zeros((1, HD_B), F32)
        for h in range(heads):
            cols = slice(h * HD_B, (h + 1) * HD_B)
            xh, dyh = x_ref[:, cols], dy_ref[:, cols]
            rs = lax.rsqrt(jnp.mean(xh * xh, axis=-1, keepdims=True) + EPS)
            dn = dyh * c + _swap32(dyh * s)
            dw = dw + jnp.sum(dn * (xh * rs), axis=0, keepdims=True)
            dxn = dn * w
            dx_ref[:, cols] = rs * dxn - xh * (rs * rs * rs * jnp.mean(dxn * xh, axis=-1, keepdims=True))
        dw_ref[...] += dw

    row = pl.BlockSpec((tr, width), lambda i: (i, 0))
    tab = pl.BlockSpec((tr, HD_B), lambda i: (i, 0))
    vec = pl.BlockSpec((1, HD_B), lambda i: (0, 0))
    return pl.pallas_call(
        body, grid=(r // tr,),
        in_specs=[row, vec, tab, tab, row], out_specs=[row, vec],
        out_shape=[jax.ShapeDtypeStruct((r, width), F32), jax.ShapeDtypeStruct((1, HD_B), F32)],
        compiler_params=pltpu.CompilerParams(dimension_semantics=("arbitrary",), vmem_limit_bytes=VMEM_LIMIT),
        name=name)(x, w, cos, sin, dy)


def _rope_tables(t):
    pos = jnp.arange(t)
    row = (pos // GRID_W).astype(F32)
    col = (pos % GRID_W).astype(F32)
    inv = ROPE_THETA ** (-jnp.arange(0, ROT_HALF, 2, dtype=F32) / ROT_HALF)
    ar, ac = row[:, None] * inv[None], col[:, None] * inv[None]
    cos = jnp.concatenate([jnp.cos(ar), jnp.cos(ar), jnp.cos(ac), jnp.cos(ac)], -1)
    sin = jnp.concatenate([-jnp.sin(ar), jnp.sin(ar), -jnp.sin(ac), jnp.sin(ac)], -1)
    return cos, sin


def _gqa_tables(t, n):
    cos, sin = _rope_tables(t)
    cos_k = jnp.concatenate([jnp.ones((n - t, HD_B), F32), cos], 0)
    sin_k = jnp.concatenate([jnp.zeros((n - t, HD_B), F32), sin], 0)
    return cos, sin, cos_k, sin_k


@jax.custom_vjp
def _gqa(pq, pk, pv, qw, kw):
    return _gqa_fwd(pq, pk, pv, qw, kw)[0]


def _gqa_fwd(pq, pk, pv, qw, kw):
    cos, sin, cos_k, sin_k = _gqa_tables(pq.shape[0], pk.shape[0])
    q = _norm_rope_fwd(pq, qw[None], cos, sin, "q_norm_rope")
    k = _norm_rope_fwd(pk, kw[None], cos_k, sin_k, "k_norm_rope")
    vb = pv.astype(BF16)
    o, lse = _attn_fwd(q, k, vb)
    return o, (pq, pk, qw, kw, q, k, vb, o, lse)


def _gqa_bwd(res, do):
    pq, pk, qw, kw, q, k, vb, o, lse = res
    cos, sin, cos_k, sin_k = _gqa_tables(pq.shape[0], pk.shape[0])
    dq, dk, dv = _attention_bwd((q, k, vb, o, lse), do)
    dpq, dqw = _norm_rope_bwd(pq, qw[None], cos, sin, dq, "q_norm_rope_bwd")
    dpk, dkw = _norm_rope_bwd(pk, kw[None], cos_k, sin_k, dk, "k_norm_rope_bwd")
    return dpq, dpk, dv, dqw[0], dkw[0]


_gqa.defvjp(_gqa_fwd, _gqa_bwd)


def _mlstm_chunk_forward(q, k, v, lir, f_pre, s0, n0, m0, reverse):
    L = q.shape[0]
    lfr = jnp.minimum(f_pre, 0.0) - jnp.log1p(jnp.exp(-jnp.abs(f_pre)))
    ti = lax.broadcasted_iota(jnp.int32, (L, L), 0)
    si = lax.broadcasted_iota(jnp.int32, (L, L), 1)
    seen = (si >= ti) if reverse else (si <= ti)
    seen_t = (ti >= si) if reverse else (ti <= si)
    eye = ti == si
    lic = jnp.sum(jnp.where(eye, lir, 0.0), axis=1, keepdims=True)
    lfc = jnp.sum(jnp.where(eye, lfr, 0.0), axis=1, keepdims=True)
    b_col = jnp.sum(jnp.where(seen, lfr, 0.0), axis=1, keepdims=True)
    b_row = jnp.sum(jnp.where(seen_t, lfc, 0.0), axis=0, keepdims=True)
    d = jnp.where(seen, b_col - b_row + lir, -jnp.inf)
    m = jnp.maximum(b_col + m0, jnp.max(d, axis=1, keepdims=True))
    w = jnp.exp(d - m)
    a = jnp.exp(b_col + m0 - m)
    qm, km, vm = q.astype(MX), k.astype(MX), v.astype(MX)
    s = _dot_nt(qm, km) * w
    qs = _dot(qm, s0.astype(MX))
    num = a * qs + _dot(s.astype(MX), vm)
    qn = jnp.sum(q * n0, axis=1, keepdims=True)
    den = a * qn + jnp.sum(s, axis=1, keepdims=True)
    floor = jnp.exp(-m)
    dd = jnp.maximum(jnp.abs(den), floor)
    b_last = jnp.sum(lfr, axis=1, keepdims=True)
    m_end = jnp.maximum(b_last + m0, jnp.max(b_last - b_row + lir, axis=1, keepdims=True))
    w_end = jnp.exp(b_last - b_col + lic - m_end)
    a_end = jnp.exp(b_last + m0 - m_end)
    return dict(eye=eye, seen=seen, w=w, a=a, s=s, qs=qs, num=num, qn=qn, den=den, floor=floor, dd=dd,
                m_end=m_end, w_end=w_end, a_end=a_end, qm=qm, km=km, vm=vm)


def _mlstm_fwd_call(q, k, v, gr, n, row_off, reverse):
    L = MLSTM_CHUNK
    nc, off = n // L, row_off // L
    pos = (lambda i: nc - 1 - i) if reverse else (lambda i: i)

    def body(q_ref, k_ref, v_ref, gr_ref, h_ref, s0_ref, n0_ref, m0_ref, s_sc, n_sc, m_sc):
        @pl.when(pl.program_id(1) == 0)
        def _():
            s_sc[...] = jnp.zeros(s_sc.shape, F32)
            n_sc[...] = jnp.zeros(n_sc.shape, F32)
            m_sc[...] = jnp.full(m_sc.shape, M_INIT, F32)

        s0, n0, m0 = s_sc[...], n_sc[...], m_sc[...]
        s0_ref[0, 0] = s0
        n0_ref[0, 0] = n0
        m0_ref[0, 0] = jnp.broadcast_to(m0, (1, DK_A))
        k, v = k_ref[...], v_ref[...]
        f = _mlstm_chunk_forward(q_ref[...], k, v, gr_ref[0, 0], gr_ref[1, 0], s0, n0, m0, reverse)
        h_ref[...] = f["num"] / f["dd"]
        s_sc[...] = f["a_end"] * s0 + _dot_tn(f["km"], (f["w_end"] * v).astype(MX))
        n_sc[...] = f["a_end"] * n0 + jnp.sum(f["w_end"] * k, axis=0, keepdims=True)
        m_sc[...] = f["m_end"]

    qk_spec = pl.BlockSpec((L, DK_A), lambda h, i: (off + pos(i), h))
    v_spec = pl.BlockSpec((L, DV_A), lambda h, i: (off + pos(i), h))
    gr_spec = pl.BlockSpec((2, 1, 1, L), lambda h, i: (0, h, 0, off + pos(i)))
    h_spec = pl.BlockSpec((L, DV_A), lambda h, i: (pos(i), h))
    st_spec = pl.BlockSpec((1, 1, DK_A, DV_A), lambda h, i: (h, pos(i), 0, 0))
    vec_spec = pl.BlockSpec((1, 1, 1, DK_A), lambda h, i: (h, pos(i), 0, 0))
    return pl.pallas_call(
        body, grid=(NH_A, nc),
        in_specs=[qk_spec, qk_spec, v_spec, gr_spec],
        out_specs=[h_spec, st_spec, vec_spec, vec_spec],
        out_shape=[jax.ShapeDtypeStruct((n, V_A), F32), jax.ShapeDtypeStruct((NH_A, nc, DK_A, DV_A), F32),
                   jax.ShapeDtypeStruct((NH_A, nc, 1, DK_A), F32), jax.ShapeDtypeStruct((NH_A, nc, 1, DK_A), F32)],
        scratch_shapes=[pltpu.VMEM((DK_A, DV_A), F32), pltpu.VMEM((1, DK_A), F32), pltpu.VMEM((1, 1), F32)],
        compiler_params=pltpu.CompilerParams(dimension_semantics=("parallel", "arbitrary"),
                                             vmem_limit_bytes=VMEM_LIMIT),
        name="mlstm_fwd")(q, k, v, gr)


def _mlstm_bwd_call(q, k, v, gr, s0_all, n0_all, m0_all, dh, n, row_off, reverse):
    L = MLSTM_CHUNK
    nc, off = n // L, row_off // L
    pos = (lambda i: i) if reverse else (lambda i: nc - 1 - i)

    def body(q_ref, k_ref, v_ref, gr_ref, s0_ref, n0_ref, m0_ref, dh_ref,
             dq_ref, dk_ref, dv_ref, dg_ref, ds_sc, dn_sc):
        @pl.when(pl.program_id(1) == 0)
        def _():
            ds_sc[...] = jnp.zeros(ds_sc.shape, F32)
            dn_sc[...] = jnp.zeros(dn_sc.shape, F32)

        q, k, v = q_ref[...], k_ref[...], v_ref[...]
        s0, n0, m0 = s0_ref[0, 0], n0_ref[0, 0], m0_ref[0, 0][:, 0:1]
        f = _mlstm_chunk_forward(q, k, v, gr_ref[0, 0], gr_ref[1, 0], s0, n0, m0, reverse)
        w, a, s = f["w"], f["a"], f["s"]
        qm, km, vm, w_end, a_end = f["qm"], f["km"], f["vm"], f["w_end"], f["a_end"]
        ds1, dn1 = ds_sc[...], dn_sc[...]
        ds1m, s0m = ds1.astype(MX), s0.astype(MX)

        inv = 1.0 / f["dd"]
        dh = dh_ref[...]
        dnum = dh * inv
        ddd = -jnp.sum(dh * (f["num"] * inv), axis=1, keepdims=True) * inv
        dden = jnp.where(jnp.abs(f["den"]) > f["floor"], jnp.sign(f["den"]) * ddd, 0.0)
        adn = (a * dnum).astype(MX)
        dnm = dnum.astype(MX)
        ds_tot = _dot_nt(dnm, vm) + dden
        dsr = (ds_tot * w).astype(MX)
        e = ds_tot * s
        wv = (w_end * v).astype(MX)
        kds = _dot(km, ds1m)
        dq_ref[...] = _dot_nt(adn, s0m) + _dot(dsr, km) + (dden * a) * n0
        dk_ref[...] = _dot_tn(dsr, qm) + _dot_nt(wv, ds1m) + w_end * dn1
        dv_ref[...] = _dot_tn(s.astype(MX), dnm) + w_end * kds

        eye = f["eye"]
        to_col = lambda r: jnp.sum(jnp.where(eye, r, 0.0), axis=1, keepdims=True)
        to_row = lambda c: jnp.sum(jnp.where(eye, c, 0.0), axis=0, keepdims=True)
        g_a = (jnp.sum(dnum * f["qs"], axis=1, keepdims=True) + dden * f["qn"]) * a
        g_w = (jnp.sum(v * kds, axis=1, keepdims=True) + jnp.sum(k * dn1, axis=1, keepdims=True)) * w_end
        g_end = (jnp.sum(jnp.sum(ds1 * s0, axis=1, keepdims=True), axis=0, keepdims=True)
                 + jnp.sum(dn1 * n0, axis=1, keepdims=True)) * a_end
        col_e = jnp.sum(e, axis=0, keepdims=True)
        db = jnp.sum(e, axis=1, keepdims=True) - to_col(col_e) + g_a - g_w
        last = lax.broadcasted_iota(jnp.int32, (L, 1), 0) == (0 if reverse else L - 1)
        db = db + jnp.where(last, jnp.sum(g_w, axis=0, keepdims=True) + g_end, 0.0)
        dg_ref[0, 0] = col_e + to_row(g_w)
        dlf = jnp.sum(jnp.where(f["seen"], db, 0.0), axis=0, keepdims=True)
        dg_ref[1, 0] = dlf * jax.nn.sigmoid(-gr_ref[1, 0])

        ds_sc[...] = a_end * ds1 + _dot_tn(qm, adn)
        dn_sc[...] = a_end * dn1 + jnp.sum((dden * a) * q, axis=0, keepdims=True)

    qk_spec = pl.BlockSpec((L, DK_A), lambda h, i: (off + pos(i), h))
    v_spec = pl.BlockSpec((L, DV_A), lambda h, i: (off + pos(i), h))
    gr_spec = pl.BlockSpec((2, 1, 1, L), lambda h, i: (0, h, 0, off + pos(i)))
    st_spec = pl.BlockSpec((1, 1, DK_A, DV_A), lambda h, i: (h, pos(i), 0, 0))
    vec_spec = pl.BlockSpec((1, 1, 1, DK_A), lambda h, i: (h, pos(i), 0, 0))
    oqk_spec = pl.BlockSpec((L, DK_A), lambda h, i: (pos(i), h))
    ov_spec = pl.BlockSpec((L, DV_A), lambda h, i: (pos(i), h))
    og_spec = pl.BlockSpec((2, 1, 1, L), lambda h, i: (0, h, 0, pos(i)))
    return pl.pallas_call(
        body, grid=(NH_A, nc),
        in_specs=[qk_spec, qk_spec, v_spec, gr_spec, st_spec, vec_spec, vec_spec, ov_spec],
        out_specs=[oqk_spec, oqk_spec, ov_spec, og_spec],
        out_shape=[jax.ShapeDtypeStruct((n, QK_A), F32), jax.ShapeDtypeStruct((n, QK_A), F32),
                   jax.ShapeDtypeStruct((n, V_A), F32), jax.ShapeDtypeStruct((2, NH_A, 1, n), F32)],
        scratch_shapes=[pltpu.VMEM((DK_A, DV_A), F32), pltpu.VMEM((1, DK_A), F32)],
        compiler_params=pltpu.CompilerParams(dimension_semantics=("parallel", "arbitrary"),
                                             vmem_limit_bytes=VMEM_LIMIT),
        name="mlstm_bwd")(q, k, v, gr, s0_all, n0_all, m0_all, dh)


def _make_mlstm(n, row_off, reverse):
    def gate_rows(li, lf):
        return jnp.stack([li, lf]).transpose(0, 2, 1)[:, :, None, :]

    @jax.custom_vjp
    def op(q, k, v, li, lf):
        return _mlstm_fwd_call(q, k, v, gate_rows(li, lf), n, row_off, reverse)[0]

    def fwd(q, k, v, li, lf):
        gr = gate_rows(li, lf)
        h, s0, n0, m0 = _mlstm_fwd_call(q, k, v, gr, n, row_off, reverse)
        return h, (q, k, v, gr, s0, n0, m0)

    def bwd(res, dh):
        q, k, v, gr, s0, n0, m0 = res
        dq, dk, dv, dg = _mlstm_bwd_call(q, k, v, gr, s0, n0, m0, dh, n, row_off, reverse)
        rows = ((row_off, q.shape[0] - row_off - n), (0, 0))
        dg = jnp.pad(dg[:, :, 0, :].transpose(0, 2, 1), ((0, 0),) + rows)
        return jnp.pad(dq, rows), jnp.pad(dk, rows), jnp.pad(dv, rows), dg[0], dg[1]

    op.defvjp(fwd, bwd)
    return op


MERGE_ROWS = 128


def _sig(x):
    return jax.nn.sigmoid(x)


def _merge_pre_fwd(h_f, h_b, o_attn, p_oa, p_za, p_zb, mh_w, tc):
    t = o_attn.shape[0]
    tr, off = MERGE_ROWS, tc // MERGE_ROWS

    def body(hf_ref, hb_ref, oat_ref, oa_ref, za_ref, zb_ref, w_ref, a_ref, b_ref):
        for hd in range(NH_A):
            cols = slice(hd * DV_A, (hd + 1) * DV_A)
            h = hf_ref[:, cols] + hb_ref[:, cols]
            hn = h * lax.rsqrt(jnp.mean(h * h, axis=-1, keepdims=True) + EPS) * w_ref[:, cols]
            za = za_ref[:, cols]
            a_ref[:, cols] = (_sig(oa_ref[:, cols]) * hn * (za * _sig(za))).astype(BF16)
        zb = zb_ref[...]
        b_ref[...] = (oat_ref[...] * (zb * _sig(zb))).astype(BF16)

    lat = pl.BlockSpec((tr, V_A), lambda i: (i + off, 0))
    row = pl.BlockSpec((tr, V_A), lambda i: (i, 0))
    return pl.pallas_call(
        body, grid=(t // tr,),
        in_specs=[lat, row, row, lat, lat, lat, pl.BlockSpec((1, V_A), lambda i: (0, 0))],
        out_specs=[row, row],
        out_shape=[jax.ShapeDtypeStruct((t, V_A), BF16), jax.ShapeDtypeStruct((t, V_A), BF16)],
        compiler_params=pltpu.CompilerParams(dimension_semantics=("parallel",), vmem_limit_bytes=VMEM_LIMIT),
        name="merge_pre")(h_f, h_b, o_attn, p_oa, p_za, p_zb, mh_w)


def _ctx_block(i, nb, off):
    k = i - nb
    return jnp.where(i < nb, i + off, jnp.where(k < off, k, k + nb))


def _merge_pre_bwd(da, db, h_f, h_b, o_attn, p_oa, p_za, p_zb, mh_w, tc):
    t = o_attn.shape[0]
    n, r = h_f.shape[0], p_oa.shape[0]
    tr, off = MERGE_ROWS, tc // MERGE_ROWS
    nb = t // tr
    n_ctx = r // tr - nb

    def body(da_ref, db_ref, hf_ref, hb_ref, oat_ref, oa_ref, za_ref, zb_ref, w_ref,
             dhf_ref, dhb_ref, doat_ref, doa_ref, dza_ref, dzb_ref, dw_ref):
        i = pl.program_id(0)

        @pl.when(i == 0)
        def _():
            dw_ref[...] = jnp.zeros(dw_ref.shape, F32)

        @pl.when(i < nb)
        def _():
            for hd in range(NH_A):
                cols = slice(hd * DV_A, (hd + 1) * DV_A)
                h = hf_ref[:, cols] + hb_ref[:, cols]
                rs = lax.rsqrt(jnp.mean(h * h, axis=-1, keepdims=True) + EPS)
                w = w_ref[:, cols]
                hn = h * rs * w
                oa, za, g = oa_ref[:, cols], za_ref[:, cols], da_ref[:, cols]
                so, sz = _sig(oa), _sig(za)
                silu_z = za * sz
                doa_ref[:, cols] = g * hn * silu_z * so * (1.0 - so)
                dza_ref[:, cols] = g * so * hn * (sz * (1.0 + za * (1.0 - sz)))
                dhn = g * so * silu_z
                dw_ref[:, cols] += jnp.sum(dhn * (h * rs), axis=0, keepdims=True)
                dxn = dhn * w
                dh = rs * dxn - h * (rs * rs * rs * jnp.mean(dxn * h, axis=-1, keepdims=True))
                dhf_ref[:, cols] = dh
                dhb_ref[:, cols] = dh
            zb, gb, oat = zb_ref[...], db_ref[...], oat_ref[...]
            sb = _sig(zb)
            doat_ref[...] = gb * (zb * sb)
            dzb_ref[...] = gb * oat * (sb * (1.0 + zb * (1.0 - sb)))

        @pl.when(i >= nb)
        def _():
            for ref in (dhf_ref, dhb_ref, doa_ref, dza_ref, dzb_ref):
                ref[...] = jnp.zeros(ref.shape, F32)

    lati = lambda i: jnp.minimum(i, nb - 1)
    lat = pl.BlockSpec((tr, V_A), lambda i: (lati(i) + off, 0))
    row = pl.BlockSpec((tr, V_A), lambda i: (lati(i), 0))
    vec = pl.BlockSpec((1, V_A), lambda i: (0, 0))
    pout = pl.BlockSpec((tr, V_A), lambda i: (_ctx_block(i, nb, off), 0))
    hf_out = pl.BlockSpec((tr, V_A), lambda i: (jnp.where(i < nb, i + off, jnp.minimum(i - nb, off - 1)), 0))
    hb_out = pl.BlockSpec((tr, V_A), lambda i: (jnp.where(i < nb, i, nb + jnp.minimum(i - nb, off - 1)), 0))
    return pl.pallas_call(
        body, grid=(nb + n_ctx,),
        in_specs=[row, row, lat, row, row, lat, lat, lat, vec],
        out_specs=[hf_out, hb_out, row, pout, pout, pout, vec],
        out_shape=[jax.ShapeDtypeStruct((n, V_A), F32), jax.ShapeDtypeStruct((n, V_A), F32),
                   jax.ShapeDtypeStruct((t, V_A), F32), jax.ShapeDtypeStruct((r, V_A), F32),
                   jax.ShapeDtypeStruct((r, V_A), F32), jax.ShapeDtypeStruct((r, V_A), F32),
                   jax.ShapeDtypeStruct((1, V_A), F32)],
        compiler_params=pltpu.CompilerParams(dimension_semantics=("arbitrary",), vmem_limit_bytes=VMEM_LIMIT),
        name="merge_pre_bwd")(da, db, h_f, h_b, o_attn, p_oa, p_za, p_zb, mh_w)


def _merge_gate_fwd(y_a, y_b, p_ga, p_gb, tc):
    t = y_a.shape[0]
    tr, off = MERGE_ROWS, tc // MERGE_ROWS

    def body(ya_ref, yb_ref, ga_ref, gb_ref, m_ref):
        m_ref[...] = (_sig(ga_ref[...]) * ya_ref[...] + _sig(gb_ref[...]) * yb_ref[...]).astype(BF16)

    lat = pl.BlockSpec((tr, D_MODEL), lambda i: (i + off, 0))
    row = pl.BlockSpec((tr, D_MODEL), lambda i: (i, 0))
    return pl.pallas_call(
        body, grid=(t // tr,), in_specs=[row, row, lat, lat], out_specs=row,
        out_shape=jax.ShapeDtypeStruct((t, D_MODEL), BF16),
        compiler_params=pltpu.CompilerParams(dimension_semantics=("parallel",), vmem_limit_bytes=VMEM_LIMIT),
        name="merge_gate")(y_a, y_b, p_ga, p_gb)


def _merge_gate_bwd(dm, y_a, y_b, p_ga, p_gb, tc):
    t, r = y_a.shape[0], p_ga.shape[0]
    tr, off = MERGE_ROWS, tc // MERGE_ROWS
    nb = t // tr
    n_ctx = r // tr - nb

    def body(dm_ref, ya_ref, yb_ref, ga_ref, gb_ref, dya_ref, dyb_ref, dga_ref, dgb_ref):
        i = pl.program_id(0)

        @pl.when(i < nb)
        def _():
            dm = dm_ref[...]
            sa, sb = _sig(ga_ref[...]), _sig(gb_ref[...])
            dya_ref[...] = (dm * sa).astype(BF16)
            dyb_ref[...] = (dm * sb).astype(BF16)
            dga_ref[...] = dm * ya_ref[...] * sa * (1.0 - sa)
            dgb_ref[...] = dm * yb_ref[...] * sb * (1.0 - sb)

        @pl.when(i >= nb)
        def _():
            dga_ref[...] = jnp.zeros(dga_ref.shape, F32)
            dgb_ref[...] = jnp.zeros(dgb_ref.shape, F32)

    lati = lambda i: jnp.minimum(i, nb - 1)
    lat = pl.BlockSpec((tr, D_MODEL), lambda i: (lati(i) + off, 0))
    row = pl.BlockSpec((tr, D_MODEL), lambda i: (lati(i), 0))
    pout = pl.BlockSpec((tr, D_MODEL), lambda i: (_ctx_block(i, nb, off), 0))
    return pl.pallas_call(
        body, grid=(nb + n_ctx,), in_specs=[row, row, row, lat, lat], out_specs=[row, row, pout, pout],
        out_shape=[jax.ShapeDtypeStruct((t, D_MODEL), BF16), jax.ShapeDtypeStruct((t, D_MODEL), BF16),
                   jax.ShapeDtypeStruct((r, D_MODEL), F32), jax.ShapeDtypeStruct((r, D_MODEL), F32)],
        compiler_params=pltpu.CompilerParams(dimension_semantics=("arbitrary",), vmem_limit_bytes=VMEM_LIMIT),
        name="merge_gate_bwd")(dm, y_a, y_b, p_ga, p_gb)


def _make_merge_block(tc):
    @jax.custom_vjp
    def block(h_f, h_b, o_attn, p_oa, p_za, p_zb, p_ga, p_gb, mh_w, w_ba, w_bb, w_out, pr_ba, pr_bb, pr_out):
        return fwd(h_f, h_b, o_attn, p_oa, p_za, p_zb, p_ga, p_gb, mh_w, w_ba, w_bb, w_out, pr_ba, pr_bb, pr_out)[0]

    def fwd(h_f, h_b, o_attn, p_oa, p_za, p_zb, p_ga, p_gb, mh_w, w_ba, w_bb, w_out, pr_ba, pr_bb, pr_out):
        a_in, b_in = _merge_pre_fwd(h_f, h_b, o_attn, p_oa, p_za, p_zb, mh_w[None], tc)
        y_a, y_b = _mm_nn(a_in, w_ba, "merge_ya"), _mm_nn(b_in, w_bb, "merge_yb")
        m_in = _merge_gate_fwd(y_a, y_b, p_ga, p_gb, tc)
        out = _mm_nn(m_in, w_out, "merge_out")
        return out, (h_f, h_b, o_attn, p_oa, p_za, p_zb, p_ga, p_gb, mh_w, w_ba, w_bb, w_out, a_in, b_in, y_a, y_b, m_in)

    def bwd(res, dout):
        h_f, h_b, o_attn, p_oa, p_za, p_zb, p_ga, p_gb, mh_w, w_ba, w_bb, w_out, a_in, b_in, y_a, y_b, m_in = res
        dm = _mm_nt(dout, w_out, "merge_out_da")
        dw_out = _mm_tn(m_in, dout, "merge_out_dw")
        dy_a, dy_b, dga, dgb = _merge_gate_bwd(dm, y_a, y_b, p_ga, p_gb, tc)
        da, db = _mm_nt(dy_a, w_ba, "merge_ya_da"), _mm_nt(dy_b, w_bb, "merge_yb_da")
        dw_ba, dw_bb = _mm_tn(a_in, dy_a, "merge_ya_dw"), _mm_tn(b_in, dy_b, "merge_yb_dw")
        dhf, dhb, doat, doa, dza, dzb, dmh = _merge_pre_bwd(da, db, h_f, h_b, o_attn, p_oa, p_za, p_zb, mh_w[None], tc)
        z = jnp.zeros_like
        return (dhf, dhb, doat, doa, dza, dzb, dga, dgb, dmh[0], z(w_ba), z(w_bb), z(w_out), dw_ba, dw_bb, dw_out)

    block.defvjp(fwd, bwd)
    return block


def _silu(x):
    return x * jax.nn.sigmoid(x)


CONV_ROWS, CONV_HALO = 256, 8


def _make_conv(t, tc):
    r = t + 2 * tc
    width = 2 * QK_A
    nblk = r // CONV_ROWS
    cb, nb = tc // CONV_ROWS, t // CONV_ROWS
    k_scale = DK_A ** -0.5
    per = CONV_ROWS // CONV_HALO

    def taps(x_ref, prev_ref, next_ref):
        i = pl.program_id(0)
        seg_first = (i == 0) | (i == cb) | (i == cb + nb)
        seg_last = (i == cb - 1) | (i == cb + nb - 1) | (i == nblk - 1)
        x = x_ref[...]
        rows = lax.broadcasted_iota(jnp.int32, (CONV_ROWS, 1), 0)
        before = jnp.where(seg_first, 0.0, prev_ref[CONV_HALO - 1:CONV_HALO, :])
        after = jnp.where(seg_last, 0.0, next_ref[0:1, :])
        xm1 = jnp.where(rows == 0, before, pltpu.roll(x, 1, 0))
        xp1 = jnp.where(rows == CONV_ROWS - 1, after, pltpu.roll(x, CONV_ROWS - 1, 0))
        return xm1, x, xp1

    row = pl.BlockSpec((CONV_ROWS, width), lambda i: (i, 0))
    prev = pl.BlockSpec((CONV_HALO, width), lambda i: (jnp.maximum(i * per - 1, 0), 0))
    nxt = pl.BlockSpec((CONV_HALO, width), lambda i: (jnp.minimum((i + 1) * per, r // CONV_HALO - 1), 0))
    half = pl.BlockSpec((CONV_ROWS, QK_A), lambda i: (i, 0))
    wspec = pl.BlockSpec((3, width), lambda i: (0, 0))
    bspec = pl.BlockSpec((1, width), lambda i: (0, 0))
    par = pltpu.CompilerParams(dimension_semantics=("parallel",), vmem_limit_bytes=VMEM_LIMIT)
    seq = pltpu.CompilerParams(dimension_semantics=("arbitrary",), vmem_limit_bytes=VMEM_LIMIT)

    def fwd_call(x, cw, cb_):
        def body(x_ref, p_ref, n_ref, w_ref, b_ref, q_ref, k_ref):
            xm1, x0, xp1 = taps(x_ref, p_ref, n_ref)
            c = b_ref[...] + xm1 * w_ref[0:1, :] + x0 * w_ref[1:2, :] + xp1 * w_ref[2:3, :]
            y = c * jax.nn.sigmoid(c)
            q_ref[...] = y[:, :QK_A]
            k_ref[...] = y[:, QK_A:] * k_scale

        return pl.pallas_call(
            body, grid=(nblk,), in_specs=[row, prev, nxt, wspec, bspec], out_specs=[half, half],
            out_shape=[jax.ShapeDtypeStruct((r, QK_A), F32), jax.ShapeDtypeStruct((r, QK_A), F32)],
            compiler_params=par, name="conv_silu")(x, x, x, cw, cb_)

    def bwd_pre_call(dq, dk, x, cw, cb_):
        def body(dq_ref, dk_ref, x_ref, p_ref, n_ref, w_ref, b_ref, dc_ref, dw_ref, db_ref):
            @pl.when(pl.program_id(0) == 0)
            def _():
                dw_ref[...] = jnp.zeros(dw_ref.shape, F32)
                db_ref[...] = jnp.zeros(db_ref.shape, F32)

            xm1, x0, xp1 = taps(x_ref, p_ref, n_ref)
            c = b_ref[...] + xm1 * w_ref[0:1, :] + x0 * w_ref[1:2, :] + xp1 * w_ref[2:3, :]
            s = jax.nn.sigmoid(c)
            dy = jnp.concatenate([dq_ref[...], dk_ref[...] * k_scale], axis=1)
            dc = dy * (s * (1.0 + c * (1.0 - s)))
            dc_ref[...] = dc
            db_ref[...] += jnp.sum(dc, axis=0, keepdims=True)
            dw_ref[0:1, :] += jnp.sum(dc * xm1, axis=0, keepdims=True)
            dw_ref[1:2, :] += jnp.sum(dc * x0, axis=0, keepdims=True)
            dw_ref[2:3, :] += jnp.sum(dc * xp1, axis=0, keepdims=True)

        return pl.pallas_call(
            body, grid=(nblk,), in_specs=[half, half, row, prev, nxt, wspec, bspec], out_specs=[row, wspec, bspec],
            out_shape=[jax.ShapeDtypeStruct((r, width), F32), jax.ShapeDtypeStruct((3, width), F32),
                       jax.ShapeDtypeStruct((1, width), F32)],
            compiler_params=seq, name="conv_silu_bwd")(dq, dk, x, x, x, cw, cb_)

    def bwd_x_call(dc, cw):
        def body(d_ref, p_ref, n_ref, w_ref, dx_ref):
            dm1, d0, dp1 = taps(d_ref, p_ref, n_ref)
            dx_ref[...] = dm1 * w_ref[2:3, :] + d0 * w_ref[1:2, :] + dp1 * w_ref[0:1, :]

        return pl.pallas_call(
            body, grid=(nblk,), in_specs=[row, prev, nxt, wspec], out_specs=row,
            out_shape=jax.ShapeDtypeStruct((r, width), F32), compiler_params=par,
            name="conv_silu_bwd_x")(dc, dc, dc, cw)

    @jax.custom_vjp
    def op(x, cw, cb_):
        return tuple(fwd_call(x, cw, cb_[None]))

    def fwd(x, cw, cb_):
        return tuple(fwd_call(x, cw, cb_[None])), (x, cw, cb_)

    def bwd(res, cot):
        x, cw, cb_ = res
        dc, dw, db = bwd_pre_call(cot[0], cot[1], x, cw, cb_[None])
        return bwd_x_call(dc, cw), dw, db[0]

    op.defvjp(fwd, bwd)
    return op


def _local_loss(diff, const):
    x, ctx, target = diff["x"], const["ctx"], const["target"]
    t, tc = x.shape[0], ctx.shape[0]
    n = tc + t

    mod = diff["mod"]
    shift, scale, gate = mod[0, :D_MODEL], mod[0, D_MODEL:2 * D_MODEL], mod[0, 2 * D_MODEL:]
    shift_c, scale_c = mod[1, :D_MODEL], mod[1, D_MODEL:2 * D_MODEL]
    (p_qk, p_va, p_kv, p_oa, p_za, p_qb, p_zb, p_ga, p_gb), p_if = _ln_project(
        x, ctx, jnp.stack([scale, shift, scale_c, shift_c]), const["w_main"], const["w_if"], diff["p_main"], diff["p_if"])
    gt = p_if[:, :N_IF] + diff["b_if"]

    q_a, k_a = _make_conv(t, tc)(p_qk, diff["conv_w"], diff["conv_b"])
    v_a = p_va
    li_f, lf_f, li_b, lf_b = gt[:, 0:8], gt[:, 8:16], gt[:, 16:24], gt[:, 24:32]

    h_f = _make_mlstm(n, 0, False)(q_a, k_a, v_a, li_f, lf_f)
    h_b = _make_mlstm(n, tc, True)(q_a, k_a, v_a, li_b, lf_b)

    lat = slice(tc, n)
    o_attn = _gqa(p_qb[lat], p_kv[:n, :KV_B], p_kv[:n, KV_B:],
                  diff["q_norm_w"], diff["k_norm_w"])

    out = _make_merge_block(tc)(h_f, h_b, o_attn, p_oa, p_za, p_zb, p_ga, p_gb, diff["mh_norm_w"],
                                const["w_ba"], const["w_bb"], const["w_out"], diff["p_ba"], diff["p_bb"], diff["p_out"])

    return _loss_head(x, out, target, gate, diff["ln_w"], diff["ln_b"])


OTHER_CHIPS = [(1, 0), (0, 1), (1, 1)]


def _flip(v, bit):
    return 1 - v if bit else v


def _gather_chips(shard, name):
    def body(x_ref, o_ref, send_sems, recv_sems, local_sem):
        x, y, c = lax.axis_index("x"), lax.axis_index("y"), lax.axis_index("c")
        mine = pltpu.make_async_copy(x_ref, o_ref.at[2 * x + y], local_sem)
        mine.start()

        def copy(r, slot):
            dx, dy = OTHER_CHIPS[r]
            return pltpu.make_async_remote_copy(
                src_ref=x_ref, dst_ref=o_ref.at[slot], send_sem=send_sems.at[r], recv_sem=recv_sems.at[r],
                device_id=(_flip(x, dx), _flip(y, dy), c), device_id_type=MESH)

        sends = [copy(r, 2 * x + y) for r in range(3)]
        for cp in sends:
            cp.start()
        for r, (dx, dy) in enumerate(OTHER_CHIPS):
            copy(r, 2 * _flip(x, dx) + _flip(y, dy)).wait_recv()
        for cp in sends:
            cp.wait_send()
        mine.wait()

    return pl.pallas_call(
        body, out_shape=jax.ShapeDtypeStruct((N_CHIPS,) + shard.shape, shard.dtype),
        in_specs=[pl.BlockSpec(memory_space=pl.ANY)], out_specs=pl.BlockSpec(memory_space=pl.ANY),
        scratch_shapes=[pltpu.SemaphoreType.DMA((3,)), pltpu.SemaphoreType.DMA((3,)), pltpu.SemaphoreType.DMA],
        name=name)(shard)


def _gather_chips_halves(shard, name):
    rows, cols = shard.shape
    halves = shard.reshape(2, rows // 2, cols)

    def body(x_ref, o_ref, send_sems, recv_sems, local_sem):
        x, y, c = lax.axis_index("x"), lax.axis_index("y"), lax.axis_index("c")
        my_chip = 2 * x + y
        mine = pltpu.make_async_copy(x_ref, o_ref.at[my_chip], local_sem)
        mine.start()

        def chip_of(r):
            dx, dy = OTHER_CHIPS[r]
            return _flip(x, dx), _flip(y, dy)

        def copy(k, chip_slot, half, to, src=None):
            dst = o_ref.at[chip_slot, half]
            return pltpu.make_async_remote_copy(
                src_ref=dst if src is None else src, dst_ref=dst, send_sem=send_sems.at[k],
                recv_sem=recv_sems.at[k], device_id=to, device_id_type=MESH)

        first = [copy(r, my_chip, c, (*chip_of(r), c), src=x_ref.at[c]) for r in range(3)]
        for cp in first:
            cp.start()
        passed = []
        for r in range(3):
            px, py = chip_of(r)
            copy(r, 2 * px + py, c, (px, py, c)).wait_recv()
            passed.append(copy(3 + r, 2 * px + py, c, (x, y, 1 - c)))
            passed[-1].start()
        for r in range(3):
            px, py = chip_of(r)
            copy(3 + r, 2 * px + py, 1 - c, (x, y, 1 - c)).wait_recv()
        for cp in first + passed:
            cp.wait_send()
        mine.wait()

    out = pl.pallas_call(
        body, out_shape=jax.ShapeDtypeStruct((N_CHIPS, 2, rows // 2, cols), shard.dtype),
        in_specs=[pl.BlockSpec(memory_space=pl.ANY)], out_specs=pl.BlockSpec(memory_space=pl.ANY),
        scratch_shapes=[pltpu.SemaphoreType.DMA((6,)), pltpu.SemaphoreType.DMA((6,)), pltpu.SemaphoreType.DMA],
        name=name)(halves)
    return out.reshape(N_CHIPS, rows, cols)


def _scatter_grads(slots, name):
    def body(g_ref, o_ref, send_sems, recv_sems, local_sem):
        x, y, c = lax.axis_index("x"), lax.axis_index("y"), lax.axis_index("c")
        me, my_chip, sibling = 4 * x + 2 * y + c, 2 * x + y, (x, y, 1 - c)
        mine = pltpu.make_async_copy(g_ref.at[my_chip], o_ref.at[me], local_sem)
        mine.start()

        def chip_of(r):
            dx, dy = OTHER_CHIPS[r]
            return _flip(x, dx), _flip(y, dy)

        def copy(k, slot, to, src=None):
            dst = o_ref.at[slot]
            return pltpu.make_async_remote_copy(
                src_ref=dst if src is None else src, dst_ref=dst, send_sem=send_sems.at[k],
                recv_sem=recv_sems.at[k], device_id=to, device_id_type=MESH)

        first = [copy(0, me, sibling, src=g_ref.at[my_chip])]
        for r in range(3):
            px, py = chip_of(r)
            first.append(copy(1 + r, me, (px, py, c), src=g_ref.at[2 * px + py]))
        for cp in first:
            cp.start()
        passed = []
        for r in range(3):
            px, py = chip_of(r)
            copy(1 + r, 4 * px + 2 * py + c, (px, py, c)).wait_recv()
            passed.append(copy(4 + r, 4 * px + 2 * py + c, sibling))
            passed[-1].start()
        copy(0, 4 * x + 2 * y + 1 - c, sibling).wait_recv()
        for r in range(3):
            px, py = chip_of(r)
            copy(4 + r, 4 * px + 2 * py + 1 - c, sibling).wait_recv()
        for cp in first + passed:
            cp.wait_send()
        mine.wait()

    return pl.pallas_call(
        body, out_shape=jax.ShapeDtypeStruct((N_DEV,) + slots.shape[1:], slots.dtype),
        in_specs=[pl.BlockSpec(memory_space=pl.ANY)], out_specs=pl.BlockSpec(memory_space=pl.ANY),
        scratch_shapes=[pltpu.SemaphoreType.DMA((N_DEV - 1,)), pltpu.SemaphoreType.DMA((N_DEV - 1,)),
                        pltpu.SemaphoreType.DMA],
        name=name)(slots)


def _allreduce_small(v, name):
    def body(v_ref, o_ref, buf, send_sems, recv_sems):
        x, y, c = lax.axis_index("x"), lax.axis_index("y"), lax.axis_index("c")
        me = 4 * x + 2 * y + c
        buf[me] = v_ref[...]

        def peer(r):
            return _flip(x, (r >> 2) & 1), _flip(y, (r >> 1) & 1), _flip(c, r & 1)

        def copy(r, dst_slot):
            return pltpu.make_async_remote_copy(
                src_ref=v_ref, dst_ref=buf.at[dst_slot], send_sem=send_sems.at[r - 1],
                recv_sem=recv_sems.at[r - 1], device_id=peer(r), device_id_type=MESH)

        sends = [copy(r, me) for r in range(1, N_DEV)]
        for cp in sends:
            cp.start()
        for r in range(1, N_DEV):
            px, py, pc = peer(r)
            copy(r, 4 * px + 2 * py + pc).wait_recv()
        for cp in sends:
            cp.wait_send()
        acc = buf[0]
        for d in range(1, N_DEV):
            acc = acc + buf[d]
        o_ref[...] = acc

    return pl.pallas_call(
        body, out_shape=jax.ShapeDtypeStruct(v.shape, v.dtype),
        in_specs=[pl.BlockSpec(memory_space=pltpu.VMEM)], out_specs=pl.BlockSpec(memory_space=pltpu.VMEM),
        scratch_shapes=[pltpu.VMEM((N_DEV,) + v.shape, v.dtype), pltpu.SemaphoreType.DMA((N_DEV - 1,)),
                        pltpu.SemaphoreType.DMA((N_DEV - 1,))],
        name=name)(v)


def _adamw_math(w, g, m, v):
    m = ADAM_B1 * m + (1.0 - ADAM_B1) * g
    v = ADAM_B2 * v + (1.0 - ADAM_B2) * jnp.square(g)
    m_hat = m / (1.0 - ADAM_B1 ** ADAM_STEP)
    v_hat = v / (1.0 - ADAM_B2 ** ADAM_STEP)
    delta = -ADAM_LR * (m_hat / (jnp.sqrt(v_hat) + ADAM_EPS) + ADAM_WD * w)
    return delta, m, v


def _adamw_sum(parts, w, m, v, name):
    npart, rows, cols = parts.shape
    tr = _pick(rows, (64, 32, 16, 8)) if rows >= 8 else rows

    def body(p_ref, w_ref, m_ref, v_ref, g_out, d_out, m_out, v_out):
        g = p_ref[0].astype(F32)
        for k in range(1, npart):
            g = g + p_ref[k].astype(F32)
        d, m2, v2 = _adamw_math(w_ref[...], g, m_ref[...], v_ref[...])
        g_out[...] = g
        d_out[...] = d
        m_out[...] = m2
        v_out[...] = v2

    spec = pl.BlockSpec((tr, cols), lambda i: (i, 0))
    shp = jax.ShapeDtypeStruct((rows, cols), F32)
    return pl.pallas_call(
        body, grid=(rows // tr,),
        in_specs=[pl.BlockSpec((npart, tr, cols), lambda i: (0, i, 0)), spec, spec, spec],
        out_specs=[spec, spec, spec, spec], out_shape=[shp, shp, shp, shp],
        compiler_params=pltpu.CompilerParams(dimension_semantics=("parallel",), vmem_limit_bytes=VMEM_LIMIT),
        name=name)(parts, w, m, v)


SMALL_ROWS = 16


def _pack_small(c_ctx, b_mod, conv_b, mh, ln_w, ln_b, conv_w_rows, b_if, qn, kn):
    last = jnp.concatenate([b_if.reshape(-1), qn.reshape(-1), kn.reshape(-1),
                            jnp.zeros((D_MODEL - N_IF - 2 * HD_B,), F32)])
    rows = [c_ctx.reshape(1, D_MODEL), b_mod.reshape(3, D_MODEL), conv_b.reshape(1, D_MODEL),
            mh.reshape(1, D_MODEL), ln_w.reshape(1, D_MODEL), ln_b.reshape(1, D_MODEL),
            conv_w_rows.reshape(3, D_MODEL), last[None], jnp.zeros((SMALL_ROWS - 12, D_MODEL), F32)]
    return jnp.concatenate(rows, 0)


def _unpack_small(pk, conv_cols):
    return dict(c_ctx=pk[0], b_mod=pk[1:4].reshape(1, 3 * D_MODEL), conv_b=pk[4:5], mh_norm_w=pk[5:6],
                ln_w=pk[6:7], ln_b=pk[7:8], conv_w=pk[8:11, :conv_cols][None], b_if=pk[11:12, :N_IF],
                q_norm_w=pk[11:12, N_IF:N_IF + HD_B], k_norm_w=pk[11:12, N_IF + HD_B:N_IF + 2 * HD_B])


def kernel(x, c, ctx, c_ctx, w_mod, b_mod, w_in, b_if, conv_w, conv_b, mh_norm_w, q_norm_w, k_norm_w, w_branch_a, w_branch_b, w_out, ln_w, ln_b, loss_target, m_c_ctx, m_w_mod, m_b_mod, m_w_in, m_b_if, m_conv_w, m_conv_b, m_mh_norm_w, m_q_norm_w, m_k_norm_w, m_w_branch_a, m_w_branch_b, m_w_out, m_ln_w, m_ln_b, v_c_ctx, v_w_mod, v_b_mod, v_w_in, v_b_if, v_conv_w, v_conv_b, v_mh_norm_w, v_q_norm_w, v_k_norm_w, v_w_branch_a, v_w_branch_b, v_w_out, v_ln_w, v_ln_b):
    core = lax.axis_index("c")
    chip = 2 * lax.axis_index("x") + lax.axis_index("y")
    me = 2 * chip + core
    mod_cols, in_cols, conv_cols = w_mod.shape[2], w_in.shape[2], conv_w.shape[2]
    br_rows = w_out.shape[1]

    def rows_at(block, first):
        return lax.dynamic_update_slice(jnp.zeros((SMALL_ROWS, block.shape[1]), F32), block, (first, 0))

    owner = (core == 0).astype(F32)
    cond = _allreduce_small(rows_at(jnp.stack([_silu(c[0]), _silu(c_ctx)]), 2 * me), "gather_cond").astype(BF16)
    w_mod_b = w_mod[0].astype(BF16)
    mod_part = _mm_nn(cond, w_mod_b, "mod_fwd") * owner
    mod_all = _allreduce_small(
        lax.dynamic_update_slice(jnp.zeros((SMALL_ROWS, 3 * D_MODEL), F32), mod_part, (0, chip * mod_cols)),
        "gather_mod") + b_mod[0]
    mod = lax.dynamic_slice(mod_all, (2 * me, 0), (2, 3 * D_MODEL))

    g_in = _gather_chips_halves(w_in[0].astype(BF16), "gather_w_in")
    g_ba = _gather_chips_halves(w_branch_a[0].astype(BF16), "gather_w_ba")
    g_bb = _gather_chips_halves(w_branch_b[0].astype(BF16), "gather_w_bb")
    g_out = _gather_chips_halves(w_out[0].astype(BF16), "gather_w_out")
    g_conv = _gather_chips(conv_w[0], "gather_conv_w")
    w_in_full =jnp.moveaxis(g_in, 0, 1).reshape(D_MODEL, N_CHIPS * in_cols)
    w_main = jnp.concatenate([w_in_full[:, :IF_START], w_in_full[:, IF_START + N_IF:]], 1)
    w_if = jnp.pad(w_in_full[:, IF_START:IF_START + N_IF], ((0, 0), (0, IF_PAD - N_IF)))
    conv_w_full = jnp.moveaxis(g_conv, 0, 1).reshape(3, N_CHIPS * conv_cols)

    const = dict(ctx=ctx[0], target=loss_target[0], w_main=w_main, w_if=w_if,
                 w_ba=g_ba.reshape(D_MODEL, D_MODEL), w_bb=g_bb.reshape(D_MODEL, D_MODEL),
                 w_out=g_out.reshape(D_MODEL, D_MODEL))
    diff = dict(x=x[0], mod=mod, b_if=b_if[0], conv_w=conv_w_full, conv_b=conv_b[0],
                mh_norm_w=mh_norm_w[0], q_norm_w=q_norm_w[0], k_norm_w=k_norm_w[0], ln_w=ln_w[0], ln_b=ln_b[0],
                p_main=jnp.zeros(w_main.shape, F32),
                p_if=jnp.zeros(w_if.shape, F32), p_ba=jnp.zeros((D_MODEL, D_MODEL), F32),
                p_bb=jnp.zeros((D_MODEL, D_MODEL), F32), p_out=jnp.zeros((D_MODEL, D_MODEL), F32))
    loss_local, g = jax.value_and_grad(_local_loss)(diff, const)
    loss = lax.psum(loss_local, ("x", "y", "c"))

    dmod_all = _allreduce_small(rows_at(g["mod"], 2 * me), "gather_dmod")
    dmod_k = lax.dynamic_slice(dmod_all, (0, chip * mod_cols), (SMALL_ROWS, mod_cols))
    g_w_mod = _mm_tn(cond, dmod_k, "mod_dw")
    g_b_mod = jnp.sum(dmod_all, axis=0) * (me == 0).astype(F32)
    d_cond = _mm_nt(dmod_k, w_mod_b, "mod_da")
    sig_ctx = jax.nn.sigmoid(c_ctx)
    g_c_ctx = owner * (sig_ctx * (1.0 + c_ctx * (1.0 - sig_ctx))) * jnp.sum(d_cond[1::2], axis=0)

    g_small = _allreduce_small(
        _pack_small(g_c_ctx, g_b_mod, g["conv_b"], g["mh_norm_w"], g["ln_w"], g["ln_b"], g["conv_w"],
                    g["b_if"], g["q_norm_w"], g["k_norm_w"]), "allreduce_small")
    conv_g = lax.dynamic_slice(g_small[8:11], (0, chip * conv_cols), (3, conv_cols))
    g_small = g_small.at[8:11].set(jnp.pad(conv_g, ((0, 0), (0, D_MODEL - conv_cols))))
    pad_conv = lambda a: jnp.pad(a[0], ((0, 0), (0, D_MODEL - conv_cols)))
    packed = [_pack_small(cc, bm[0], cb[0], mh[0], lw[0], lb[0], pad_conv(cw), bi[0], qn[0], kn[0])
              for cc, bm, cb, mh, lw, lb, cw, bi, qn, kn in (
                  (c_ctx, b_mod, conv_b, mh_norm_w, ln_w, ln_b, conv_w, b_if, q_norm_w, k_norm_w),
                  (m_c_ctx, m_b_mod, m_conv_b, m_mh_norm_w, m_ln_w, m_ln_b, m_conv_w, m_b_if, m_q_norm_w, m_k_norm_w),
                  (v_c_ctx, v_b_mod, v_conv_b, v_mh_norm_w, v_ln_w, v_ln_b, v_conv_w, v_b_if, v_q_norm_w, v_k_norm_w))]
    small = [_unpack_small(a, conv_cols)
             for a in _adamw_sum(g_small[None], packed[0], packed[1], packed[2], "adamw_small")]

    def col_slots(gfull, cols):
        return jnp.moveaxis(gfull.reshape(D_MODEL, N_CHIPS, cols), 1, 0).astype(BF16)

    g_in_full = jnp.concatenate([g["p_main"][:, :IF_START], g["p_if"][:, :N_IF], g["p_main"][:, IF_START:]], 1)
    big = {"w_mod": [a[None] for a in _adamw_sum(g_w_mod[None], w_mod[0], m_w_mod[0], v_w_mod[0], "adamw_w_mod")]}
    for nm, slots, w_, m_, v_ in (
            ("w_in", col_slots(g_in_full, in_cols), w_in, m_w_in, v_w_in),
            ("w_branch_a", g["p_ba"].reshape(N_CHIPS, br_rows, D_MODEL).astype(BF16), w_branch_a, m_w_branch_a, v_w_branch_a),
            ("w_branch_b", g["p_bb"].reshape(N_CHIPS, br_rows, D_MODEL).astype(BF16), w_branch_b, m_w_branch_b, v_w_branch_b),
            ("w_out", g["p_out"].reshape(N_CHIPS, br_rows, D_MODEL).astype(BF16), w_out, m_w_out, v_w_out)):
        parts = _scatter_grads(slots, "scatter_" + nm)
        big[nm] = [a[None] for a in _adamw_sum(parts, w_[0], m_[0], v_[0], "adamw_" + nm)]

    names = ["c_ctx", "w_mod", "b_mod", "w_in", "b_if", "conv_w", "conv_b", "mh_norm_w", "q_norm_w", "k_norm_w",
             "w_branch_a", "w_branch_b", "w_out", "ln_w", "ln_b"]
    outs = [[big[nm][k] if nm in big else small[k][nm] for nm in names] for k in range(4)]
    return (loss, g["x"][None], *outs[0], *outs[1], *outs[2], *outs[3])
```

```python
import functools

import jax
import jax.numpy as jnp
from jax import lax
from jax.experimental import pallas as pl
from jax.experimental.pallas import tpu as pltpu

F32 = jnp.float32
BF16 = jnp.bfloat16
MESH = pl.DeviceIdType.MESH

D_MODEL = 2048
NH_A, DK_A, DV_A = 8, 128, 256
QK_A, V_A = NH_A * DK_A, NH_A * DV_A
NH_B, NKV_B, HD_B = 16, 4, 128
Q_B, KV_B = NH_B * HD_B, NKV_B * HD_B
GRID_W = 64
ROT_HALF = HD_B // 2
ROPE_THETA = 10000.0
M_INIT = -1e30
EPS = 1e-6
ALPHA = 2.0 ** 0.25
N_IN = 17440
IF_START, N_IF, IF_PAD = 4096, 32, 128
N_MAIN = N_IN - N_IF
O_QK, O_VA, O_KB, O_VB, O_OA, O_ZA, O_QB, O_ZB, O_GA, O_GB = (
    0, 2048, 4096, 4608, 5120, 7168, 9216, 11264, 13312, 15360)
MLSTM_CHUNK = 256

ADAM_LR, ADAM_B1, ADAM_B2, ADAM_EPS, ADAM_WD, ADAM_STEP = 0.001, 0.9, 0.999, 1e-08, 0.01, 10

VMEM_LIMIT = 48 * 1024 * 1024
N_CHIPS, N_DEV = 4, 8
MX = BF16


def _pick(n, cands):
    for c in cands:
        if n % c == 0:
            return c
    raise ValueError(f"no tile for {n} in {cands}")


def _dot(a, b):
    return jnp.dot(a, b, preferred_element_type=F32)


def _dot_nt(a, b):
    return lax.dot_general(a, b, (((1,), (1,)), ((), ())), preferred_element_type=F32)


def _dot_tn(a, b):
    return lax.dot_general(a, b, (((0,), (0,)), ((), ())), preferred_element_type=F32)


def _mm_nn(a, b, name):
    m, k = a.shape
    _, n = b.shape
    tm = _pick(m, (512, 256, 128, 64, 32, 16))
    tn = _pick(n, (1024, 512, 256, 128))

    def body(a_ref, b_ref, o_ref):
        o_ref[...] = _dot(a_ref[...], b_ref[...])

    return pl.pallas_call(
        body, grid=(m // tm, n // tn),
        in_specs=[pl.BlockSpec((tm, k), lambda i, j: (i, 0)), pl.BlockSpec((k, tn), lambda i, j: (0, j))],
        out_specs=pl.BlockSpec((tm, tn), lambda i, j: (i, j)),
        out_shape=jax.ShapeDtypeStruct((m, n), F32),
        compiler_params=pltpu.CompilerParams(dimension_semantics=("parallel", "parallel"),
                                             vmem_limit_bytes=VMEM_LIMIT),
        name=name)(a, b)


def _mm_nt(g, w, name):
    m, n = g.shape
    k, _ = w.shape
    tm = _pick(m, (512, 256, 128, 64, 32, 16))
    tn = _pick(n, (1024, 512, 256, 128))

    def body(g_ref, w_ref, o_ref):
        part = _dot_nt(g_ref[...].astype(BF16), w_ref[...])

        @pl.when(pl.program_id(1) == 0)
        def _():
            o_ref[...] = part

        @pl.when(pl.program_id(1) > 0)
        def _():
            o_ref[...] += part

    return pl.pallas_call(
        body, grid=(m // tm, n // tn),
        in_specs=[pl.BlockSpec((tm, tn), lambda i, j: (i, j)), pl.BlockSpec((k, tn), lambda i, j: (0, j))],
        out_specs=pl.BlockSpec((tm, k), lambda i, j: (i, 0)),
        out_shape=jax.ShapeDtypeStruct((m, k), F32),
        compiler_params=pltpu.CompilerParams(dimension_semantics=("parallel", "arbitrary"),
                                             vmem_limit_bytes=VMEM_LIMIT),
        name=name)(g, w)


def _mm_tn(a, g, name):
    m, k = a.shape
    _, n = g.shape
    tm = _pick(m, (512, 256, 128, 64, 32, 16))
    tn = _pick(n, (1024, 512, 256, 128))

    def body(a_ref, g_ref, o_ref):
        part = _dot_tn(a_ref[...], g_ref[...].astype(BF16))

        @pl.when(pl.program_id(1) == 0)
        def _():
            o_ref[...] = part

        @pl.when(pl.program_id(1) > 0)
        def _():
            o_ref[...] += part

    return pl.pallas_call(
        body, grid=(n // tn, m // tm),
        in_specs=[pl.BlockSpec((tm, k), lambda j, i: (i, 0)), pl.BlockSpec((tm, tn), lambda j, i: (i, j))],
        out_specs=pl.BlockSpec((k, tn), lambda j, i: (0, j)),
        out_shape=jax.ShapeDtypeStruct((k, n), F32),
        compiler_params=pltpu.CompilerParams(dimension_semantics=("parallel", "arbitrary"),
                                             vmem_limit_bytes=VMEM_LIMIT),
        name=name)(a, g)


SLABS = (("qk", O_QK, 2 * QK_A), ("va", O_VA, V_A), ("kv", O_KB, 2 * KV_B), ("oa", O_OA, V_A), ("za", O_ZA, V_A),
         ("qb", O_QB, Q_B), ("zb", O_ZB, Q_B), ("ga", O_GA, D_MODEL), ("gb", O_GB, D_MODEL))
SLAB_FWD_TN = 1024
SLAB_TN = 512


def _slab_blocks():
    return [(off // SLAB_TN, (off + width) // SLAB_TN) for _, off, width in SLABS]


def _proj_fwd_slab(a, w, off, width, name):
    m, k = a.shape
    tm = _pick(m, (1088, 512, 256, 128))
    tn = min(SLAB_FWD_TN, width)

    def body(a_ref, b_ref, o_ref):
        o_ref[...] = _dot(a_ref[...], b_ref[...])

    return pl.pallas_call(
        body, grid=(m // tm, width // tn),
        in_specs=[pl.BlockSpec((tm, k), lambda i, j: (i, 0)), pl.BlockSpec((k, tn), lambda i, j: (0, j + off // tn))],
        out_specs=pl.BlockSpec((tm, tn), lambda i, j: (i, j)),
        out_shape=jax.ShapeDtypeStruct((m, width), F32),
        compiler_params=pltpu.CompilerParams(dimension_semantics=("parallel", "parallel"),
                                             vmem_limit_bytes=VMEM_LIMIT),
        name=name)(a, w)


def _slab_spec(tm, blocks, rows_inner):
    b, e = blocks

    def index(r, c):
        inside = (c >= b) & (c < e)
        return jnp.where(inside, r, 0), jnp.clip(c - b, 0, e - b - 1)

    if rows_inner:
        return pl.BlockSpec((tm, SLAB_TN), lambda c, r: index(r, c))
    return pl.BlockSpec((tm, SLAB_TN), lambda r, c: index(r, c))


def _proj_da(gs, w, name):
    m = gs[0].shape[0]
    k, n = w.shape
    tm = _pick(m, (512, 256, 128))
    blocks = _slab_blocks()

    def body(*refs):
        g_refs, w_ref, o_ref = refs[:len(blocks)], refs[len(blocks)], refs[len(blocks) + 1]
        c = pl.program_id(1)

        @pl.when(c == 0)
        def _():
            o_ref[...] = jnp.zeros(o_ref.shape, F32)

        for g_ref, (b, e) in zip(g_refs, blocks):
            @pl.when((c >= b) & (c < e))
            def _(g_ref=g_ref):
                o_ref[...] += _dot_nt(g_ref[...].astype(BF16), w_ref[...])

    return pl.pallas_call(
        body, grid=(m // tm, n // SLAB_TN),
        in_specs=[_slab_spec(tm, blk, False) for blk in blocks] + [pl.BlockSpec((k, SLAB_TN), lambda r, c: (0, c))],
        out_specs=pl.BlockSpec((tm, k), lambda r, c: (r, 0)),
        out_shape=jax.ShapeDtypeStruct((m, k), F32),
        compiler_params=pltpu.CompilerParams(dimension_semantics=("parallel", "arbitrary"),
                                             vmem_limit_bytes=VMEM_LIMIT),
        name=name)(*gs, w)


def _proj_dw(a, gs, n, name):
    m, k = a.shape
    tm = _pick(m, (512, 256, 128))
    blocks = _slab_blocks()

    def body(*refs):
        a_ref, g_refs, o_ref = refs[0], refs[1:1 + len(blocks)], refs[1 + len(blocks)]
        c, r = pl.program_id(0), pl.program_id(1)

        @pl.when(r == 0)
        def _():
            o_ref[...] = jnp.zeros(o_ref.shape, F32)

        for g_ref, (b, e) in zip(g_refs, blocks):
            @pl.when((c >= b) & (c < e))
            def _(g_ref=g_ref):
                o_ref[...] += _dot_tn(a_ref[...], g_ref[...].astype(BF16))

    return pl.pallas_call(
        body, grid=(n // SLAB_TN, m // tm),
        in_specs=[pl.BlockSpec((tm, k), lambda c, r: (r, 0))] + [_slab_spec(tm, blk, True) for blk in blocks],
        out_specs=pl.BlockSpec((k, SLAB_TN), lambda c, r: (0, c)),
        out_shape=jax.ShapeDtypeStruct((k, n), F32),
        compiler_params=pltpu.CompilerParams(dimension_semantics=("parallel", "arbitrary"),
                                             vmem_limit_bytes=VMEM_LIMIT),
        name=name)(a, *gs)


LN_ROWS = 256


def _ln_stats(x):
    mu = jnp.mean(x, axis=-1, keepdims=True)
    xc = x - mu
    rstd = lax.rsqrt(jnp.mean(xc * xc, axis=-1, keepdims=True) + EPS)
    return xc * rstd, rstd


def _ln_bwd(dxh, xh, rstd):
    return rstd * (dxh - jnp.mean(dxh, axis=-1, keepdims=True) - xh * jnp.mean(dxh * xh, axis=-1, keepdims=True))


def _seg_maps(t, tc):
    cb, nb = tc // LN_ROWS, t // LN_ROWS
    ctx_blk = lambda i: jnp.where(i < cb, i, jnp.clip(i - cb - nb, 0, cb - 1))
    lat_blk = lambda i: jnp.clip(i - cb, 0, nb - 1)
    return cb, nb, ctx_blk, lat_blk


def _ln_mod_fwd(x, ctx, mod):
    t, tc = x.shape[0], ctx.shape[0]
    cb, nb, ctx_blk, lat_blk = _seg_maps(t, tc)

    def body(x_ref, c_ref, m_ref, o_ref):
        i = pl.program_id(0)
        lat = (i >= cb) & (i < cb + nb)

        @pl.when(lat)
        def _():
            xh, _ = _ln_stats(x_ref[...])
            o_ref[...] = (xh * (1 + m_ref[0:1, :]) + m_ref[1:2, :]).astype(BF16)

        @pl.when(jnp.logical_not(lat))
        def _():
            xh, _ = _ln_stats(c_ref[...])
            o_ref[...] = (xh * (1 + m_ref[2:3, :]) + m_ref[3:4, :]).astype(BF16)

    blk = lambda f: pl.BlockSpec((LN_ROWS, D_MODEL), lambda i: (f(i), 0))
    return pl.pallas_call(
        body, grid=(2 * cb + nb,),
        in_specs=[blk(lat_blk), blk(ctx_blk), pl.BlockSpec((4, D_MODEL), lambda i: (0, 0))],
        out_specs=pl.BlockSpec((LN_ROWS, D_MODEL), lambda i: (i, 0)),
        out_shape=jax.ShapeDtypeStruct((t + 2 * tc, D_MODEL), BF16),
        compiler_params=pltpu.CompilerParams(dimension_semantics=("parallel",), vmem_limit_bytes=VMEM_LIMIT),
        name="ln_mod")(x, ctx, mod)


def _ln_mod_bwd(du_a, du_b, x, ctx, mod):
    t, tc = x.shape[0], ctx.shape[0]
    cb, nb, ctx_blk, lat_blk = _seg_maps(t, tc)

    def body(da_ref, db_ref, x_ref, c_ref, m_ref, dx_ref, dm_ref):
        i = pl.program_id(0)
        lat = (i >= cb) & (i < cb + nb)

        @pl.when(i == 0)
        def _():
            dm_ref[...] = jnp.zeros(dm_ref.shape, F32)

        du = da_ref[...] + db_ref[...]

        @pl.when(lat)
        def _():
            xh, rstd = _ln_stats(x_ref[...])
            dm_ref[0:1, :] += jnp.sum(du * xh, axis=0, keepdims=True)
            dm_ref[1:2, :] += jnp.sum(du, axis=0, keepdims=True)
            dx_ref[...] = _ln_bwd(du * (1 + m_ref[0:1, :]), xh, rstd)

        @pl.when(jnp.logical_not(lat))
        def _():
            xh, _ = _ln_stats(c_ref[...])
            dm_ref[2:3, :] += jnp.sum(du * xh, axis=0, keepdims=True)
            dm_ref[3:4, :] += jnp.sum(du, axis=0, keepdims=True)

    blk = lambda f: pl.BlockSpec((LN_ROWS, D_MODEL), lambda i: (f(i), 0))
    row = pl.BlockSpec((LN_ROWS, D_MODEL), lambda i: (i, 0))
    vec = pl.BlockSpec((4, D_MODEL), lambda i: (0, 0))
    return pl.pallas_call(
        body, grid=(2 * cb + nb,),
        in_specs=[row, row, blk(lat_blk), blk(ctx_blk), vec],
        out_specs=[blk(lat_blk), vec],
        out_shape=[jax.ShapeDtypeStruct((t, D_MODEL), F32), jax.ShapeDtypeStruct((4, D_MODEL), F32)],
        compiler_params=pltpu.CompilerParams(dimension_semantics=("arbitrary",), vmem_limit_bytes=VMEM_LIMIT),
        name="ln_mod_bwd")(du_a, du_b, x, ctx, mod)


@jax.custom_vjp
def _ln_project(x, ctx, mod, w_main, w_if, pr_main, pr_if):
    return _ln_project_fwd(x, ctx, mod, w_main, w_if, pr_main, pr_if)[0]


def _ln_project_fwd(x, ctx, mod, w_main, w_if, pr_main, pr_if):
    del pr_main, pr_if
    ub = _ln_mod_fwd(x, ctx, mod)
    slabs = tuple(_proj_fwd_slab(ub, w_main, off, width, "proj_" + nm) for nm, off, width in SLABS)
    return (slabs, _mm_nn(ub, w_if, "proj_if")), (x, ctx, mod, ub, w_main, w_if)


def _ln_project_bwd(res, cot):
    x, ctx, mod, ub, w_main, w_if = res
    gs, g_if = cot
    dx, dmod = _ln_mod_bwd(_proj_da(gs, w_main, "proj_da"), _mm_nt(g_if, w_if, "proj_if_da"), x, ctx, mod)
    return (dx, jnp.zeros_like(ctx), dmod, jnp.zeros_like(w_main), jnp.zeros_like(w_if),
            _proj_dw(ub, gs, w_main.shape[1], "proj_dw"), _mm_tn(ub, g_if, "proj_if_dw"))


_ln_project.defvjp(_ln_project_fwd, _ln_project_bwd)


def _head_fwd(x, out, target, vecs):
    t = x.shape[0]

    def body(x_ref, o_ref, t_ref, v_ref, l_ref):
        @pl.when(pl.program_id(0) == 0)
        def _():
            l_ref[...] = jnp.zeros(l_ref.shape, F32)

        rh, _ = _ln_stats(ALPHA * x_ref[...] + v_ref[0:1, :] * o_ref[...])
        err = rh * v_ref[1:2, :] + v_ref[2:3, :] - t_ref[...]
        part = jnp.sum(jnp.mean(err * err, axis=-1, keepdims=True), axis=0, keepdims=True)
        l_ref[...] += 0.5 * part

    row = pl.BlockSpec((LN_ROWS, D_MODEL), lambda i: (i, 0))
    return pl.pallas_call(
        body, grid=(t // LN_ROWS,),
        in_specs=[row, row, row, pl.BlockSpec((3, D_MODEL), lambda i: (0, 0))],
        out_specs=pl.BlockSpec((1, 128), lambda i: (0, 0)),
        out_shape=jax.ShapeDtypeStruct((1, 128), F32),
        compiler_params=pltpu.CompilerParams(dimension_semantics=("arbitrary",), vmem_limit_bytes=VMEM_LIMIT),
        name="loss_head")(x, out, target, vecs)


def _head_bwd(g, x, out, target, vecs):
    t = x.shape[0]

    def body(g_ref, x_ref, o_ref, t_ref, v_ref, dx_ref, do_ref, dv_ref):
        @pl.when(pl.program_id(0) == 0)
        def _():
            dv_ref[...] = jnp.zeros(dv_ref.shape, F32)

        o = o_ref[...]
        gate, ln_w = v_ref[0:1, :], v_ref[1:2, :]
        rh, rstd = _ln_stats(ALPHA * x_ref[...] + gate * o)
        dy = (rh * ln_w + v_ref[2:3, :] - t_ref[...]) * (g_ref[0:1, 0:1] * (1.0 / D_MODEL))
        dv_ref[1:2, :] += jnp.sum(dy * rh, axis=0, keepdims=True)
        dv_ref[2:3, :] += jnp.sum(dy, axis=0, keepdims=True)
        dr = _ln_bwd(dy * ln_w, rh, rstd)
        dv_ref[0:1, :] += jnp.sum(dr * o, axis=0, keepdims=True)
        dx_ref[...] = ALPHA * dr
        do_ref[...] = gate * dr

    row = pl.BlockSpec((LN_ROWS, D_MODEL), lambda i: (i, 0))
    vec = pl.BlockSpec((3, D_MODEL), lambda i: (0, 0))
    return pl.pallas_call(
        body, grid=(t // LN_ROWS,),
        in_specs=[pl.BlockSpec((1, 128), lambda i: (0, 0)), row, row, row, vec],
        out_specs=[row, row, vec],
        out_shape=[jax.ShapeDtypeStruct((t, D_MODEL), F32), jax.ShapeDtypeStruct((t, D_MODEL), F32),
                   jax.ShapeDtypeStruct((3, D_MODEL), F32)],
        compiler_params=pltpu.CompilerParams(dimension_semantics=("arbitrary",), vmem_limit_bytes=VMEM_LIMIT),
        name="loss_head_bwd")(g, x, out, target, vecs)


@jax.custom_vjp
def _loss_head(x, out, target, gate, ln_w, ln_b):
    return _head_fwd(x, out, target, jnp.stack([gate, ln_w, ln_b]))[0, 0]


def _loss_head_fwd(x, out, target, gate, ln_w, ln_b):
    vecs = jnp.stack([gate, ln_w, ln_b])
    return _head_fwd(x, out, target, vecs)[0, 0], (x, out, target, vecs)


def _loss_head_bwd(res, g):
    x, out, target, vecs = res
    dx, dout, dv = _head_bwd(jnp.full((1, 128), g, F32), x, out, target, vecs)
    return dx, dout, jnp.zeros_like(target), dv[0], dv[1], dv[2]


_loss_head.defvjp(_loss_head_fwd, _loss_head_bwd)


ATT_SCALE = HD_B ** -0.5
GROUP = NH_B // NKV_B


LOG2E, LN2 = 1.4426950408889634, 0.6931471805599453
ATT_C = ATT_SCALE * LOG2E
STRIP_Q, STRIP_K = 128, 256


def _attn_tiles(t, n):
    return _pick(t, (512, 256, 128)), _pick(n, (768, 512, 256))


def _attn_fwd(q, k, v):
    t, n = q.shape[0], k.shape[0]
    tq, tk = _pick(t, (1024, 512, 256, 128)), _attn_tiles(t, n)[1]
    nk = n // tk

    def body(q_ref, k_ref, v_ref, o_ref, lse_ref, m_sc, acc_sc):
        j = pl.program_id(2)

        @pl.when(j == 0)
        def _():
            m_sc[...] = jnp.full(m_sc.shape, -jnp.inf, F32)
            acc_sc[...] = jnp.zeros(acc_sc.shape, F32)

        kb = k_ref[...]
        v_ones = jnp.concatenate([v_ref[...], jnp.ones((tk, HD_B), BF16)], axis=1)
        for g in range(GROUP):
            s2 = _dot_nt(q_ref[:, g * HD_B:(g + 1) * HD_B], kb) * ATT_C
            m_prev = m_sc[g]
            m_new = jnp.maximum(m_prev, jnp.max(s2, axis=-1, keepdims=True))
            p = jnp.exp2(s2 - m_new).astype(BF16)
            acc_sc[g] = jnp.exp2(m_prev - m_new) * acc_sc[g] + _dot(p, v_ones)
            m_sc[g] = m_new

        @pl.when(j == nk - 1)
        def _():
            for g in range(GROUP):
                cols = slice(g * HD_B, (g + 1) * HD_B)
                l = acc_sc[g, :, HD_B:]
                o_ref[:, cols] = acc_sc[g, :, :HD_B] / l
                lse_ref[:, cols] = m_sc[g] + jnp.log(l) * LOG2E

    qspec = pl.BlockSpec((tq, GROUP * HD_B), lambda kh, i, j: (i, kh))
    kspec = pl.BlockSpec((tk, HD_B), lambda kh, i, j: (j, kh))
    return pl.pallas_call(
        body, grid=(NKV_B, t // tq, nk),
        in_specs=[qspec, kspec, kspec], out_specs=[qspec, qspec],
        out_shape=[jax.ShapeDtypeStruct((t, Q_B), F32), jax.ShapeDtypeStruct((t, Q_B), F32)],
        scratch_shapes=[pltpu.VMEM((GROUP, tq, 1), F32), pltpu.VMEM((GROUP, tq, 2 * HD_B), F32)],
        compiler_params=pltpu.CompilerParams(dimension_semantics=("parallel", "parallel", "arbitrary"),
                                             vmem_limit_bytes=VMEM_LIMIT),
        name="attn_fwd")(q, k, v)


def _attn_dq(q, k, v, do, lse, delta):
    t, n = q.shape[0], k.shape[0]
    tq, tk = _pick(t, (1024, 512, 256, 128)), _attn_tiles(t, n)[1]
    nk = n // tk

    def body(q_ref, k_ref, v_ref, do_ref, lse_ref, dl_ref, dq_ref):
        j = pl.program_id(2)
        kb, vb = k_ref[...], v_ref[...]
        parts = []
        for g in range(GROUP):
            cols = slice(g * HD_B, (g + 1) * HD_B)
            p = jnp.exp2(_dot_nt(q_ref[:, cols], kb) * ATT_C - lse_ref[:, g * HD_B:g * HD_B + 1])
            dp = _dot_nt(do_ref[:, cols], vb)
            ds = p * (dp - dl_ref[:, g * HD_B:g * HD_B + 1])
            parts.append(_dot(ds.astype(BF16), kb))

        @pl.when(j == 0)
        def _():
            for g in range(GROUP):
                dq_ref[:, g * HD_B:(g + 1) * HD_B] = parts[g]

        @pl.when(j > 0)
        def _():
            for g in range(GROUP):
                dq_ref[:, g * HD_B:(g + 1) * HD_B] += parts[g]

        @pl.when(j == nk - 1)
        def _():
            dq_ref[...] = dq_ref[...] * ATT_SCALE

    qspec = pl.BlockSpec((tq, GROUP * HD_B), lambda kh, i, j: (i, kh))
    kspec = pl.BlockSpec((tk, HD_B), lambda kh, i, j: (j, kh))
    return pl.pallas_call(
        body, grid=(NKV_B, t // tq, n // tk),
        in_specs=[qspec, kspec, kspec, qspec, qspec, qspec],
        out_specs=qspec,
        out_shape=jax.ShapeDtypeStruct((t, Q_B), F32),
        compiler_params=pltpu.CompilerParams(dimension_semantics=("parallel", "parallel", "arbitrary"),
                                             vmem_limit_bytes=VMEM_LIMIT),
        name="attn_dq")(q, k, v, do, lse, delta)


def _attn_dkv(q, k, v, do, lse_t, delta_t):
    t, n = q.shape[0], k.shape[0]
    tq, tk = _attn_tiles(t, n)
    nq = t // tq
    n_r, n_c = tq // STRIP_Q, tk // STRIP_K

    def body(q_ref, k_ref, v_ref, do_ref, lse_ref, dl_ref, dk_ref, dv_ref, dk_sc, dv_sc):
        i = pl.program_id(2)

        @pl.when(i == 0)
        def _():
            dk_sc[...] = jnp.zeros(dk_sc.shape, F32)
            dv_sc[...] = jnp.zeros(dv_sc.shape, F32)

        for r in range(n_r):
            rows = slice(r * STRIP_Q, (r + 1) * STRIP_Q)
            for c in range(n_c):
                kv = slice(c * STRIP_K, (c + 1) * STRIP_K)
                kc, vc = k_ref[kv, :], v_ref[kv, :]
                dk_part = dv_part = None
                for g in range(GROUP):
                    cols = slice(g * HD_B, (g + 1) * HD_B)
                    qg, dog = q_ref[rows, cols], do_ref[rows, cols]
                    st = _dot_nt(kc, qg)
                    pt = jnp.exp2(st * ATT_C - lse_ref[8 * g:8 * g + 1, rows])
                    dvg = _dot(pt.astype(BF16), dog)
                    dpt = _dot_nt(vc, dog)
                    dst = pt * (dpt - dl_ref[8 * g:8 * g + 1, rows])
                    dkg = _dot(dst.astype(BF16), qg)
                    dk_part = dkg if dk_part is None else dk_part + dkg
                    dv_part = dvg if dv_part is None else dv_part + dvg
                dk_sc[kv, :] += dk_part
                dv_sc[kv, :] += dv_part

        @pl.when(i == nq - 1)
        def _():
            dk_ref[...] = dk_sc[...] * ATT_SCALE
            dv_ref[...] = dv_sc[...]

    qspec = pl.BlockSpec((tq, GROUP * HD_B), lambda kh, j, i: (i, kh))
    tspec = pl.BlockSpec((8 * GROUP, tq), lambda kh, j, i: (kh, i))
    kspec = pl.BlockSpec((tk, HD_B), lambda kh, j, i: (j, kh))
    return pl.pallas_call(
        body, grid=(NKV_B, n // tk, nq),
        in_specs=[qspec, kspec, kspec, qspec, tspec, tspec],
        out_specs=[kspec, kspec],
        out_shape=[jax.ShapeDtypeStruct((n, KV_B), F32), jax.ShapeDtypeStruct((n, KV_B), F32)],
        scratch_shapes=[pltpu.VMEM((tk, HD_B), F32), pltpu.VMEM((tk, HD_B), F32)],
        compiler_params=pltpu.CompilerParams(dimension_semantics=("parallel", "parallel", "arbitrary"),
                                             vmem_limit_bytes=VMEM_LIMIT),
        name="attn_dkv")(q, k, v, do, lse_t, delta_t)


def _attention_bwd(res, do):
    qb, kb, vb, o, lse = res
    t = qb.shape[0]
    delta = jnp.sum((do * o).reshape(t, NH_B, HD_B), axis=-1)
    lse_h = lse.reshape(t, NH_B, HD_B)[:, :, 0]
    delta_b = jnp.broadcast_to(delta[:, :, None], (t, NH_B, HD_B)).reshape(t, Q_B)
    lse_t = jnp.broadcast_to(lse_h.T[:, None, :], (NH_B, 8, t)).reshape(NH_B * 8, t)
    delta_t = jnp.broadcast_to(delta.T[:, None, :], (NH_B, 8, t)).reshape(NH_B * 8, t)
    dob = do.astype(BF16)
    dq = _attn_dq(qb, kb, vb, dob, lse, delta_b)
    dk, dv = _attn_dkv(qb, kb, vb, dob, lse_t, delta_t)
    return dq, dk, dv


def _swap32(y):
    lane = lax.broadcasted_iota(jnp.int32, y.shape, 1)
    return jnp.where((lane // 32) % 2 == 0, pltpu.roll(y, 96, 1), pltpu.roll(y, 32, 1))


ROPE_ROWS = 256


def _norm_rope_fwd(x, w, cos, sin, row_off, rows, heads, name):
    tr, off, width = ROPE_ROWS, row_off // ROPE_ROWS, heads * HD_B

    def body(x_ref, w_ref, c_ref, s_ref, o_ref):
        w, c, s = w_ref[...], c_ref[...], s_ref[...]
        for h in range(heads):
            cols = slice(h * HD_B, (h + 1) * HD_B)
            xh = x_ref[:, cols]
            y = xh * lax.rsqrt(jnp.mean(xh * xh, axis=-1, keepdims=True) + EPS) * w
            o_ref[:, cols] = (y * c + _swap32(y) * s).astype(o_ref.dtype)

    row = pl.BlockSpec((tr, width), lambda i: (i, 0))
    tab = pl.BlockSpec((tr, HD_B), lambda i: (i, 0))
    return pl.pallas_call(
        body, grid=(rows // tr,),
        in_specs=[pl.BlockSpec((tr, width), lambda i: (i + off, 0)), pl.BlockSpec((1, HD_B), lambda i: (0, 0)), tab, tab],
        out_specs=row, out_shape=jax.ShapeDtypeStruct((rows, width), BF16),
        compiler_params=pltpu.CompilerParams(dimension_semantics=("parallel",), vmem_limit_bytes=VMEM_LIMIT),
        name=name)(x, w, cos, sin)


def _norm_rope_bwd(x, w, cos, sin, dy, row_off, extra, name):
    rows, width = dy.shape
    heads = width // HD_B
    r, full = x.shape
    tr, off, nb = ROPE_ROWS, row_off // ROPE_ROWS, rows // ROPE_ROWS

    def body(*refs):
        x_ref, w_ref, c_ref, s_ref, dy_ref = refs[:5]
        e_ref = refs[5] if extra is not None else None
        dx_ref, dw_ref = refs[-2], refs[-1]
        i = pl.program_id(0)

        @pl.when(i == 0)
        def _():
            dw_ref[...] = jnp.zeros(dw_ref.shape, F32)

        @pl.when((i >= off) & (i < off + nb))
        def _():
            w, c, s = w_ref[...], c_ref[...], s_ref[...]
            dw = jnp.zeros((1, HD_B), F32)
            for h in range(heads):
                cols = slice(h * HD_B, (h + 1) * HD_B)
                xh, dyh = x_ref[:, cols], dy_ref[:, cols]
                rs = lax.rsqrt(jnp.mean(xh * xh, axis=-1, keepdims=True) + EPS)
                dn = dyh * c + _swap32(dyh * s)
                dw = dw + jnp.sum(dn * (xh * rs), axis=0, keepdims=True)
                dxn = dn * w
                dx_ref[:, cols] = rs * dxn - xh * (rs * rs * rs * jnp.mean(dxn * xh, axis=-1, keepdims=True))
            if e_ref is not None:
                dx_ref[:, width:] = e_ref[...]
            dw_ref[...] += dw

        @pl.when((i < off) | (i >= off + nb))
        def _():
            dx_ref[...] = jnp.zeros(dx_ref.shape, F32)

    inner = lambda i: jnp.clip(i - off, 0, nb - 1)
    tab = pl.BlockSpec((tr, HD_B), lambda i: (inner(i), 0))
    vec = pl.BlockSpec((1, HD_B), lambda i: (0, 0))
    in_specs = [pl.BlockSpec((tr, width), lambda i: (i, 0)), vec, tab, tab,
                pl.BlockSpec((tr, width), lambda i: (inner(i), 0))]
    args = [x, w, cos, sin, dy]
    if extra is not None:
        in_specs.append(pl.BlockSpec((tr, full - width), lambda i: (inner(i), 0)))
        args.append(extra)
    return pl.pallas_call(
        body, grid=(r // tr,), in_specs=in_specs,
        out_specs=[pl.BlockSpec((tr, full), lambda i: (i, 0)), vec],
        out_shape=[jax.ShapeDtypeStruct((r, full), F32), jax.ShapeDtypeStruct((1, HD_B), F32)],
        compiler_params=pltpu.CompilerParams(dimension_semantics=("arbitrary",), vmem_limit_bytes=VMEM_LIMIT),
        name=name)(*args)


def _rope_tables(t):
    pos = jnp.arange(t)
    row = (pos // GRID_W).astype(F32)
    col = (pos % GRID_W).astype(F32)
    inv = ROPE_THETA ** (-jnp.arange(0, ROT_HALF, 2, dtype=F32) / ROT_HALF)
    ar, ac = row[:, None] * inv[None], col[:, None] * inv[None]
    cos = jnp.concatenate([jnp.cos(ar), jnp.cos(ar), jnp.cos(ac), jnp.cos(ac)], -1)
    sin = jnp.concatenate([-jnp.sin(ar), jnp.sin(ar), -jnp.sin(ac), jnp.sin(ac)], -1)
    return cos, sin


def _gqa_tables(t, n):
    cos, sin = _rope_tables(t)
    cos_k = jnp.concatenate([jnp.ones((n - t, HD_B), F32), cos], 0)
    sin_k = jnp.concatenate([jnp.zeros((n - t, HD_B), F32), sin], 0)
    return cos, sin, cos_k, sin_k


def _make_gqa(t, tc):
    n = tc + t

    @jax.custom_vjp
    def gqa(p_qb, p_kv, qw, kw):
        return fwd(p_qb, p_kv, qw, kw)[0]

    def fwd(p_qb, p_kv, qw, kw):
        cos, sin, cos_k, sin_k = _gqa_tables(t, n)
        q = _norm_rope_fwd(p_qb, qw[None], cos, sin, tc, t, NH_B, "q_norm_rope")
        k = _norm_rope_fwd(p_kv, kw[None], cos_k, sin_k, 0, n, NKV_B, "k_norm_rope")
        vb = p_kv[:n, KV_B:].astype(BF16)
        o, lse = _attn_fwd(q, k, vb)
        return o, (p_qb, p_kv, qw, kw, q, k, vb, o, lse)

    def bwd(res, do):
        p_qb, p_kv, qw, kw, q, k, vb, o, lse = res
        cos, sin, cos_k, sin_k = _gqa_tables(t, n)
        dq, dk, dv = _attention_bwd((q, k, vb, o, lse), do)
        d_qb, dqw = _norm_rope_bwd(p_qb, qw[None], cos, sin, dq, tc, None, "q_norm_rope_bwd")
        d_kv, dkw = _norm_rope_bwd(p_kv, kw[None], cos_k, sin_k, dk, 0, dv, "k_norm_rope_bwd")
        return d_qb, d_kv, dqw[0], dkw[0]

    gqa.defvjp(fwd, bwd)
    return gqa


def _mlstm_chunk_forward(q, k, v, lir, f_pre, s0, n0, m0, reverse):
    L = q.shape[0]
    lfr = jnp.minimum(f_pre, 0.0) - jnp.log1p(jnp.exp(-jnp.abs(f_pre)))
    ti = lax.broadcasted_iota(jnp.int32, (L, L), 0)
    si = lax.broadcasted_iota(jnp.int32, (L, L), 1)
    seen = (si >= ti) if reverse else (si <= ti)
    seen_t = (ti >= si) if reverse else (ti <= si)
    eye = ti == si
    lic = jnp.sum(jnp.where(eye, lir, 0.0), axis=1, keepdims=True)
    lfc = jnp.sum(jnp.where(eye, lfr, 0.0), axis=1, keepdims=True)
    b_col = jnp.sum(jnp.where(seen, lfr, 0.0), axis=1, keepdims=True)
    b_row = jnp.sum(jnp.where(seen_t, lfc, 0.0), axis=0, keepdims=True)
    d = jnp.where(seen, b_col - b_row + lir, -jnp.inf)
    m = jnp.maximum(b_col + m0, jnp.max(d, axis=1, keepdims=True))
    w = jnp.exp(d - m)
    a = jnp.exp(b_col + m0 - m)
    qm, km, vm = q.astype(MX), k.astype(MX), v.astype(MX)
    s = _dot_nt(qm, km) * w
    qs = _dot(qm, s0.astype(MX))
    num = a * qs + _dot(s.astype(MX), vm)
    qn = jnp.sum(q * n0, axis=1, keepdims=True)
    den = a * qn + jnp.sum(s, axis=1, keepdims=True)
    floor = jnp.exp(-m)
    dd = jnp.maximum(jnp.abs(den), floor)
    b_last = jnp.sum(lfr, axis=1, keepdims=True)
    m_end = jnp.maximum(b_last + m0, jnp.max(b_last - b_row + lir, axis=1, keepdims=True))
    w_end = jnp.exp(b_last - b_col + lic - m_end)
    a_end = jnp.exp(b_last + m0 - m_end)
    return dict(eye=eye, seen=seen, w=w, a=a, s=s, qs=qs, num=num, qn=qn, den=den, floor=floor, dd=dd,
                m_end=m_end, w_end=w_end, a_end=a_end, qm=qm, km=km, vm=vm)


def _mlstm_fwd_call(q, k, v, gr, n, row_off, reverse):
    L = MLSTM_CHUNK
    nc, off = n // L, row_off // L
    pos = (lambda i: nc - 1 - i) if reverse else (lambda i: i)

    def body(q_ref, k_ref, v_ref, gr_ref, h_ref, s0_ref, n0_ref, m0_ref, s_sc, n_sc, m_sc):
        @pl.when(pl.program_id(1) == 0)
        def _():
            s_sc[...] = jnp.zeros(s_sc.shape, F32)
            n_sc[...] = jnp.zeros(n_sc.shape, F32)
            m_sc[...] = jnp.full(m_sc.shape, M_INIT, F32)

        s0, n0, m0 = s_sc[...], n_sc[...], m_sc[...]
        s0_ref[0, 0] = s0
        n0_ref[0, 0] = n0
        m0_ref[0, 0] = jnp.broadcast_to(m0, (1, DK_A))
        k, v = k_ref[...], v_ref[...]
        f = _mlstm_chunk_forward(q_ref[...], k, v, gr_ref[0, 0], gr_ref[1, 0], s0, n0, m0, reverse)
        h_ref[...] = f["num"] / f["dd"]
        s_sc[...] = f["a_end"] * s0 + _dot_tn(f["km"], (f["w_end"] * v).astype(MX))
        n_sc[...] = f["a_end"] * n0 + jnp.sum(f["w_end"] * k, axis=0, keepdims=True)
        m_sc[...] = f["m_end"]

    qk_spec = pl.BlockSpec((L, DK_A), lambda h, i: (off + pos(i), h))
    v_spec = pl.BlockSpec((L, DV_A), lambda h, i: (off + pos(i), h))
    gr_spec = pl.BlockSpec((2, 1, 1, L), lambda h, i: (0, h, 0, off + pos(i)))
    h_spec = pl.BlockSpec((L, DV_A), lambda h, i: (pos(i), h))
    st_spec = pl.BlockSpec((1, 1, DK_A, DV_A), lambda h, i: (h, pos(i), 0, 0))
    vec_spec = pl.BlockSpec((1, 1, 1, DK_A), lambda h, i: (h, pos(i), 0, 0))
    return pl.pallas_call(
        body, grid=(NH_A, nc),
        in_specs=[qk_spec, qk_spec, v_spec, gr_spec],
        out_specs=[h_spec, st_spec, vec_spec, vec_spec],
        out_shape=[jax.ShapeDtypeStruct((n, V_A), F32), jax.ShapeDtypeStruct((NH_A, nc, DK_A, DV_A), F32),
                   jax.ShapeDtypeStruct((NH_A, nc, 1, DK_A), F32), jax.ShapeDtypeStruct((NH_A, nc, 1, DK_A), F32)],
        scratch_shapes=[pltpu.VMEM((DK_A, DV_A), F32), pltpu.VMEM((1, DK_A), F32), pltpu.VMEM((1, 1), F32)],
        compiler_params=pltpu.CompilerParams(dimension_semantics=("parallel", "arbitrary"),
                                             vmem_limit_bytes=VMEM_LIMIT),
        name="mlstm_fwd")(q, k, v, gr)


def _mlstm_bwd_call(q, k, v, gr, s0_all, n0_all, m0_all, dh, n, row_off, reverse):
    L = MLSTM_CHUNK
    nc, off = n // L, row_off // L
    pos = (lambda i: i) if reverse else (lambda i: nc - 1 - i)

    def body(q_ref, k_ref, v_ref, gr_ref, s0_ref, n0_ref, m0_ref, dh_ref,
             dq_ref, dk_ref, dv_ref, dg_ref, ds_sc, dn_sc):
        @pl.when(pl.program_id(1) == 0)
        def _():
            ds_sc[...] = jnp.zeros(ds_sc.shape, F32)
            dn_sc[...] = jnp.zeros(dn_sc.shape, F32)

        q, k, v = q_ref[...], k_ref[...], v_ref[...]
        s0, n0, m0 = s0_ref[0, 0], n0_ref[0, 0], m0_ref[0, 0][:, 0:1]
        f = _mlstm_chunk_forward(q, k, v, gr_ref[0, 0], gr_ref[1, 0], s0, n0, m0, reverse)
        w, a, s = f["w"], f["a"], f["s"]
        qm, km, vm, w_end, a_end = f["qm"], f["km"], f["vm"], f["w_end"], f["a_end"]
        ds1, dn1 = ds_sc[...], dn_sc[...]
        ds1m, s0m = ds1.astype(MX), s0.astype(MX)

        inv = 1.0 / f["dd"]
        dh = dh_ref[...]
        dnum = dh * inv
        ddd = -jnp.sum(dh * (f["num"] * inv), axis=1, keepdims=True) * inv
        dden = jnp.where(jnp.abs(f["den"]) > f["floor"], jnp.sign(f["den"]) * ddd, 0.0)
        adn = (a * dnum).astype(MX)
        dnm = dnum.astype(MX)
        ds_tot = _dot_nt(dnm, vm) + dden
        dsr = (ds_tot * w).astype(MX)
        e = ds_tot * s
        wv = (w_end * v).astype(MX)
        kds = _dot(km, ds1m)
        dq_ref[...] = _dot_nt(adn, s0m) + _dot(dsr, km) + (dden * a) * n0
        dk_ref[...] = _dot_tn(dsr, qm) + _dot_nt(wv, ds1m) + w_end * dn1
        dv_ref[...] = _dot_tn(s.astype(MX), dnm) + w_end * kds

        eye = f["eye"]
        to_col = lambda r: jnp.sum(jnp.where(eye, r, 0.0), axis=1, keepdims=True)
        to_row = lambda c: jnp.sum(jnp.where(eye, c, 0.0), axis=0, keepdims=True)
        g_a = (jnp.sum(dnum * f["qs"], axis=1, keepdims=True) + dden * f["qn"]) * a
        g_w = (jnp.sum(v * kds, axis=1, keepdims=True) + jnp.sum(k * dn1, axis=1, keepdims=True)) * w_end
        g_end = (jnp.sum(jnp.sum(ds1 * s0, axis=1, keepdims=True), axis=0, keepdims=True)
                 + jnp.sum(dn1 * n0, axis=1, keepdims=True)) * a_end
        col_e = jnp.sum(e, axis=0, keepdims=True)
        db = jnp.sum(e, axis=1, keepdims=True) - to_col(col_e) + g_a - g_w
        last = lax.broadcasted_iota(jnp.int32, (L, 1), 0) == (0 if reverse else L - 1)
        db = db + jnp.where(last, jnp.sum(g_w, axis=0, keepdims=True) + g_end, 0.0)
        dg_ref[0, 0] = col_e + to_row(g_w)
        dlf = jnp.sum(jnp.where(f["seen"], db, 0.0), axis=0, keepdims=True)
        dg_ref[1, 0] = dlf * jax.nn.sigmoid(-gr_ref[1, 0])

        ds_sc[...] = a_end * ds1 + _dot_tn(qm, adn)
        dn_sc[...] = a_end * dn1 + jnp.sum((dden * a) * q, axis=0, keepdims=True)

    qk_spec = pl.BlockSpec((L, DK_A), lambda h, i: (off + pos(i), h))
    v_spec = pl.BlockSpec((L, DV_A), lambda h, i: (off + pos(i), h))
    gr_spec = pl.BlockSpec((2, 1, 1, L), lambda h, i: (0, h, 0, off + pos(i)))
    st_spec = pl.BlockSpec((1, 1, DK_A, DV_A), lambda h, i: (h, pos(i), 0, 0))
    vec_spec = pl.BlockSpec((1, 1, 1, DK_A), lambda h, i: (h, pos(i), 0, 0))
    oqk_spec = pl.BlockSpec((L, DK_A), lambda h, i: (pos(i), h))
    ov_spec = pl.BlockSpec((L, DV_A), lambda h, i: (pos(i), h))
    og_spec = pl.BlockSpec((2, 1, 1, L), lambda h, i: (0, h, 0, pos(i)))
    return pl.pallas_call(
        body, grid=(NH_A, nc),
        in_specs=[qk_spec, qk_spec, v_spec, gr_spec, st_spec, vec_spec, vec_spec, ov_spec],
        out_specs=[oqk_spec, oqk_spec, ov_spec, og_spec],
        out_shape=[jax.ShapeDtypeStruct((n, QK_A), F32), jax.ShapeDtypeStruct((n, QK_A), F32),
                   jax.ShapeDtypeStruct((n, V_A), F32), jax.ShapeDtypeStruct((2, NH_A, 1, n), F32)],
        scratch_shapes=[pltpu.VMEM((DK_A, DV_A), F32), pltpu.VMEM((1, DK_A), F32)],
        compiler_params=pltpu.CompilerParams(dimension_semantics=("parallel", "arbitrary"),
                                             vmem_limit_bytes=VMEM_LIMIT),
        name="mlstm_bwd")(q, k, v, gr, s0_all, n0_all, m0_all, dh)


def _make_mlstm(n, row_off, reverse):
    def gate_rows(li, lf):
        return jnp.stack([li, lf]).transpose(0, 2, 1)[:, :, None, :]

    @jax.custom_vjp
    def op(q, k, v, li, lf):
        return _mlstm_fwd_call(q, k, v, gate_rows(li, lf), n, row_off, reverse)[0]

    def fwd(q, k, v, li, lf):
        gr = gate_rows(li, lf)
        h, s0, n0, m0 = _mlstm_fwd_call(q, k, v, gr, n, row_off, reverse)
        return h, (q, k, v, gr, s0, n0, m0)

    def bwd(res, dh):
        q, k, v, gr, s0, n0, m0 = res
        dq, dk, dv, dg = _mlstm_bwd_call(q, k, v, gr, s0, n0, m0, dh, n, row_off, reverse)
        rows = ((row_off, q.shape[0] - row_off - n), (0, 0))
        dg = jnp.pad(dg[:, :, 0, :].transpose(0, 2, 1), ((0, 0),) + rows)
        return jnp.pad(dq, rows), jnp.pad(dk, rows), jnp.pad(dv, rows), dg[0], dg[1]

    op.defvjp(fwd, bwd)
    return op


MERGE_ROWS = 128


def _sig(x):
    return jax.nn.sigmoid(x)


def _merge_pre_fwd(h_f, h_b, o_attn, p_oa, p_za, p_zb, mh_w, tc):
    t = o_attn.shape[0]
    tr, off = MERGE_ROWS, tc // MERGE_ROWS

    def body(hf_ref, hb_ref, oat_ref, oa_ref, za_ref, zb_ref, w_ref, a_ref, b_ref):
        for hd in range(NH_A):
            cols = slice(hd * DV_A, (hd + 1) * DV_A)
            h = hf_ref[:, cols] + hb_ref[:, cols]
            hn = h * lax.rsqrt(jnp.mean(h * h, axis=-1, keepdims=True) + EPS) * w_ref[:, cols]
            za = za_ref[:, cols]
            a_ref[:, cols] = (_sig(oa_ref[:, cols]) * hn * (za * _sig(za))).astype(BF16)
        zb = zb_ref[...]
        b_ref[...] = (oat_ref[...] * (zb * _sig(zb))).astype(BF16)

    lat = pl.BlockSpec((tr, V_A), lambda i: (i + off, 0))
    row = pl.BlockSpec((tr, V_A), lambda i: (i, 0))
    return pl.pallas_call(
        body, grid=(t // tr,),
        in_specs=[lat, row, row, lat, lat, lat, pl.BlockSpec((1, V_A), lambda i: (0, 0))],
        out_specs=[row, row],
        out_shape=[jax.ShapeDtypeStruct((t, V_A), BF16), jax.ShapeDtypeStruct((t, V_A), BF16)],
        compiler_params=pltpu.CompilerParams(dimension_semantics=("parallel",), vmem_limit_bytes=VMEM_LIMIT),
        name="merge_pre")(h_f, h_b, o_attn, p_oa, p_za, p_zb, mh_w)


def _ctx_block(i, nb, off):
    k = i - nb
    return jnp.where(i < nb, i + off, jnp.where(k < off, k, k + nb))


def _merge_pre_bwd(da, db, h_f, h_b, o_attn, p_oa, p_za, p_zb, mh_w, tc):
    t = o_attn.shape[0]
    n, r = h_f.shape[0], p_oa.shape[0]
    tr, off = MERGE_ROWS, tc // MERGE_ROWS
    nb = t // tr
    n_ctx = r // tr - nb

    def body(da_ref, db_ref, hf_ref, hb_ref, oat_ref, oa_ref, za_ref, zb_ref, w_ref,
             dhf_ref, dhb_ref, doat_ref, doa_ref, dza_ref, dzb_ref, dw_ref):
        i = pl.program_id(0)

        @pl.when(i == 0)
        def _():
            dw_ref[...] = jnp.zeros(dw_ref.shape, F32)

        @pl.when(i < nb)
        def _():
            for hd in range(NH_A):
                cols = slice(hd * DV_A, (hd + 1) * DV_A)
                h = hf_ref[:, cols] + hb_ref[:, cols]
                rs = lax.rsqrt(jnp.mean(h * h, axis=-1, keepdims=True) + EPS)
                w = w_ref[:, cols]
                hn = h * rs * w
                oa, za, g = oa_ref[:, cols], za_ref[:, cols], da_ref[:, cols]
                so, sz = _sig(oa), _sig(za)
                silu_z = za * sz
                doa_ref[:, cols] = g * hn * silu_z * so * (1.0 - so)
                dza_ref[:, cols] = g * so * hn * (sz * (1.0 + za * (1.0 - sz)))
                dhn = g * so * silu_z
                dw_ref[:, cols] += jnp.sum(dhn * (h * rs), axis=0, keepdims=True)
                dxn = dhn * w
                dh = rs * dxn - h * (rs * rs * rs * jnp.mean(dxn * h, axis=-1, keepdims=True))
                dhf_ref[:, cols] = dh
                dhb_ref[:, cols] = dh
            zb, gb, oat = zb_ref[...], db_ref[...], oat_ref[...]
            sb = _sig(zb)
            doat_ref[...] = gb * (zb * sb)
            dzb_ref[...] = gb * oat * (sb * (1.0 + zb * (1.0 - sb)))

        @pl.when(i >= nb)
        def _():
            for ref in (dhf_ref, dhb_ref, doa_ref, dza_ref, dzb_ref):
                ref[...] = jnp.zeros(ref.shape, F32)

    lati = lambda i: jnp.minimum(i, nb - 1)
    lat = pl.BlockSpec((tr, V_A), lambda i: (lati(i) + off, 0))
    row = pl.BlockSpec((tr, V_A), lambda i: (lati(i), 0))
    vec = pl.BlockSpec((1, V_A), lambda i: (0, 0))
    pout = pl.BlockSpec((tr, V_A), lambda i: (_ctx_block(i, nb, off), 0))
    hf_out = pl.BlockSpec((tr, V_A), lambda i: (jnp.where(i < nb, i + off, jnp.minimum(i - nb, off - 1)), 0))
    hb_out = pl.BlockSpec((tr, V_A), lambda i: (jnp.where(i < nb, i, nb + jnp.minimum(i - nb, off - 1)), 0))
    return pl.pallas_call(
        body, grid=(nb + n_ctx,),
        in_specs=[row, row, lat, row, row, lat, lat, lat, vec],
        out_specs=[hf_out, hb_out, row, pout, pout, pout, vec],
        out_shape=[jax.ShapeDtypeStruct((n, V_A), F32), jax.ShapeDtypeStruct((n, V_A), F32),
                   jax.ShapeDtypeStruct((t, V_A), F32), jax.ShapeDtypeStruct((r, V_A), F32),
                   jax.ShapeDtypeStruct((r, V_A), F32), jax.ShapeDtypeStruct((r, V_A), F32),
                   jax.ShapeDtypeStruct((1, V_A), F32)],
        compiler_params=pltpu.CompilerParams(dimension_semantics=("arbitrary",), vmem_limit_bytes=VMEM_LIMIT),
        name="merge_pre_bwd")(da, db, h_f, h_b, o_attn, p_oa, p_za, p_zb, mh_w)


def _merge_gate_fwd(y_a, y_b, p_ga, p_gb, tc):
    t = y_a.shape[0]
    tr, off = MERGE_ROWS, tc // MERGE_ROWS

    def body(ya_ref, yb_ref, ga_ref, gb_ref, m_ref):
        m_ref[...] = (_sig(ga_ref[...]) * ya_ref[...] + _sig(gb_ref[...]) * yb_ref[...]).astype(BF16)

    lat = pl.BlockSpec((tr, D_MODEL), lambda i: (i + off, 0))
    row = pl.BlockSpec((tr, D_MODEL), lambda i: (i, 0))
    return pl.pallas_call(
        body, grid=(t // tr,), in_specs=[row, row, lat, lat], out_specs=row,
        out_shape=jax.ShapeDtypeStruct((t, D_MODEL), BF16),
        compiler_params=pltpu.CompilerParams(dimension_semantics=("parallel",), vmem_limit_bytes=VMEM_LIMIT),
        name="merge_gate")(y_a, y_b, p_ga, p_gb)


def _merge_gate_bwd(dm, y_a, y_b, p_ga, p_gb, tc):
    t, r = y_a.shape[0], p_ga.shape[0]
    tr, off = MERGE_ROWS, tc // MERGE_ROWS
    nb = t // tr
    n_ctx = r // tr - nb

    def body(dm_ref, ya_ref, yb_ref, ga_ref, gb_ref, dya_ref, dyb_ref, dga_ref, dgb_ref):
        i = pl.program_id(0)

        @pl.when(i < nb)
        def _():
            dm = dm_ref[...]
            sa, sb = _sig(ga_ref[...]), _sig(gb_ref[...])
            dya_ref[...] = (dm * sa).astype(BF16)
            dyb_ref[...] = (dm * sb).astype(BF16)
            dga_ref[...] = dm * ya_ref[...] * sa * (1.0 - sa)
            dgb_ref[...] = dm * yb_ref[...] * sb * (1.0 - sb)

        @pl.when(i >= nb)
        def _():
            dga_ref[...] = jnp.zeros(dga_ref.shape, F32)
            dgb_ref[...] = jnp.zeros(dgb_ref.shape, F32)

    lati = lambda i: jnp.minimum(i, nb - 1)
    lat = pl.BlockSpec((tr, D_MODEL), lambda i: (lati(i) + off, 0))
    row = pl.BlockSpec((tr, D_MODEL), lambda i: (lati(i), 0))
    pout = pl.BlockSpec((tr, D_MODEL), lambda i: (_ctx_block(i, nb, off), 0))
    return pl.pallas_call(
        body, grid=(nb + n_ctx,), in_specs=[row, row, row, lat, lat], out_specs=[row, row, pout, pout],
        out_shape=[jax.ShapeDtypeStruct((t, D_MODEL), BF16), jax.ShapeDtypeStruct((t, D_MODEL), BF16),
                   jax.ShapeDtypeStruct((r, D_MODEL), F32), jax.ShapeDtypeStruct((r, D_MODEL), F32)],
        compiler_params=pltpu.CompilerParams(dimension_semantics=("arbitrary",), vmem_limit_bytes=VMEM_LIMIT),
        name="merge_gate_bwd")(dm, y_a, y_b, p_ga, p_gb)


def _make_merge_block(tc):
    @jax.custom_vjp
    def block(h_f, h_b, o_attn, p_oa, p_za, p_zb, p_ga, p_gb, mh_w, w_ba, w_bb, w_out, pr_ba, pr_bb, pr_out):
        return fwd(h_f, h_b, o_attn, p_oa, p_za, p_zb, p_ga, p_gb, mh_w, w_ba, w_bb, w_out, pr_ba, pr_bb, pr_out)[0]

    def fwd(h_f, h_b, o_attn, p_oa, p_za, p_zb, p_ga, p_gb, mh_w, w_ba, w_bb, w_out, pr_ba, pr_bb, pr_out):
        a_in, b_in = _merge_pre_fwd(h_f, h_b, o_attn, p_oa, p_za, p_zb, mh_w[None], tc)
        y_a, y_b = _mm_nn(a_in, w_ba, "merge_ya"), _mm_nn(b_in, w_bb, "merge_yb")
        m_in = _merge_gate_fwd(y_a, y_b, p_ga, p_gb, tc)
        out = _mm_nn(m_in, w_out, "merge_out")
        return out, (h_f, h_b, o_attn, p_oa, p_za, p_zb, p_ga, p_gb, mh_w, w_ba, w_bb, w_out, a_in, b_in, y_a, y_b, m_in)

    def bwd(res, dout):
        h_f, h_b, o_attn, p_oa, p_za, p_zb, p_ga, p_gb, mh_w, w_ba, w_bb, w_out, a_in, b_in, y_a, y_b, m_in = res
        dm = _mm_nt(dout, w_out, "merge_out_da")
        dw_out = _mm_tn(m_in, dout, "merge_out_dw")
        dy_a, dy_b, dga, dgb = _merge_gate_bwd(dm, y_a, y_b, p_ga, p_gb, tc)
        da, db = _mm_nt(dy_a, w_ba, "merge_ya_da"), _mm_nt(dy_b, w_bb, "merge_yb_da")
        dw_ba, dw_bb = _mm_tn(a_in, dy_a, "merge_ya_dw"), _mm_tn(b_in, dy_b, "merge_yb_dw")
        dhf, dhb, doat, doa, dza, dzb, dmh = _merge_pre_bwd(da, db, h_f, h_b, o_attn, p_oa, p_za, p_zb, mh_w[None], tc)
        z = jnp.zeros_like
        return (dhf, dhb, doat, doa, dza, dzb, dga, dgb, dmh[0], z(w_ba), z(w_bb), z(w_out), dw_ba, dw_bb, dw_out)

    block.defvjp(fwd, bwd)
    return block


def _silu(x):
    return x * jax.nn.sigmoid(x)


CONV_ROWS, CONV_HALO = 256, 8


def _make_conv(t, tc):
    r = t + 2 * tc
    width = 2 * QK_A
    nblk = r // CONV_ROWS
    cb, nb = tc // CONV_ROWS, t // CONV_ROWS
    k_scale = DK_A ** -0.5
    per = CONV_ROWS // CONV_HALO

    def taps(x_ref, prev_ref, next_ref):
        i = pl.program_id(0)
        seg_first = (i == 0) | (i == cb) | (i == cb + nb)
        seg_last = (i == cb - 1) | (i == cb + nb - 1) | (i == nblk - 1)
        x = x_ref[...]
        rows = lax.broadcasted_iota(jnp.int32, (CONV_ROWS, 1), 0)
        before = jnp.where(seg_first, 0.0, prev_ref[CONV_HALO - 1:CONV_HALO, :])
        after = jnp.where(seg_last, 0.0, next_ref[0:1, :])
        xm1 = jnp.where(rows == 0, before, pltpu.roll(x, 1, 0))
        xp1 = jnp.where(rows == CONV_ROWS - 1, after, pltpu.roll(x, CONV_ROWS - 1, 0))
        return xm1, x, xp1

    row = pl.BlockSpec((CONV_ROWS, width), lambda i: (i, 0))
    prev = pl.BlockSpec((CONV_HALO, width), lambda i: (jnp.maximum(i * per - 1, 0), 0))
    nxt = pl.BlockSpec((CONV_HALO, width), lambda i: (jnp.minimum((i + 1) * per, r // CONV_HALO - 1), 0))
    half = pl.BlockSpec((CONV_ROWS, QK_A), lambda i: (i, 0))
    wspec = pl.BlockSpec((3, width), lambda i: (0, 0))
    bspec = pl.BlockSpec((1, width), lambda i: (0, 0))
    par = pltpu.CompilerParams(dimension_semantics=("parallel",), vmem_limit_bytes=VMEM_LIMIT)
    seq = pltpu.CompilerParams(dimension_semantics=("arbitrary",), vmem_limit_bytes=VMEM_LIMIT)

    def fwd_call(x, cw, cb_):
        def body(x_ref, p_ref, n_ref, w_ref, b_ref, q_ref, k_ref):
            xm1, x0, xp1 = taps(x_ref, p_ref, n_ref)
            c = b_ref[...] + xm1 * w_ref[0:1, :] + x0 * w_ref[1:2, :] + xp1 * w_ref[2:3, :]
            y = c * jax.nn.sigmoid(c)
            q_ref[...] = y[:, :QK_A]
            k_ref[...] = y[:, QK_A:] * k_scale

        return pl.pallas_call(
            body, grid=(nblk,), in_specs=[row, prev, nxt, wspec, bspec], out_specs=[half, half],
            out_shape=[jax.ShapeDtypeStruct((r, QK_A), F32), jax.ShapeDtypeStruct((r, QK_A), F32)],
            compiler_params=par, name="conv_silu")(x, x, x, cw, cb_)

    def bwd_pre_call(dq, dk, x, cw, cb_):
        def body(dq_ref, dk_ref, x_ref, p_ref, n_ref, w_ref, b_ref, dc_ref, dw_ref, db_ref):
            @pl.when(pl.program_id(0) == 0)
            def _():
                dw_ref[...] = jnp.zeros(dw_ref.shape, F32)
                db_ref[...] = jnp.zeros(db_ref.shape, F32)

            xm1, x0, xp1 = taps(x_ref, p_ref, n_ref)
            c = b_ref[...] + xm1 * w_ref[0:1, :] + x0 * w_ref[1:2, :] + xp1 * w_ref[2:3, :]
            s = jax.nn.sigmoid(c)
            dy = jnp.concatenate([dq_ref[...], dk_ref[...] * k_scale], axis=1)
            dc = dy * (s * (1.0 + c * (1.0 - s)))
            dc_ref[...] = dc
            db_ref[...] += jnp.sum(dc, axis=0, keepdims=True)
            dw_ref[0:1, :] += jnp.sum(dc * xm1, axis=0, keepdims=True)
            dw_ref[1:2, :] += jnp.sum(dc * x0, axis=0, keepdims=True)
            dw_ref[2:3, :] += jnp.sum(dc * xp1, axis=0, keepdims=True)

        return pl.pallas_call(
            body, grid=(nblk,), in_specs=[half, half, row, prev, nxt, wspec, bspec], out_specs=[row, wspec, bspec],
            out_shape=[jax.ShapeDtypeStruct((r, width), F32), jax.ShapeDtypeStruct((3, width), F32),
                       jax.ShapeDtypeStruct((1, width), F32)],
            compiler_params=seq, name="conv_silu_bwd")(dq, dk, x, x, x, cw, cb_)

    def bwd_x_call(dc, cw):
        def body(d_ref, p_ref, n_ref, w_ref, dx_ref):
            dm1, d0, dp1 = taps(d_ref, p_ref, n_ref)
            dx_ref[...] = dm1 * w_ref[2:3, :] + d0 * w_ref[1:2, :] + dp1 * w_ref[0:1, :]

        return pl.pallas_call(
            body, grid=(nblk,), in_specs=[row, prev, nxt, wspec], out_specs=row,
            out_shape=jax.ShapeDtypeStruct((r, width), F32), compiler_params=par,
            name="conv_silu_bwd_x")(dc, dc, dc, cw)

    @jax.custom_vjp
    def op(x, cw, cb_):
        return tuple(fwd_call(x, cw, cb_[None]))

    def fwd(x, cw, cb_):
        return tuple(fwd_call(x, cw, cb_[None])), (x, cw, cb_)

    def bwd(res, cot):
        x, cw, cb_ = res
        dc, dw, db = bwd_pre_call(cot[0], cot[1], x, cw, cb_[None])
        return bwd_x_call(dc, cw), dw, db[0]

    op.defvjp(fwd, bwd)
    return op


def _local_loss(diff, const):
    x, ctx, target = diff["x"], const["ctx"], const["target"]
    t, tc = x.shape[0], ctx.shape[0]
    n = tc + t

    mod = diff["mod"]
    shift, scale, gate = mod[0, :D_MODEL], mod[0, D_MODEL:2 * D_MODEL], mod[0, 2 * D_MODEL:]
    shift_c, scale_c = mod[1, :D_MODEL], mod[1, D_MODEL:2 * D_MODEL]
    (p_qk, p_va, p_kv, p_oa, p_za, p_qb, p_zb, p_ga, p_gb), p_if = _ln_project(
        x, ctx, jnp.stack([scale, shift, scale_c, shift_c]), const["w_main"], const["w_if"], diff["p_main"], diff["p_if"])
    gt = p_if[:, :N_IF] + diff["b_if"]

    q_a, k_a = _make_conv(t, tc)(p_qk, diff["conv_w"], diff["conv_b"])
    v_a = p_va
    li_f, lf_f, li_b, lf_b = gt[:, 0:8], gt[:, 8:16], gt[:, 16:24], gt[:, 24:32]

    h_f = _make_mlstm(n, 0, False)(q_a, k_a, v_a, li_f, lf_f)
    h_b = _make_mlstm(n, tc, True)(q_a, k_a, v_a, li_b, lf_b)

    o_attn = _make_gqa(t, tc)(p_qb, p_kv, diff["q_norm_w"], diff["k_norm_w"])

    out = _make_merge_block(tc)(h_f, h_b, o_attn, p_oa, p_za, p_zb, p_ga, p_gb, diff["mh_norm_w"],
                                const["w_ba"], const["w_bb"], const["w_out"], diff["p_ba"], diff["p_bb"], diff["p_out"])

    return _loss_head(x, out, target, gate, diff["ln_w"], diff["ln_b"])


OTHER_CHIPS = [(1, 0), (0, 1), (1, 1)]


def _flip(v, bit):
    return 1 - v if bit else v


def _gather_chips(shard, name):
    def body(x_ref, o_ref, send_sems, recv_sems, local_sem):
        x, y, c = lax.axis_index("x"), lax.axis_index("y"), lax.axis_index("c")
        mine = pltpu.make_async_copy(x_ref, o_ref.at[2 * x + y], local_sem)
        mine.start()

        def copy(r, slot):
            dx, dy = OTHER_CHIPS[r]
            return pltpu.make_async_remote_copy(
                src_ref=x_ref, dst_ref=o_ref.at[slot], send_sem=send_sems.at[r], recv_sem=recv_sems.at[r],
                device_id=(_flip(x, dx), _flip(y, dy), c), device_id_type=MESH)

        sends = [copy(r, 2 * x + y) for r in range(3)]
        for cp in sends:
            cp.start()
        for r, (dx, dy) in enumerate(OTHER_CHIPS):
            copy(r, 2 * _flip(x, dx) + _flip(y, dy)).wait_recv()
        for cp in sends:
            cp.wait_send()
        mine.wait()

    return pl.pallas_call(
        body, out_shape=jax.ShapeDtypeStruct((N_CHIPS,) + shard.shape, shard.dtype),
        in_specs=[pl.BlockSpec(memory_space=pl.ANY)], out_specs=pl.BlockSpec(memory_space=pl.ANY),
        scratch_shapes=[pltpu.SemaphoreType.DMA((3,)), pltpu.SemaphoreType.DMA((3,)), pltpu.SemaphoreType.DMA],
        name=name)(shard)


def _gather_chips_halves(shard, name):
    rows, cols = shard.shape
    halves = shard.reshape(2, rows // 2, cols)

    def body(x_ref, o_ref, send_sems, recv_sems, local_sem):
        x, y, c = lax.axis_index("x"), lax.axis_index("y"), lax.axis_index("c")
        my_chip = 2 * x + y
        mine = pltpu.make_async_copy(x_ref, o_ref.at[my_chip], local_sem)
        mine.start()

        def chip_of(r):
            dx, dy = OTHER_CHIPS[r]
            return _flip(x, dx), _flip(y, dy)

        def copy(k, chip_slot, half, to, src=None):
            dst = o_ref.at[chip_slot, half]
            return pltpu.make_async_remote_copy(
                src_ref=dst if src is None else src, dst_ref=dst, send_sem=send_sems.at[k],
                recv_sem=recv_sems.at[k], device_id=to, device_id_type=MESH)

        first = [copy(r, my_chip, c, (*chip_of(r), c), src=x_ref.at[c]) for r in range(3)]
        for cp in first:
            cp.start()
        passed = []
        for r in range(3):
            px, py = chip_of(r)
            copy(r, 2 * px + py, c, (px, py, c)).wait_recv()
            passed.append(copy(3 + r, 2 * px + py, c, (x, y, 1 - c)))
            passed[-1].start()
        for r in range(3):
            px, py = chip_of(r)
            copy(3 + r, 2 * px + py, 1 - c, (x, y, 1 - c)).wait_recv()
        for cp in first + passed:
            cp.wait_send()
        mine.wait()

    out = pl.pallas_call(
        body, out_shape=jax.ShapeDtypeStruct((N_CHIPS, 2, rows // 2, cols), shard.dtype),
        in_specs=[pl.BlockSpec(memory_space=pl.ANY)], out_specs=pl.BlockSpec(memory_space=pl.ANY),
        scratch_shapes=[pltpu.SemaphoreType.DMA((6,)), pltpu.SemaphoreType.DMA((6,)), pltpu.SemaphoreType.DMA],
        name=name)(halves)
    return out.reshape(N_CHIPS, rows, cols)


def _scatter_grads(slots, name):
    def body(g_ref, o_ref, send_sems, recv_sems, local_sem):
        x, y, c = lax.axis_index("x"), lax.axis_index("y"), lax.axis_index("c")
        me, my_chip, sibling = 4 * x + 2 * y + c, 2 * x + y, (x, y, 1 - c)
        mine = pltpu.make_async_copy(g_ref.at[my_chip], o_ref.at[me], local_sem)
        mine.start()

        def chip_of(r):
            dx, dy = OTHER_CHIPS[r]
            return _flip(x, dx), _flip(y, dy)

        def copy(k, slot, to, src=None):
            dst = o_ref.at[slot]
            return pltpu.make_async_remote_copy(
                src_ref=dst if src is None else src, dst_ref=dst, send_sem=send_sems.at[k],
                recv_sem=recv_sems.at[k], device_id=to, device_id_type=MESH)

        first = [copy(0, me, sibling, src=g_ref.at[my_chip])]
        for r in range(3):
            px, py = chip_of(r)
            first.append(copy(1 + r, me, (px, py, c), src=g_ref.at[2 * px + py]))
        for cp in first:
            cp.start()
        passed = []
        for r in range(3):
            px, py = chip_of(r)
            copy(1 + r, 4 * px + 2 * py + c, (px, py, c)).wait_recv()
            passed.append(copy(4 + r, 4 * px + 2 * py + c, sibling))
            passed[-1].start()
        copy(0, 4 * x + 2 * y + 1 - c, sibling).wait_recv()
        for r in range(3):
            px, py = chip_of(r)
            copy(4 + r, 4 * px + 2 * py + 1 - c, sibling).wait_recv()
        for cp in first + passed:
            cp.wait_send()
        mine.wait()

    return pl.pallas_call(
        body, out_shape=jax.ShapeDtypeStruct((N_DEV,) + slots.shape[1:], slots.dtype),
        in_specs=[pl.BlockSpec(memory_space=pl.ANY)], out_specs=pl.BlockSpec(memory_space=pl.ANY),
        scratch_shapes=[pltpu.SemaphoreType.DMA((N_DEV - 1,)), pltpu.SemaphoreType.DMA((N_DEV - 1,)),
                        pltpu.SemaphoreType.DMA],
        name=name)(slots)


def _allreduce_small(v, name):
    def body(v_ref, o_ref, buf, send_sems, recv_sems):
        x, y, c = lax.axis_index("x"), lax.axis_index("y"), lax.axis_index("c")
        me = 4 * x + 2 * y + c
        buf[me] = v_ref[...]

        def peer(r):
            return _flip(x, (r >> 2) & 1), _flip(y, (r >> 1) & 1), _flip(c, r & 1)

        def copy(r, dst_slot):
            return pltpu.make_async_remote_copy(
                src_ref=v_ref, dst_ref=buf.at[dst_slot], send_sem=send_sems.at[r - 1],
                recv_sem=recv_sems.at[r - 1], device_id=peer(r), device_id_type=MESH)

        sends = [copy(r, me) for r in range(1, N_DEV)]
        for cp in sends:
            cp.start()
        for r in range(1, N_DEV):
            px, py, pc = peer(r)
            copy(r, 4 * px + 2 * py + pc).wait_recv()
        for cp in sends:
            cp.wait_send()
        acc = buf[0]
        for d in range(1, N_DEV):
            acc = acc + buf[d]
        o_ref[...] = acc

    return pl.pallas_call(
        body, out_shape=jax.ShapeDtypeStruct(v.shape, v.dtype),
        in_specs=[pl.BlockSpec(memory_space=pltpu.VMEM)], out_specs=pl.BlockSpec(memory_space=pltpu.VMEM),
        scratch_shapes=[pltpu.VMEM((N_DEV,) + v.shape, v.dtype), pltpu.SemaphoreType.DMA((N_DEV - 1,)),
                        pltpu.SemaphoreType.DMA((N_DEV - 1,))],
        name=name)(v)


def _adamw_math(w, g, m, v):
    m = ADAM_B1 * m + (1.0 - ADAM_B1) * g
    v = ADAM_B2 * v + (1.0 - ADAM_B2) * jnp.square(g)
    m_hat = m / (1.0 - ADAM_B1 ** ADAM_STEP)
    v_hat = v / (1.0 - ADAM_B2 ** ADAM_STEP)
    delta = -ADAM_LR * (m_hat / (jnp.sqrt(v_hat) + ADAM_EPS) + ADAM_WD * w)
    return delta, m, v


def _adamw_sum(parts, w, m, v, name):
    npart, rows, cols = parts.shape
    tr = _pick(rows, (64, 32, 16, 8)) if rows >= 8 else rows

    def body(p_ref, w_ref, m_ref, v_ref, g_out, d_out, m_out, v_out):
        g = p_ref[0].astype(F32)
        for k in range(1, npart):
            g = g + p_ref[k].astype(F32)
        d, m2, v2 = _adamw_math(w_ref[...], g, m_ref[...], v_ref[...])
        g_out[...] = g
        d_out[...] = d
        m_out[...] = m2
        v_out[...] = v2

    spec = pl.BlockSpec((tr, cols), lambda i: (i, 0))
    shp = jax.ShapeDtypeStruct((rows, cols), F32)
    return pl.pallas_call(
        body, grid=(rows // tr,),
        in_specs=[pl.BlockSpec((npart, tr, cols), lambda i: (0, i, 0)), spec, spec, spec],
        out_specs=[spec, spec, spec, spec], out_shape=[shp, shp, shp, shp],
        compiler_params=pltpu.CompilerParams(dimension_semantics=("parallel",), vmem_limit_bytes=VMEM_LIMIT),
        name=name)(parts, w, m, v)


SMALL_ROWS = 16


def _pack_small(c_ctx, b_mod, conv_b, mh, ln_w, ln_b, conv_w_rows, b_if, qn, kn):
    last = jnp.concatenate([b_if.reshape(-1), qn.reshape(-1), kn.reshape(-1),
                            jnp.zeros((D_MODEL - N_IF - 2 * HD_B,), F32)])
    rows = [c_ctx.reshape(1, D_MODEL), b_mod.reshape(3, D_MODEL), conv_b.reshape(1, D_MODEL),
            mh.reshape(1, D_MODEL), ln_w.reshape(1, D_MODEL), ln_b.reshape(1, D_MODEL),
            conv_w_rows.reshape(3, D_MODEL), last[None], jnp.zeros((SMALL_ROWS - 12, D_MODEL), F32)]
    return jnp.concatenate(rows, 0)


def _unpack_small(pk, conv_cols):
    return dict(c_ctx=pk[0], b_mod=pk[1:4].reshape(1, 3 * D_MODEL), conv_b=pk[4:5], mh_norm_w=pk[5:6],
                ln_w=pk[6:7], ln_b=pk[7:8], conv_w=pk[8:11, :conv_cols][None], b_if=pk[11:12, :N_IF],
                q_norm_w=pk[11:12, N_IF:N_IF + HD_B], k_norm_w=pk[11:12, N_IF + HD_B:N_IF + 2 * HD_B])


def kernel(x, c, ctx, c_ctx, w_mod, b_mod, w_in, b_if, conv_w, conv_b, mh_norm_w, q_norm_w, k_norm_w, w_branch_a, w_branch_b, w_out, ln_w, ln_b, loss_target, m_c_ctx, m_w_mod, m_b_mod, m_w_in, m_b_if, m_conv_w, m_conv_b, m_mh_norm_w, m_q_norm_w, m_k_norm_w, m_w_branch_a, m_w_branch_b, m_w_out, m_ln_w, m_ln_b, v_c_ctx, v_w_mod, v_b_mod, v_w_in, v_b_if, v_conv_w, v_conv_b, v_mh_norm_w, v_q_norm_w, v_k_norm_w, v_w_branch_a, v_w_branch_b, v_w_out, v_ln_w, v_ln_b):
    core = lax.axis_index("c")
    chip = 2 * lax.axis_index("x") + lax.axis_index("y")
    me = 2 * chip + core
    mod_cols, in_cols, conv_cols = w_mod.shape[2], w_in.shape[2], conv_w.shape[2]
    br_rows = w_out.shape[1]

    def rows_at(block, first):
        return lax.dynamic_update_slice(jnp.zeros((SMALL_ROWS, block.shape[1]), F32), block, (first, 0))

    owner = (core == 0).astype(F32)
    cond = _allreduce_small(rows_at(jnp.stack([_silu(c[0]), _silu(c_ctx)]), 2 * me), "gather_cond").astype(BF16)
    w_mod_b = w_mod[0].astype(BF16)
    mod_part = _mm_nn(cond, w_mod_b, "mod_fwd") * owner
    mod_all = _allreduce_small(
        lax.dynamic_update_slice(jnp.zeros((SMALL_ROWS, 3 * D_MODEL), F32), mod_part, (0, chip * mod_cols)),
        "gather_mod") + b_mod[0]
    mod = lax.dynamic_slice(mod_all, (2 * me, 0), (2, 3 * D_MODEL))

    g_in = _gather_chips_halves(w_in[0].astype(BF16), "gather_w_in")
    g_br = _gather_chips_halves(jnp.concatenate([w_branch_a[0], w_branch_b[0], w_out[0]], 0).astype(BF16),
                                "gather_w_branches").reshape(N_CHIPS, 3, br_rows, D_MODEL)
    g_ba, g_bb, g_out = g_br[:, 0], g_br[:, 1], g_br[:, 2]
    g_conv = _gather_chips(conv_w[0], "gather_conv_w")
    w_in_full = jnp.moveaxis(g_in, 0, 1).reshape(D_MODEL, N_CHIPS * in_cols)
    w_main = jnp.concatenate([w_in_full[:, :IF_START], w_in_full[:, IF_START + N_IF:]], 1)
    w_if = jnp.pad(w_in_full[:, IF_START:IF_START + N_IF], ((0, 0), (0, IF_PAD - N_IF)))
    conv_w_full = jnp.moveaxis(g_conv, 0, 1).reshape(3, N_CHIPS * conv_cols)

    const = dict(ctx=ctx[0], target=loss_target[0], w_main=w_main, w_if=w_if,
                 w_ba=g_ba.reshape(D_MODEL, D_MODEL), w_bb=g_bb.reshape(D_MODEL, D_MODEL),
                 w_out=g_out.reshape(D_MODEL, D_MODEL))
    diff = dict(x=x[0], mod=mod, b_if=b_if[0], conv_w=conv_w_full, conv_b=conv_b[0],
                mh_norm_w=mh_norm_w[0], q_norm_w=q_norm_w[0], k_norm_w=k_norm_w[0], ln_w=ln_w[0], ln_b=ln_b[0],
                p_main=jnp.zeros(w_main.shape, F32),
                p_if=jnp.zeros(w_if.shape, F32), p_ba=jnp.zeros((D_MODEL, D_MODEL), F32),
                p_bb=jnp.zeros((D_MODEL, D_MODEL), F32), p_out=jnp.zeros((D_MODEL, D_MODEL), F32))
    loss_local, g = jax.value_and_grad(_local_loss)(diff, const)
    loss = lax.psum(loss_local, ("x", "y", "c"))

    dmod_all = _allreduce_small(rows_at(g["mod"], 2 * me), "gather_dmod")
    dmod_k = lax.dynamic_slice(dmod_all, (0, chip * mod_cols), (SMALL_ROWS, mod_cols))
    g_w_mod = _mm_tn(cond, dmod_k, "mod_dw")
    g_b_mod = jnp.sum(dmod_all, axis=0) * (me == 0).astype(F32)
    d_cond = _mm_nt(dmod_k, w_mod_b, "mod_da")
    sig_ctx = jax.nn.sigmoid(c_ctx)
    g_c_ctx = owner * (sig_ctx * (1.0 + c_ctx * (1.0 - sig_ctx))) * jnp.sum(d_cond[1::2], axis=0)

    g_small = _allreduce_small(
        _pack_small(g_c_ctx, g_b_mod, g["conv_b"], g["mh_norm_w"], g["ln_w"], g["ln_b"], g["conv_w"],
                    g["b_if"], g["q_norm_w"], g["k_norm_w"]), "allreduce_small")
    conv_g = lax.dynamic_slice(g_small[8:11], (0, chip * conv_cols), (3, conv_cols))
    g_small = g_small.at[8:11].set(jnp.pad(conv_g, ((0, 0), (0, D_MODEL - conv_cols))))
    pad_conv = lambda a: jnp.pad(a[0], ((0, 0), (0, D_MODEL - conv_cols)))
    packed = [_pack_small(cc, bm[0], cb[0], mh[0], lw[0], lb[0], pad_conv(cw), bi[0], qn[0], kn[0])
              for cc, bm, cb, mh, lw, lb, cw, bi, qn, kn in (
                  (c_ctx, b_mod, conv_b, mh_norm_w, ln_w, ln_b, conv_w, b_if, q_norm_w, k_norm_w),
                  (m_c_ctx, m_b_mod, m_conv_b, m_mh_norm_w, m_ln_w, m_ln_b, m_conv_w, m_b_if, m_q_norm_w, m_k_norm_w),
                  (v_c_ctx, v_b_mod, v_conv_b, v_mh_norm_w, v_ln_w, v_ln_b, v_conv_w, v_b_if, v_q_norm_w, v_k_norm_w))]
    small = [_unpack_small(a, conv_cols)
             for a in _adamw_sum(g_small[None], packed[0], packed[1], packed[2], "adamw_small")]

    def col_slots(gfull, cols):
        return jnp.moveaxis(gfull.reshape(D_MODEL, N_CHIPS, cols), 1, 0).astype(BF16)

    g_in_full = jnp.concatenate([g["p_main"][:, :IF_START], g["p_if"][:, :N_IF], g["p_main"][:, IF_START:]], 1)
    big = {"w_mod": [a[None] for a in _adamw_sum(g_w_mod[None], w_mod[0], m_w_mod[0], v_w_mod[0], "adamw_w_mod")]}
    parts_in = _scatter_grads(col_slots(g_in_full, in_cols), "scatter_w_in")
    big["w_in"] = [a[None] for a in _adamw_sum(parts_in, w_in[0], m_w_in[0], v_w_in[0], "adamw_w_in")]
    br_slots = jnp.concatenate([g[p].reshape(N_CHIPS, br_rows, D_MODEL) for p in ("p_ba", "p_bb", "p_out")], 1)
    parts_br = _scatter_grads(br_slots.astype(BF16), "scatter_w_branches")
    for j, (nm, w_, m_, v_) in enumerate((("w_branch_a", w_branch_a, m_w_branch_a, v_w_branch_a),
                                          ("w_branch_b", w_branch_b, m_w_branch_b, v_w_branch_b),
                                          ("w_out", w_out, m_w_out, v_w_out))):
        parts = parts_br[:, j * br_rows:(j + 1) * br_rows]
        big[nm] = [a[None] for a in _adamw_sum(parts, w_[0], m_[0], v_[0], "adamw_" + nm)]

    names = ["c_ctx", "w_mod", "b_mod", "w_in", "b_if", "conv_w", "conv_b", "mh_norm_w", "q_norm_w", "k_norm_w",
             "w_branch_a", "w_branch_b", "w_out", "ln_w", "ln_b"]
    outs = [[big[nm][k] if nm in big else small[k][nm] for nm in names] for k in range(4)]
    return (loss, g["x"][None], *outs[0], *outs[1], *outs[2], *outs[3])
```

```python
import jax
import jax.numpy as jnp
from jax import lax
from jax.experimental import pallas as pl
from jax.experimental.pallas import tpu as pltpu

F32 = jnp.float32
BF16 = jnp.bfloat16
MESH = pl.DeviceIdType.MESH

D_MODEL = 2048
NH_A, DK_A, DV_A = 8, 128, 256
QK_A, V_A = NH_A * DK_A, NH_A * DV_A
NH_B, NKV_B, HD_B = 16, 4, 128
Q_B, KV_B = NH_B * HD_B, NKV_B * HD_B
GRID_W = 64
ROT_HALF = HD_B // 2
ROPE_THETA = 10000.0
M_INIT = -1e30
EPS = 1e-6
ALPHA = 2.0 ** 0.25
N_IN = 17440
IF_START, N_IF, IF_PAD = 4096, 32, 128
N_MAIN = N_IN - N_IF
O_QK, O_VA, O_KB, O_VB, O_OA, O_ZA, O_QB, O_ZB, O_GA, O_GB = (
    0, 2048, 4096, 4608, 5120, 7168, 9216, 11264, 13312, 15360)
MLSTM_CHUNK = 256

ADAM_LR, ADAM_B1, ADAM_B2, ADAM_EPS, ADAM_WD, ADAM_STEP = 0.001, 0.9, 0.999, 1e-08, 0.01, 10

VMEM_LIMIT = 48 * 1024 * 1024
N_CHIPS, N_DEV = 4, 8
MX = BF16


def _pick(n, cands):
    for c in cands:
        if n % c == 0:
            return c
    raise ValueError(f"no tile for {n} in {cands}")


def _dot(a, b):
    return jnp.dot(a, b, preferred_element_type=F32)


def _dot_nt(a, b):
    return lax.dot_general(a, b, (((1,), (1,)), ((), ())), preferred_element_type=F32)


def _dot_tn(a, b):
    return lax.dot_general(a, b, (((0,), (0,)), ((), ())), preferred_element_type=F32)


def _mm_nn(a, b, name):
    m, k = a.shape
    _, n = b.shape
    tm = _pick(m, (1024, 512, 256, 128, 64, 32, 16))
    tn = _pick(n, (1024, 512, 256, 128))

    def body(a_ref, b_ref, o_ref):
        o_ref[...] = _dot(a_ref[...], b_ref[...])

    return pl.pallas_call(
        body, grid=(m // tm, n // tn),
        in_specs=[pl.BlockSpec((tm, k), lambda i, j: (i, 0)), pl.BlockSpec((k, tn), lambda i, j: (0, j))],
        out_specs=pl.BlockSpec((tm, tn), lambda i, j: (i, j)),
        out_shape=jax.ShapeDtypeStruct((m, n), F32),
        compiler_params=pltpu.CompilerParams(dimension_semantics=("parallel", "parallel"),
                                             vmem_limit_bytes=VMEM_LIMIT),
        name=name)(a, b)


def _mm_nt(g, w, name):
    m, n = g.shape
    k, _ = w.shape
    tm = _pick(m, (1024, 512, 256, 128, 64, 32, 16))
    tn = _pick(n, (1024, 512, 256, 128))

    def body(g_ref, w_ref, o_ref):
        part = _dot_nt(g_ref[...].astype(BF16), w_ref[...])

        @pl.when(pl.program_id(1) == 0)
        def _():
            o_ref[...] = part

        @pl.when(pl.program_id(1) > 0)
        def _():
            o_ref[...] += part

    return pl.pallas_call(
        body, grid=(m // tm, n // tn),
        in_specs=[pl.BlockSpec((tm, tn), lambda i, j: (i, j)), pl.BlockSpec((k, tn), lambda i, j: (0, j))],
        out_specs=pl.BlockSpec((tm, k), lambda i, j: (i, 0)),
        out_shape=jax.ShapeDtypeStruct((m, k), F32),
        compiler_params=pltpu.CompilerParams(dimension_semantics=("parallel", "arbitrary"),
                                             vmem_limit_bytes=VMEM_LIMIT),
        name=name)(g, w)


def _mm_tn(a, g, name):
    m, k = a.shape
    _, n = g.shape
    tm = _pick(m, (1024, 512, 256, 128, 64, 32, 16))
    tn = _pick(n, (1024, 512, 256, 128))

    def body(a_ref, g_ref, o_ref):
        part = _dot_tn(a_ref[...], g_ref[...].astype(BF16))

        @pl.when(pl.program_id(1) == 0)
        def _():
            o_ref[...] = part

        @pl.when(pl.program_id(1) > 0)
        def _():
            o_ref[...] += part

    return pl.pallas_call(
        body, grid=(n // tn, m // tm),
        in_specs=[pl.BlockSpec((tm, k), lambda j, i: (i, 0)), pl.BlockSpec((tm, tn), lambda j, i: (i, j))],
        out_specs=pl.BlockSpec((k, tn), lambda j, i: (0, j)),
        out_shape=jax.ShapeDtypeStruct((k, n), F32),
        compiler_params=pltpu.CompilerParams(dimension_semantics=("parallel", "arbitrary"),
                                             vmem_limit_bytes=VMEM_LIMIT),
        name=name)(a, g)


SLABS = (("qk", O_QK, 2 * QK_A), ("va", O_VA, V_A), ("kv", O_KB, 2 * KV_B), ("oa", O_OA, V_A), ("za", O_ZA, V_A),
         ("qb", O_QB, Q_B), ("zb", O_ZB, Q_B), ("ga", O_GA, D_MODEL), ("gb", O_GB, D_MODEL))
SLAB_FWD_TN = 1024
SLAB_TN = 512


def _slab_blocks():
    return [(off // SLAB_TN, (off + width) // SLAB_TN) for _, off, width in SLABS]


def _proj_fwd_slab(a, w, off, width, name):
    m, k = a.shape
    tm = _pick(m, (1088, 512, 256, 128))
    tn = min(SLAB_FWD_TN, width)

    def body(a_ref, b_ref, o_ref):
        o_ref[...] = _dot(a_ref[...], b_ref[...])

    return pl.pallas_call(
        body, grid=(m // tm, width // tn),
        in_specs=[pl.BlockSpec((tm, k), lambda i, j: (i, 0)), pl.BlockSpec((k, tn), lambda i, j: (0, j + off // tn))],
        out_specs=pl.BlockSpec((tm, tn), lambda i, j: (i, j)),
        out_shape=jax.ShapeDtypeStruct((m, width), F32),
        compiler_params=pltpu.CompilerParams(dimension_semantics=("parallel", "parallel"),
                                             vmem_limit_bytes=VMEM_LIMIT),
        name=name)(a, w)


def _slab_spec(tm, blocks, rows_inner):
    b, e = blocks

    def index(r, c):
        inside = (c >= b) & (c < e)
        return jnp.where(inside, r, 0), jnp.clip(c - b, 0, e - b - 1)

    if rows_inner:
        return pl.BlockSpec((tm, SLAB_TN), lambda c, r: index(r, c))
    return pl.BlockSpec((tm, SLAB_TN), lambda r, c: index(r, c))


def _proj_da(gs, w, name):
    m = gs[0].shape[0]
    k, n = w.shape
    tm = _pick(m, (512, 256, 128))
    blocks = _slab_blocks()

    def body(*refs):
        g_refs, w_ref, o_ref = refs[:len(blocks)], refs[len(blocks)], refs[len(blocks) + 1]
        c = pl.program_id(1)

        @pl.when(c == 0)
        def _():
            o_ref[...] = jnp.zeros(o_ref.shape, F32)

        for g_ref, (b, e) in zip(g_refs, blocks):
            @pl.when((c >= b) & (c < e))
            def _(g_ref=g_ref):
                o_ref[...] += _dot_nt(g_ref[...].astype(BF16), w_ref[...])

    return pl.pallas_call(
        body, grid=(m // tm, n // SLAB_TN),
        in_specs=[_slab_spec(tm, blk, False) for blk in blocks] + [pl.BlockSpec((k, SLAB_TN), lambda r, c: (0, c))],
        out_specs=pl.BlockSpec((tm, k), lambda r, c: (r, 0)),
        out_shape=jax.ShapeDtypeStruct((m, k), F32),
        compiler_params=pltpu.CompilerParams(dimension_semantics=("parallel", "arbitrary"),
                                             vmem_limit_bytes=VMEM_LIMIT),
        name=name)(*gs, w)


def _proj_dw(a, gs, n, name):
    m, k = a.shape
    tm = _pick(m, (512, 256, 128))
    blocks = _slab_blocks()

    def body(*refs):
        a_ref, g_refs, o_ref = refs[0], refs[1:1 + len(blocks)], refs[1 + len(blocks)]
        c, r = pl.program_id(0), pl.program_id(1)

        @pl.when(r == 0)
        def _():
            o_ref[...] = jnp.zeros(o_ref.shape, F32)

        for g_ref, (b, e) in zip(g_refs, blocks):
            @pl.when((c >= b) & (c < e))
            def _(g_ref=g_ref):
                o_ref[...] += _dot_tn(a_ref[...], g_ref[...].astype(BF16))

    return pl.pallas_call(
        body, grid=(n // SLAB_TN, m // tm),
        in_specs=[pl.BlockSpec((tm, k), lambda c, r: (r, 0))] + [_slab_spec(tm, blk, True) for blk in blocks],
        out_specs=pl.BlockSpec((k, SLAB_TN), lambda c, r: (0, c)),
        out_shape=jax.ShapeDtypeStruct((k, n), F32),
        compiler_params=pltpu.CompilerParams(dimension_semantics=("parallel", "arbitrary"),
                                             vmem_limit_bytes=VMEM_LIMIT),
        name=name)(a, *gs)


LN_ROWS = 256


def _ln_stats(x):
    mu = jnp.mean(x, axis=-1, keepdims=True)
    xc = x - mu
    rstd = lax.rsqrt(jnp.mean(xc * xc, axis=-1, keepdims=True) + EPS)
    return xc * rstd, rstd


def _ln_bwd(dxh, xh, rstd):
    return rstd * (dxh - jnp.mean(dxh, axis=-1, keepdims=True) - xh * jnp.mean(dxh * xh, axis=-1, keepdims=True))


def _seg_maps(t, tc):
    cb, nb = tc // LN_ROWS, t // LN_ROWS
    ctx_blk = lambda i: jnp.where(i < cb, i, jnp.clip(i - cb - nb, 0, cb - 1))
    lat_blk = lambda i: jnp.clip(i - cb, 0, nb - 1)
    return cb, nb, ctx_blk, lat_blk


def _ln_mod_fwd(x, ctx, mod):
    t, tc = x.shape[0], ctx.shape[0]
    cb, nb, ctx_blk, lat_blk = _seg_maps(t, tc)

    def body(x_ref, c_ref, m_ref, o_ref):
        i = pl.program_id(0)
        lat = (i >= cb) & (i < cb + nb)

        @pl.when(lat)
        def _():
            xh, _ = _ln_stats(x_ref[...])
            o_ref[...] = (xh * (1 + m_ref[0:1, :]) + m_ref[1:2, :]).astype(BF16)

        @pl.when(jnp.logical_not(lat))
        def _():
            xh, _ = _ln_stats(c_ref[...])
            o_ref[...] = (xh * (1 + m_ref[2:3, :]) + m_ref[3:4, :]).astype(BF16)

    blk = lambda f: pl.BlockSpec((LN_ROWS, D_MODEL), lambda i: (f(i), 0))
    return pl.pallas_call(
        body, grid=(2 * cb + nb,),
        in_specs=[blk(lat_blk), blk(ctx_blk), pl.BlockSpec((4, D_MODEL), lambda i: (0, 0))],
        out_specs=pl.BlockSpec((LN_ROWS, D_MODEL), lambda i: (i, 0)),
        out_shape=jax.ShapeDtypeStruct((t + 2 * tc, D_MODEL), BF16),
        compiler_params=pltpu.CompilerParams(dimension_semantics=("parallel",), vmem_limit_bytes=VMEM_LIMIT),
        name="ln_mod")(x, ctx, mod)


def _ln_mod_bwd(du_a, du_b, x, ctx, mod):
    t, tc = x.shape[0], ctx.shape[0]
    cb, nb, ctx_blk, lat_blk = _seg_maps(t, tc)

    def body(da_ref, db_ref, x_ref, c_ref, m_ref, dx_ref, dm_ref):
        i = pl.program_id(0)
        lat = (i >= cb) & (i < cb + nb)

        @pl.when(i == 0)
        def _():
            dm_ref[...] = jnp.zeros(dm_ref.shape, F32)

        du = da_ref[...] + db_ref[...]

        @pl.when(lat)
        def _():
            xh, rstd = _ln_stats(x_ref[...])
            dm_ref[0:1, :] += jnp.sum(du * xh, axis=0, keepdims=True)
            dm_ref[1:2, :] += jnp.sum(du, axis=0, keepdims=True)
            dx_ref[...] = _ln_bwd(du * (1 + m_ref[0:1, :]), xh, rstd)

        @pl.when(jnp.logical_not(lat))
        def _():
            xh, _ = _ln_stats(c_ref[...])
            dm_ref[2:3, :] += jnp.sum(du * xh, axis=0, keepdims=True)
            dm_ref[3:4, :] += jnp.sum(du, axis=0, keepdims=True)

    blk = lambda f: pl.BlockSpec((LN_ROWS, D_MODEL), lambda i: (f(i), 0))
    row = pl.BlockSpec((LN_ROWS, D_MODEL), lambda i: (i, 0))
    vec = pl.BlockSpec((4, D_MODEL), lambda i: (0, 0))
    return pl.pallas_call(
        body, grid=(2 * cb + nb,),
        in_specs=[row, row, blk(lat_blk), blk(ctx_blk), vec],
        out_specs=[blk(lat_blk), vec],
        out_shape=[jax.ShapeDtypeStruct((t, D_MODEL), F32), jax.ShapeDtypeStruct((4, D_MODEL), F32)],
        compiler_params=pltpu.CompilerParams(dimension_semantics=("arbitrary",), vmem_limit_bytes=VMEM_LIMIT),
        name="ln_mod_bwd")(du_a, du_b, x, ctx, mod)


@jax.custom_vjp
def _ln_project(x, ctx, mod, w_main, w_if, pr_main, pr_if):
    return _ln_project_fwd(x, ctx, mod, w_main, w_if, pr_main, pr_if)[0]


def _ln_project_fwd(x, ctx, mod, w_main, w_if, pr_main, pr_if):
    del pr_main, pr_if
    ub = _ln_mod_fwd(x, ctx, mod)
    slabs = tuple(_proj_fwd_slab(ub, w_main, off, width, "proj_" + nm) for nm, off, width in SLABS)
    return (slabs, _mm_nn(ub, w_if, "proj_if")), (x, ctx, mod, ub, w_main, w_if)


def _ln_project_bwd(res, cot):
    x, ctx, mod, ub, w_main, w_if = res
    gs, g_if = cot
    dx, dmod = _ln_mod_bwd(_proj_da(gs, w_main, "proj_da"), _mm_nt(g_if, w_if, "proj_if_da"), x, ctx, mod)
    return (dx, jnp.zeros_like(ctx), dmod, jnp.zeros_like(w_main), jnp.zeros_like(w_if),
            _proj_dw(ub, gs, w_main.shape[1], "proj_dw"), _mm_tn(ub, g_if, "proj_if_dw"))


_ln_project.defvjp(_ln_project_fwd, _ln_project_bwd)


def _head_fwd(x, out, target, vecs):
    t = x.shape[0]

    def body(x_ref, o_ref, t_ref, v_ref, l_ref):
        @pl.when(pl.program_id(0) == 0)
        def _():
            l_ref[...] = jnp.zeros(l_ref.shape, F32)

        rh, _ = _ln_stats(ALPHA * x_ref[...] + v_ref[0:1, :] * o_ref[...])
        err = rh * v_ref[1:2, :] + v_ref[2:3, :] - t_ref[...]
        part = jnp.sum(jnp.mean(err * err, axis=-1, keepdims=True), axis=0, keepdims=True)
        l_ref[...] += 0.5 * part

    row = pl.BlockSpec((LN_ROWS, D_MODEL), lambda i: (i, 0))
    return pl.pallas_call(
        body, grid=(t // LN_ROWS,),
        in_specs=[row, row, row, pl.BlockSpec((3, D_MODEL), lambda i: (0, 0))],
        out_specs=pl.BlockSpec((1, 128), lambda i: (0, 0)),
        out_shape=jax.ShapeDtypeStruct((1, 128), F32),
        compiler_params=pltpu.CompilerParams(dimension_semantics=("arbitrary",), vmem_limit_bytes=VMEM_LIMIT),
        name="loss_head")(x, out, target, vecs)


def _head_bwd(g, x, out, target, vecs):
    t = x.shape[0]

    def body(g_ref, x_ref, o_ref, t_ref, v_ref, dx_ref, do_ref, dv_ref):
        @pl.when(pl.program_id(0) == 0)
        def _():
            dv_ref[...] = jnp.zeros(dv_ref.shape, F32)

        o = o_ref[...]
        gate, ln_w = v_ref[0:1, :], v_ref[1:2, :]
        rh, rstd = _ln_stats(ALPHA * x_ref[...] + gate * o)
        dy = (rh * ln_w + v_ref[2:3, :] - t_ref[...]) * (g_ref[0:1, 0:1] * (1.0 / D_MODEL))
        dv_ref[1:2, :] += jnp.sum(dy * rh, axis=0, keepdims=True)
        dv_ref[2:3, :] += jnp.sum(dy, axis=0, keepdims=True)
        dr = _ln_bwd(dy * ln_w, rh, rstd)
        dv_ref[0:1, :] += jnp.sum(dr * o, axis=0, keepdims=True)
        dx_ref[...] = ALPHA * dr
        do_ref[...] = gate * dr

    row = pl.BlockSpec((LN_ROWS, D_MODEL), lambda i: (i, 0))
    vec = pl.BlockSpec((3, D_MODEL), lambda i: (0, 0))
    return pl.pallas_call(
        body, grid=(t // LN_ROWS,),
        in_specs=[pl.BlockSpec((1, 128), lambda i: (0, 0)), row, row, row, vec],
        out_specs=[row, row, vec],
        out_shape=[jax.ShapeDtypeStruct((t, D_MODEL), F32), jax.ShapeDtypeStruct((t, D_MODEL), F32),
                   jax.ShapeDtypeStruct((3, D_MODEL), F32)],
        compiler_params=pltpu.CompilerParams(dimension_semantics=("arbitrary",), vmem_limit_bytes=VMEM_LIMIT),
        name="loss_head_bwd")(g, x, out, target, vecs)


@jax.custom_vjp
def _loss_head(x, out, target, gate, ln_w, ln_b):
    return _head_fwd(x, out, target, jnp.stack([gate, ln_w, ln_b]))[0, 0]


def _loss_head_fwd(x, out, target, gate, ln_w, ln_b):
    vecs = jnp.stack([gate, ln_w, ln_b])
    return _head_fwd(x, out, target, vecs)[0, 0], (x, out, target, vecs)


def _loss_head_bwd(res, g):
    x, out, target, vecs = res
    dx, dout, dv = _head_bwd(jnp.full((1, 128), g, F32), x, out, target, vecs)
    return dx, dout, jnp.zeros_like(target), dv[0], dv[1], dv[2]


_loss_head.defvjp(_loss_head_fwd, _loss_head_bwd)


ATT_SCALE = HD_B ** -0.5
GROUP = NH_B // NKV_B


LOG2E, LN2 = 1.4426950408889634, 0.6931471805599453
ATT_C = ATT_SCALE * LOG2E
STRIP_Q, STRIP_K = 128, 256


def _attn_tiles(t, n):
    return _pick(t, (512, 256, 128)), _pick(n, (768, 512, 256))


def _attn_fwd(q, k, v):
    t, n = q.shape[0], k.shape[0]
    tq, tk = _pick(t, (1024, 512, 256, 128)), _attn_tiles(t, n)[1]
    nk = n // tk

    def body(q_ref, k_ref, v_ref, o_ref, lse_ref, m_sc, acc_sc):
        j = pl.program_id(2)

        @pl.when(j == 0)
        def _():
            m_sc[...] = jnp.full(m_sc.shape, -jnp.inf, F32)
            acc_sc[...] = jnp.zeros(acc_sc.shape, F32)

        kb = k_ref[...]
        v_ones = jnp.concatenate([v_ref[...], jnp.ones((tk, HD_B), BF16)], axis=1)
        for g in range(GROUP):
            s2 = _dot_nt(q_ref[:, g * HD_B:(g + 1) * HD_B], kb) * ATT_C
            m_prev = m_sc[g]
            m_new = jnp.maximum(m_prev, jnp.max(s2, axis=-1, keepdims=True))
            p = jnp.exp2(s2 - m_new).astype(BF16)
            acc_sc[g] = jnp.exp2(m_prev - m_new) * acc_sc[g] + _dot(p, v_ones)
            m_sc[g] = m_new

        @pl.when(j == nk - 1)
        def _():
            for g in range(GROUP):
                cols = slice(g * HD_B, (g + 1) * HD_B)
                l = acc_sc[g, :, HD_B:]
                o_ref[:, cols] = acc_sc[g, :, :HD_B] / l
                lse_ref[:, cols] = m_sc[g] + jnp.log(l) * LOG2E

    qspec = pl.BlockSpec((tq, GROUP * HD_B), lambda kh, i, j: (i, kh))
    kspec = pl.BlockSpec((tk, HD_B), lambda kh, i, j: (j, kh))
    return pl.pallas_call(
        body, grid=(NKV_B, t // tq, nk),
        in_specs=[qspec, kspec, kspec], out_specs=[qspec, qspec],
        out_shape=[jax.ShapeDtypeStruct((t, Q_B), F32), jax.ShapeDtypeStruct((t, Q_B), F32)],
        scratch_shapes=[pltpu.VMEM((GROUP, tq, 1), F32), pltpu.VMEM((GROUP, tq, 2 * HD_B), F32)],
        compiler_params=pltpu.CompilerParams(dimension_semantics=("parallel", "parallel", "arbitrary"),
                                             vmem_limit_bytes=VMEM_LIMIT),
        name="attn_fwd")(q, k, v)


def _attn_dq(q, k, v, do, lse, delta):
    t, n = q.shape[0], k.shape[0]
    tq, tk = _pick(t, (1024, 512, 256, 128)), _attn_tiles(t, n)[1]
    nk = n // tk

    def body(q_ref, k_ref, v_ref, do_ref, lse_ref, dl_ref, dq_ref):
        j = pl.program_id(2)
        kb, vb = k_ref[...], v_ref[...]
        parts = []
        for g in range(GROUP):
            cols = slice(g * HD_B, (g + 1) * HD_B)
            p = jnp.exp2(_dot_nt(q_ref[:, cols], kb) * ATT_C - lse_ref[:, g * HD_B:g * HD_B + 1])
            dp = _dot_nt(do_ref[:, cols], vb)
            ds = p * (dp - dl_ref[:, g * HD_B:g * HD_B + 1])
            parts.append(_dot(ds.astype(BF16), kb))

        @pl.when(j == 0)
        def _():
            for g in range(GROUP):
                dq_ref[:, g * HD_B:(g + 1) * HD_B] = parts[g]

        @pl.when(j > 0)
        def _():
            for g in range(GROUP):
                dq_ref[:, g * HD_B:(g + 1) * HD_B] += parts[g]

        @pl.when(j == nk - 1)
        def _():
            dq_ref[...] = dq_ref[...] * ATT_SCALE

    qspec = pl.BlockSpec((tq, GROUP * HD_B), lambda kh, i, j: (i, kh))
    kspec = pl.BlockSpec((tk, HD_B), lambda kh, i, j: (j, kh))
    return pl.pallas_call(
        body, grid=(NKV_B, t // tq, n // tk),
        in_specs=[qspec, kspec, kspec, qspec, qspec, qspec],
        out_specs=qspec,
        out_shape=jax.ShapeDtypeStruct((t, Q_B), F32),
        compiler_params=pltpu.CompilerParams(dimension_semantics=("parallel", "parallel", "arbitrary"),
                                             vmem_limit_bytes=VMEM_LIMIT),
        name="attn_dq")(q, k, v, do, lse, delta)


def _attn_dkv(q, k, v, do, lse_t, delta_t):
    t, n = q.shape[0], k.shape[0]
    tq, tk = _attn_tiles(t, n)
    nq = t // tq
    n_r, n_c = tq // STRIP_Q, tk // STRIP_K

    def body(q_ref, k_ref, v_ref, do_ref, lse_ref, dl_ref, dk_ref, dv_ref, dk_sc, dv_sc):
        i = pl.program_id(2)

        @pl.when(i == 0)
        def _():
            dk_sc[...] = jnp.zeros(dk_sc.shape, F32)
            dv_sc[...] = jnp.zeros(dv_sc.shape, F32)

        for r in range(n_r):
            rows = slice(r * STRIP_Q, (r + 1) * STRIP_Q)
            for c in range(n_c):
                kv = slice(c * STRIP_K, (c + 1) * STRIP_K)
                kc, vc = k_ref[kv, :], v_ref[kv, :]
                dk_part = dv_part = None
                for g in range(GROUP):
                    cols = slice(g * HD_B, (g + 1) * HD_B)
                    qg, dog = q_ref[rows, cols], do_ref[rows, cols]
                    st = _dot_nt(kc, qg)
                    pt = jnp.exp2(st * ATT_C - lse_ref[8 * g:8 * g + 1, rows])
                    dvg = _dot(pt.astype(BF16), dog)
                    dpt = _dot_nt(vc, dog)
                    dst = pt * (dpt - dl_ref[8 * g:8 * g + 1, rows])
                    dkg = _dot(dst.astype(BF16), qg)
                    dk_part = dkg if dk_part is None else dk_part + dkg
                    dv_part = dvg if dv_part is None else dv_part + dvg
                dk_sc[kv, :] += dk_part
                dv_sc[kv, :] += dv_part

        @pl.when(i == nq - 1)
        def _():
            dk_ref[...] = dk_sc[...] * ATT_SCALE
            dv_ref[...] = dv_sc[...]

    qspec = pl.BlockSpec((tq, GROUP * HD_B), lambda kh, j, i: (i, kh))
    tspec = pl.BlockSpec((8 * GROUP, tq), lambda kh, j, i: (kh, i))
    kspec = pl.BlockSpec((tk, HD_B), lambda kh, j, i: (j, kh))
    return pl.pallas_call(
        body, grid=(NKV_B, n // tk, nq),
        in_specs=[qspec, kspec, kspec, qspec, tspec, tspec],
        out_specs=[kspec, kspec],
        out_shape=[jax.ShapeDtypeStruct((n, KV_B), F32), jax.ShapeDtypeStruct((n, KV_B), F32)],
        scratch_shapes=[pltpu.VMEM((tk, HD_B), F32), pltpu.VMEM((tk, HD_B), F32)],
        compiler_params=pltpu.CompilerParams(dimension_semantics=("parallel", "parallel", "arbitrary"),
                                             vmem_limit_bytes=VMEM_LIMIT),
        name="attn_dkv")(q, k, v, do, lse_t, delta_t)


def _attention_bwd(res, do):
    qb, kb, vb, o, lse = res
    t = qb.shape[0]
    delta = jnp.sum((do * o).reshape(t, NH_B, HD_B), axis=-1)
    lse_h = lse.reshape(t, NH_B, HD_B)[:, :, 0]
    delta_b = jnp.broadcast_to(delta[:, :, None], (t, NH_B, HD_B)).reshape(t, Q_B)
    lse_t = jnp.broadcast_to(lse_h.T[:, None, :], (NH_B, 8, t)).reshape(NH_B * 8, t)
    delta_t = jnp.broadcast_to(delta.T[:, None, :], (NH_B, 8, t)).reshape(NH_B * 8, t)
    dob = do.astype(BF16)
    dq = _attn_dq(qb, kb, vb, dob, lse, delta_b)
    dk, dv = _attn_dkv(qb, kb, vb, dob, lse_t, delta_t)
    return dq, dk, dv


def _swap32(y):
    lane = lax.broadcasted_iota(jnp.int32, y.shape, 1)
    return jnp.where((lane // 32) % 2 == 0, pltpu.roll(y, 96, 1), pltpu.roll(y, 32, 1))


ROPE_ROWS = 256


def _norm_rope_fwd(x, w, cos, sin, row_off, rows, heads, name):
    tr, off, width = ROPE_ROWS, row_off // ROPE_ROWS, heads * HD_B

    def body(x_ref, w_ref, c_ref, s_ref, o_ref):
        w, c, s = w_ref[...], c_ref[...], s_ref[...]
        for h in range(heads):
            cols = slice(h * HD_B, (h + 1) * HD_B)
            xh = x_ref[:, cols]
            y = xh * lax.rsqrt(jnp.mean(xh * xh, axis=-1, keepdims=True) + EPS) * w
            o_ref[:, cols] = (y * c + _swap32(y) * s).astype(o_ref.dtype)

    row = pl.BlockSpec((tr, width), lambda i: (i, 0))
    tab = pl.BlockSpec((tr, HD_B), lambda i: (i, 0))
    return pl.pallas_call(
        body, grid=(rows // tr,),
        in_specs=[pl.BlockSpec((tr, width), lambda i: (i + off, 0)), pl.BlockSpec((1, HD_B), lambda i: (0, 0)), tab, tab],
        out_specs=row, out_shape=jax.ShapeDtypeStruct((rows, width), BF16),
        compiler_params=pltpu.CompilerParams(dimension_semantics=("parallel",), vmem_limit_bytes=VMEM_LIMIT),
        name=name)(x, w, cos, sin)


def _norm_rope_bwd(x, w, cos, sin, dy, row_off, extra, name):
    rows, width = dy.shape
    heads = width // HD_B
    r, full = x.shape
    tr, off, nb = ROPE_ROWS, row_off // ROPE_ROWS, rows // ROPE_ROWS

    def body(*refs):
        x_ref, w_ref, c_ref, s_ref, dy_ref = refs[:5]
        e_ref = refs[5] if extra is not None else None
        dx_ref, dw_ref = refs[-2], refs[-1]
        i = pl.program_id(0)

        @pl.when(i == 0)
        def _():
            dw_ref[...] = jnp.zeros(dw_ref.shape, F32)

        @pl.when((i >= off) & (i < off + nb))
        def _():
            w, c, s = w_ref[...], c_ref[...], s_ref[...]
            dw = jnp.zeros((1, HD_B), F32)
            for h in range(heads):
                cols = slice(h * HD_B, (h + 1) * HD_B)
                xh, dyh = x_ref[:, cols], dy_ref[:, cols]
                rs = lax.rsqrt(jnp.mean(xh * xh, axis=-1, keepdims=True) + EPS)
                dn = dyh * c + _swap32(dyh * s)
                dw = dw + jnp.sum(dn * (xh * rs), axis=0, keepdims=True)
                dxn = dn * w
                dx_ref[:, cols] = rs * dxn - xh * (rs * rs * rs * jnp.mean(dxn * xh, axis=-1, keepdims=True))
            if e_ref is not None:
                dx_ref[:, width:] = e_ref[...]
            dw_ref[...] += dw

        @pl.when((i < off) | (i >= off + nb))
        def _():
            dx_ref[...] = jnp.zeros(dx_ref.shape, F32)

    inner = lambda i: jnp.clip(i - off, 0, nb - 1)
    tab = pl.BlockSpec((tr, HD_B), lambda i: (inner(i), 0))
    vec = pl.BlockSpec((1, HD_B), lambda i: (0, 0))
    in_specs = [pl.BlockSpec((tr, width), lambda i: (i, 0)), vec, tab, tab,
                pl.BlockSpec((tr, width), lambda i: (inner(i), 0))]
    args = [x, w, cos, sin, dy]
    if extra is not None:
        in_specs.append(pl.BlockSpec((tr, full - width), lambda i: (inner(i), 0)))
        args.append(extra)
    return pl.pallas_call(
        body, grid=(r // tr,), in_specs=in_specs,
        out_specs=[pl.BlockSpec((tr, full), lambda i: (i, 0)), vec],
        out_shape=[jax.ShapeDtypeStruct((r, full), F32), jax.ShapeDtypeStruct((1, HD_B), F32)],
        compiler_params=pltpu.CompilerParams(dimension_semantics=("arbitrary",), vmem_limit_bytes=VMEM_LIMIT),
        name=name)(*args)


def _rope_tables(t):
    pos = jnp.arange(t)
    row = (pos // GRID_W).astype(F32)
    col = (pos % GRID_W).astype(F32)
    inv = ROPE_THETA ** (-jnp.arange(0, ROT_HALF, 2, dtype=F32) / ROT_HALF)
    ar, ac = row[:, None] * inv[None], col[:, None] * inv[None]
    cos = jnp.concatenate([jnp.cos(ar), jnp.cos(ar), jnp.cos(ac), jnp.cos(ac)], -1)
    sin = jnp.concatenate([-jnp.sin(ar), jnp.sin(ar), -jnp.sin(ac), jnp.sin(ac)], -1)
    return cos, sin


def _gqa_tables(t, n):
    cos, sin = _rope_tables(t)
    cos_k = jnp.concatenate([jnp.ones((n - t, HD_B), F32), cos], 0)
    sin_k = jnp.concatenate([jnp.zeros((n - t, HD_B), F32), sin], 0)
    return cos, sin, cos_k, sin_k


def _make_gqa(t, tc):
    n = tc + t

    @jax.custom_vjp
    def gqa(p_qb, p_kv, qw, kw):
        return fwd(p_qb, p_kv, qw, kw)[0]

    def fwd(p_qb, p_kv, qw, kw):
        cos, sin, cos_k, sin_k = _gqa_tables(t, n)
        q = _norm_rope_fwd(p_qb, qw[None], cos, sin, tc, t, NH_B, "q_norm_rope")
        k = _norm_rope_fwd(p_kv, kw[None], cos_k, sin_k, 0, n, NKV_B, "k_norm_rope")
        vb = p_kv[:n, KV_B:].astype(BF16)
        o, lse = _attn_fwd(q, k, vb)
        return o, (p_qb, p_kv, qw, kw, q, k, vb, o, lse)

    def bwd(res, do):
        p_qb, p_kv, qw, kw, q, k, vb, o, lse = res
        cos, sin, cos_k, sin_k = _gqa_tables(t, n)
        dq, dk, dv = _attention_bwd((q, k, vb, o, lse), do)
        d_qb, dqw = _norm_rope_bwd(p_qb, qw[None], cos, sin, dq, tc, None, "q_norm_rope_bwd")
        d_kv, dkw = _norm_rope_bwd(p_kv, kw[None], cos_k, sin_k, dk, 0, dv, "k_norm_rope_bwd")
        return d_qb, d_kv, dqw[0], dkw[0]

    gqa.defvjp(fwd, bwd)
    return gqa


def _mlstm_chunk_forward(q, k, v, lir, f_pre, s0, n0, m0, reverse):
    L = q.shape[0]
    lfr = jnp.minimum(f_pre, 0.0) - jnp.log1p(jnp.exp(-jnp.abs(f_pre)))
    ti = lax.broadcasted_iota(jnp.int32, (L, L), 0)
    si = lax.broadcasted_iota(jnp.int32, (L, L), 1)
    seen = (si >= ti) if reverse else (si <= ti)
    seen_t = (ti >= si) if reverse else (ti <= si)
    eye = ti == si
    lic = jnp.sum(jnp.where(eye, lir, 0.0), axis=1, keepdims=True)
    lfc = jnp.sum(jnp.where(eye, lfr, 0.0), axis=1, keepdims=True)
    b_col = jnp.sum(jnp.where(seen, lfr, 0.0), axis=1, keepdims=True)
    b_row = jnp.sum(jnp.where(seen_t, lfc, 0.0), axis=0, keepdims=True)
    d = jnp.where(seen, b_col - b_row + lir, -jnp.inf)
    m = jnp.maximum(b_col + m0, jnp.max(d, axis=1, keepdims=True))
    w = jnp.exp(d - m)
    a = jnp.exp(b_col + m0 - m)
    qm, km, vm = q.astype(MX), k.astype(MX), v.astype(MX)
    s = _dot_nt(qm, km) * w
    qs = _dot(qm, s0.astype(MX))
    num = a * qs + _dot(s.astype(MX), vm)
    qn = jnp.sum(q * n0, axis=1, keepdims=True)
    den = a * qn + jnp.sum(s, axis=1, keepdims=True)
    floor = jnp.exp(-m)
    dd = jnp.maximum(jnp.abs(den), floor)
    b_last = jnp.sum(lfr, axis=1, keepdims=True)
    m_end = jnp.maximum(b_last + m0, jnp.max(b_last - b_row + lir, axis=1, keepdims=True))
    w_end = jnp.exp(b_last - b_col + lic - m_end)
    a_end = jnp.exp(b_last + m0 - m_end)
    return dict(eye=eye, seen=seen, w=w, a=a, s=s, qs=qs, num=num, qn=qn, den=den, floor=floor, dd=dd,
                m_end=m_end, w_end=w_end, a_end=a_end, qm=qm, km=km, vm=vm)


def _mlstm_fwd_call(q, k, v, gr, n, row_off, reverse):
    L = MLSTM_CHUNK
    nc, off = n // L, row_off // L
    pos = (lambda i: nc - 1 - i) if reverse else (lambda i: i)

    def body(q_ref, k_ref, v_ref, gr_ref, h_ref, s0_ref, n0_ref, m0_ref, s_sc, n_sc, m_sc):
        @pl.when(pl.program_id(1) == 0)
        def _():
            s_sc[...] = jnp.zeros(s_sc.shape, F32)
            n_sc[...] = jnp.zeros(n_sc.shape, F32)
            m_sc[...] = jnp.full(m_sc.shape, M_INIT, F32)

        s0, n0, m0 = s_sc[...], n_sc[...], m_sc[...]
        s0_ref[0, 0] = s0
        n0_ref[0, 0] = n0
        m0_ref[0, 0] = jnp.broadcast_to(m0, (1, DK_A))
        k, v = k_ref[...], v_ref[...]
        f = _mlstm_chunk_forward(q_ref[...], k, v, gr_ref[0, 0], gr_ref[1, 0], s0, n0, m0, reverse)
        h_ref[...] = f["num"] / f["dd"]
        s_sc[...] = f["a_end"] * s0 + _dot_tn(f["km"], (f["w_end"] * v).astype(MX))
        n_sc[...] = f["a_end"] * n0 + jnp.sum(f["w_end"] * k, axis=0, keepdims=True)
        m_sc[...] = f["m_end"]

    qk_spec = pl.BlockSpec((L, DK_A), lambda h, i: (off + pos(i), h))
    v_spec = pl.BlockSpec((L, DV_A), lambda h, i: (off + pos(i), h))
    gr_spec = pl.BlockSpec((2, 1, 1, L), lambda h, i: (0, h, 0, off + pos(i)))
    h_spec = pl.BlockSpec((L, DV_A), lambda h, i: (pos(i), h))
    st_spec = pl.BlockSpec((1, 1, DK_A, DV_A), lambda h, i: (h, pos(i), 0, 0))
    vec_spec = pl.BlockSpec((1, 1, 1, DK_A), lambda h, i: (h, pos(i), 0, 0))
    return pl.pallas_call(
        body, grid=(NH_A, nc),
        in_specs=[qk_spec, qk_spec, v_spec, gr_spec],
        out_specs=[h_spec, st_spec, vec_spec, vec_spec],
        out_shape=[jax.ShapeDtypeStruct((n, V_A), F32), jax.ShapeDtypeStruct((NH_A, nc, DK_A, DV_A), F32),
                   jax.ShapeDtypeStruct((NH_A, nc, 1, DK_A), F32), jax.ShapeDtypeStruct((NH_A, nc, 1, DK_A), F32)],
        scratch_shapes=[pltpu.VMEM((DK_A, DV_A), F32), pltpu.VMEM((1, DK_A), F32), pltpu.VMEM((1, 1), F32)],
        compiler_params=pltpu.CompilerParams(dimension_semantics=("parallel", "arbitrary"),
                                             vmem_limit_bytes=VMEM_LIMIT),
        name="mlstm_fwd")(q, k, v, gr)


def _mlstm_bwd_call(q, k, v, gr, s0_all, n0_all, m0_all, dh, n, row_off, reverse):
    L = MLSTM_CHUNK
    nc, off = n // L, row_off // L
    pos = (lambda i: i) if reverse else (lambda i: nc - 1 - i)

    def body(q_ref, k_ref, v_ref, gr_ref, s0_ref, n0_ref, m0_ref, dh_ref,
             dq_ref, dk_ref, dv_ref, dg_ref, ds_sc, dn_sc):
        @pl.when(pl.program_id(1) == 0)
        def _():
            ds_sc[...] = jnp.zeros(ds_sc.shape, F32)
            dn_sc[...] = jnp.zeros(dn_sc.shape, F32)

        q, k, v = q_ref[...], k_ref[...], v_ref[...]
        s0, n0, m0 = s0_ref[0, 0], n0_ref[0, 0], m0_ref[0, 0][:, 0:1]
        f = _mlstm_chunk_forward(q, k, v, gr_ref[0, 0], gr_ref[1, 0], s0, n0, m0, reverse)
        w, a, s = f["w"], f["a"], f["s"]
        qm, km, vm, w_end, a_end = f["qm"], f["km"], f["vm"], f["w_end"], f["a_end"]
        ds1, dn1 = ds_sc[...], dn_sc[...]
        ds1m, s0m = ds1.astype(MX), s0.astype(MX)

        inv = 1.0 / f["dd"]
        dh = dh_ref[...]
        dnum = dh * inv
        ddd = -jnp.sum(dh * (f["num"] * inv), axis=1, keepdims=True) * inv
        dden = jnp.where(jnp.abs(f["den"]) > f["floor"], jnp.sign(f["den"]) * ddd, 0.0)
        adn = (a * dnum).astype(MX)
        dnm = dnum.astype(MX)
        ds_tot = _dot_nt(dnm, vm) + dden
        dsr = (ds_tot * w).astype(MX)
        e = ds_tot * s
        wv = (w_end * v).astype(MX)
        kds = _dot(km, ds1m)
        dq_ref[...] = _dot_nt(adn, s0m) + _dot(dsr, km) + (dden * a) * n0
        dk_ref[...] = _dot_tn(dsr, qm) + _dot_nt(wv, ds1m) + w_end * dn1
        dv_ref[...] = _dot_tn(s.astype(MX), dnm) + w_end * kds

        eye = f["eye"]
        to_col = lambda r: jnp.sum(jnp.where(eye, r, 0.0), axis=1, keepdims=True)
        to_row = lambda c: jnp.sum(jnp.where(eye, c, 0.0), axis=0, keepdims=True)
        g_a = (jnp.sum(dnum * f["qs"], axis=1, keepdims=True) + dden * f["qn"]) * a
        g_w = (jnp.sum(v * kds, axis=1, keepdims=True) + jnp.sum(k * dn1, axis=1, keepdims=True)) * w_end
        g_end = (jnp.sum(jnp.sum(ds1 * s0, axis=1, keepdims=True), axis=0, keepdims=True)
                 + jnp.sum(dn1 * n0, axis=1, keepdims=True)) * a_end
        col_e = jnp.sum(e, axis=0, keepdims=True)
        db = jnp.sum(e, axis=1, keepdims=True) - to_col(col_e) + g_a - g_w
        last = lax.broadcasted_iota(jnp.int32, (L, 1), 0) == (0 if reverse else L - 1)
        db = db + jnp.where(last, jnp.sum(g_w, axis=0, keepdims=True) + g_end, 0.0)
        dg_ref[0, 0] = col_e + to_row(g_w)
        dlf = jnp.sum(jnp.where(f["seen"], db, 0.0), axis=0, keepdims=True)
        dg_ref[1, 0] = dlf * jax.nn.sigmoid(-gr_ref[1, 0])

        ds_sc[...] = a_end * ds1 + _dot_tn(qm, adn)
        dn_sc[...] = a_end * dn1 + jnp.sum((dden * a) * q, axis=0, keepdims=True)

    qk_spec = pl.BlockSpec((L, DK_A), lambda h, i: (off + pos(i), h))
    v_spec = pl.BlockSpec((L, DV_A), lambda h, i: (off + pos(i), h))
    gr_spec = pl.BlockSpec((2, 1, 1, L), lambda h, i: (0, h, 0, off + pos(i)))
    st_spec = pl.BlockSpec((1, 1, DK_A, DV_A), lambda h, i: (h, pos(i), 0, 0))
    vec_spec = pl.BlockSpec((1, 1, 1, DK_A), lambda h, i: (h, pos(i), 0, 0))
    oqk_spec = pl.BlockSpec((L, DK_A), lambda h, i: (pos(i), h))
    ov_spec = pl.BlockSpec((L, DV_A), lambda h, i: (pos(i), h))
    og_spec = pl.BlockSpec((2, 1, 1, L), lambda h, i: (0, h, 0, pos(i)))
    return pl.pallas_call(
        body, grid=(NH_A, nc),
        in_specs=[qk_spec, qk_spec, v_spec, gr_spec, st_spec, vec_spec, vec_spec, ov_spec],
        out_specs=[oqk_spec, oqk_spec, ov_spec, og_spec],
        out_shape=[jax.ShapeDtypeStruct((n, QK_A), F32), jax.ShapeDtypeStruct((n, QK_A), F32),
                   jax.ShapeDtypeStruct((n, V_A), F32), jax.ShapeDtypeStruct((2, NH_A, 1, n), F32)],
        scratch_shapes=[pltpu.VMEM((DK_A, DV_A), F32), pltpu.VMEM((1, DK_A), F32)],
        compiler_params=pltpu.CompilerParams(dimension_semantics=("parallel", "arbitrary"),
                                             vmem_limit_bytes=VMEM_LIMIT),
        name="mlstm_bwd")(q, k, v, gr, s0_all, n0_all, m0_all, dh)


def _make_mlstm(n, row_off, reverse):
    def gate_rows(li, lf):
        return jnp.stack([li, lf]).transpose(0, 2, 1)[:, :, None, :]

    @jax.custom_vjp
    def op(q, k, v, li, lf):
        return _mlstm_fwd_call(q, k, v, gate_rows(li, lf), n, row_off, reverse)[0]

    def fwd(q, k, v, li, lf):
        gr = gate_rows(li, lf)
        h, s0, n0, m0 = _mlstm_fwd_call(q, k, v, gr, n, row_off, reverse)
        return h, (q, k, v, gr, s0, n0, m0)

    def bwd(res, dh):
        q, k, v, gr, s0, n0, m0 = res
        dq, dk, dv, dg = _mlstm_bwd_call(q, k, v, gr, s0, n0, m0, dh, n, row_off, reverse)
        rows = ((row_off, q.shape[0] - row_off - n), (0, 0))
        dg = jnp.pad(dg[:, :, 0, :].transpose(0, 2, 1), ((0, 0),) + rows)
        return jnp.pad(dq, rows), jnp.pad(dk, rows), jnp.pad(dv, rows), dg[0], dg[1]

    op.defvjp(fwd, bwd)
    return op


MERGE_ROWS = 128


def _sig(x):
    return jax.nn.sigmoid(x)


def _merge_pre_fwd(h_f, h_b, o_attn, p_oa, p_za, p_zb, mh_w, tc):
    t = o_attn.shape[0]
    tr, off = MERGE_ROWS, tc // MERGE_ROWS

    def body(hf_ref, hb_ref, oat_ref, oa_ref, za_ref, zb_ref, w_ref, a_ref, b_ref):
        for hd in range(NH_A):
            cols = slice(hd * DV_A, (hd + 1) * DV_A)
            h = hf_ref[:, cols] + hb_ref[:, cols]
            hn = h * lax.rsqrt(jnp.mean(h * h, axis=-1, keepdims=True) + EPS) * w_ref[:, cols]
            za = za_ref[:, cols]
            a_ref[:, cols] = (_sig(oa_ref[:, cols]) * hn * (za * _sig(za))).astype(BF16)
        zb = zb_ref[...]
        b_ref[...] = (oat_ref[...] * (zb * _sig(zb))).astype(BF16)

    lat = pl.BlockSpec((tr, V_A), lambda i: (i + off, 0))
    row = pl.BlockSpec((tr, V_A), lambda i: (i, 0))
    return pl.pallas_call(
        body, grid=(t // tr,),
        in_specs=[lat, row, row, lat, lat, lat, pl.BlockSpec((1, V_A), lambda i: (0, 0))],
        out_specs=[row, row],
        out_shape=[jax.ShapeDtypeStruct((t, V_A), BF16), jax.ShapeDtypeStruct((t, V_A), BF16)],
        compiler_params=pltpu.CompilerParams(dimension_semantics=("parallel",), vmem_limit_bytes=VMEM_LIMIT),
        name="merge_pre")(h_f, h_b, o_attn, p_oa, p_za, p_zb, mh_w)


def _ctx_block(i, nb, off):
    k = i - nb
    return jnp.where(i < nb, i + off, jnp.where(k < off, k, k + nb))


def _merge_pre_bwd(da, db, h_f, h_b, o_attn, p_oa, p_za, p_zb, mh_w, tc):
    t = o_attn.shape[0]
    n, r = h_f.shape[0], p_oa.shape[0]
    tr, off = MERGE_ROWS, tc // MERGE_ROWS
    nb = t // tr
    n_ctx = r // tr - nb

    def body(da_ref, db_ref, hf_ref, hb_ref, oat_ref, oa_ref, za_ref, zb_ref, w_ref,
             dhf_ref, dhb_ref, doat_ref, doa_ref, dza_ref, dzb_ref, dw_ref):
        i = pl.program_id(0)

        @pl.when(i == 0)
        def _():
            dw_ref[...] = jnp.zeros(dw_ref.shape, F32)

        @pl.when(i < nb)
        def _():
            for hd in range(NH_A):
                cols = slice(hd * DV_A, (hd + 1) * DV_A)
                h = hf_ref[:, cols] + hb_ref[:, cols]
                rs = lax.rsqrt(jnp.mean(h * h, axis=-1, keepdims=True) + EPS)
                w = w_ref[:, cols]
                hn = h * rs * w
                oa, za, g = oa_ref[:, cols], za_ref[:, cols], da_ref[:, cols]
                so, sz = _sig(oa), _sig(za)
                silu_z = za * sz
                doa_ref[:, cols] = g * hn * silu_z * so * (1.0 - so)
                dza_ref[:, cols] = g * so * hn * (sz * (1.0 + za * (1.0 - sz)))
                dhn = g * so * silu_z
                dw_ref[:, cols] += jnp.sum(dhn * (h * rs), axis=0, keepdims=True)
                dxn = dhn * w
                dh = rs * dxn - h * (rs * rs * rs * jnp.mean(dxn * h, axis=-1, keepdims=True))
                dhf_ref[:, cols] = dh
                dhb_ref[:, cols] = dh
            zb, gb, oat = zb_ref[...], db_ref[...], oat_ref[...]
            sb = _sig(zb)
            doat_ref[...] = gb * (zb * sb)
            dzb_ref[...] = gb * oat * (sb * (1.0 + zb * (1.0 - sb)))

        @pl.when(i >= nb)
        def _():
            for ref in (dhf_ref, dhb_ref, doa_ref, dza_ref, dzb_ref):
                ref[...] = jnp.zeros(ref.shape, F32)

    lati = lambda i: jnp.minimum(i, nb - 1)
    lat = pl.BlockSpec((tr, V_A), lambda i: (lati(i) + off, 0))
    row = pl.BlockSpec((tr, V_A), lambda i: (lati(i), 0))
    vec = pl.BlockSpec((1, V_A), lambda i: (0, 0))
    pout = pl.BlockSpec((tr, V_A), lambda i: (_ctx_block(i, nb, off), 0))
    hf_out = pl.BlockSpec((tr, V_A), lambda i: (jnp.where(i < nb, i + off, jnp.minimum(i - nb, off - 1)), 0))
    hb_out = pl.BlockSpec((tr, V_A), lambda i: (jnp.where(i < nb, i, nb + jnp.minimum(i - nb, off - 1)), 0))
    return pl.pallas_call(
        body, grid=(nb + n_ctx,),
        in_specs=[row, row, lat, row, row, lat, lat, lat, vec],
        out_specs=[hf_out, hb_out, row, pout, pout, pout, vec],
        out_shape=[jax.ShapeDtypeStruct((n, V_A), F32), jax.ShapeDtypeStruct((n, V_A), F32),
                   jax.ShapeDtypeStruct((t, V_A), F32), jax.ShapeDtypeStruct((r, V_A), F32),
                   jax.ShapeDtypeStruct((r, V_A), F32), jax.ShapeDtypeStruct((r, V_A), F32),
                   jax.ShapeDtypeStruct((1, V_A), F32)],
        compiler_params=pltpu.CompilerParams(dimension_semantics=("arbitrary",), vmem_limit_bytes=VMEM_LIMIT),
        name="merge_pre_bwd")(da, db, h_f, h_b, o_attn, p_oa, p_za, p_zb, mh_w)


def _merge_gate_fwd(y_a, y_b, p_ga, p_gb, tc):
    t = y_a.shape[0]
    tr, off = MERGE_ROWS, tc // MERGE_ROWS

    def body(ya_ref, yb_ref, ga_ref, gb_ref, m_ref):
        m_ref[...] = (_sig(ga_ref[...]) * ya_ref[...] + _sig(gb_ref[...]) * yb_ref[...]).astype(BF16)

    lat = pl.BlockSpec((tr, D_MODEL), lambda i: (i + off, 0))
    row = pl.BlockSpec((tr, D_MODEL), lambda i: (i, 0))
    return pl.pallas_call(
        body, grid=(t // tr,), in_specs=[row, row, lat, lat], out_specs=row,
        out_shape=jax.ShapeDtypeStruct((t, D_MODEL), BF16),
        compiler_params=pltpu.CompilerParams(dimension_semantics=("parallel",), vmem_limit_bytes=VMEM_LIMIT),
        name="merge_gate")(y_a, y_b, p_ga, p_gb)


def _merge_gate_bwd(dm, y_a, y_b, p_ga, p_gb, tc):
    t, r = y_a.shape[0], p_ga.shape[0]
    tr, off = MERGE_ROWS, tc // MERGE_ROWS
    nb = t // tr
    n_ctx = r // tr - nb

    def body(dm_ref, ya_ref, yb_ref, ga_ref, gb_ref, dya_ref, dyb_ref, dga_ref, dgb_ref):
        i = pl.program_id(0)

        @pl.when(i < nb)
        def _():
            dm = dm_ref[...]
            sa, sb = _sig(ga_ref[...]), _sig(gb_ref[...])
            dya_ref[...] = (dm * sa).astype(BF16)
            dyb_ref[...] = (dm * sb).astype(BF16)
            dga_ref[...] = dm * ya_ref[...] * sa * (1.0 - sa)
            dgb_ref[...] = dm * yb_ref[...] * sb * (1.0 - sb)

        @pl.when(i >= nb)
        def _():
            dga_ref[...] = jnp.zeros(dga_ref.shape, F32)
            dgb_ref[...] = jnp.zeros(dgb_ref.shape, F32)

    lati = lambda i: jnp.minimum(i, nb - 1)
    lat = pl.BlockSpec((tr, D_MODEL), lambda i: (lati(i) + off, 0))
    row = pl.BlockSpec((tr, D_MODEL), lambda i: (lati(i), 0))
    pout = pl.BlockSpec((tr, D_MODEL), lambda i: (_ctx_block(i, nb, off), 0))
    return pl.pallas_call(
        body, grid=(nb + n_ctx,), in_specs=[row, row, row, lat, lat], out_specs=[row, row, pout, pout],
        out_shape=[jax.ShapeDtypeStruct((t, D_MODEL), BF16), jax.ShapeDtypeStruct((t, D_MODEL), BF16),
                   jax.ShapeDtypeStruct((r, D_MODEL), F32), jax.ShapeDtypeStruct((r, D_MODEL), F32)],
        compiler_params=pltpu.CompilerParams(dimension_semantics=("arbitrary",), vmem_limit_bytes=VMEM_LIMIT),
        name="merge_gate_bwd")(dm, y_a, y_b, p_ga, p_gb)


def _make_merge_block(tc):
    @jax.custom_vjp
    def block(h_f, h_b, o_attn, p_oa, p_za, p_zb, p_ga, p_gb, mh_w, w_ba, w_bb, w_out, pr_ba, pr_bb, pr_out):
        return fwd(h_f, h_b, o_attn, p_oa, p_za, p_zb, p_ga, p_gb, mh_w, w_ba, w_bb, w_out, pr_ba, pr_bb, pr_out)[0]

    def fwd(h_f, h_b, o_attn, p_oa, p_za, p_zb, p_ga, p_gb, mh_w, w_ba, w_bb, w_out, pr_ba, pr_bb, pr_out):
        a_in, b_in = _merge_pre_fwd(h_f, h_b, o_attn, p_oa, p_za, p_zb, mh_w[None], tc)
        y_a, y_b = _mm_nn(a_in, w_ba, "merge_ya"), _mm_nn(b_in, w_bb, "merge_yb")
        m_in = _merge_gate_fwd(y_a, y_b, p_ga, p_gb, tc)
        out = _mm_nn(m_in, w_out, "merge_out")
        return out, (h_f, h_b, o_attn, p_oa, p_za, p_zb, p_ga, p_gb, mh_w, w_ba, w_bb, w_out, a_in, b_in, y_a, y_b, m_in)

    def bwd(res, dout):
        h_f, h_b, o_attn, p_oa, p_za, p_zb, p_ga, p_gb, mh_w, w_ba, w_bb, w_out, a_in, b_in, y_a, y_b, m_in = res
        dm = _mm_nt(dout, w_out, "merge_out_da")
        dw_out = _mm_tn(m_in, dout, "merge_out_dw")
        dy_a, dy_b, dga, dgb = _merge_gate_bwd(dm, y_a, y_b, p_ga, p_gb, tc)
        da, db = _mm_nt(dy_a, w_ba, "merge_ya_da"), _mm_nt(dy_b, w_bb, "merge_yb_da")
        dw_ba, dw_bb = _mm_tn(a_in, dy_a, "merge_ya_dw"), _mm_tn(b_in, dy_b, "merge_yb_dw")
        dhf, dhb, doat, doa, dza, dzb, dmh = _merge_pre_bwd(da, db, h_f, h_b, o_attn, p_oa, p_za, p_zb, mh_w[None], tc)
        z = jnp.zeros_like
        return (dhf, dhb, doat, doa, dza, dzb, dga, dgb, dmh[0], z(w_ba), z(w_bb), z(w_out), dw_ba, dw_bb, dw_out)

    block.defvjp(fwd, bwd)
    return block


def _silu(x):
    return x * jax.nn.sigmoid(x)


CONV_ROWS, CONV_HALO = 256, 8


def _make_conv(t, tc):
    r = t + 2 * tc
    width = 2 * QK_A
    nblk = r // CONV_ROWS
    cb, nb = tc // CONV_ROWS, t // CONV_ROWS
    k_scale = DK_A ** -0.5
    per = CONV_ROWS // CONV_HALO

    def taps(x_ref, prev_ref, next_ref):
        i = pl.program_id(0)
        seg_first = (i == 0) | (i == cb) | (i == cb + nb)
        seg_last = (i == cb - 1) | (i == cb + nb - 1) | (i == nblk - 1)
        x = x_ref[...]
        rows = lax.broadcasted_iota(jnp.int32, (CONV_ROWS, 1), 0)
        before = jnp.where(seg_first, 0.0, prev_ref[CONV_HALO - 1:CONV_HALO, :])
        after = jnp.where(seg_last, 0.0, next_ref[0:1, :])
        xm1 = jnp.where(rows == 0, before, pltpu.roll(x, 1, 0))
        xp1 = jnp.where(rows == CONV_ROWS - 1, after, pltpu.roll(x, CONV_ROWS - 1, 0))
        return xm1, x, xp1

    row = pl.BlockSpec((CONV_ROWS, width), lambda i: (i, 0))
    prev = pl.BlockSpec((CONV_HALO, width), lambda i: (jnp.maximum(i * per - 1, 0), 0))
    nxt = pl.BlockSpec((CONV_HALO, width), lambda i: (jnp.minimum((i + 1) * per, r // CONV_HALO - 1), 0))
    half = pl.BlockSpec((CONV_ROWS, QK_A), lambda i: (i, 0))
    wspec = pl.BlockSpec((3, width), lambda i: (0, 0))
    bspec = pl.BlockSpec((1, width), lambda i: (0, 0))
    par = pltpu.CompilerParams(dimension_semantics=("parallel",), vmem_limit_bytes=VMEM_LIMIT)
    seq = pltpu.CompilerParams(dimension_semantics=("arbitrary",), vmem_limit_bytes=VMEM_LIMIT)

    def fwd_call(x, cw, cb_):
        def body(x_ref, p_ref, n_ref, w_ref, b_ref, q_ref, k_ref):
            xm1, x0, xp1 = taps(x_ref, p_ref, n_ref)
            c = b_ref[...] + xm1 * w_ref[0:1, :] + x0 * w_ref[1:2, :] + xp1 * w_ref[2:3, :]
            y = c * jax.nn.sigmoid(c)
            q_ref[...] = y[:, :QK_A]
            k_ref[...] = y[:, QK_A:] * k_scale

        return pl.pallas_call(
            body, grid=(nblk,), in_specs=[row, prev, nxt, wspec, bspec], out_specs=[half, half],
            out_shape=[jax.ShapeDtypeStruct((r, QK_A), F32), jax.ShapeDtypeStruct((r, QK_A), F32)],
            compiler_params=par, name="conv_silu")(x, x, x, cw, cb_)

    def bwd_pre_call(dq, dk, x, cw, cb_):
        def body(dq_ref, dk_ref, x_ref, p_ref, n_ref, w_ref, b_ref, dc_ref, dw_ref, db_ref):
            @pl.when(pl.program_id(0) == 0)
            def _():
                dw_ref[...] = jnp.zeros(dw_ref.shape, F32)
                db_ref[...] = jnp.zeros(db_ref.shape, F32)

            xm1, x0, xp1 = taps(x_ref, p_ref, n_ref)
            c = b_ref[...] + xm1 * w_ref[0:1, :] + x0 * w_ref[1:2, :] + xp1 * w_ref[2:3, :]
            s = jax.nn.sigmoid(c)
            dy = jnp.concatenate([dq_ref[...], dk_ref[...] * k_scale], axis=1)
            dc = dy * (s * (1.0 + c * (1.0 - s)))
            dc_ref[...] = dc
            db_ref[...] += jnp.sum(dc, axis=0, keepdims=True)
            dw_ref[0:1, :] += jnp.sum(dc * xm1, axis=0, keepdims=True)
            dw_ref[1:2, :] += jnp.sum(dc * x0, axis=0, keepdims=True)
            dw_ref[2:3, :] += jnp.sum(dc * xp1, axis=0, keepdims=True)

        return pl.pallas_call(
            body, grid=(nblk,), in_specs=[half, half, row, prev, nxt, wspec, bspec], out_specs=[row, wspec, bspec],
            out_shape=[jax.ShapeDtypeStruct((r, width), F32), jax.ShapeDtypeStruct((3, width), F32),
                       jax.ShapeDtypeStruct((1, width), F32)],
            compiler_params=seq, name="conv_silu_bwd")(dq, dk, x, x, x, cw, cb_)

    def bwd_x_call(dc, cw):
        def body(d_ref, p_ref, n_ref, w_ref, dx_ref):
            dm1, d0, dp1 = taps(d_ref, p_ref, n_ref)
            dx_ref[...] = dm1 * w_ref[2:3, :] + d0 * w_ref[1:2, :] + dp1 * w_ref[0:1, :]

        return pl.pallas_call(
            body, grid=(nblk,), in_specs=[row, prev, nxt, wspec], out_specs=row,
            out_shape=jax.ShapeDtypeStruct((r, width), F32), compiler_params=par,
            name="conv_silu_bwd_x")(dc, dc, dc, cw)

    @jax.custom_vjp
    def op(x, cw, cb_):
        return tuple(fwd_call(x, cw, cb_[None]))

    def fwd(x, cw, cb_):
        return tuple(fwd_call(x, cw, cb_[None])), (x, cw, cb_)

    def bwd(res, cot):
        x, cw, cb_ = res
        dc, dw, db = bwd_pre_call(cot[0], cot[1], x, cw, cb_[None])
        return bwd_x_call(dc, cw), dw, db[0]

    op.defvjp(fwd, bwd)
    return op


def _local_loss(diff, const):
    x, ctx, target = diff["x"], const["ctx"], const["target"]
    t, tc = x.shape[0], ctx.shape[0]
    n = tc + t

    mod = diff["mod"]
    shift, scale, gate = mod[0, :D_MODEL], mod[0, D_MODEL:2 * D_MODEL], mod[0, 2 * D_MODEL:]
    shift_c, scale_c = mod[1, :D_MODEL], mod[1, D_MODEL:2 * D_MODEL]
    (p_qk, p_va, p_kv, p_oa, p_za, p_qb, p_zb, p_ga, p_gb), p_if = _ln_project(
        x, ctx, jnp.stack([scale, shift, scale_c, shift_c]), const["w_main"], const["w_if"], diff["p_main"], diff["p_if"])
    gt = p_if[:, :N_IF] + diff["b_if"]

    q_a, k_a = _make_conv(t, tc)(p_qk, diff["conv_w"], diff["conv_b"])
    v_a = p_va
    li_f, lf_f, li_b, lf_b = gt[:, 0:8], gt[:, 8:16], gt[:, 16:24], gt[:, 24:32]

    h_f = _make_mlstm(n, 0, False)(q_a, k_a, v_a, li_f, lf_f)
    h_b = _make_mlstm(n, tc, True)(q_a, k_a, v_a, li_b, lf_b)

    o_attn = _make_gqa(t, tc)(p_qb, p_kv, diff["q_norm_w"], diff["k_norm_w"])

    out = _make_merge_block(tc)(h_f, h_b, o_attn, p_oa, p_za, p_zb, p_ga, p_gb, diff["mh_norm_w"],
                                const["w_ba"], const["w_bb"], const["w_out"], diff["p_ba"], diff["p_bb"], diff["p_out"])

    return _loss_head(x, out, target, gate, diff["ln_w"], diff["ln_b"])


OTHER_CHIPS = [(1, 0), (0, 1), (1, 1)]


def _flip(v, bit):
    return 1 - v if bit else v


def _gather_chips(shard, name):
    def body(x_ref, o_ref, send_sems, recv_sems, local_sem):
        x, y, c = lax.axis_index("x"), lax.axis_index("y"), lax.axis_index("c")
        mine = pltpu.make_async_copy(x_ref, o_ref.at[2 * x + y], local_sem)
        mine.start()

        def copy(r, slot):
            dx, dy = OTHER_CHIPS[r]
            return pltpu.make_async_remote_copy(
                src_ref=x_ref, dst_ref=o_ref.at[slot], send_sem=send_sems.at[r], recv_sem=recv_sems.at[r],
                device_id=(_flip(x, dx), _flip(y, dy), c), device_id_type=MESH)

        sends = [copy(r, 2 * x + y) for r in range(3)]
        for cp in sends:
            cp.start()
        for r, (dx, dy) in enumerate(OTHER_CHIPS):
            copy(r, 2 * _flip(x, dx) + _flip(y, dy)).wait_recv()
        for cp in sends:
            cp.wait_send()
        mine.wait()

    return pl.pallas_call(
        body, out_shape=jax.ShapeDtypeStruct((N_CHIPS,) + shard.shape, shard.dtype),
        in_specs=[pl.BlockSpec(memory_space=pl.ANY)], out_specs=pl.BlockSpec(memory_space=pl.ANY),
        scratch_shapes=[pltpu.SemaphoreType.DMA((3,)), pltpu.SemaphoreType.DMA((3,)), pltpu.SemaphoreType.DMA],
        name=name)(shard)


def _gather_chips_halves(shard, name):
    rows, cols = shard.shape
    halves = shard.reshape(2, rows // 2, cols)

    def body(x_ref, o_ref, send_sems, recv_sems, local_sem):
        x, y, c = lax.axis_index("x"), lax.axis_index("y"), lax.axis_index("c")
        my_chip = 2 * x + y
        mine = pltpu.make_async_copy(x_ref, o_ref.at[my_chip], local_sem)
        mine.start()

        def chip_of(r):
            dx, dy = OTHER_CHIPS[r]
            return _flip(x, dx), _flip(y, dy)

        def copy(k, chip_slot, half, to, src=None):
            dst = o_ref.at[chip_slot, half]
            return pltpu.make_async_remote_copy(
                src_ref=dst if src is None else src, dst_ref=dst, send_sem=send_sems.at[k],
                recv_sem=recv_sems.at[k], device_id=to, device_id_type=MESH)

        first = [copy(r, my_chip, c, (*chip_of(r), c), src=x_ref.at[c]) for r in range(3)]
        for cp in first:
            cp.start()
        passed = []
        for r in range(3):
            px, py = chip_of(r)
            copy(r, 2 * px + py, c, (px, py, c)).wait_recv()
            passed.append(copy(3 + r, 2 * px + py, c, (x, y, 1 - c)))
            passed[-1].start()
        for r in range(3):
            px, py = chip_of(r)
            copy(3 + r, 2 * px + py, 1 - c, (x, y, 1 - c)).wait_recv()
        for cp in first + passed:
            cp.wait_send()
        mine.wait()

    out = pl.pallas_call(
        body, out_shape=jax.ShapeDtypeStruct((N_CHIPS, 2, rows // 2, cols), shard.dtype),
        in_specs=[pl.BlockSpec(memory_space=pl.ANY)], out_specs=pl.BlockSpec(memory_space=pl.ANY),
        scratch_shapes=[pltpu.SemaphoreType.DMA((6,)), pltpu.SemaphoreType.DMA((6,)), pltpu.SemaphoreType.DMA],
        name=name)(halves)
    return out.reshape(N_CHIPS, rows, cols)


def _scatter_grads(slots, name):
    def body(g_ref, o_ref, send_sems, recv_sems, local_sem):
        x, y, c = lax.axis_index("x"), lax.axis_index("y"), lax.axis_index("c")
        me, my_chip, sibling = 4 * x + 2 * y + c, 2 * x + y, (x, y, 1 - c)
        mine = pltpu.make_async_copy(g_ref.at[my_chip], o_ref.at[me], local_sem)
        mine.start()

        def chip_of(r):
            dx, dy = OTHER_CHIPS[r]
            return _flip(x, dx), _flip(y, dy)

        def copy(k, slot, to, src=None):
            dst = o_ref.at[slot]
            return pltpu.make_async_remote_copy(
                src_ref=dst if src is None else src, dst_ref=dst, send_sem=send_sems.at[k],
                recv_sem=recv_sems.at[k], device_id=to, device_id_type=MESH)

        first = [copy(0, me, sibling, src=g_ref.at[my_chip])]
        for r in range(3):
            px, py = chip_of(r)
            first.append(copy(1 + r, me, (px, py, c), src=g_ref.at[2 * px + py]))
        for cp in first:
            cp.start()
        passed = []
        for r in range(3):
            px, py = chip_of(r)
            copy(1 + r, 4 * px + 2 * py + c, (px, py, c)).wait_recv()
            passed.append(copy(4 + r, 4 * px + 2 * py + c, sibling))
            passed[-1].start()
        copy(0, 4 * x + 2 * y + 1 - c, sibling).wait_recv()
        for r in range(3):
            px, py = chip_of(r)
            copy(4 + r, 4 * px + 2 * py + 1 - c, sibling).wait_recv()
        for cp in first + passed:
            cp.wait_send()
        mine.wait()

    return pl.pallas_call(
        body, out_shape=jax.ShapeDtypeStruct((N_DEV,) + slots.shape[1:], slots.dtype),
        in_specs=[pl.BlockSpec(memory_space=pl.ANY)], out_specs=pl.BlockSpec(memory_space=pl.ANY),
        scratch_shapes=[pltpu.SemaphoreType.DMA((N_DEV - 1,)), pltpu.SemaphoreType.DMA((N_DEV - 1,)),
                        pltpu.SemaphoreType.DMA],
        name=name)(slots)


def _allreduce_small(v, name):
    def body(v_ref, o_ref, buf, send_sems, recv_sems):
        x, y, c = lax.axis_index("x"), lax.axis_index("y"), lax.axis_index("c")
        me = 4 * x + 2 * y + c
        buf[me] = v_ref[...]

        def peer(r):
            return _flip(x, (r >> 2) & 1), _flip(y, (r >> 1) & 1), _flip(c, r & 1)

        def copy(r, dst_slot):
            return pltpu.make_async_remote_copy(
                src_ref=v_ref, dst_ref=buf.at[dst_slot], send_sem=send_sems.at[r - 1],
                recv_sem=recv_sems.at[r - 1], device_id=peer(r), device_id_type=MESH)

        sends = [copy(r, me) for r in range(1, N_DEV)]
        for cp in sends:
            cp.start()
        for r in range(1, N_DEV):
            px, py, pc = peer(r)
            copy(r, 4 * px + 2 * py + pc).wait_recv()
        for cp in sends:
            cp.wait_send()
        acc = buf[0]
        for d in range(1, N_DEV):
            acc = acc + buf[d]
        o_ref[...] = acc

    return pl.pallas_call(
        body, out_shape=jax.ShapeDtypeStruct(v.shape, v.dtype),
        in_specs=[pl.BlockSpec(memory_space=pltpu.VMEM)], out_specs=pl.BlockSpec(memory_space=pltpu.VMEM),
        scratch_shapes=[pltpu.VMEM((N_DEV,) + v.shape, v.dtype), pltpu.SemaphoreType.DMA((N_DEV - 1,)),
                        pltpu.SemaphoreType.DMA((N_DEV - 1,))],
        name=name)(v)


def _adamw_math(w, g, m, v):
    m = ADAM_B1 * m + (1.0 - ADAM_B1) * g
    v = ADAM_B2 * v + (1.0 - ADAM_B2) * jnp.square(g)
    m_hat = m / (1.0 - ADAM_B1 ** ADAM_STEP)
    v_hat = v / (1.0 - ADAM_B2 ** ADAM_STEP)
    delta = -ADAM_LR * (m_hat / (jnp.sqrt(v_hat) + ADAM_EPS) + ADAM_WD * w)
    return delta, m, v


def _adamw_sum(parts, w, m, v, name):
    npart, rows, cols = parts.shape
    tr = _pick(rows, (64, 32, 16, 8)) if rows >= 8 else rows

    def body(p_ref, w_ref, m_ref, v_ref, g_out, d_out, m_out, v_out):
        g = p_ref[0].astype(F32)
        for k in range(1, npart):
            g = g + p_ref[k].astype(F32)
        d, m2, v2 = _adamw_math(w_ref[...], g, m_ref[...], v_ref[...])
        g_out[...] = g
        d_out[...] = d
        m_out[...] = m2
        v_out[...] = v2

    spec = pl.BlockSpec((tr, cols), lambda i: (i, 0))
    shp = jax.ShapeDtypeStruct((rows, cols), F32)
    return pl.pallas_call(
        body, grid=(rows // tr,),
        in_specs=[pl.BlockSpec((npart, tr, cols), lambda i: (0, i, 0)), spec, spec, spec],
        out_specs=[spec, spec, spec, spec], out_shape=[shp, shp, shp, shp],
        compiler_params=pltpu.CompilerParams(dimension_semantics=("parallel",), vmem_limit_bytes=VMEM_LIMIT),
        name=name)(parts, w, m, v)


SMALL_ROWS = 16


def _pack_small(c_ctx, b_mod, conv_b, mh, ln_w, ln_b, conv_w_rows, b_if, qn, kn):
    last = jnp.concatenate([b_if.reshape(-1), qn.reshape(-1), kn.reshape(-1),
                            jnp.zeros((D_MODEL - N_IF - 2 * HD_B,), F32)])
    rows = [c_ctx.reshape(1, D_MODEL), b_mod.reshape(3, D_MODEL), conv_b.reshape(1, D_MODEL),
            mh.reshape(1, D_MODEL), ln_w.reshape(1, D_MODEL), ln_b.reshape(1, D_MODEL),
            conv_w_rows.reshape(3, D_MODEL), last[None], jnp.zeros((SMALL_ROWS - 12, D_MODEL), F32)]
    return jnp.concatenate(rows, 0)


def _unpack_small(pk, conv_cols):
    return dict(c_ctx=pk[0], b_mod=pk[1:4].reshape(1, 3 * D_MODEL), conv_b=pk[4:5], mh_norm_w=pk[5:6],
                ln_w=pk[6:7], ln_b=pk[7:8], conv_w=pk[8:11, :conv_cols][None], b_if=pk[11:12, :N_IF],
                q_norm_w=pk[11:12, N_IF:N_IF + HD_B], k_norm_w=pk[11:12, N_IF + HD_B:N_IF + 2 * HD_B])


def kernel(x, c, ctx, c_ctx, w_mod, b_mod, w_in, b_if, conv_w, conv_b, mh_norm_w, q_norm_w, k_norm_w, w_branch_a, w_branch_b, w_out, ln_w, ln_b, loss_target, m_c_ctx, m_w_mod, m_b_mod, m_w_in, m_b_if, m_conv_w, m_conv_b, m_mh_norm_w, m_q_norm_w, m_k_norm_w, m_w_branch_a, m_w_branch_b, m_w_out, m_ln_w, m_ln_b, v_c_ctx, v_w_mod, v_b_mod, v_w_in, v_b_if, v_conv_w, v_conv_b, v_mh_norm_w, v_q_norm_w, v_k_norm_w, v_w_branch_a, v_w_branch_b, v_w_out, v_ln_w, v_ln_b):
    core = lax.axis_index("c")
    chip = 2 * lax.axis_index("x") + lax.axis_index("y")
    me = 2 * chip + core
    mod_cols, in_cols, conv_cols = w_mod.shape[2], w_in.shape[2], conv_w.shape[2]
    br_rows = w_out.shape[1]

    def rows_at(block, first):
        return lax.dynamic_update_slice(jnp.zeros((SMALL_ROWS, block.shape[1]), F32), block, (first, 0))

    owner = (core == 0).astype(F32)
    cond = _allreduce_small(rows_at(jnp.stack([_silu(c[0]), _silu(c_ctx)]), 2 * me), "gather_cond").astype(BF16)
    w_mod_b = w_mod[0].astype(BF16)
    mod_part = _mm_nn(cond, w_mod_b, "mod_fwd") * owner
    mod_all = _allreduce_small(
        lax.dynamic_update_slice(jnp.zeros((SMALL_ROWS, 3 * D_MODEL), F32), mod_part, (0, chip * mod_cols)),
        "gather_mod") + b_mod[0]
    mod = lax.dynamic_slice(mod_all, (2 * me, 0), (2, 3 * D_MODEL))

    g_in = _gather_chips_halves(w_in[0].astype(BF16), "gather_w_in")
    g_br = _gather_chips_halves(jnp.concatenate([w_branch_a[0], w_branch_b[0], w_out[0]], 0).astype(BF16),
                                "gather_w_branches").reshape(N_CHIPS, 3, br_rows, D_MODEL)
    g_ba, g_bb, g_out = g_br[:, 0], g_br[:, 1], g_br[:, 2]
    g_conv = _gather_chips(conv_w[0], "gather_conv_w")
    w_in_full = jnp.moveaxis(g_in, 0, 1).reshape(D_MODEL, N_CHIPS * in_cols)
    w_main = jnp.concatenate([w_in_full[:, :IF_START], w_in_full[:, IF_START + N_IF:]], 1)
    w_if = jnp.pad(w_in_full[:, IF_START:IF_START + N_IF], ((0, 0), (0, IF_PAD - N_IF)))
    conv_w_full = jnp.moveaxis(g_conv, 0, 1).reshape(3, N_CHIPS * conv_cols)

    const = dict(ctx=ctx[0], target=loss_target[0], w_main=w_main, w_if=w_if,
                 w_ba=g_ba.reshape(D_MODEL, D_MODEL), w_bb=g_bb.reshape(D_MODEL, D_MODEL),
                 w_out=g_out.reshape(D_MODEL, D_MODEL))
    diff = dict(x=x[0], mod=mod, b_if=b_if[0], conv_w=conv_w_full, conv_b=conv_b[0],
                mh_norm_w=mh_norm_w[0], q_norm_w=q_norm_w[0], k_norm_w=k_norm_w[0], ln_w=ln_w[0], ln_b=ln_b[0],
                p_main=jnp.zeros(w_main.shape, F32),
                p_if=jnp.zeros(w_if.shape, F32), p_ba=jnp.zeros((D_MODEL, D_MODEL), F32),
                p_bb=jnp.zeros((D_MODEL, D_MODEL), F32), p_out=jnp.zeros((D_MODEL, D_MODEL), F32))
    loss_local, g = jax.value_and_grad(_local_loss)(diff, const)
    loss = lax.psum(loss_local, ("x", "y", "c"))

    dmod_all = _allreduce_small(rows_at(g["mod"], 2 * me), "gather_dmod")
    dmod_k = lax.dynamic_slice(dmod_all, (0, chip * mod_cols), (SMALL_ROWS, mod_cols))
    g_w_mod = _mm_tn(cond, dmod_k, "mod_dw")
    g_b_mod = jnp.sum(dmod_all, axis=0) * (me == 0).astype(F32)
    d_cond = _mm_nt(dmod_k, w_mod_b, "mod_da")
    sig_ctx = jax.nn.sigmoid(c_ctx)
    g_c_ctx = owner * (sig_ctx * (1.0 + c_ctx * (1.0 - sig_ctx))) * jnp.sum(d_cond[1::2], axis=0)

    g_small = _allreduce_small(
        _pack_small(g_c_ctx, g_b_mod, g["conv_b"], g["mh_norm_w"], g["ln_w"], g["ln_b"], g["conv_w"],
                    g["b_if"], g["q_norm_w"], g["k_norm_w"]), "allreduce_small")
    conv_g = lax.dynamic_slice(g_small[8:11], (0, chip * conv_cols), (3, conv_cols))
    g_small = g_small.at[8:11].set(jnp.pad(conv_g, ((0, 0), (0, D_MODEL - conv_cols))))
    pad_conv = lambda a: jnp.pad(a[0], ((0, 0), (0, D_MODEL - conv_cols)))
    packed = [_pack_small(cc, bm[0], cb[0], mh[0], lw[0], lb[0], pad_conv(cw), bi[0], qn[0], kn[0])
              for cc, bm, cb, mh, lw, lb, cw, bi, qn, kn in (
                  (c_ctx, b_mod, conv_b, mh_norm_w, ln_w, ln_b, conv_w, b_if, q_norm_w, k_norm_w),
                  (m_c_ctx, m_b_mod, m_conv_b, m_mh_norm_w, m_ln_w, m_ln_b, m_conv_w, m_b_if, m_q_norm_w, m_k_norm_w),
                  (v_c_ctx, v_b_mod, v_conv_b, v_mh_norm_w, v_ln_w, v_ln_b, v_conv_w, v_b_if, v_q_norm_w, v_k_norm_w))]
    small = [_unpack_small(a, conv_cols)
             for a in _adamw_sum(g_small[None], packed[0], packed[1], packed[2], "adamw_small")]

    def col_slots(gfull, cols):
        return jnp.moveaxis(gfull.reshape(D_MODEL, N_CHIPS, cols), 1, 0).astype(BF16)

    g_in_full = jnp.concatenate([g["p_main"][:, :IF_START], g["p_if"][:, :N_IF], g["p_main"][:, IF_START:]], 1)
    big = {"w_mod": [a[None] for a in _adamw_sum(g_w_mod[None], w_mod[0], m_w_mod[0], v_w_mod[0], "adamw_w_mod")]}
    parts_in = _scatter_grads(col_slots(g_in_full, in_cols), "scatter_w_in")
    big["w_in"] = [a[None] for a in _adamw_sum(parts_in, w_in[0], m_w_in[0], v_w_in[0], "adamw_w_in")]
    br_slots = jnp.concatenate([g[p].reshape(N_CHIPS, br_rows, D_MODEL) for p in ("p_ba", "p_bb", "p_out")], 1)
    parts_br = _scatter_grads(br_slots.astype(BF16), "scatter_w_branches")
    for j, (nm, w_, m_, v_) in enumerate((("w_branch_a", w_branch_a, m_w_branch_a, v_w_branch_a),
                                          ("w_branch_b", w_branch_b, m_w_branch_b, v_w_branch_b),
                                          ("w_out", w_out, m_w_out, v_w_out))):
        parts = parts_br[:, j * br_rows:(j + 1) * br_rows]
        big[nm] = [a[None] for a in _adamw_sum(parts, w_[0], m_[0], v_[0], "adamw_" + nm)]

    names = ["c_ctx", "w_mod", "b_mod", "w_in", "b_if", "conv_w", "conv_b", "mh_norm_w", "q_norm_w", "k_norm_w",
             "w_branch_a", "w_branch_b", "w_out", "ln_w", "ln_b"]
    outs = [[big[nm][k] if nm in big else small[k][nm] for nm in names] for k in range(4)]
    return (loss, g["x"][None], *outs[0], *outs[1], *outs[2], *outs[3])
```

```python
import jax
import jax.numpy as jnp
from jax import lax
from jax.experimental import pallas as pl
from jax.experimental.pallas import tpu as pltpu

F32 = jnp.float32
BF16 = jnp.bfloat16
MESH = pl.DeviceIdType.MESH

D_MODEL = 2048
NH_A, DK_A, DV_A = 8, 128, 256
QK_A, V_A = NH_A * DK_A, NH_A * DV_A
NH_B, NKV_B, HD_B = 16, 4, 128
Q_B, KV_B = NH_B * HD_B, NKV_B * HD_B
GRID_W = 64
ROT_HALF = HD_B // 2
ROPE_THETA = 10000.0
M_INIT = -1e30
EPS = 1e-6
ALPHA = 2.0 ** 0.25
N_IN = 17440
IF_START, N_IF, IF_PAD = 4096, 32, 128
N_MAIN = N_IN - N_IF
O_QK, O_VA, O_KB, O_VB, O_OA, O_ZA, O_QB, O_ZB, O_GA, O_GB = (
    0, 2048, 4096, 4608, 5120, 7168, 9216, 11264, 13312, 15360)
MLSTM_CHUNK = 256

ADAM_LR, ADAM_B1, ADAM_B2, ADAM_EPS, ADAM_WD, ADAM_STEP = 0.001, 0.9, 0.999, 1e-08, 0.01, 10

VMEM_LIMIT = 48 * 1024 * 1024
N_CHIPS, N_DEV = 4, 8
MX = BF16


def _pick(n, cands):
    for c in cands:
        if n % c == 0:
            return c
    raise ValueError(f"no tile for {n} in {cands}")


def _dot(a, b):
    return jnp.dot(a, b, preferred_element_type=F32)


def _dot_nt(a, b):
    return lax.dot_general(a, b, (((1,), (1,)), ((), ())), preferred_element_type=F32)


def _dot_tn(a, b):
    return lax.dot_general(a, b, (((0,), (0,)), ((), ())), preferred_element_type=F32)


def _mm_nn(a, b, name):
    m, k = a.shape
    _, n = b.shape
    tm = _pick(m, (1024, 512, 256, 128, 64, 32, 16))
    tn = _pick(n, (1024, 512, 256, 128))

    def body(a_ref, b_ref, o_ref):
        o_ref[...] = _dot(a_ref[...], b_ref[...])

    return pl.pallas_call(
        body, grid=(m // tm, n // tn),
        in_specs=[pl.BlockSpec((tm, k), lambda i, j: (i, 0)), pl.BlockSpec((k, tn), lambda i, j: (0, j))],
        out_specs=pl.BlockSpec((tm, tn), lambda i, j: (i, j)),
        out_shape=jax.ShapeDtypeStruct((m, n), F32),
        compiler_params=pltpu.CompilerParams(dimension_semantics=("parallel", "parallel"),
                                             vmem_limit_bytes=VMEM_LIMIT),
        name=name)(a, b)


def _mm_nt(g, w, name):
    m, n = g.shape
    k, _ = w.shape
    tm = _pick(m, (1024, 512, 256, 128, 64, 32, 16))
    tn = _pick(n, (1024, 512, 256, 128))

    def body(g_ref, w_ref, o_ref):
        part = _dot_nt(g_ref[...].astype(BF16), w_ref[...])

        @pl.when(pl.program_id(1) == 0)
        def _():
            o_ref[...] = part

        @pl.when(pl.program_id(1) > 0)
        def _():
            o_ref[...] += part

    return pl.pallas_call(
        body, grid=(m // tm, n // tn),
        in_specs=[pl.BlockSpec((tm, tn), lambda i, j: (i, j)), pl.BlockSpec((k, tn), lambda i, j: (0, j))],
        out_specs=pl.BlockSpec((tm, k), lambda i, j: (i, 0)),
        out_shape=jax.ShapeDtypeStruct((m, k), F32),
        compiler_params=pltpu.CompilerParams(dimension_semantics=("parallel", "arbitrary"),
                                             vmem_limit_bytes=VMEM_LIMIT),
        name=name)(g, w)


def _mm_tn(a, g, name):
    m, k = a.shape
    _, n = g.shape
    tm = _pick(m, (1024, 512, 256, 128, 64, 32, 16))
    tn = _pick(n, (1024, 512, 256, 128))

    def body(a_ref, g_ref, o_ref):
        part = _dot_tn(a_ref[...], g_ref[...].astype(BF16))

        @pl.when(pl.program_id(1) == 0)
        def _():
            o_ref[...] = part

        @pl.when(pl.program_id(1) > 0)
        def _():
            o_ref[...] += part

    return pl.pallas_call(
        body, grid=(n // tn, m // tm),
        in_specs=[pl.BlockSpec((tm, k), lambda j, i: (i, 0)), pl.BlockSpec((tm, tn), lambda j, i: (i, j))],
        out_specs=pl.BlockSpec((k, tn), lambda j, i: (0, j)),
        out_shape=jax.ShapeDtypeStruct((k, n), F32),
        compiler_params=pltpu.CompilerParams(dimension_semantics=("parallel", "arbitrary"),
                                             vmem_limit_bytes=VMEM_LIMIT),
        name=name)(a, g)


SLABS = (("qk", O_QK, 2 * QK_A), ("va", O_VA, V_A), ("kv", O_KB, 2 * KV_B), ("oa", O_OA, V_A), ("za", O_ZA, V_A),
         ("qb", O_QB, Q_B), ("zb", O_ZB, Q_B), ("ga", O_GA, D_MODEL), ("gb", O_GB, D_MODEL))
SLAB_FWD_TN = 1024
SLAB_TN = 512
SLAB_DA_TN = 1024
PROJ_DA_VMEM = 58 * 1024 * 1024


def _slab_blocks(tn):
    return [(off // tn, (off + width) // tn) for _, off, width in SLABS]


def _proj_fwd_slab(a, w, off, width, name):
    m, k = a.shape
    tm = _pick(m, (1088, 512, 256, 128))
    tn = min(SLAB_FWD_TN, width)

    def body(a_ref, b_ref, o_ref):
        o_ref[...] = _dot(a_ref[...], b_ref[...])

    return pl.pallas_call(
        body, grid=(m // tm, width // tn),
        in_specs=[pl.BlockSpec((tm, k), lambda i, j: (i, 0)), pl.BlockSpec((k, tn), lambda i, j: (0, j + off // tn))],
        out_specs=pl.BlockSpec((tm, tn), lambda i, j: (i, j)),
        out_shape=jax.ShapeDtypeStruct((m, width), F32),
        compiler_params=pltpu.CompilerParams(dimension_semantics=("parallel", "parallel"),
                                             vmem_limit_bytes=VMEM_LIMIT),
        name=name)(a, w)


def _slab_spec(tm, tn, blocks, rows_inner):
    b, e = blocks

    def index(r, c):
        inside = (c >= b) & (c < e)
        return jnp.where(inside, r, 0), jnp.clip(c - b, 0, e - b - 1)

    if rows_inner:
        return pl.BlockSpec((tm, tn), lambda c, r: index(r, c))
    return pl.BlockSpec((tm, tn), lambda r, c: index(r, c))


def _proj_da(gs, w, name):
    m = gs[0].shape[0]
    k, n = w.shape
    tm, tn = _pick(m, (512, 256, 128)), SLAB_DA_TN
    blocks = _slab_blocks(tn)

    def body(*refs):
        g_refs, w_ref, o_ref = refs[:len(blocks)], refs[len(blocks)], refs[len(blocks) + 1]
        c = pl.program_id(1)

        @pl.when(c == 0)
        def _():
            o_ref[...] = jnp.zeros(o_ref.shape, F32)

        for g_ref, (b, e) in zip(g_refs, blocks):
            @pl.when((c >= b) & (c < e))
            def _(g_ref=g_ref):
                o_ref[...] += _dot_nt(g_ref[...].astype(BF16), w_ref[...])

    return pl.pallas_call(
        body, grid=(m // tm, n // tn),
        in_specs=[_slab_spec(tm, tn, blk, False) for blk in blocks] + [pl.BlockSpec((k, tn), lambda r, c: (0, c))],
        out_specs=pl.BlockSpec((tm, k), lambda r, c: (r, 0)),
        out_shape=jax.ShapeDtypeStruct((m, k), F32),
        compiler_params=pltpu.CompilerParams(dimension_semantics=("parallel", "arbitrary"),
                                             vmem_limit_bytes=PROJ_DA_VMEM),
        name=name)(*gs, w)


def _proj_dw(a, gs, n, name):
    m, k = a.shape
    tm = _pick(m, (512, 256, 128))
    blocks = _slab_blocks(SLAB_TN)

    def body(*refs):
        a_ref, g_refs, o_ref = refs[0], refs[1:1 + len(blocks)], refs[1 + len(blocks)]
        c, r = pl.program_id(0), pl.program_id(1)

        @pl.when(r == 0)
        def _():
            o_ref[...] = jnp.zeros(o_ref.shape, F32)

        for g_ref, (b, e) in zip(g_refs, blocks):
            @pl.when((c >= b) & (c < e))
            def _(g_ref=g_ref):
                o_ref[...] += _dot_tn(a_ref[...], g_ref[...].astype(BF16))

    return pl.pallas_call(
        body, grid=(n // SLAB_TN, m // tm),
        in_specs=[pl.BlockSpec((tm, k), lambda c, r: (r, 0))] + [_slab_spec(tm, SLAB_TN, blk, True) for blk in blocks],
        out_specs=pl.BlockSpec((k, SLAB_TN), lambda c, r: (0, c)),
        out_shape=jax.ShapeDtypeStruct((k, n), F32),
        compiler_params=pltpu.CompilerParams(dimension_semantics=("parallel", "arbitrary"),
                                             vmem_limit_bytes=VMEM_LIMIT),
        name=name)(a, *gs)


LN_ROWS = 256


def _ln_stats(x):
    mu = jnp.mean(x, axis=-1, keepdims=True)
    xc = x - mu
    rstd = lax.rsqrt(jnp.mean(xc * xc, axis=-1, keepdims=True) + EPS)
    return xc * rstd, rstd


def _ln_bwd(dxh, xh, rstd):
    return rstd * (dxh - jnp.mean(dxh, axis=-1, keepdims=True) - xh * jnp.mean(dxh * xh, axis=-1, keepdims=True))


def _seg_maps(t, tc):
    cb, nb = tc // LN_ROWS, t // LN_ROWS
    ctx_blk = lambda i: jnp.where(i < cb, i, jnp.clip(i - cb - nb, 0, cb - 1))
    lat_blk = lambda i: jnp.clip(i - cb, 0, nb - 1)
    return cb, nb, ctx_blk, lat_blk


def _ln_mod_fwd(x, ctx, mod):
    t, tc = x.shape[0], ctx.shape[0]
    cb, nb, ctx_blk, lat_blk = _seg_maps(t, tc)

    def body(x_ref, c_ref, m_ref, o_ref):
        i = pl.program_id(0)
        lat = (i >= cb) & (i < cb + nb)

        @pl.when(lat)
        def _():
            xh, _ = _ln_stats(x_ref[...])
            o_ref[...] = (xh * (1 + m_ref[0:1, :]) + m_ref[1:2, :]).astype(BF16)

        @pl.when(jnp.logical_not(lat))
        def _():
            xh, _ = _ln_stats(c_ref[...])
            o_ref[...] = (xh * (1 + m_ref[2:3, :]) + m_ref[3:4, :]).astype(BF16)

    blk = lambda f: pl.BlockSpec((LN_ROWS, D_MODEL), lambda i: (f(i), 0))
    return pl.pallas_call(
        body, grid=(2 * cb + nb,),
        in_specs=[blk(lat_blk), blk(ctx_blk), pl.BlockSpec((4, D_MODEL), lambda i: (0, 0))],
        out_specs=pl.BlockSpec((LN_ROWS, D_MODEL), lambda i: (i, 0)),
        out_shape=jax.ShapeDtypeStruct((t + 2 * tc, D_MODEL), BF16),
        compiler_params=pltpu.CompilerParams(dimension_semantics=("parallel",), vmem_limit_bytes=VMEM_LIMIT),
        name="ln_mod")(x, ctx, mod)


def _ln_mod_bwd(du_a, du_b, x, ctx, mod):
    t, tc = x.shape[0], ctx.shape[0]
    cb, nb, ctx_blk, lat_blk = _seg_maps(t, tc)

    def body(da_ref, db_ref, x_ref, c_ref, m_ref, dx_ref, dm_ref):
        i = pl.program_id(0)
        lat = (i >= cb) & (i < cb + nb)

        @pl.when(i == 0)
        def _():
            dm_ref[...] = jnp.zeros(dm_ref.shape, F32)

        du = da_ref[...] + db_ref[...]

        @pl.when(lat)
        def _():
            xh, rstd = _ln_stats(x_ref[...])
            dm_ref[0:1, :] += jnp.sum(du * xh, axis=0, keepdims=True)
            dm_ref[1:2, :] += jnp.sum(du, axis=0, keepdims=True)
            dx_ref[...] = _ln_bwd(du * (1 + m_ref[0:1, :]), xh, rstd)

        @pl.when(jnp.logical_not(lat))
        def _():
            xh, _ = _ln_stats(c_ref[...])
            dm_ref[2:3, :] += jnp.sum(du * xh, axis=0, keepdims=True)
            dm_ref[3:4, :] += jnp.sum(du, axis=0, keepdims=True)

    blk = lambda f: pl.BlockSpec((LN_ROWS, D_MODEL), lambda i: (f(i), 0))
    row = pl.BlockSpec((LN_ROWS, D_MODEL), lambda i: (i, 0))
    vec = pl.BlockSpec((4, D_MODEL), lambda i: (0, 0))
    return pl.pallas_call(
        body, grid=(2 * cb + nb,),
        in_specs=[row, row, blk(lat_blk), blk(ctx_blk), vec],
        out_specs=[blk(lat_blk), vec],
        out_shape=[jax.ShapeDtypeStruct((t, D_MODEL), F32), jax.ShapeDtypeStruct((4, D_MODEL), F32)],
        compiler_params=pltpu.CompilerParams(dimension_semantics=("arbitrary",), vmem_limit_bytes=VMEM_LIMIT),
        name="ln_mod_bwd")(du_a, du_b, x, ctx, mod)


@jax.custom_vjp
def _ln_project(x, ctx, mod, w_main, w_if, pr_main, pr_if):
    return _ln_project_fwd(x, ctx, mod, w_main, w_if, pr_main, pr_if)[0]


def _ln_project_fwd(x, ctx, mod, w_main, w_if, pr_main, pr_if):
    del pr_main, pr_if
    ub = _ln_mod_fwd(x, ctx, mod)
    slabs = tuple(_proj_fwd_slab(ub, w_main, off, width, "proj_" + nm) for nm, off, width in SLABS)
    return (slabs, _mm_nn(ub, w_if, "proj_if")), (x, ctx, mod, ub, w_main, w_if)


def _ln_project_bwd(res, cot):
    x, ctx, mod, ub, w_main, w_if = res
    gs, g_if = cot
    dx, dmod = _ln_mod_bwd(_proj_da(gs, w_main, "proj_da"), _mm_nt(g_if, w_if, "proj_if_da"), x, ctx, mod)
    return (dx, jnp.zeros_like(ctx), dmod, jnp.zeros_like(w_main), jnp.zeros_like(w_if),
            _proj_dw(ub, gs, w_main.shape[1], "proj_dw"), _mm_tn(ub, g_if, "proj_if_dw"))


_ln_project.defvjp(_ln_project_fwd, _ln_project_bwd)


def _head_fwd(x, out, target, vecs):
    t = x.shape[0]

    def body(x_ref, o_ref, t_ref, v_ref, l_ref):
        @pl.when(pl.program_id(0) == 0)
        def _():
            l_ref[...] = jnp.zeros(l_ref.shape, F32)

        rh, _ = _ln_stats(ALPHA * x_ref[...] + v_ref[0:1, :] * o_ref[...])
        err = rh * v_ref[1:2, :] + v_ref[2:3, :] - t_ref[...]
        part = jnp.sum(jnp.mean(err * err, axis=-1, keepdims=True), axis=0, keepdims=True)
        l_ref[...] += 0.5 * part

    row = pl.BlockSpec((LN_ROWS, D_MODEL), lambda i: (i, 0))
    return pl.pallas_call(
        body, grid=(t // LN_ROWS,),
        in_specs=[row, row, row, pl.BlockSpec((3, D_MODEL), lambda i: (0, 0))],
        out_specs=pl.BlockSpec((1, 128), lambda i: (0, 0)),
        out_shape=jax.ShapeDtypeStruct((1, 128), F32),
        compiler_params=pltpu.CompilerParams(dimension_semantics=("arbitrary",), vmem_limit_bytes=VMEM_LIMIT),
        name="loss_head")(x, out, target, vecs)


def _head_bwd(g, x, out, target, vecs):
    t = x.shape[0]

    def body(g_ref, x_ref, o_ref, t_ref, v_ref, dx_ref, do_ref, dv_ref):
        @pl.when(pl.program_id(0) == 0)
        def _():
            dv_ref[...] = jnp.zeros(dv_ref.shape, F32)

        o = o_ref[...]
        gate, ln_w = v_ref[0:1, :], v_ref[1:2, :]
        rh, rstd = _ln_stats(ALPHA * x_ref[...] + gate * o)
        dy = (rh * ln_w + v_ref[2:3, :] - t_ref[...]) * (g_ref[0:1, 0:1] * (1.0 / D_MODEL))
        dv_ref[1:2, :] += jnp.sum(dy * rh, axis=0, keepdims=True)
        dv_ref[2:3, :] += jnp.sum(dy, axis=0, keepdims=True)
        dr = _ln_bwd(dy * ln_w, rh, rstd)
        dv_ref[0:1, :] += jnp.sum(dr * o, axis=0, keepdims=True)
        dx_ref[...] = ALPHA * dr
        do_ref[...] = gate * dr

    row = pl.BlockSpec((LN_ROWS, D_MODEL), lambda i: (i, 0))
    vec = pl.BlockSpec((3, D_MODEL), lambda i: (0, 0))
    return pl.pallas_call(
        body, grid=(t // LN_ROWS,),
        in_specs=[pl.BlockSpec((1, 128), lambda i: (0, 0)), row, row, row, vec],
        out_specs=[row, row, vec],
        out_shape=[jax.ShapeDtypeStruct((t, D_MODEL), F32), jax.ShapeDtypeStruct((t, D_MODEL), F32),
                   jax.ShapeDtypeStruct((3, D_MODEL), F32)],
        compiler_params=pltpu.CompilerParams(dimension_semantics=("arbitrary",), vmem_limit_bytes=VMEM_LIMIT),
        name="loss_head_bwd")(g, x, out, target, vecs)


@jax.custom_vjp
def _loss_head(x, out, target, gate, ln_w, ln_b):
    return _head_fwd(x, out, target, jnp.stack([gate, ln_w, ln_b]))[0, 0]


def _loss_head_fwd(x, out, target, gate, ln_w, ln_b):
    vecs = jnp.stack([gate, ln_w, ln_b])
    return _head_fwd(x, out, target, vecs)[0, 0], (x, out, target, vecs)


def _loss_head_bwd(res, g):
    x, out, target, vecs = res
    dx, dout, dv = _head_bwd(jnp.full((1, 128), g, F32), x, out, target, vecs)
    return dx, dout, jnp.zeros_like(target), dv[0], dv[1], dv[2]


_loss_head.defvjp(_loss_head_fwd, _loss_head_bwd)


ATT_SCALE = HD_B ** -0.5
GROUP = NH_B // NKV_B


LOG2E, LN2 = 1.4426950408889634, 0.6931471805599453
ATT_C = ATT_SCALE * LOG2E
STRIP_Q, STRIP_K = 128, 256


def _attn_tiles(t, n):
    return _pick(t, (512, 256, 128)), _pick(n, (768, 512, 256))


def _attn_fwd(q, k, v):
    t, n = q.shape[0], k.shape[0]
    tq, tk = _pick(t, (1024, 512, 256, 128)), _attn_tiles(t, n)[1]
    nk = n // tk

    def body(q_ref, k_ref, v_ref, o_ref, lse_ref, m_sc, acc_sc):
        j = pl.program_id(2)

        @pl.when(j == 0)
        def _():
            m_sc[...] = jnp.full(m_sc.shape, -jnp.inf, F32)
            acc_sc[...] = jnp.zeros(acc_sc.shape, F32)

        kb = k_ref[...]
        v_ones = jnp.concatenate([v_ref[...], jnp.ones((tk, HD_B), BF16)], axis=1)
        for g in range(GROUP):
            s2 = _dot_nt(q_ref[:, g * HD_B:(g + 1) * HD_B], kb) * ATT_C
            m_prev = m_sc[g]
            m_new = jnp.maximum(m_prev, jnp.max(s2, axis=-1, keepdims=True))
            p = jnp.exp2(s2 - m_new).astype(BF16)
            acc_sc[g] = jnp.exp2(m_prev - m_new) * acc_sc[g] + _dot(p, v_ones)
            m_sc[g] = m_new

        @pl.when(j == nk - 1)
        def _():
            for g in range(GROUP):
                cols = slice(g * HD_B, (g + 1) * HD_B)
                l = acc_sc[g, :, HD_B:]
                o_ref[:, cols] = acc_sc[g, :, :HD_B] / l
                lse_ref[:, cols] = m_sc[g] + jnp.log(l) * LOG2E

    qspec = pl.BlockSpec((tq, GROUP * HD_B), lambda kh, i, j: (i, kh))
    kspec = pl.BlockSpec((tk, HD_B), lambda kh, i, j: (j, kh))
    return pl.pallas_call(
        body, grid=(NKV_B, t // tq, nk),
        in_specs=[qspec, kspec, kspec], out_specs=[qspec, qspec],
        out_shape=[jax.ShapeDtypeStruct((t, Q_B), F32), jax.ShapeDtypeStruct((t, Q_B), F32)],
        scratch_shapes=[pltpu.VMEM((GROUP, tq, 1), F32), pltpu.VMEM((GROUP, tq, 2 * HD_B), F32)],
        compiler_params=pltpu.CompilerParams(dimension_semantics=("parallel", "parallel", "arbitrary"),
                                             vmem_limit_bytes=VMEM_LIMIT),
        name="attn_fwd")(q, k, v)


def _attn_dq(q, k, v, do, lse, delta):
    t, n = q.shape[0], k.shape[0]
    tq, tk = _pick(t, (1024, 512, 256, 128)), _attn_tiles(t, n)[1]
    nk = n // tk

    def body(q_ref, k_ref, v_ref, do_ref, lse_ref, dl_ref, dq_ref):
        j = pl.program_id(2)
        kb, vb = k_ref[...], v_ref[...]
        parts = []
        for g in range(GROUP):
            cols = slice(g * HD_B, (g + 1) * HD_B)
            p = jnp.exp2(_dot_nt(q_ref[:, cols], kb) * ATT_C - lse_ref[:, g * HD_B:g * HD_B + 1])
            dp = _dot_nt(do_ref[:, cols], vb)
            ds = p * (dp - dl_ref[:, g * HD_B:g * HD_B + 1])
            parts.append(_dot(ds.astype(BF16), kb))

        @pl.when(j == 0)
        def _():
            for g in range(GROUP):
                dq_ref[:, g * HD_B:(g + 1) * HD_B] = parts[g]

        @pl.when(j > 0)
        def _():
            for g in range(GROUP):
                dq_ref[:, g * HD_B:(g + 1) * HD_B] += parts[g]

        @pl.when(j == nk - 1)
        def _():
            dq_ref[...] = dq_ref[...] * ATT_SCALE

    qspec = pl.BlockSpec((tq, GROUP * HD_B), lambda kh, i, j: (i, kh))
    kspec = pl.BlockSpec((tk, HD_B), lambda kh, i, j: (j, kh))
    return pl.pallas_call(
        body, grid=(NKV_B, t // tq, n // tk),
        in_specs=[qspec, kspec, kspec, qspec, qspec, qspec],
        out_specs=qspec,
        out_shape=jax.ShapeDtypeStruct((t, Q_B), F32),
        compiler_params=pltpu.CompilerParams(dimension_semantics=("parallel", "parallel", "arbitrary"),
                                             vmem_limit_bytes=VMEM_LIMIT),
        name="attn_dq")(q, k, v, do, lse, delta)


def _attn_dkv(q, k, v, do, lse_t, delta_t):
    t, n = q.shape[0], k.shape[0]
    tq, tk = _attn_tiles(t, n)
    nq = t // tq
    n_r, n_c = tq // STRIP_Q, tk // STRIP_K

    def body(q_ref, k_ref, v_ref, do_ref, lse_ref, dl_ref, dk_ref, dv_ref, dk_sc, dv_sc):
        i = pl.program_id(2)

        @pl.when(i == 0)
        def _():
            dk_sc[...] = jnp.zeros(dk_sc.shape, F32)
            dv_sc[...] = jnp.zeros(dv_sc.shape, F32)

        for r in range(n_r):
            rows = slice(r * STRIP_Q, (r + 1) * STRIP_Q)
            for c in range(n_c):
                kv = slice(c * STRIP_K, (c + 1) * STRIP_K)
                kc, vc = k_ref[kv, :], v_ref[kv, :]
                dk_part = dv_part = None
                for g in range(GROUP):
                    cols = slice(g * HD_B, (g + 1) * HD_B)
                    qg, dog = q_ref[rows, cols], do_ref[rows, cols]
                    st = _dot_nt(kc, qg)
                    pt = jnp.exp2(st * ATT_C - lse_ref[8 * g:8 * g + 1, rows])
                    dvg = _dot(pt.astype(BF16), dog)
                    dpt = _dot_nt(vc, dog)
                    dst = pt * (dpt - dl_ref[8 * g:8 * g + 1, rows])
                    dkg = _dot(dst.astype(BF16), qg)
                    dk_part = dkg if dk_part is None else dk_part + dkg
                    dv_part = dvg if dv_part is None else dv_part + dvg
                dk_sc[kv, :] += dk_part
                dv_sc[kv, :] += dv_part

        @pl.when(i == nq - 1)
        def _():
            dk_ref[...] = dk_sc[...] * ATT_SCALE
            dv_ref[...] = dv_sc[...]

    qspec = pl.BlockSpec((tq, GROUP * HD_B), lambda kh, j, i: (i, kh))
    tspec = pl.BlockSpec((8 * GROUP, tq), lambda kh, j, i: (kh, i))
    kspec = pl.BlockSpec((tk, HD_B), lambda kh, j, i: (j, kh))
    return pl.pallas_call(
        body, grid=(NKV_B, n // tk, nq),
        in_specs=[qspec, kspec, kspec, qspec, tspec, tspec],
        out_specs=[kspec, kspec],
        out_shape=[jax.ShapeDtypeStruct((n, KV_B), F32), jax.ShapeDtypeStruct((n, KV_B), F32)],
        scratch_shapes=[pltpu.VMEM((tk, HD_B), F32), pltpu.VMEM((tk, HD_B), F32)],
        compiler_params=pltpu.CompilerParams(dimension_semantics=("parallel", "parallel", "arbitrary"),
                                             vmem_limit_bytes=VMEM_LIMIT),
        name="attn_dkv")(q, k, v, do, lse_t, delta_t)


def _attention_bwd(res, do):
    qb, kb, vb, o, lse = res
    t = qb.shape[0]
    delta = jnp.sum((do * o).reshape(t, NH_B, HD_B), axis=-1)
    lse_h = lse.reshape(t, NH_B, HD_B)[:, :, 0]
    delta_b = jnp.broadcast_to(delta[:, :, None], (t, NH_B, HD_B)).reshape(t, Q_B)
    lse_t = jnp.broadcast_to(lse_h.T[:, None, :], (NH_B, 8, t)).reshape(NH_B * 8, t)
    delta_t = jnp.broadcast_to(delta.T[:, None, :], (NH_B, 8, t)).reshape(NH_B * 8, t)
    dob = do.astype(BF16)
    dq = _attn_dq(qb, kb, vb, dob, lse, delta_b)
    dk, dv = _attn_dkv(qb, kb, vb, dob, lse_t, delta_t)
    return dq, dk, dv


def _swap32(y):
    lane = lax.broadcasted_iota(jnp.int32, y.shape, 1)
    return jnp.where((lane // 32) % 2 == 0, pltpu.roll(y, 96, 1), pltpu.roll(y, 32, 1))


ROPE_ROWS = 256


def _norm_rope_fwd(x, w, cos, sin, row_off, rows, heads, name):
    tr, off, width = ROPE_ROWS, row_off // ROPE_ROWS, heads * HD_B

    def body(x_ref, w_ref, c_ref, s_ref, o_ref):
        w, c, s = w_ref[...], c_ref[...], s_ref[...]
        for h in range(heads):
            cols = slice(h * HD_B, (h + 1) * HD_B)
            xh = x_ref[:, cols]
            y = xh * lax.rsqrt(jnp.mean(xh * xh, axis=-1, keepdims=True) + EPS) * w
            o_ref[:, cols] = (y * c + _swap32(y) * s).astype(o_ref.dtype)

    row = pl.BlockSpec((tr, width), lambda i: (i, 0))
    tab = pl.BlockSpec((tr, HD_B), lambda i: (i, 0))
    return pl.pallas_call(
        body, grid=(rows // tr,),
        in_specs=[pl.BlockSpec((tr, width), lambda i: (i + off, 0)), pl.BlockSpec((1, HD_B), lambda i: (0, 0)), tab, tab],
        out_specs=row, out_shape=jax.ShapeDtypeStruct((rows, width), BF16),
        compiler_params=pltpu.CompilerParams(dimension_semantics=("parallel",), vmem_limit_bytes=VMEM_LIMIT),
        name=name)(x, w, cos, sin)


def _norm_rope_bwd(x, w, cos, sin, dy, row_off, extra, name):
    rows, width = dy.shape
    heads = width // HD_B
    r, full = x.shape
    tr, off, nb = ROPE_ROWS, row_off // ROPE_ROWS, rows // ROPE_ROWS

    def body(*refs):
        x_ref, w_ref, c_ref, s_ref, dy_ref = refs[:5]
        e_ref = refs[5] if extra is not None else None
        dx_ref, dw_ref = refs[-2], refs[-1]
        i = pl.program_id(0)

        @pl.when(i == 0)
        def _():
            dw_ref[...] = jnp.zeros(dw_ref.shape, F32)

        @pl.when((i >= off) & (i < off + nb))
        def _():
            w, c, s = w_ref[...], c_ref[...], s_ref[...]
            dw = jnp.zeros((1, HD_B), F32)
            for h in range(heads):
                cols = slice(h * HD_B, (h + 1) * HD_B)
                xh, dyh = x_ref[:, cols], dy_ref[:, cols]
                rs = lax.rsqrt(jnp.mean(xh * xh, axis=-1, keepdims=True) + EPS)
                dn = dyh * c + _swap32(dyh * s)
                dw = dw + jnp.sum(dn * (xh * rs), axis=0, keepdims=True)
                dxn = dn * w
                dx_ref[:, cols] = rs * dxn - xh * (rs * rs * rs * jnp.mean(dxn * xh, axis=-1, keepdims=True))
            if e_ref is not None:
                dx_ref[:, width:] = e_ref[...]
            dw_ref[...] += dw

        @pl.when((i < off) | (i >= off + nb))
        def _():
            dx_ref[...] = jnp.zeros(dx_ref.shape, F32)

    inner = lambda i: jnp.clip(i - off, 0, nb - 1)
    tab = pl.BlockSpec((tr, HD_B), lambda i: (inner(i), 0))
    vec = pl.BlockSpec((1, HD_B), lambda i: (0, 0))
    in_specs = [pl.BlockSpec((tr, width), lambda i: (i, 0)), vec, tab, tab,
                pl.BlockSpec((tr, width), lambda i: (inner(i), 0))]
    args = [x, w, cos, sin, dy]
    if extra is not None:
        in_specs.append(pl.BlockSpec((tr, full - width), lambda i: (inner(i), 0)))
        args.append(extra)
    return pl.pallas_call(
        body, grid=(r // tr,), in_specs=in_specs,
        out_specs=[pl.BlockSpec((tr, full), lambda i: (i, 0)), vec],
        out_shape=[jax.ShapeDtypeStruct((r, full), F32), jax.ShapeDtypeStruct((1, HD_B), F32)],
        compiler_params=pltpu.CompilerParams(dimension_semantics=("arbitrary",), vmem_limit_bytes=VMEM_LIMIT),
        name=name)(*args)


def _rope_tables(t):
    pos = jnp.arange(t)
    row = (pos // GRID_W).astype(F32)
    col = (pos % GRID_W).astype(F32)
    inv = ROPE_THETA ** (-jnp.arange(0, ROT_HALF, 2, dtype=F32) / ROT_HALF)
    ar, ac = row[:, None] * inv[None], col[:, None] * inv[None]
    cos = jnp.concatenate([jnp.cos(ar), jnp.cos(ar), jnp.cos(ac), jnp.cos(ac)], -1)
    sin = jnp.concatenate([-jnp.sin(ar), jnp.sin(ar), -jnp.sin(ac), jnp.sin(ac)], -1)
    return cos, sin


def _gqa_tables(t, n):
    cos, sin = _rope_tables(t)
    cos_k = jnp.concatenate([jnp.ones((n - t, HD_B), F32), cos], 0)
    sin_k = jnp.concatenate([jnp.zeros((n - t, HD_B), F32), sin], 0)
    return cos, sin, cos_k, sin_k


def _make_gqa(t, tc):
    n = tc + t

    @jax.custom_vjp
    def gqa(p_qb, p_kv, qw, kw):
        return fwd(p_qb, p_kv, qw, kw)[0]

    def fwd(p_qb, p_kv, qw, kw):
        cos, sin, cos_k, sin_k = _gqa_tables(t, n)
        q = _norm_rope_fwd(p_qb, qw[None], cos, sin, tc, t, NH_B, "q_norm_rope")
        k = _norm_rope_fwd(p_kv, kw[None], cos_k, sin_k, 0, n, NKV_B, "k_norm_rope")
        vb = p_kv[:n, KV_B:].astype(BF16)
        o, lse = _attn_fwd(q, k, vb)
        return o, (p_qb, p_kv, qw, kw, q, k, vb, o, lse)

    def bwd(res, do):
        p_qb, p_kv, qw, kw, q, k, vb, o, lse = res
        cos, sin, cos_k, sin_k = _gqa_tables(t, n)
        dq, dk, dv = _attention_bwd((q, k, vb, o, lse), do)
        d_qb, dqw = _norm_rope_bwd(p_qb, qw[None], cos, sin, dq, tc, None, "q_norm_rope_bwd")
        d_kv, dkw = _norm_rope_bwd(p_kv, kw[None], cos_k, sin_k, dk, 0, dv, "k_norm_rope_bwd")
        return d_qb, d_kv, dqw[0], dkw[0]

    gqa.defvjp(fwd, bwd)
    return gqa


def _mlstm_chunk_forward(q, k, v, lir, f_pre, s0, n0, m0, reverse):
    L = q.shape[0]
    lfr = jnp.minimum(f_pre, 0.0) - jnp.log1p(jnp.exp(-jnp.abs(f_pre)))
    ti = lax.broadcasted_iota(jnp.int32, (L, L), 0)
    si = lax.broadcasted_iota(jnp.int32, (L, L), 1)
    seen = (si >= ti) if reverse else (si <= ti)
    seen_t = (ti >= si) if reverse else (ti <= si)
    eye = ti == si
    lic = jnp.sum(jnp.where(eye, lir, 0.0), axis=1, keepdims=True)
    lfc = jnp.sum(jnp.where(eye, lfr, 0.0), axis=1, keepdims=True)
    b_col = jnp.sum(jnp.where(seen, lfr, 0.0), axis=1, keepdims=True)
    b_row = jnp.sum(jnp.where(seen_t, lfc, 0.0), axis=0, keepdims=True)
    d = jnp.where(seen, b_col - b_row + lir, -jnp.inf)
    m = jnp.maximum(b_col + m0, jnp.max(d, axis=1, keepdims=True))
    w = jnp.exp(d - m)
    a = jnp.exp(b_col + m0 - m)
    qm, km, vm = q.astype(MX), k.astype(MX), v.astype(MX)
    s = _dot_nt(qm, km) * w
    qs = _dot(qm, s0.astype(MX))
    num = a * qs + _dot(s.astype(MX), vm)
    qn = jnp.sum(q * n0, axis=1, keepdims=True)
    den = a * qn + jnp.sum(s, axis=1, keepdims=True)
    floor = jnp.exp(-m)
    dd = jnp.maximum(jnp.abs(den), floor)
    b_last = jnp.sum(lfr, axis=1, keepdims=True)
    m_end = jnp.maximum(b_last + m0, jnp.max(b_last - b_row + lir, axis=1, keepdims=True))
    w_end = jnp.exp(b_last - b_col + lic - m_end)
    a_end = jnp.exp(b_last + m0 - m_end)
    return dict(eye=eye, seen=seen, w=w, a=a, s=s, qs=qs, num=num, qn=qn, den=den, floor=floor, dd=dd,
                m_end=m_end, w_end=w_end, a_end=a_end, qm=qm, km=km, vm=vm)


def _mlstm_fwd_call(q, k, v, gr, n, row_off, reverse):
    L = MLSTM_CHUNK
    nc, off = n // L, row_off // L
    pos = (lambda i: nc - 1 - i) if reverse else (lambda i: i)

    def body(q_ref, k_ref, v_ref, gr_ref, h_ref, s0_ref, n0_ref, m0_ref, s_sc, n_sc, m_sc):
        @pl.when(pl.program_id(1) == 0)
        def _():
            s_sc[...] = jnp.zeros(s_sc.shape, F32)
            n_sc[...] = jnp.zeros(n_sc.shape, F32)
            m_sc[...] = jnp.full(m_sc.shape, M_INIT, F32)

        s0, n0, m0 = s_sc[...], n_sc[...], m_sc[...]
        s0_ref[0, 0] = s0
        n0_ref[0, 0] = n0
        m0_ref[0, 0] = jnp.broadcast_to(m0, (1, DK_A))
        k, v = k_ref[...], v_ref[...]
        f = _mlstm_chunk_forward(q_ref[...], k, v, gr_ref[0, 0], gr_ref[1, 0], s0, n0, m0, reverse)
        h_ref[...] = f["num"] / f["dd"]
        s_sc[...] = f["a_end"] * s0 + _dot_tn(f["km"], (f["w_end"] * v).astype(MX))
        n_sc[...] = f["a_end"] * n0 + jnp.sum(f["w_end"] * k, axis=0, keepdims=True)
        m_sc[...] = f["m_end"]

    qk_spec = pl.BlockSpec((L, DK_A), lambda h, i: (off + pos(i), h))
    v_spec = pl.BlockSpec((L, DV_A), lambda h, i: (off + pos(i), h))
    gr_spec = pl.BlockSpec((2, 1, 1, L), lambda h, i: (0, h, 0, off + pos(i)))
    h_spec = pl.BlockSpec((L, DV_A), lambda h, i: (pos(i), h))
    st_spec = pl.BlockSpec((1, 1, DK_A, DV_A), lambda h, i: (h, pos(i), 0, 0))
    vec_spec = pl.BlockSpec((1, 1, 1, DK_A), lambda h, i: (h, pos(i), 0, 0))
    return pl.pallas_call(
        body, grid=(NH_A, nc),
        in_specs=[qk_spec, qk_spec, v_spec, gr_spec],
        out_specs=[h_spec, st_spec, vec_spec, vec_spec],
        out_shape=[jax.ShapeDtypeStruct((n, V_A), F32), jax.ShapeDtypeStruct((NH_A, nc, DK_A, DV_A), F32),
                   jax.ShapeDtypeStruct((NH_A, nc, 1, DK_A), F32), jax.ShapeDtypeStruct((NH_A, nc, 1, DK_A), F32)],
        scratch_shapes=[pltpu.VMEM((DK_A, DV_A), F32), pltpu.VMEM((1, DK_A), F32), pltpu.VMEM((1, 1), F32)],
        compiler_params=pltpu.CompilerParams(dimension_semantics=("parallel", "arbitrary"),
                                             vmem_limit_bytes=VMEM_LIMIT),
        name="mlstm_fwd")(q, k, v, gr)


def _mlstm_bwd_call(q, k, v, gr, s0_all, n0_all, m0_all, dh, n, row_off, reverse):
    L = MLSTM_CHUNK
    nc, off = n // L, row_off // L
    pos = (lambda i: i) if reverse else (lambda i: nc - 1 - i)

    def body(q_ref, k_ref, v_ref, gr_ref, s0_ref, n0_ref, m0_ref, dh_ref,
             dq_ref, dk_ref, dv_ref, dg_ref, ds_sc, dn_sc):
        @pl.when(pl.program_id(1) == 0)
        def _():
            ds_sc[...] = jnp.zeros(ds_sc.shape, F32)
            dn_sc[...] = jnp.zeros(dn_sc.shape, F32)

        q, k, v = q_ref[...], k_ref[...], v_ref[...]
        s0, n0, m0 = s0_ref[0, 0], n0_ref[0, 0], m0_ref[0, 0][:, 0:1]
        f = _mlstm_chunk_forward(q, k, v, gr_ref[0, 0], gr_ref[1, 0], s0, n0, m0, reverse)
        w, a, s = f["w"], f["a"], f["s"]
        qm, km, vm, w_end, a_end = f["qm"], f["km"], f["vm"], f["w_end"], f["a_end"]
        ds1, dn1 = ds_sc[...], dn_sc[...]
        ds1m, s0m = ds1.astype(MX), s0.astype(MX)

        inv = 1.0 / f["dd"]
        dh = dh_ref[...]
        dnum = dh * inv
        ddd = -jnp.sum(dh * (f["num"] * inv), axis=1, keepdims=True) * inv
        dden = jnp.where(jnp.abs(f["den"]) > f["floor"], jnp.sign(f["den"]) * ddd, 0.0)
        adn = (a * dnum).astype(MX)
        dnm = dnum.astype(MX)
        ds_tot = _dot_nt(dnm, vm) + dden
        dsr = (ds_tot * w).astype(MX)
        e = ds_tot * s
        wv = (w_end * v).astype(MX)
        kds = _dot(km, ds1m)
        dq_ref[...] = _dot_nt(adn, s0m) + _dot(dsr, km) + (dden * a) * n0
        dk_ref[...] = _dot_tn(dsr, qm) + _dot_nt(wv, ds1m) + w_end * dn1
        dv_ref[...] = _dot_tn(s.astype(MX), dnm) + w_end * kds

        eye = f["eye"]
        to_col = lambda r: jnp.sum(jnp.where(eye, r, 0.0), axis=1, keepdims=True)
        to_row = lambda c: jnp.sum(jnp.where(eye, c, 0.0), axis=0, keepdims=True)
        g_a = (jnp.sum(dnum * f["qs"], axis=1, keepdims=True) + dden * f["qn"]) * a
        g_w = (jnp.sum(v * kds, axis=1, keepdims=True) + jnp.sum(k * dn1, axis=1, keepdims=True)) * w_end
        g_end = (jnp.sum(jnp.sum(ds1 * s0, axis=1, keepdims=True), axis=0, keepdims=True)
                 + jnp.sum(dn1 * n0, axis=1, keepdims=True)) * a_end
        col_e = jnp.sum(e, axis=0, keepdims=True)
        db = jnp.sum(e, axis=1, keepdims=True) - to_col(col_e) + g_a - g_w
        last = lax.broadcasted_iota(jnp.int32, (L, 1), 0) == (0 if reverse else L - 1)
        db = db + jnp.where(last, jnp.sum(g_w, axis=0, keepdims=True) + g_end, 0.0)
        dg_ref[0, 0] = col_e + to_row(g_w)
        dlf = jnp.sum(jnp.where(f["seen"], db, 0.0), axis=0, keepdims=True)
        dg_ref[1, 0] = dlf * jax.nn.sigmoid(-gr_ref[1, 0])

        ds_sc[...] = a_end * ds1 + _dot_tn(qm, adn)
        dn_sc[...] = a_end * dn1 + jnp.sum((dden * a) * q, axis=0, keepdims=True)

    qk_spec = pl.BlockSpec((L, DK_A), lambda h, i: (off + pos(i), h))
    v_spec = pl.BlockSpec((L, DV_A), lambda h, i: (off + pos(i), h))
    gr_spec = pl.BlockSpec((2, 1, 1, L), lambda h, i: (0, h, 0, off + pos(i)))
    st_spec = pl.BlockSpec((1, 1, DK_A, DV_A), lambda h, i: (h, pos(i), 0, 0))
    vec_spec = pl.BlockSpec((1, 1, 1, DK_A), lambda h, i: (h, pos(i), 0, 0))
    oqk_spec = pl.BlockSpec((L, DK_A), lambda h, i: (pos(i), h))
    ov_spec = pl.BlockSpec((L, DV_A), lambda h, i: (pos(i), h))
    og_spec = pl.BlockSpec((2, 1, 1, L), lambda h, i: (0, h, 0, pos(i)))
    return pl.pallas_call(
        body, grid=(NH_A, nc),
        in_specs=[qk_spec, qk_spec, v_spec, gr_spec, st_spec, vec_spec, vec_spec, ov_spec],
        out_specs=[oqk_spec, oqk_spec, ov_spec, og_spec],
        out_shape=[jax.ShapeDtypeStruct((n, QK_A), F32), jax.ShapeDtypeStruct((n, QK_A), F32),
                   jax.ShapeDtypeStruct((n, V_A), F32), jax.ShapeDtypeStruct((2, NH_A, 1, n), F32)],
        scratch_shapes=[pltpu.VMEM((DK_A, DV_A), F32), pltpu.VMEM((1, DK_A), F32)],
        compiler_params=pltpu.CompilerParams(dimension_semantics=("parallel", "arbitrary"),
                                             vmem_limit_bytes=VMEM_LIMIT),
        name="mlstm_bwd")(q, k, v, gr, s0_all, n0_all, m0_all, dh)


def _make_mlstm(n, row_off, reverse):
    def gate_rows(li, lf):
        return jnp.stack([li, lf]).transpose(0, 2, 1)[:, :, None, :]

    @jax.custom_vjp
    def op(q, k, v, li, lf):
        return _mlstm_fwd_call(q, k, v, gate_rows(li, lf), n, row_off, reverse)[0]

    def fwd(q, k, v, li, lf):
        gr = gate_rows(li, lf)
        h, s0, n0, m0 = _mlstm_fwd_call(q, k, v, gr, n, row_off, reverse)
        return h, (q, k, v, gr, s0, n0, m0)

    def bwd(res, dh):
        q, k, v, gr, s0, n0, m0 = res
        dq, dk, dv, dg = _mlstm_bwd_call(q, k, v, gr, s0, n0, m0, dh, n, row_off, reverse)
        rows = ((row_off, q.shape[0] - row_off - n), (0, 0))
        dg = jnp.pad(dg[:, :, 0, :].transpose(0, 2, 1), ((0, 0),) + rows)
        return jnp.pad(dq, rows), jnp.pad(dk, rows), jnp.pad(dv, rows), dg[0], dg[1]

    op.defvjp(fwd, bwd)
    return op


MERGE_ROWS = 128


def _sig(x):
    return jax.nn.sigmoid(x)


def _merge_pre_fwd(h_f, h_b, o_attn, p_oa, p_za, p_zb, mh_w, tc):
    t = o_attn.shape[0]
    tr, off = MERGE_ROWS, tc // MERGE_ROWS

    def body(hf_ref, hb_ref, oat_ref, oa_ref, za_ref, zb_ref, w_ref, a_ref, b_ref):
        for hd in range(NH_A):
            cols = slice(hd * DV_A, (hd + 1) * DV_A)
            h = hf_ref[:, cols] + hb_ref[:, cols]
            hn = h * lax.rsqrt(jnp.mean(h * h, axis=-1, keepdims=True) + EPS) * w_ref[:, cols]
            za = za_ref[:, cols]
            a_ref[:, cols] = (_sig(oa_ref[:, cols]) * hn * (za * _sig(za))).astype(BF16)
        zb = zb_ref[...]
        b_ref[...] = (oat_ref[...] * (zb * _sig(zb))).astype(BF16)

    lat = pl.BlockSpec((tr, V_A), lambda i: (i + off, 0))
    row = pl.BlockSpec((tr, V_A), lambda i: (i, 0))
    return pl.pallas_call(
        body, grid=(t // tr,),
        in_specs=[lat, row, row, lat, lat, lat, pl.BlockSpec((1, V_A), lambda i: (0, 0))],
        out_specs=[row, row],
        out_shape=[jax.ShapeDtypeStruct((t, V_A), BF16), jax.ShapeDtypeStruct((t, V_A), BF16)],
        compiler_params=pltpu.CompilerParams(dimension_semantics=("parallel",), vmem_limit_bytes=VMEM_LIMIT),
        name="merge_pre")(h_f, h_b, o_attn, p_oa, p_za, p_zb, mh_w)


def _ctx_block(i, nb, off):
    k = i - nb
    return jnp.where(i < nb, i + off, jnp.where(k < off, k, k + nb))


def _merge_pre_bwd(da, db, h_f, h_b, o_attn, p_oa, p_za, p_zb, mh_w, tc):
    t = o_attn.shape[0]
    n, r = h_f.shape[0], p_oa.shape[0]
    tr, off = MERGE_ROWS, tc // MERGE_ROWS
    nb = t // tr
    n_ctx = r // tr - nb

    def body(da_ref, db_ref, hf_ref, hb_ref, oat_ref, oa_ref, za_ref, zb_ref, w_ref,
             dhf_ref, dhb_ref, doat_ref, doa_ref, dza_ref, dzb_ref, dw_ref):
        i = pl.program_id(0)

        @pl.when(i == 0)
        def _():
            dw_ref[...] = jnp.zeros(dw_ref.shape, F32)

        @pl.when(i < nb)
        def _():
            for hd in range(NH_A):
                cols = slice(hd * DV_A, (hd + 1) * DV_A)
                h = hf_ref[:, cols] + hb_ref[:, cols]
                rs = lax.rsqrt(jnp.mean(h * h, axis=-1, keepdims=True) + EPS)
                w = w_ref[:, cols]
                hn = h * rs * w
                oa, za, g = oa_ref[:, cols], za_ref[:, cols], da_ref[:, cols]
                so, sz = _sig(oa), _sig(za)
                silu_z = za * sz
                doa_ref[:, cols] = g * hn * silu_z * so * (1.0 - so)
                dza_ref[:, cols] = g * so * hn * (sz * (1.0 + za * (1.0 - sz)))
                dhn = g * so * silu_z
                dw_ref[:, cols] += jnp.sum(dhn * (h * rs), axis=0, keepdims=True)
                dxn = dhn * w
                dh = rs * dxn - h * (rs * rs * rs * jnp.mean(dxn * h, axis=-1, keepdims=True))
                dhf_ref[:, cols] = dh
                dhb_ref[:, cols] = dh
            zb, gb, oat = zb_ref[...], db_ref[...], oat_ref[...]
            sb = _sig(zb)
            doat_ref[...] = gb * (zb * sb)
            dzb_ref[...] = gb * oat * (sb * (1.0 + zb * (1.0 - sb)))

        @pl.when(i >= nb)
        def _():
            for ref in (dhf_ref, dhb_ref, doa_ref, dza_ref, dzb_ref):
                ref[...] = jnp.zeros(ref.shape, F32)

    lati = lambda i: jnp.minimum(i, nb - 1)
    lat = pl.BlockSpec((tr, V_A), lambda i: (lati(i) + off, 0))
    row = pl.BlockSpec((tr, V_A), lambda i: (lati(i), 0))
    vec = pl.BlockSpec((1, V_A), lambda i: (0, 0))
    pout = pl.BlockSpec((tr, V_A), lambda i: (_ctx_block(i, nb, off), 0))
    hf_out = pl.BlockSpec((tr, V_A), lambda i: (jnp.where(i < nb, i + off, jnp.minimum(i - nb, off - 1)), 0))
    hb_out = pl.BlockSpec((tr, V_A), lambda i: (jnp.where(i < nb, i, nb + jnp.minimum(i - nb, off - 1)), 0))
    return pl.pallas_call(
        body, grid=(nb + n_ctx,),
        in_specs=[row, row, lat, row, row, lat, lat, lat, vec],
        out_specs=[hf_out, hb_out, row, pout, pout, pout, vec],
        out_shape=[jax.ShapeDtypeStruct((n, V_A), F32), jax.ShapeDtypeStruct((n, V_A), F32),
                   jax.ShapeDtypeStruct((t, V_A), F32), jax.ShapeDtypeStruct((r, V_A), F32),
                   jax.ShapeDtypeStruct((r, V_A), F32), jax.ShapeDtypeStruct((r, V_A), F32),
                   jax.ShapeDtypeStruct((1, V_A), F32)],
        compiler_params=pltpu.CompilerParams(dimension_semantics=("arbitrary",), vmem_limit_bytes=VMEM_LIMIT),
        name="merge_pre_bwd")(da, db, h_f, h_b, o_attn, p_oa, p_za, p_zb, mh_w)


def _merge_gate_fwd(y_a, y_b, p_ga, p_gb, tc):
    t = y_a.shape[0]
    tr, off = MERGE_ROWS, tc // MERGE_ROWS

    def body(ya_ref, yb_ref, ga_ref, gb_ref, m_ref):
        m_ref[...] = (_sig(ga_ref[...]) * ya_ref[...] + _sig(gb_ref[...]) * yb_ref[...]).astype(BF16)

    lat = pl.BlockSpec((tr, D_MODEL), lambda i: (i + off, 0))
    row = pl.BlockSpec((tr, D_MODEL), lambda i: (i, 0))
    return pl.pallas_call(
        body, grid=(t // tr,), in_specs=[row, row, lat, lat], out_specs=row,
        out_shape=jax.ShapeDtypeStruct((t, D_MODEL), BF16),
        compiler_params=pltpu.CompilerParams(dimension_semantics=("parallel",), vmem_limit_bytes=VMEM_LIMIT),
        name="merge_gate")(y_a, y_b, p_ga, p_gb)


def _merge_gate_bwd(dm, y_a, y_b, p_ga, p_gb, tc):
    t, r = y_a.shape[0], p_ga.shape[0]
    tr, off = MERGE_ROWS, tc // MERGE_ROWS
    nb = t // tr
    n_ctx = r // tr - nb

    def body(dm_ref, ya_ref, yb_ref, ga_ref, gb_ref, dya_ref, dyb_ref, dga_ref, dgb_ref):
        i = pl.program_id(0)

        @pl.when(i < nb)
        def _():
            dm = dm_ref[...]
            sa, sb = _sig(ga_ref[...]), _sig(gb_ref[...])
            dya_ref[...] = (dm * sa).astype(BF16)
            dyb_ref[...] = (dm * sb).astype(BF16)
            dga_ref[...] = dm * ya_ref[...] * sa * (1.0 - sa)
            dgb_ref[...] = dm * yb_ref[...] * sb * (1.0 - sb)

        @pl.when(i >= nb)
        def _():
            dga_ref[...] = jnp.zeros(dga_ref.shape, F32)
            dgb_ref[...] = jnp.zeros(dgb_ref.shape, F32)

    lati = lambda i: jnp.minimum(i, nb - 1)
    lat = pl.BlockSpec((tr, D_MODEL), lambda i: (lati(i) + off, 0))
    row = pl.BlockSpec((tr, D_MODEL), lambda i: (lati(i), 0))
    pout = pl.BlockSpec((tr, D_MODEL), lambda i: (_ctx_block(i, nb, off), 0))
    return pl.pallas_call(
        body, grid=(nb + n_ctx,), in_specs=[row, row, row, lat, lat], out_specs=[row, row, pout, pout],
        out_shape=[jax.ShapeDtypeStruct((t, D_MODEL), BF16), jax.ShapeDtypeStruct((t, D_MODEL), BF16),
                   jax.ShapeDtypeStruct((r, D_MODEL), F32), jax.ShapeDtypeStruct((r, D_MODEL), F32)],
        compiler_params=pltpu.CompilerParams(dimension_semantics=("arbitrary",), vmem_limit_bytes=VMEM_LIMIT),
        name="merge_gate_bwd")(dm, y_a, y_b, p_ga, p_gb)


def _make_merge_block(tc):
    @jax.custom_vjp
    def block(h_f, h_b, o_attn, p_oa, p_za, p_zb, p_ga, p_gb, mh_w, w_ba, w_bb, w_out, pr_ba, pr_bb, pr_out):
        return fwd(h_f, h_b, o_attn, p_oa, p_za, p_zb, p_ga, p_gb, mh_w, w_ba, w_bb, w_out, pr_ba, pr_bb, pr_out)[0]

    def fwd(h_f, h_b, o_attn, p_oa, p_za, p_zb, p_ga, p_gb, mh_w, w_ba, w_bb, w_out, pr_ba, pr_bb, pr_out):
        a_in, b_in = _merge_pre_fwd(h_f, h_b, o_attn, p_oa, p_za, p_zb, mh_w[None], tc)
        y_a, y_b = _mm_nn(a_in, w_ba, "merge_ya"), _mm_nn(b_in, w_bb, "merge_yb")
        m_in = _merge_gate_fwd(y_a, y_b, p_ga, p_gb, tc)
        out = _mm_nn(m_in, w_out, "merge_out")
        return out, (h_f, h_b, o_attn, p_oa, p_za, p_zb, p_ga, p_gb, mh_w, w_ba, w_bb, w_out, a_in, b_in, y_a, y_b, m_in)

    def bwd(res, dout):
        h_f, h_b, o_attn, p_oa, p_za, p_zb, p_ga, p_gb, mh_w, w_ba, w_bb, w_out, a_in, b_in, y_a, y_b, m_in = res
        dm = _mm_nt(dout, w_out, "merge_out_da")
        dw_out = _mm_tn(m_in, dout, "merge_out_dw")
        dy_a, dy_b, dga, dgb = _merge_gate_bwd(dm, y_a, y_b, p_ga, p_gb, tc)
        da, db = _mm_nt(dy_a, w_ba, "merge_ya_da"), _mm_nt(dy_b, w_bb, "merge_yb_da")
        dw_ba, dw_bb = _mm_tn(a_in, dy_a, "merge_ya_dw"), _mm_tn(b_in, dy_b, "merge_yb_dw")
        dhf, dhb, doat, doa, dza, dzb, dmh = _merge_pre_bwd(da, db, h_f, h_b, o_attn, p_oa, p_za, p_zb, mh_w[None], tc)
        z = jnp.zeros_like
        return (dhf, dhb, doat, doa, dza, dzb, dga, dgb, dmh[0], z(w_ba), z(w_bb), z(w_out), dw_ba, dw_bb, dw_out)

    block.defvjp(fwd, bwd)
    return block


def _silu(x):
    return x * jax.nn.sigmoid(x)


CONV_ROWS, CONV_HALO = 256, 8


def _make_conv(t, tc):
    r = t + 2 * tc
    width = 2 * QK_A
    nblk = r // CONV_ROWS
    cb, nb = tc // CONV_ROWS, t // CONV_ROWS
    k_scale = DK_A ** -0.5
    per = CONV_ROWS // CONV_HALO

    def taps(x_ref, prev_ref, next_ref):
        i = pl.program_id(0)
        seg_first = (i == 0) | (i == cb) | (i == cb + nb)
        seg_last = (i == cb - 1) | (i == cb + nb - 1) | (i == nblk - 1)
        x = x_ref[...]
        rows = lax.broadcasted_iota(jnp.int32, (CONV_ROWS, 1), 0)
        before = jnp.where(seg_first, 0.0, prev_ref[CONV_HALO - 1:CONV_HALO, :])
        after = jnp.where(seg_last, 0.0, next_ref[0:1, :])
        xm1 = jnp.where(rows == 0, before, pltpu.roll(x, 1, 0))
        xp1 = jnp.where(rows == CONV_ROWS - 1, after, pltpu.roll(x, CONV_ROWS - 1, 0))
        return xm1, x, xp1

    row = pl.BlockSpec((CONV_ROWS, width), lambda i: (i, 0))
    prev = pl.BlockSpec((CONV_HALO, width), lambda i: (jnp.maximum(i * per - 1, 0), 0))
    nxt = pl.BlockSpec((CONV_HALO, width), lambda i: (jnp.minimum((i + 1) * per, r // CONV_HALO - 1), 0))
    half = pl.BlockSpec((CONV_ROWS, QK_A), lambda i: (i, 0))
    wspec = pl.BlockSpec((3, width), lambda i: (0, 0))
    bspec = pl.BlockSpec((1, width), lambda i: (0, 0))
    par = pltpu.CompilerParams(dimension_semantics=("parallel",), vmem_limit_bytes=VMEM_LIMIT)
    seq = pltpu.CompilerParams(dimension_semantics=("arbitrary",), vmem_limit_bytes=VMEM_LIMIT)

    def fwd_call(x, cw, cb_):
        def body(x_ref, p_ref, n_ref, w_ref, b_ref, q_ref, k_ref):
            xm1, x0, xp1 = taps(x_ref, p_ref, n_ref)
            c = b_ref[...] + xm1 * w_ref[0:1, :] + x0 * w_ref[1:2, :] + xp1 * w_ref[2:3, :]
            y = c * jax.nn.sigmoid(c)
            q_ref[...] = y[:, :QK_A]
            k_ref[...] = y[:, QK_A:] * k_scale

        return pl.pallas_call(
            body, grid=(nblk,), in_specs=[row, prev, nxt, wspec, bspec], out_specs=[half, half],
            out_shape=[jax.ShapeDtypeStruct((r, QK_A), F32), jax.ShapeDtypeStruct((r, QK_A), F32)],
            compiler_params=par, name="conv_silu")(x, x, x, cw, cb_)

    def bwd_pre_call(dq, dk, x, cw, cb_):
        def body(dq_ref, dk_ref, x_ref, p_ref, n_ref, w_ref, b_ref, dc_ref, dw_ref, db_ref):
            @pl.when(pl.program_id(0) == 0)
            def _():
                dw_ref[...] = jnp.zeros(dw_ref.shape, F32)
                db_ref[...] = jnp.zeros(db_ref.shape, F32)

            xm1, x0, xp1 = taps(x_ref, p_ref, n_ref)
            c = b_ref[...] + xm1 * w_ref[0:1, :] + x0 * w_ref[1:2, :] + xp1 * w_ref[2:3, :]
            s = jax.nn.sigmoid(c)
            dy = jnp.concatenate([dq_ref[...], dk_ref[...] * k_scale], axis=1)
            dc = dy * (s * (1.0 + c * (1.0 - s)))
            dc_ref[...] = dc
            db_ref[...] += jnp.sum(dc, axis=0, keepdims=True)
            dw_ref[0:1, :] += jnp.sum(dc * xm1, axis=0, keepdims=True)
            dw_ref[1:2, :] += jnp.sum(dc * x0, axis=0, keepdims=True)
            dw_ref[2:3, :] += jnp.sum(dc * xp1, axis=0, keepdims=True)

        return pl.pallas_call(
            body, grid=(nblk,), in_specs=[half, half, row, prev, nxt, wspec, bspec], out_specs=[row, wspec, bspec],
            out_shape=[jax.ShapeDtypeStruct((r, width), F32), jax.ShapeDtypeStruct((3, width), F32),
                       jax.ShapeDtypeStruct((1, width), F32)],
            compiler_params=seq, name="conv_silu_bwd")(dq, dk, x, x, x, cw, cb_)

    def bwd_x_call(dc, cw):
        def body(d_ref, p_ref, n_ref, w_ref, dx_ref):
            dm1, d0, dp1 = taps(d_ref, p_ref, n_ref)
            dx_ref[...] = dm1 * w_ref[2:3, :] + d0 * w_ref[1:2, :] + dp1 * w_ref[0:1, :]

        return pl.pallas_call(
            body, grid=(nblk,), in_specs=[row, prev, nxt, wspec], out_specs=row,
            out_shape=jax.ShapeDtypeStruct((r, width), F32), compiler_params=par,
            name="conv_silu_bwd_x")(dc, dc, dc, cw)

    @jax.custom_vjp
    def op(x, cw, cb_):
        return tuple(fwd_call(x, cw, cb_[None]))

    def fwd(x, cw, cb_):
        return tuple(fwd_call(x, cw, cb_[None])), (x, cw, cb_)

    def bwd(res, cot):
        x, cw, cb_ = res
        dc, dw, db = bwd_pre_call(cot[0], cot[1], x, cw, cb_[None])
        return bwd_x_call(dc, cw), dw, db[0]

    op.defvjp(fwd, bwd)
    return op


def _local_loss(diff, const):
    x, ctx, target = diff["x"], const["ctx"], const["target"]
    t, tc = x.shape[0], ctx.shape[0]
    n = tc + t

    mod = diff["mod"]
    shift, scale, gate = mod[0, :D_MODEL], mod[0, D_MODEL:2 * D_MODEL], mod[0, 2 * D_MODEL:]
    shift_c, scale_c = mod[1, :D_MODEL], mod[1, D_MODEL:2 * D_MODEL]
    (p_qk, p_va, p_kv, p_oa, p_za, p_qb, p_zb, p_ga, p_gb), p_if = _ln_project(
        x, ctx, jnp.stack([scale, shift, scale_c, shift_c]), const["w_main"], const["w_if"], diff["p_main"], diff["p_if"])
    gt = p_if[:, :N_IF] + diff["b_if"]

    q_a, k_a = _make_conv(t, tc)(p_qk, diff["conv_w"], diff["conv_b"])
    v_a = p_va
    li_f, lf_f, li_b, lf_b = gt[:, 0:8], gt[:, 8:16], gt[:, 16:24], gt[:, 24:32]

    h_f = _make_mlstm(n, 0, False)(q_a, k_a, v_a, li_f, lf_f)
    h_b = _make_mlstm(n, tc, True)(q_a, k_a, v_a, li_b, lf_b)

    o_attn = _make_gqa(t, tc)(p_qb, p_kv, diff["q_norm_w"], diff["k_norm_w"])

    out = _make_merge_block(tc)(h_f, h_b, o_attn, p_oa, p_za, p_zb, p_ga, p_gb, diff["mh_norm_w"],
                                const["w_ba"], const["w_bb"], const["w_out"], diff["p_ba"], diff["p_bb"], diff["p_out"])

    return _loss_head(x, out, target, gate, diff["ln_w"], diff["ln_b"])


OTHER_CHIPS = [(1, 0), (0, 1), (1, 1)]


def _flip(v, bit):
    return 1 - v if bit else v


def _gather_chips(shard, name):
    def body(x_ref, o_ref, send_sems, recv_sems, local_sem):
        x, y, c = lax.axis_index("x"), lax.axis_index("y"), lax.axis_index("c")
        mine = pltpu.make_async_copy(x_ref, o_ref.at[2 * x + y], local_sem)
        mine.start()

        def copy(r, slot):
            dx, dy = OTHER_CHIPS[r]
            return pltpu.make_async_remote_copy(
                src_ref=x_ref, dst_ref=o_ref.at[slot], send_sem=send_sems.at[r], recv_sem=recv_sems.at[r],
                device_id=(_flip(x, dx), _flip(y, dy), c), device_id_type=MESH)

        sends = [copy(r, 2 * x + y) for r in range(3)]
        for cp in sends:
            cp.start()
        for r, (dx, dy) in enumerate(OTHER_CHIPS):
            copy(r, 2 * _flip(x, dx) + _flip(y, dy)).wait_recv()
        for cp in sends:
            cp.wait_send()
        mine.wait()

    return pl.pallas_call(
        body, out_shape=jax.ShapeDtypeStruct((N_CHIPS,) + shard.shape, shard.dtype),
        in_specs=[pl.BlockSpec(memory_space=pl.ANY)], out_specs=pl.BlockSpec(memory_space=pl.ANY),
        scratch_shapes=[pltpu.SemaphoreType.DMA((3,)), pltpu.SemaphoreType.DMA((3,)), pltpu.SemaphoreType.DMA],
        name=name)(shard)


def _gather_chips_halves(shard, name):
    rows, cols = shard.shape
    halves = shard.reshape(2, rows // 2, cols)

    def body(x_ref, o_ref, send_sems, recv_sems, local_sem):
        x, y, c = lax.axis_index("x"), lax.axis_index("y"), lax.axis_index("c")
        my_chip = 2 * x + y
        mine = pltpu.make_async_copy(x_ref, o_ref.at[my_chip], local_sem)
        mine.start()

        def chip_of(r):
            dx, dy = OTHER_CHIPS[r]
            return _flip(x, dx), _flip(y, dy)

        def copy(k, chip_slot, half, to, src=None):
            dst = o_ref.at[chip_slot, half]
            return pltpu.make_async_remote_copy(
                src_ref=dst if src is None else src, dst_ref=dst, send_sem=send_sems.at[k],
                recv_sem=recv_sems.at[k], device_id=to, device_id_type=MESH)

        first = [copy(r, my_chip, c, (*chip_of(r), c), src=x_ref.at[c]) for r in range(3)]
        for cp in first:
            cp.start()
        passed = []
        for r in range(3):
            px, py = chip_of(r)
            copy(r, 2 * px + py, c, (px, py, c)).wait_recv()
            passed.append(copy(3 + r, 2 * px + py, c, (x, y, 1 - c)))
            passed[-1].start()
        for r in range(3):
            px, py = chip_of(r)
            copy(3 + r, 2 * px + py, 1 - c, (x, y, 1 - c)).wait_recv()
        for cp in first + passed:
            cp.wait_send()
        mine.wait()

    out = pl.pallas_call(
        body, out_shape=jax.ShapeDtypeStruct((N_CHIPS, 2, rows // 2, cols), shard.dtype),
        in_specs=[pl.BlockSpec(memory_space=pl.ANY)], out_specs=pl.BlockSpec(memory_space=pl.ANY),
        scratch_shapes=[pltpu.SemaphoreType.DMA((6,)), pltpu.SemaphoreType.DMA((6,)), pltpu.SemaphoreType.DMA],
        name=name)(halves)
    return out.reshape(N_CHIPS, rows, cols)


def _scatter_grads(slots, name):
    def body(g_ref, o_ref, send_sems, recv_sems, local_sem):
        x, y, c = lax.axis_index("x"), lax.axis_index("y"), lax.axis_index("c")
        me, my_chip, sibling = 4 * x + 2 * y + c, 2 * x + y, (x, y, 1 - c)
        mine = pltpu.make_async_copy(g_ref.at[my_chip], o_ref.at[me], local_sem)
        mine.start()

        def chip_of(r):
            dx, dy = OTHER_CHIPS[r]
            return _flip(x, dx), _flip(y, dy)

        def copy(k, slot, to, src=None):
            dst = o_ref.at[slot]
            return pltpu.make_async_remote_copy(
                src_ref=dst if src is None else src, dst_ref=dst, send_sem=send_sems.at[k],
                recv_sem=recv_sems.at[k], device_id=to, device_id_type=MESH)

        first = [copy(0, me, sibling, src=g_ref.at[my_chip])]
        for r in range(3):
            px, py = chip_of(r)
            first.append(copy(1 + r, me, (px, py, c), src=g_ref.at[2 * px + py]))
        for cp in first:
            cp.start()
        passed = []
        for r in range(3):
            px, py = chip_of(r)
            copy(1 + r, 4 * px + 2 * py + c, (px, py, c)).wait_recv()
            passed.append(copy(4 + r, 4 * px + 2 * py + c, sibling))
            passed[-1].start()
        copy(0, 4 * x + 2 * y + 1 - c, sibling).wait_recv()
        for r in range(3):
            px, py = chip_of(r)
            copy(4 + r, 4 * px + 2 * py + 1 - c, sibling).wait_recv()
        for cp in first + passed:
            cp.wait_send()
        mine.wait()

    return pl.pallas_call(
        body, out_shape=jax.ShapeDtypeStruct((N_DEV,) + slots.shape[1:], slots.dtype),
        in_specs=[pl.BlockSpec(memory_space=pl.ANY)], out_specs=pl.BlockSpec(memory_space=pl.ANY),
        scratch_shapes=[pltpu.SemaphoreType.DMA((N_DEV - 1,)), pltpu.SemaphoreType.DMA((N_DEV - 1,)),
                        pltpu.SemaphoreType.DMA],
        name=name)(slots)


def _allreduce_small(v, name):
    def body(v_ref, o_ref, buf, send_sems, recv_sems):
        x, y, c = lax.axis_index("x"), lax.axis_index("y"), lax.axis_index("c")
        me = 4 * x + 2 * y + c
        buf[me] = v_ref[...]

        def peer(r):
            return _flip(x, (r >> 2) & 1), _flip(y, (r >> 1) & 1), _flip(c, r & 1)

        def copy(r, dst_slot):
            return pltpu.make_async_remote_copy(
                src_ref=v_ref, dst_ref=buf.at[dst_slot], send_sem=send_sems.at[r - 1],
                recv_sem=recv_sems.at[r - 1], device_id=peer(r), device_id_type=MESH)

        sends = [copy(r, me) for r in range(1, N_DEV)]
        for cp in sends:
            cp.start()
        for r in range(1, N_DEV):
            px, py, pc = peer(r)
            copy(r, 4 * px + 2 * py + pc).wait_recv()
        for cp in sends:
            cp.wait_send()
        acc = buf[0]
        for d in range(1, N_DEV):
            acc = acc + buf[d]
        o_ref[...] = acc

    return pl.pallas_call(
        body, out_shape=jax.ShapeDtypeStruct(v.shape, v.dtype),
        in_specs=[pl.BlockSpec(memory_space=pltpu.VMEM)], out_specs=pl.BlockSpec(memory_space=pltpu.VMEM),
        scratch_shapes=[pltpu.VMEM((N_DEV,) + v.shape, v.dtype), pltpu.SemaphoreType.DMA((N_DEV - 1,)),
                        pltpu.SemaphoreType.DMA((N_DEV - 1,))],
        name=name)(v)


def _adamw_math(w, g, m, v):
    m = ADAM_B1 * m + (1.0 - ADAM_B1) * g
    v = ADAM_B2 * v + (1.0 - ADAM_B2) * jnp.square(g)
    m_hat = m / (1.0 - ADAM_B1 ** ADAM_STEP)
    v_hat = v / (1.0 - ADAM_B2 ** ADAM_STEP)
    delta = -ADAM_LR * (m_hat / (jnp.sqrt(v_hat) + ADAM_EPS) + ADAM_WD * w)
    return delta, m, v


def _adamw_sum(parts, w, m, v, name):
    npart, rows, cols = parts.shape
    tr = _pick(rows, (64, 32, 16, 8)) if rows >= 8 else rows

    def body(p_ref, w_ref, m_ref, v_ref, g_out, d_out, m_out, v_out):
        g = p_ref[0].astype(F32)
        for k in range(1, npart):
            g = g + p_ref[k].astype(F32)
        d, m2, v2 = _adamw_math(w_ref[...], g, m_ref[...], v_ref[...])
        g_out[...] = g
        d_out[...] = d
        m_out[...] = m2
        v_out[...] = v2

    spec = pl.BlockSpec((tr, cols), lambda i: (i, 0))
    shp = jax.ShapeDtypeStruct((rows, cols), F32)
    return pl.pallas_call(
        body, grid=(rows // tr,),
        in_specs=[pl.BlockSpec((npart, tr, cols), lambda i: (0, i, 0)), spec, spec, spec],
        out_specs=[spec, spec, spec, spec], out_shape=[shp, shp, shp, shp],
        compiler_params=pltpu.CompilerParams(dimension_semantics=("parallel",), vmem_limit_bytes=VMEM_LIMIT),
        name=name)(parts, w, m, v)


SMALL_ROWS = 16


def _pack_small(c_ctx, b_mod, conv_b, mh, ln_w, ln_b, conv_w_rows, b_if, qn, kn):
    last = jnp.concatenate([b_if.reshape(-1), qn.reshape(-1), kn.reshape(-1),
                            jnp.zeros((D_MODEL - N_IF - 2 * HD_B,), F32)])
    rows = [c_ctx.reshape(1, D_MODEL), b_mod.reshape(3, D_MODEL), conv_b.reshape(1, D_MODEL),
            mh.reshape(1, D_MODEL), ln_w.reshape(1, D_MODEL), ln_b.reshape(1, D_MODEL),
            conv_w_rows.reshape(3, D_MODEL), last[None], jnp.zeros((SMALL_ROWS - 12, D_MODEL), F32)]
    return jnp.concatenate(rows, 0)


def _unpack_small(pk, conv_cols):
    return dict(c_ctx=pk[0], b_mod=pk[1:4].reshape(1, 3 * D_MODEL), conv_b=pk[4:5], mh_norm_w=pk[5:6],
                ln_w=pk[6:7], ln_b=pk[7:8], conv_w=pk[8:11, :conv_cols][None], b_if=pk[11:12, :N_IF],
                q_norm_w=pk[11:12, N_IF:N_IF + HD_B], k_norm_w=pk[11:12, N_IF + HD_B:N_IF + 2 * HD_B])


def kernel(x, c, ctx, c_ctx, w_mod, b_mod, w_in, b_if, conv_w, conv_b, mh_norm_w, q_norm_w, k_norm_w, w_branch_a, w_branch_b, w_out, ln_w, ln_b, loss_target, m_c_ctx, m_w_mod, m_b_mod, m_w_in, m_b_if, m_conv_w, m_conv_b, m_mh_norm_w, m_q_norm_w, m_k_norm_w, m_w_branch_a, m_w_branch_b, m_w_out, m_ln_w, m_ln_b, v_c_ctx, v_w_mod, v_b_mod, v_w_in, v_b_if, v_conv_w, v_conv_b, v_mh_norm_w, v_q_norm_w, v_k_norm_w, v_w_branch_a, v_w_branch_b, v_w_out, v_ln_w, v_ln_b):
    core = lax.axis_index("c")
    chip = 2 * lax.axis_index("x") + lax.axis_index("y")
    me = 2 * chip + core
    mod_cols, in_cols, conv_cols = w_mod.shape[2], w_in.shape[2], conv_w.shape[2]
    br_rows = w_out.shape[1]

    def rows_at(block, first):
        return lax.dynamic_update_slice(jnp.zeros((SMALL_ROWS, block.shape[1]), F32), block, (first, 0))

    owner = (core == 0).astype(F32)
    cond = _allreduce_small(rows_at(jnp.stack([_silu(c[0]), _silu(c_ctx)]), 2 * me), "gather_cond").astype(BF16)
    w_mod_b = w_mod[0].astype(BF16)
    mod_part = _mm_nn(cond, w_mod_b, "mod_fwd") * owner
    mod_all = _allreduce_small(
        lax.dynamic_update_slice(jnp.zeros((SMALL_ROWS, 3 * D_MODEL), F32), mod_part, (0, chip * mod_cols)),
        "gather_mod") + b_mod[0]
    mod = lax.dynamic_slice(mod_all, (2 * me, 0), (2, 3 * D_MODEL))

    g_in = _gather_chips_halves(w_in[0].astype(BF16), "gather_w_in")
    g_br = _gather_chips_halves(jnp.concatenate([w_branch_a[0], w_branch_b[0], w_out[0]], 0).astype(BF16),
                                "gather_w_branches").reshape(N_CHIPS, 3, br_rows, D_MODEL)
    g_ba, g_bb, g_out = g_br[:, 0], g_br[:, 1], g_br[:, 2]
    g_conv = _gather_chips(conv_w[0], "gather_conv_w")
    w_in_full = jnp.moveaxis(g_in, 0, 1).reshape(D_MODEL, N_CHIPS * in_cols)
    w_main = jnp.concatenate([w_in_full[:, :IF_START], w_in_full[:, IF_START + N_IF:]], 1)
    w_if = jnp.pad(w_in_full[:, IF_START:IF_START + N_IF], ((0, 0), (0, IF_PAD - N_IF)))
    conv_w_full = jnp.moveaxis(g_conv, 0, 1).reshape(3, N_CHIPS * conv_cols)

    const = dict(ctx=ctx[0], target=loss_target[0], w_main=w_main, w_if=w_if,
                 w_ba=g_ba.reshape(D_MODEL, D_MODEL), w_bb=g_bb.reshape(D_MODEL, D_MODEL),
                 w_out=g_out.reshape(D_MODEL, D_MODEL))
    diff = dict(x=x[0], mod=mod, b_if=b_if[0], conv_w=conv_w_full, conv_b=conv_b[0],
                mh_norm_w=mh_norm_w[0], q_norm_w=q_norm_w[0], k_norm_w=k_norm_w[0], ln_w=ln_w[0], ln_b=ln_b[0],
                p_main=jnp.zeros(w_main.shape, F32),
                p_if=jnp.zeros(w_if.shape, F32), p_ba=jnp.zeros((D_MODEL, D_MODEL), F32),
                p_bb=jnp.zeros((D_MODEL, D_MODEL), F32), p_out=jnp.zeros((D_MODEL, D_MODEL), F32))
    loss_local, g = jax.value_and_grad(_local_loss)(diff, const)
    loss = lax.psum(loss_local, ("x", "y", "c"))

    dmod_all = _allreduce_small(rows_at(g["mod"], 2 * me), "gather_dmod")
    dmod_k = lax.dynamic_slice(dmod_all, (0, chip * mod_cols), (SMALL_ROWS, mod_cols))
    g_w_mod = _mm_tn(cond, dmod_k, "mod_dw")
    g_b_mod = jnp.sum(dmod_all, axis=0) * (me == 0).astype(F32)
    d_cond = _mm_nt(dmod_k, w_mod_b, "mod_da")
    sig_ctx = jax.nn.sigmoid(c_ctx)
    g_c_ctx = owner * (sig_ctx * (1.0 + c_ctx * (1.0 - sig_ctx))) * jnp.sum(d_cond[1::2], axis=0)

    g_small = _allreduce_small(
        _pack_small(g_c_ctx, g_b_mod, g["conv_b"], g["mh_norm_w"], g["ln_w"], g["ln_b"], g["conv_w"],
                    g["b_if"], g["q_norm_w"], g["k_norm_w"]), "allreduce_small")
    conv_g = lax.dynamic_slice(g_small[8:11], (0, chip * conv_cols), (3, conv_cols))
    g_small = g_small.at[8:11].set(jnp.pad(conv_g, ((0, 0), (0, D_MODEL - conv_cols))))
    pad_conv = lambda a: jnp.pad(a[0], ((0, 0), (0, D_MODEL - conv_cols)))
    packed = [_pack_small(cc, bm[0], cb[0], mh[0], lw[0], lb[0], pad_conv(cw), bi[0], qn[0], kn[0])
              for cc, bm, cb, mh, lw, lb, cw, bi, qn, kn in (
                  (c_ctx, b_mod, conv_b, mh_norm_w, ln_w, ln_b, conv_w, b_if, q_norm_w, k_norm_w),
                  (m_c_ctx, m_b_mod, m_conv_b, m_mh_norm_w, m_ln_w, m_ln_b, m_conv_w, m_b_if, m_q_norm_w, m_k_norm_w),
                  (v_c_ctx, v_b_mod, v_conv_b, v_mh_norm_w, v_ln_w, v_ln_b, v_conv_w, v_b_if, v_q_norm_w, v_k_norm_w))]
    small = [_unpack_small(a, conv_cols)
             for a in _adamw_sum(g_small[None], packed[0], packed[1], packed[2], "adamw_small")]

    def col_slots(gfull, cols):
        return jnp.moveaxis(gfull.reshape(D_MODEL, N_CHIPS, cols), 1, 0).astype(BF16)

    g_in_full = jnp.concatenate([g["p_main"][:, :IF_START], g["p_if"][:, :N_IF], g["p_main"][:, IF_START:]], 1)
    big = {"w_mod": [a[None] for a in _adamw_sum(g_w_mod[None], w_mod[0], m_w_mod[0], v_w_mod[0], "adamw_w_mod")]}
    parts_in = _scatter_grads(col_slots(g_in_full, in_cols), "scatter_w_in")
    big["w_in"] = [a[None] for a in _adamw_sum(parts_in, w_in[0], m_w_in[0], v_w_in[0], "adamw_w_in")]
    br_slots = jnp.concatenate([g[p].reshape(N_CHIPS, br_rows, D_MODEL) for p in ("p_ba", "p_bb", "p_out")], 1)
    parts_br = _scatter_grads(br_slots.astype(BF16), "scatter_w_branches")
    for j, (nm, w_, m_, v_) in enumerate((("w_branch_a", w_branch_a, m_w_branch_a, v_w_branch_a),
                                          ("w_branch_b", w_branch_b, m_w_branch_b, v_w_branch_b),
                                          ("w_out", w_out, m_w_out, v_w_out))):
        parts = parts_br[:, j * br_rows:(j + 1) * br_rows]
        big[nm] = [a[None] for a in _adamw_sum(parts, w_[0], m_[0], v_[0], "adamw_" + nm)]

    names = ["c_ctx", "w_mod", "b_mod", "w_in", "b_if", "conv_w", "conv_b", "mh_norm_w", "q_norm_w", "k_norm_w",
             "w_branch_a", "w_branch_b", "w_out", "ln_w", "ln_b"]
    outs = [[big[nm][k] if nm in big else small[k][nm] for nm in names] for k in range(4)]
    return (loss, g["x"][None], *outs[0], *outs[1], *outs[2], *outs[3])
```

```python
import jax
import jax.numpy as jnp
from jax import lax
from jax.experimental import pallas as pl
from jax.experimental.pallas import tpu as pltpu

F32 = jnp.float32
BF16 = jnp.bfloat16
MESH = pl.DeviceIdType.MESH

D_MODEL = 2048
NH_A, DK_A, DV_A = 8, 128, 256
QK_A, V_A = NH_A * DK_A, NH_A * DV_A
NH_B, NKV_B, HD_B = 16, 4, 128
Q_B, KV_B = NH_B * HD_B, NKV_B * HD_B
GRID_W = 64
ROT_HALF = HD_B // 2
ROPE_THETA = 10000.0
M_INIT = -1e30
EPS = 1e-6
ALPHA = 2.0 ** 0.25
N_IN = 17440
IF_START, N_IF, IF_PAD = 4096, 32, 128
N_MAIN = N_IN - N_IF
O_QK, O_VA, O_KB, O_VB, O_OA, O_ZA, O_QB, O_ZB, O_GA, O_GB = (
    0, 2048, 4096, 4608, 5120, 7168, 9216, 11264, 13312, 15360)
MLSTM_CHUNK = 256

ADAM_LR, ADAM_B1, ADAM_B2, ADAM_EPS, ADAM_WD, ADAM_STEP = 0.001, 0.9, 0.999, 1e-08, 0.01, 10

VMEM_LIMIT = 48 * 1024 * 1024
N_CHIPS, N_DEV = 4, 8
MX = BF16


def _pick(n, cands):
    for c in cands:
        if n % c == 0:
            return c
    raise ValueError(f"no tile for {n} in {cands}")


def _dot(a, b):
    return jnp.dot(a, b, preferred_element_type=F32)


def _dot_nt(a, b):
    return lax.dot_general(a, b, (((1,), (1,)), ((), ())), preferred_element_type=F32)


def _dot_tn(a, b):
    return lax.dot_general(a, b, (((0,), (0,)), ((), ())), preferred_element_type=F32)


def _mm_nn(a, b, name):
    m, k = a.shape
    _, n = b.shape
    tm = _pick(m, (1024, 512, 256, 128, 64, 32, 16))
    tn = _pick(n, (1024, 512, 256, 128))

    def body(a_ref, b_ref, o_ref):
        o_ref[...] = _dot(a_ref[...], b_ref[...])

    return pl.pallas_call(
        body, grid=(m // tm, n // tn),
        in_specs=[pl.BlockSpec((tm, k), lambda i, j: (i, 0)), pl.BlockSpec((k, tn), lambda i, j: (0, j))],
        out_specs=pl.BlockSpec((tm, tn), lambda i, j: (i, j)),
        out_shape=jax.ShapeDtypeStruct((m, n), F32),
        compiler_params=pltpu.CompilerParams(dimension_semantics=("parallel", "parallel"),
                                             vmem_limit_bytes=VMEM_LIMIT),
        name=name)(a, b)


def _mm_nt(g, w, name):
    m, n = g.shape
    k, _ = w.shape
    tm = _pick(m, (1024, 512, 256, 128, 64, 32, 16))
    tn = _pick(n, (1024, 512, 256, 128))

    def body(g_ref, w_ref, o_ref):
        part = _dot_nt(g_ref[...].astype(BF16), w_ref[...])

        @pl.when(pl.program_id(1) == 0)
        def _():
            o_ref[...] = part

        @pl.when(pl.program_id(1) > 0)
        def _():
            o_ref[...] += part

    return pl.pallas_call(
        body, grid=(m // tm, n // tn),
        in_specs=[pl.BlockSpec((tm, tn), lambda i, j: (i, j)), pl.BlockSpec((k, tn), lambda i, j: (0, j))],
        out_specs=pl.BlockSpec((tm, k), lambda i, j: (i, 0)),
        out_shape=jax.ShapeDtypeStruct((m, k), F32),
        compiler_params=pltpu.CompilerParams(dimension_semantics=("parallel", "arbitrary"),
                                             vmem_limit_bytes=VMEM_LIMIT),
        name=name)(g, w)


def _mm_tn(a, g, name):
    m, k = a.shape
    _, n = g.shape
    tm = _pick(m, (1024, 512, 256, 128, 64, 32, 16))
    tn = _pick(n, (1024, 512, 256, 128))

    def body(a_ref, g_ref, o_ref):
        part = _dot_tn(a_ref[...], g_ref[...].astype(BF16))

        @pl.when(pl.program_id(1) == 0)
        def _():
            o_ref[...] = part

        @pl.when(pl.program_id(1) > 0)
        def _():
            o_ref[...] += part

    return pl.pallas_call(
        body, grid=(n // tn, m // tm),
        in_specs=[pl.BlockSpec((tm, k), lambda j, i: (i, 0)), pl.BlockSpec((tm, tn), lambda j, i: (i, j))],
        out_specs=pl.BlockSpec((k, tn), lambda j, i: (0, j)),
        out_shape=jax.ShapeDtypeStruct((k, n), F32),
        compiler_params=pltpu.CompilerParams(dimension_semantics=("parallel", "arbitrary"),
                                             vmem_limit_bytes=VMEM_LIMIT),
        name=name)(a, g)


SLABS = (("qk", O_QK, 2 * QK_A), ("va", O_VA, V_A), ("kv", O_KB, 2 * KV_B), ("oa", O_OA, V_A), ("za", O_ZA, V_A),
         ("qb", O_QB, Q_B), ("zb", O_ZB, Q_B), ("ga", O_GA, D_MODEL), ("gb", O_GB, D_MODEL))
SLAB_FWD_TN = 1024
SLAB_TN = 512
SLAB_DA_TN = 1024
PROJ_DA_VMEM = 58 * 1024 * 1024


def _slab_blocks(tn):
    return [(off // tn, (off + width) // tn) for _, off, width in SLABS]


def _proj_fwd_slab(a, w, off, width, name):
    m, k = a.shape
    tm = _pick(m, (1088, 512, 256, 128))
    tn = min(SLAB_FWD_TN, width)

    def body(a_ref, b_ref, o_ref):
        o_ref[...] = _dot(a_ref[...], b_ref[...])

    return pl.pallas_call(
        body, grid=(m // tm, width // tn),
        in_specs=[pl.BlockSpec((tm, k), lambda i, j: (i, 0)), pl.BlockSpec((k, tn), lambda i, j: (0, j + off // tn))],
        out_specs=pl.BlockSpec((tm, tn), lambda i, j: (i, j)),
        out_shape=jax.ShapeDtypeStruct((m, width), F32),
        compiler_params=pltpu.CompilerParams(dimension_semantics=("parallel", "parallel"),
                                             vmem_limit_bytes=VMEM_LIMIT),
        name=name)(a, w)


def _slab_spec(tm, tn, blocks, rows_inner):
    b, e = blocks

    def index(r, c):
        inside = (c >= b) & (c < e)
        return jnp.where(inside, r, 0), jnp.clip(c - b, 0, e - b - 1)

    if rows_inner:
        return pl.BlockSpec((tm, tn), lambda c, r: index(r, c))
    return pl.BlockSpec((tm, tn), lambda r, c: index(r, c))


def _proj_da(gs, w, name):
    m = gs[0].shape[0]
    k, n = w.shape
    tm, tn = _pick(m, (512, 256, 128)), SLAB_DA_TN
    blocks = _slab_blocks(tn)

    def body(*refs):
        g_refs, w_ref, o_ref = refs[:len(blocks)], refs[len(blocks)], refs[len(blocks) + 1]
        c = pl.program_id(1)

        @pl.when(c == 0)
        def _():
            o_ref[...] = jnp.zeros(o_ref.shape, F32)

        for g_ref, (b, e) in zip(g_refs, blocks):
            @pl.when((c >= b) & (c < e))
            def _(g_ref=g_ref):
                o_ref[...] += _dot_nt(g_ref[...].astype(BF16), w_ref[...])

    return pl.pallas_call(
        body, grid=(m // tm, n // tn),
        in_specs=[_slab_spec(tm, tn, blk, False) for blk in blocks] + [pl.BlockSpec((k, tn), lambda r, c: (0, c))],
        out_specs=pl.BlockSpec((tm, k), lambda r, c: (r, 0)),
        out_shape=jax.ShapeDtypeStruct((m, k), F32),
        compiler_params=pltpu.CompilerParams(dimension_semantics=("parallel", "arbitrary"),
                                             vmem_limit_bytes=PROJ_DA_VMEM),
        name=name)(*gs, w)


def _proj_dw(a, gs, n, name):
    m, k = a.shape
    tm = _pick(m, (512, 256, 128))
    blocks = _slab_blocks(SLAB_TN)

    def body(*refs):
        a_ref, g_refs, o_ref = refs[0], refs[1:1 + len(blocks)], refs[1 + len(blocks)]
        c, r = pl.program_id(0), pl.program_id(1)

        @pl.when(r == 0)
        def _():
            o_ref[...] = jnp.zeros(o_ref.shape, F32)

        for g_ref, (b, e) in zip(g_refs, blocks):
            @pl.when((c >= b) & (c < e))
            def _(g_ref=g_ref):
                o_ref[...] += _dot_tn(a_ref[...], g_ref[...].astype(BF16))

    return pl.pallas_call(
        body, grid=(n // SLAB_TN, m // tm),
        in_specs=[pl.BlockSpec((tm, k), lambda c, r: (r, 0))] + [_slab_spec(tm, SLAB_TN, blk, True) for blk in blocks],
        out_specs=pl.BlockSpec((k, SLAB_TN), lambda c, r: (0, c)),
        out_shape=jax.ShapeDtypeStruct((k, n), F32),
        compiler_params=pltpu.CompilerParams(dimension_semantics=("parallel", "arbitrary"),
                                             vmem_limit_bytes=VMEM_LIMIT),
        name=name)(a, *gs)


LN_ROWS = 256


def _ln_stats(x):
    mu = jnp.mean(x, axis=-1, keepdims=True)
    xc = x - mu
    rstd = lax.rsqrt(jnp.mean(xc * xc, axis=-1, keepdims=True) + EPS)
    return xc * rstd, rstd


def _ln_bwd(dxh, xh, rstd):
    return rstd * (dxh - jnp.mean(dxh, axis=-1, keepdims=True) - xh * jnp.mean(dxh * xh, axis=-1, keepdims=True))


def _seg_maps(t, tc):
    cb, nb = tc // LN_ROWS, t // LN_ROWS
    ctx_blk = lambda i: jnp.where(i < cb, i, jnp.clip(i - cb - nb, 0, cb - 1))
    lat_blk = lambda i: jnp.clip(i - cb, 0, nb - 1)
    return cb, nb, ctx_blk, lat_blk


def _ln_mod_fwd(x, ctx, mod):
    t, tc = x.shape[0], ctx.shape[0]
    cb, nb, ctx_blk, lat_blk = _seg_maps(t, tc)

    def body(x_ref, c_ref, m_ref, o_ref):
        i = pl.program_id(0)
        lat = (i >= cb) & (i < cb + nb)

        @pl.when(lat)
        def _():
            xh, _ = _ln_stats(x_ref[...])
            o_ref[...] = (xh * (1 + m_ref[0:1, :]) + m_ref[1:2, :]).astype(BF16)

        @pl.when(jnp.logical_not(lat))
        def _():
            xh, _ = _ln_stats(c_ref[...])
            o_ref[...] = (xh * (1 + m_ref[2:3, :]) + m_ref[3:4, :]).astype(BF16)

    blk = lambda f: pl.BlockSpec((LN_ROWS, D_MODEL), lambda i: (f(i), 0))
    return pl.pallas_call(
        body, grid=(2 * cb + nb,),
        in_specs=[blk(lat_blk), blk(ctx_blk), pl.BlockSpec((4, D_MODEL), lambda i: (0, 0))],
        out_specs=pl.BlockSpec((LN_ROWS, D_MODEL), lambda i: (i, 0)),
        out_shape=jax.ShapeDtypeStruct((t + 2 * tc, D_MODEL), BF16),
        compiler_params=pltpu.CompilerParams(dimension_semantics=("parallel",), vmem_limit_bytes=VMEM_LIMIT),
        name="ln_mod")(x, ctx, mod)


def _ln_mod_bwd(du_a, du_b, x, ctx, mod):
    t, tc = x.shape[0], ctx.shape[0]
    cb, nb, ctx_blk, lat_blk = _seg_maps(t, tc)

    def body(da_ref, db_ref, x_ref, c_ref, m_ref, dx_ref, dm_ref):
        i = pl.program_id(0)
        lat = (i >= cb) & (i < cb + nb)

        @pl.when(i == 0)
        def _():
            dm_ref[...] = jnp.zeros(dm_ref.shape, F32)

        du = da_ref[...] + db_ref[...]

        @pl.when(lat)
        def _():
            xh, rstd = _ln_stats(x_ref[...])
            dm_ref[0:1, :] += jnp.sum(du * xh, axis=0, keepdims=True)
            dm_ref[1:2, :] += jnp.sum(du, axis=0, keepdims=True)
            dx_ref[...] = _ln_bwd(du * (1 + m_ref[0:1, :]), xh, rstd)

        @pl.when(jnp.logical_not(lat))
        def _():
            xh, _ = _ln_stats(c_ref[...])
            dm_ref[2:3, :] += jnp.sum(du * xh, axis=0, keepdims=True)
            dm_ref[3:4, :] += jnp.sum(du, axis=0, keepdims=True)

    blk = lambda f: pl.BlockSpec((LN_ROWS, D_MODEL), lambda i: (f(i), 0))
    row = pl.BlockSpec((LN_ROWS, D_MODEL), lambda i: (i, 0))
    vec = pl.BlockSpec((4, D_MODEL), lambda i: (0, 0))
    return pl.pallas_call(
        body, grid=(2 * cb + nb,),
        in_specs=[row, row, blk(lat_blk), blk(ctx_blk), vec],
        out_specs=[blk(lat_blk), vec],
        out_shape=[jax.ShapeDtypeStruct((t, D_MODEL), F32), jax.ShapeDtypeStruct((4, D_MODEL), F32)],
        compiler_params=pltpu.CompilerParams(dimension_semantics=("arbitrary",), vmem_limit_bytes=VMEM_LIMIT),
        name="ln_mod_bwd")(du_a, du_b, x, ctx, mod)


@jax.custom_vjp
def _ln_project(x, ctx, mod, w_main, w_if, pr_main, pr_if):
    return _ln_project_fwd(x, ctx, mod, w_main, w_if, pr_main, pr_if)[0]


def _ln_project_fwd(x, ctx, mod, w_main, w_if, pr_main, pr_if):
    del pr_main, pr_if
    ub = _ln_mod_fwd(x, ctx, mod)
    slabs = tuple(_proj_fwd_slab(ub, w_main, off, width, "proj_" + nm) for nm, off, width in SLABS)
    return (slabs, _mm_nn(ub, w_if, "proj_if")), (x, ctx, mod, ub, w_main, w_if)


def _ln_project_bwd(res, cot):
    x, ctx, mod, ub, w_main, w_if = res
    gs, g_if = cot
    dx, dmod = _ln_mod_bwd(_proj_da(gs, w_main, "proj_da"), _mm_nt(g_if, w_if, "proj_if_da"), x, ctx, mod)
    return (dx, jnp.zeros_like(ctx), dmod, jnp.zeros_like(w_main), jnp.zeros_like(w_if),
            _proj_dw(ub, gs, w_main.shape[1], "proj_dw"), _mm_tn(ub, g_if, "proj_if_dw"))


_ln_project.defvjp(_ln_project_fwd, _ln_project_bwd)


def _head_fwd(x, out, target, vecs):
    t = x.shape[0]

    def body(x_ref, o_ref, t_ref, v_ref, l_ref):
        @pl.when(pl.program_id(0) == 0)
        def _():
            l_ref[...] = jnp.zeros(l_ref.shape, F32)

        rh, _ = _ln_stats(ALPHA * x_ref[...] + v_ref[0:1, :] * o_ref[...])
        err = rh * v_ref[1:2, :] + v_ref[2:3, :] - t_ref[...]
        part = jnp.sum(jnp.mean(err * err, axis=-1, keepdims=True), axis=0, keepdims=True)
        l_ref[...] += 0.5 * part

    row = pl.BlockSpec((LN_ROWS, D_MODEL), lambda i: (i, 0))
    return pl.pallas_call(
        body, grid=(t // LN_ROWS,),
        in_specs=[row, row, row, pl.BlockSpec((3, D_MODEL), lambda i: (0, 0))],
        out_specs=pl.BlockSpec((1, 128), lambda i: (0, 0)),
        out_shape=jax.ShapeDtypeStruct((1, 128), F32),
        compiler_params=pltpu.CompilerParams(dimension_semantics=("arbitrary",), vmem_limit_bytes=VMEM_LIMIT),
        name="loss_head")(x, out, target, vecs)


def _head_bwd(g, x, out, target, vecs):
    t = x.shape[0]

    def body(g_ref, x_ref, o_ref, t_ref, v_ref, dx_ref, do_ref, dv_ref):
        @pl.when(pl.program_id(0) == 0)
        def _():
            dv_ref[...] = jnp.zeros(dv_ref.shape, F32)

        o = o_ref[...]
        gate, ln_w = v_ref[0:1, :], v_ref[1:2, :]
        rh, rstd = _ln_stats(ALPHA * x_ref[...] + gate * o)
        dy = (rh * ln_w + v_ref[2:3, :] - t_ref[...]) * (g_ref[0:1, 0:1] * (1.0 / D_MODEL))
        dv_ref[1:2, :] += jnp.sum(dy * rh, axis=0, keepdims=True)
        dv_ref[2:3, :] += jnp.sum(dy, axis=0, keepdims=True)
        dr = _ln_bwd(dy * ln_w, rh, rstd)
        dv_ref[0:1, :] += jnp.sum(dr * o, axis=0, keepdims=True)
        dx_ref[...] = ALPHA * dr
        do_ref[...] = gate * dr

    row = pl.BlockSpec((LN_ROWS, D_MODEL), lambda i: (i, 0))
    vec = pl.BlockSpec((3, D_MODEL), lambda i: (0, 0))
    return pl.pallas_call(
        body, grid=(t // LN_ROWS,),
        in_specs=[pl.BlockSpec((1, 128), lambda i: (0, 0)), row, row, row, vec],
        out_specs=[row, row, vec],
        out_shape=[jax.ShapeDtypeStruct((t, D_MODEL), F32), jax.ShapeDtypeStruct((t, D_MODEL), F32),
                   jax.ShapeDtypeStruct((3, D_MODEL), F32)],
        compiler_params=pltpu.CompilerParams(dimension_semantics=("arbitrary",), vmem_limit_bytes=VMEM_LIMIT),
        name="loss_head_bwd")(g, x, out, target, vecs)


@jax.custom_vjp
def _loss_head(x, out, target, gate, ln_w, ln_b):
    return _head_fwd(x, out, target, jnp.stack([gate, ln_w, ln_b]))[0, 0]


def _loss_head_fwd(x, out, target, gate, ln_w, ln_b):
    vecs = jnp.stack([gate, ln_w, ln_b])
    return _head_fwd(x, out, target, vecs)[0, 0], (x, out, target, vecs)


def _loss_head_bwd(res, g):
    x, out, target, vecs = res
    dx, dout, dv = _head_bwd(jnp.full((1, 128), g, F32), x, out, target, vecs)
    return dx, dout, jnp.zeros_like(target), dv[0], dv[1], dv[2]


_loss_head.defvjp(_loss_head_fwd, _loss_head_bwd)


ATT_SCALE = HD_B ** -0.5
GROUP = NH_B // NKV_B


LOG2E, LN2 = 1.4426950408889634, 0.6931471805599453
ATT_C = ATT_SCALE * LOG2E
STRIP_Q, STRIP_K = 128, 256


def _attn_tiles(t, n):
    return _pick(t, (512, 256, 128)), _pick(n, (768, 512, 256))


def _attn_fwd(q, k, v):
    t, n = q.shape[0], k.shape[0]
    tq, tk = _pick(t, (1024, 512, 256, 128)), _attn_tiles(t, n)[1]
    nk = n // tk

    def body(q_ref, k_ref, v_ref, o_ref, lse_ref, m_sc, acc_sc):
        j = pl.program_id(2)

        @pl.when(j == 0)
        def _():
            m_sc[...] = jnp.full(m_sc.shape, -jnp.inf, F32)
            acc_sc[...] = jnp.zeros(acc_sc.shape, F32)

        kb = k_ref[...]
        v_ones = jnp.concatenate([v_ref[...], jnp.ones((tk, HD_B), BF16)], axis=1)
        for g in range(GROUP):
            s2 = _dot_nt(q_ref[:, g * HD_B:(g + 1) * HD_B], kb) * ATT_C
            m_prev = m_sc[g]
            m_new = jnp.maximum(m_prev, jnp.max(s2, axis=-1, keepdims=True))
            p = jnp.exp2(s2 - m_new).astype(BF16)
            acc_sc[g] = jnp.exp2(m_prev - m_new) * acc_sc[g] + _dot(p, v_ones)
            m_sc[g] = m_new

        @pl.when(j == nk - 1)
        def _():
            for g in range(GROUP):
                cols = slice(g * HD_B, (g + 1) * HD_B)
                l = acc_sc[g, :, HD_B:]
                o_ref[:, cols] = acc_sc[g, :, :HD_B] / l
                lse_ref[:, cols] = m_sc[g] + jnp.log(l) * LOG2E

    qspec = pl.BlockSpec((tq, GROUP * HD_B), lambda kh, i, j: (i, kh))
    kspec = pl.BlockSpec((tk, HD_B), lambda kh, i, j: (j, kh))
    return pl.pallas_call(
        body, grid=(NKV_B, t // tq, nk),
        in_specs=[qspec, kspec, kspec], out_specs=[qspec, qspec],
        out_shape=[jax.ShapeDtypeStruct((t, Q_B), F32), jax.ShapeDtypeStruct((t, Q_B), F32)],
        scratch_shapes=[pltpu.VMEM((GROUP, tq, 1), F32), pltpu.VMEM((GROUP, tq, 2 * HD_B), F32)],
        compiler_params=pltpu.CompilerParams(dimension_semantics=("parallel", "parallel", "arbitrary"),
                                             vmem_limit_bytes=VMEM_LIMIT),
        name="attn_fwd")(q, k, v)


def _attn_dq(q, k, v, do, lse, delta):
    t, n = q.shape[0], k.shape[0]
    tq, tk = _pick(t, (1024, 512, 256, 128)), _attn_tiles(t, n)[1]
    nk = n // tk

    def body(q_ref, k_ref, v_ref, do_ref, lse_ref, dl_ref, dq_ref):
        j = pl.program_id(2)
        kb, vb = k_ref[...], v_ref[...]
        parts = []
        for g in range(GROUP):
            cols = slice(g * HD_B, (g + 1) * HD_B)
            p = jnp.exp2(_dot_nt(q_ref[:, cols], kb) * ATT_C - lse_ref[:, g * HD_B:g * HD_B + 1])
            dp = _dot_nt(do_ref[:, cols], vb)
            ds = p * (dp - dl_ref[:, g * HD_B:g * HD_B + 1])
            parts.append(_dot(ds.astype(BF16), kb))

        @pl.when(j == 0)
        def _():
            for g in range(GROUP):
                dq_ref[:, g * HD_B:(g + 1) * HD_B] = parts[g]

        @pl.when(j > 0)
        def _():
            for g in range(GROUP):
                dq_ref[:, g * HD_B:(g + 1) * HD_B] += parts[g]

        @pl.when(j == nk - 1)
        def _():
            dq_ref[...] = dq_ref[...] * ATT_SCALE

    qspec = pl.BlockSpec((tq, GROUP * HD_B), lambda kh, i, j: (i, kh))
    kspec = pl.BlockSpec((tk, HD_B), lambda kh, i, j: (j, kh))
    return pl.pallas_call(
        body, grid=(NKV_B, t // tq, n // tk),
        in_specs=[qspec, kspec, kspec, qspec, qspec, qspec],
        out_specs=qspec,
        out_shape=jax.ShapeDtypeStruct((t, Q_B), F32),
        compiler_params=pltpu.CompilerParams(dimension_semantics=("parallel", "parallel", "arbitrary"),
                                             vmem_limit_bytes=VMEM_LIMIT),
        name="attn_dq")(q, k, v, do, lse, delta)


def _attn_dkv(q, k, v, do, lse_t, delta_t):
    t, n = q.shape[0], k.shape[0]
    tq, tk = _attn_tiles(t, n)
    nq = t // tq
    n_r, n_c = tq // STRIP_Q, tk // STRIP_K

    def body(q_ref, k_ref, v_ref, do_ref, lse_ref, dl_ref, dk_ref, dv_ref, dk_sc, dv_sc):
        i = pl.program_id(2)

        @pl.when(i == 0)
        def _():
            dk_sc[...] = jnp.zeros(dk_sc.shape, F32)
            dv_sc[...] = jnp.zeros(dv_sc.shape, F32)

        for r in range(n_r):
            rows = slice(r * STRIP_Q, (r + 1) * STRIP_Q)
            for c in range(n_c):
                kv = slice(c * STRIP_K, (c + 1) * STRIP_K)
                kc, vc = k_ref[kv, :], v_ref[kv, :]
                dk_part = dv_part = None
                for g in range(GROUP):
                    cols = slice(g * HD_B, (g + 1) * HD_B)
                    qg, dog = q_ref[rows, cols], do_ref[rows, cols]
                    st = _dot_nt(kc, qg)
                    pt = jnp.exp2(st * ATT_C - lse_ref[8 * g:8 * g + 1, rows])
                    dvg = _dot(pt.astype(BF16), dog)
                    dpt = _dot_nt(vc, dog)
                    dst = pt * (dpt - dl_ref[8 * g:8 * g + 1, rows])
                    dkg = _dot(dst.astype(BF16), qg)
                    dk_part = dkg if dk_part is None else dk_part + dkg
                    dv_part = dvg if dv_part is None else dv_part + dvg
                dk_sc[kv, :] += dk_part
                dv_sc[kv, :] += dv_part

        @pl.when(i == nq - 1)
        def _():
            dk_ref[...] = dk_sc[...] * ATT_SCALE
            dv_ref[...] = dv_sc[...]

    qspec = pl.BlockSpec((tq, GROUP * HD_B), lambda kh, j, i: (i, kh))
    tspec = pl.BlockSpec((8 * GROUP, tq), lambda kh, j, i: (kh, i))
    kspec = pl.BlockSpec((tk, HD_B), lambda kh, j, i: (j, kh))
    return pl.pallas_call(
        body, grid=(NKV_B, n // tk, nq),
        in_specs=[qspec, kspec, kspec, qspec, tspec, tspec],
        out_specs=[kspec, kspec],
        out_shape=[jax.ShapeDtypeStruct((n, KV_B), F32), jax.ShapeDtypeStruct((n, KV_B), F32)],
        scratch_shapes=[pltpu.VMEM((tk, HD_B), F32), pltpu.VMEM((tk, HD_B), F32)],
        compiler_params=pltpu.CompilerParams(dimension_semantics=("parallel", "parallel", "arbitrary"),
                                             vmem_limit_bytes=VMEM_LIMIT),
        name="attn_dkv")(q, k, v, do, lse_t, delta_t)


def _attention_bwd(res, do):
    qb, kb, vb, o, lse = res
    t = qb.shape[0]
    delta = jnp.sum((do * o).reshape(t, NH_B, HD_B), axis=-1)
    lse_h = lse.reshape(t, NH_B, HD_B)[:, :, 0]
    delta_b = jnp.broadcast_to(delta[:, :, None], (t, NH_B, HD_B)).reshape(t, Q_B)
    lse_t = jnp.broadcast_to(lse_h.T[:, None, :], (NH_B, 8, t)).reshape(NH_B * 8, t)
    delta_t = jnp.broadcast_to(delta.T[:, None, :], (NH_B, 8, t)).reshape(NH_B * 8, t)
    dob = do.astype(BF16)
    dq = _attn_dq(qb, kb, vb, dob, lse, delta_b)
    dk, dv = _attn_dkv(qb, kb, vb, dob, lse_t, delta_t)
    return dq, dk, dv


def _swap32(y):
    lane = lax.broadcasted_iota(jnp.int32, y.shape, 1)
    return jnp.where((lane // 32) % 2 == 0, pltpu.roll(y, 96, 1), pltpu.roll(y, 32, 1))


ROPE_ROWS = 256


def _norm_rope_fwd(x, w, cos, sin, row_off, rows, heads, name):
    tr, off, width = ROPE_ROWS, row_off // ROPE_ROWS, heads * HD_B

    def body(x_ref, w_ref, c_ref, s_ref, o_ref):
        w, c, s = w_ref[...], c_ref[...], s_ref[...]
        for h in range(heads):
            cols = slice(h * HD_B, (h + 1) * HD_B)
            xh = x_ref[:, cols]
            y = xh * lax.rsqrt(jnp.mean(xh * xh, axis=-1, keepdims=True) + EPS) * w
            o_ref[:, cols] = (y * c + _swap32(y) * s).astype(o_ref.dtype)

    row = pl.BlockSpec((tr, width), lambda i: (i, 0))
    tab = pl.BlockSpec((tr, HD_B), lambda i: (i, 0))
    return pl.pallas_call(
        body, grid=(rows // tr,),
        in_specs=[pl.BlockSpec((tr, width), lambda i: (i + off, 0)), pl.BlockSpec((1, HD_B), lambda i: (0, 0)), tab, tab],
        out_specs=row, out_shape=jax.ShapeDtypeStruct((rows, width), BF16),
        compiler_params=pltpu.CompilerParams(dimension_semantics=("parallel",), vmem_limit_bytes=VMEM_LIMIT),
        name=name)(x, w, cos, sin)


def _norm_rope_bwd(x, w, cos, sin, dy, row_off, extra, name):
    rows, width = dy.shape
    heads = width // HD_B
    r, full = x.shape
    tr, off, nb = ROPE_ROWS, row_off // ROPE_ROWS, rows // ROPE_ROWS

    def body(*refs):
        x_ref, w_ref, c_ref, s_ref, dy_ref = refs[:5]
        e_ref = refs[5] if extra is not None else None
        dx_ref, dw_ref = refs[-2], refs[-1]
        i = pl.program_id(0)

        @pl.when(i == 0)
        def _():
            dw_ref[...] = jnp.zeros(dw_ref.shape, F32)

        @pl.when((i >= off) & (i < off + nb))
        def _():
            w, c, s = w_ref[...], c_ref[...], s_ref[...]
            dw = jnp.zeros((1, HD_B), F32)
            for h in range(heads):
                cols = slice(h * HD_B, (h + 1) * HD_B)
                xh, dyh = x_ref[:, cols], dy_ref[:, cols]
                rs = lax.rsqrt(jnp.mean(xh * xh, axis=-1, keepdims=True) + EPS)
                dn = dyh * c + _swap32(dyh * s)
                dw = dw + jnp.sum(dn * (xh * rs), axis=0, keepdims=True)
                dxn = dn * w
                dx_ref[:, cols] = rs * dxn - xh * (rs * rs * rs * jnp.mean(dxn * xh, axis=-1, keepdims=True))
            if e_ref is not None:
                dx_ref[:, width:] = e_ref[...]
            dw_ref[...] += dw

        @pl.when((i < off) | (i >= off + nb))
        def _():
            dx_ref[...] = jnp.zeros(dx_ref.shape, F32)

    inner = lambda i: jnp.clip(i - off, 0, nb - 1)
    tab = pl.BlockSpec((tr, HD_B), lambda i: (inner(i), 0))
    vec = pl.BlockSpec((1, HD_B), lambda i: (0, 0))
    in_specs = [pl.BlockSpec((tr, width), lambda i: (i, 0)), vec, tab, tab,
                pl.BlockSpec((tr, width), lambda i: (inner(i), 0))]
    args = [x, w, cos, sin, dy]
    if extra is not None:
        in_specs.append(pl.BlockSpec((tr, full - width), lambda i: (inner(i), 0)))
        args.append(extra)
    return pl.pallas_call(
        body, grid=(r // tr,), in_specs=in_specs,
        out_specs=[pl.BlockSpec((tr, full), lambda i: (i, 0)), vec],
        out_shape=[jax.ShapeDtypeStruct((r, full), F32), jax.ShapeDtypeStruct((1, HD_B), F32)],
        compiler_params=pltpu.CompilerParams(dimension_semantics=("arbitrary",), vmem_limit_bytes=VMEM_LIMIT),
        name=name)(*args)


def _rope_tables(t):
    pos = jnp.arange(t)
    row = (pos // GRID_W).astype(F32)
    col = (pos % GRID_W).astype(F32)
    inv = ROPE_THETA ** (-jnp.arange(0, ROT_HALF, 2, dtype=F32) / ROT_HALF)
    ar, ac = row[:, None] * inv[None], col[:, None] * inv[None]
    cos = jnp.concatenate([jnp.cos(ar), jnp.cos(ar), jnp.cos(ac), jnp.cos(ac)], -1)
    sin = jnp.concatenate([-jnp.sin(ar), jnp.sin(ar), -jnp.sin(ac), jnp.sin(ac)], -1)
    return cos, sin


def _gqa_tables(t, n):
    cos, sin = _rope_tables(t)
    cos_k = jnp.concatenate([jnp.ones((n - t, HD_B), F32), cos], 0)
    sin_k = jnp.concatenate([jnp.zeros((n - t, HD_B), F32), sin], 0)
    return cos, sin, cos_k, sin_k


def _make_gqa(t, tc):
    n = tc + t

    @jax.custom_vjp
    def gqa(p_qb, p_kv, qw, kw):
        return fwd(p_qb, p_kv, qw, kw)[0]

    def fwd(p_qb, p_kv, qw, kw):
        cos, sin, cos_k, sin_k = _gqa_tables(t, n)
        q = _norm_rope_fwd(p_qb, qw[None], cos, sin, tc, t, NH_B, "q_norm_rope")
        k = _norm_rope_fwd(p_kv, kw[None], cos_k, sin_k, 0, n, NKV_B, "k_norm_rope")
        vb = p_kv[:n, KV_B:].astype(BF16)
        o, lse = _attn_fwd(q, k, vb)
        return o, (p_qb, p_kv, qw, kw, q, k, vb, o, lse)

    def bwd(res, do):
        p_qb, p_kv, qw, kw, q, k, vb, o, lse = res
        cos, sin, cos_k, sin_k = _gqa_tables(t, n)
        dq, dk, dv = _attention_bwd((q, k, vb, o, lse), do)
        d_qb, dqw = _norm_rope_bwd(p_qb, qw[None], cos, sin, dq, tc, None, "q_norm_rope_bwd")
        d_kv, dkw = _norm_rope_bwd(p_kv, kw[None], cos_k, sin_k, dk, 0, dv, "k_norm_rope_bwd")
        return d_qb, d_kv, dqw[0], dkw[0]

    gqa.defvjp(fwd, bwd)
    return gqa


def _mlstm_chunk_forward(q, k, v, lir, f_pre, s0, n0, m0, reverse):
    L = q.shape[0]
    lfr = jnp.minimum(f_pre, 0.0) - jnp.log1p(jnp.exp(-jnp.abs(f_pre)))
    ti = lax.broadcasted_iota(jnp.int32, (L, L), 0)
    si = lax.broadcasted_iota(jnp.int32, (L, L), 1)
    seen = (si >= ti) if reverse else (si <= ti)
    seen_t = (ti >= si) if reverse else (ti <= si)
    eye = ti == si
    lic = jnp.sum(jnp.where(eye, lir, 0.0), axis=1, keepdims=True)
    lfc = jnp.sum(jnp.where(eye, lfr, 0.0), axis=1, keepdims=True)
    b_col = jnp.sum(jnp.where(seen, lfr, 0.0), axis=1, keepdims=True)
    b_row = jnp.sum(jnp.where(seen_t, lfc, 0.0), axis=0, keepdims=True)
    d = jnp.where(seen, b_col - b_row + lir, -jnp.inf)
    m = jnp.maximum(b_col + m0, jnp.max(d, axis=1, keepdims=True))
    w = jnp.exp(d - m)
    a = jnp.exp(b_col + m0 - m)
    qm, km, vm = q.astype(MX), k.astype(MX), v.astype(MX)
    s = _dot_nt(qm, km) * w
    qs = _dot(qm, s0.astype(MX))
    sv = _dot(s.astype(MX), jnp.concatenate([vm, jnp.ones((L, DK_A), MX)], axis=1))
    num = a * qs + sv[:, :DV_A]
    qn = jnp.sum(q * n0, axis=1, keepdims=True)
    den = a * qn + sv[:, DV_A:DV_A + 1]
    floor = jnp.exp(-m)
    dd = jnp.maximum(jnp.abs(den), floor)
    b_last = jnp.sum(lfr, axis=1, keepdims=True)
    m_end = jnp.maximum(b_last + m0, jnp.max(b_last - b_row + lir, axis=1, keepdims=True))
    w_end = jnp.exp(b_last - b_col + lic - m_end)
    a_end = jnp.exp(b_last + m0 - m_end)
    return dict(eye=eye, seen=seen, w=w, a=a, s=s, qs=qs, num=num, qn=qn, den=den, floor=floor, dd=dd,
                m_end=m_end, w_end=w_end, a_end=a_end, qm=qm, km=km, vm=vm)


def _mlstm_fwd_call(q, k, v, gr, n, row_off, reverse):
    L = MLSTM_CHUNK
    nc, off = n // L, row_off // L
    pos = (lambda i: nc - 1 - i) if reverse else (lambda i: i)

    def body(q_ref, k_ref, v_ref, gr_ref, h_ref, s0_ref, n0_ref, m0_ref, s_sc, n_sc, m_sc):
        @pl.when(pl.program_id(1) == 0)
        def _():
            s_sc[...] = jnp.zeros(s_sc.shape, F32)
            n_sc[...] = jnp.zeros(n_sc.shape, F32)
            m_sc[...] = jnp.full(m_sc.shape, M_INIT, F32)

        s0, n0, m0 = s_sc[...], n_sc[...], m_sc[...]
        s0_ref[0, 0] = s0
        n0_ref[0, 0] = n0
        m0_ref[0, 0] = jnp.broadcast_to(m0, (1, DK_A))
        k, v = k_ref[...], v_ref[...]
        f = _mlstm_chunk_forward(q_ref[...], k, v, gr_ref[0, 0], gr_ref[1, 0], s0, n0, m0, reverse)
        h_ref[...] = f["num"] / f["dd"]
        s_sc[...] = f["a_end"] * s0 + _dot_tn(f["km"], (f["w_end"] * v).astype(MX))
        n_sc[...] = f["a_end"] * n0 + jnp.sum(f["w_end"] * k, axis=0, keepdims=True)
        m_sc[...] = f["m_end"]

    qk_spec = pl.BlockSpec((L, DK_A), lambda h, i: (off + pos(i), h))
    v_spec = pl.BlockSpec((L, DV_A), lambda h, i: (off + pos(i), h))
    gr_spec = pl.BlockSpec((2, 1, 1, L), lambda h, i: (0, h, 0, off + pos(i)))
    h_spec = pl.BlockSpec((L, DV_A), lambda h, i: (pos(i), h))
    st_spec = pl.BlockSpec((1, 1, DK_A, DV_A), lambda h, i: (h, pos(i), 0, 0))
    vec_spec = pl.BlockSpec((1, 1, 1, DK_A), lambda h, i: (h, pos(i), 0, 0))
    return pl.pallas_call(
        body, grid=(NH_A, nc),
        in_specs=[qk_spec, qk_spec, v_spec, gr_spec],
        out_specs=[h_spec, st_spec, vec_spec, vec_spec],
        out_shape=[jax.ShapeDtypeStruct((n, V_A), F32), jax.ShapeDtypeStruct((NH_A, nc, DK_A, DV_A), F32),
                   jax.ShapeDtypeStruct((NH_A, nc, 1, DK_A), F32), jax.ShapeDtypeStruct((NH_A, nc, 1, DK_A), F32)],
        scratch_shapes=[pltpu.VMEM((DK_A, DV_A), F32), pltpu.VMEM((1, DK_A), F32), pltpu.VMEM((1, 1), F32)],
        compiler_params=pltpu.CompilerParams(dimension_semantics=("parallel", "arbitrary"),
                                             vmem_limit_bytes=VMEM_LIMIT),
        name="mlstm_fwd")(q, k, v, gr)


def _mlstm_bwd_call(q, k, v, gr, s0_all, n0_all, m0_all, dh, n, row_off, reverse):
    L = MLSTM_CHUNK
    nc, off = n // L, row_off // L
    pos = (lambda i: i) if reverse else (lambda i: nc - 1 - i)

    def body(q_ref, k_ref, v_ref, gr_ref, s0_ref, n0_ref, m0_ref, dh_ref,
             dq_ref, dk_ref, dv_ref, dg_ref, ds_sc, dn_sc):
        @pl.when(pl.program_id(1) == 0)
        def _():
            ds_sc[...] = jnp.zeros(ds_sc.shape, F32)
            dn_sc[...] = jnp.zeros(dn_sc.shape, F32)

        q, k, v = q_ref[...], k_ref[...], v_ref[...]
        s0, n0, m0 = s0_ref[0, 0], n0_ref[0, 0], m0_ref[0, 0][:, 0:1]
        f = _mlstm_chunk_forward(q, k, v, gr_ref[0, 0], gr_ref[1, 0], s0, n0, m0, reverse)
        w, a, s = f["w"], f["a"], f["s"]
        qm, km, vm, w_end, a_end = f["qm"], f["km"], f["vm"], f["w_end"], f["a_end"]
        ds1, dn1 = ds_sc[...], dn_sc[...]
        ds1m, s0m = ds1.astype(MX), s0.astype(MX)

        inv = 1.0 / f["dd"]
        dh = dh_ref[...]
        dnum = dh * inv
        ddd = -jnp.sum(dh * (f["num"] * inv), axis=1, keepdims=True) * inv
        dden = jnp.where(jnp.abs(f["den"]) > f["floor"], jnp.sign(f["den"]) * ddd, 0.0)
        adn = (a * dnum).astype(MX)
        dnm = dnum.astype(MX)
        ds_tot = _dot_nt(dnm, vm) + dden
        dsr = (ds_tot * w).astype(MX)
        e = ds_tot * s
        wv = (w_end * v).astype(MX)
        kds = _dot(km, ds1m)
        dq_ref[...] = _dot_nt(adn, s0m) + _dot(dsr, km) + (dden * a) * n0
        dk_ref[...] = _dot_tn(dsr, qm) + _dot_nt(wv, ds1m) + w_end * dn1
        dv_ref[...] = _dot_tn(s.astype(MX), dnm) + w_end * kds

        eye = f["eye"]
        to_col = lambda r: jnp.sum(jnp.where(eye, r, 0.0), axis=1, keepdims=True)
        to_row = lambda c: jnp.sum(jnp.where(eye, c, 0.0), axis=0, keepdims=True)
        g_a = (jnp.sum(dnum * f["qs"], axis=1, keepdims=True) + dden * f["qn"]) * a
        g_w = (jnp.sum(v * kds, axis=1, keepdims=True) + jnp.sum(k * dn1, axis=1, keepdims=True)) * w_end
        g_end = (jnp.sum(jnp.sum(ds1 * s0, axis=1, keepdims=True), axis=0, keepdims=True)
                 + jnp.sum(dn1 * n0, axis=1, keepdims=True)) * a_end
        col_e = jnp.sum(e, axis=0, keepdims=True)
        db = jnp.sum(e, axis=1, keepdims=True) - to_col(col_e) + g_a - g_w
        last = lax.broadcasted_iota(jnp.int32, (L, 1), 0) == (0 if reverse else L - 1)
        db = db + jnp.where(last, jnp.sum(g_w, axis=0, keepdims=True) + g_end, 0.0)
        dg_ref[0, 0] = col_e + to_row(g_w)
        dlf = jnp.sum(jnp.where(f["seen"], db, 0.0), axis=0, keepdims=True)
        dg_ref[1, 0] = dlf * jax.nn.sigmoid(-gr_ref[1, 0])

        ds_sc[...] = a_end * ds1 + _dot_tn(qm, adn)
        dn_sc[...] = a_end * dn1 + jnp.sum((dden * a) * q, axis=0, keepdims=True)

    qk_spec = pl.BlockSpec((L, DK_A), lambda h, i: (off + pos(i), h))
    v_spec = pl.BlockSpec((L, DV_A), lambda h, i: (off + pos(i), h))
    gr_spec = pl.BlockSpec((2, 1, 1, L), lambda h, i: (0, h, 0, off + pos(i)))
    st_spec = pl.BlockSpec((1, 1, DK_A, DV_A), lambda h, i: (h, pos(i), 0, 0))
    vec_spec = pl.BlockSpec((1, 1, 1, DK_A), lambda h, i: (h, pos(i), 0, 0))
    oqk_spec = pl.BlockSpec((L, DK_A), lambda h, i: (pos(i), h))
    ov_spec = pl.BlockSpec((L, DV_A), lambda h, i: (pos(i), h))
    og_spec = pl.BlockSpec((2, 1, 1, L), lambda h, i: (0, h, 0, pos(i)))
    return pl.pallas_call(
        body, grid=(NH_A, nc),
        in_specs=[qk_spec, qk_spec, v_spec, gr_spec, st_spec, vec_spec, vec_spec, ov_spec],
        out_specs=[oqk_spec, oqk_spec, ov_spec, og_spec],
        out_shape=[jax.ShapeDtypeStruct((n, QK_A), F32), jax.ShapeDtypeStruct((n, QK_A), F32),
                   jax.ShapeDtypeStruct((n, V_A), F32), jax.ShapeDtypeStruct((2, NH_A, 1, n), F32)],
        scratch_shapes=[pltpu.VMEM((DK_A, DV_A), F32), pltpu.VMEM((1, DK_A), F32)],
        compiler_params=pltpu.CompilerParams(dimension_semantics=("parallel", "arbitrary"),
                                             vmem_limit_bytes=VMEM_LIMIT),
        name="mlstm_bwd")(q, k, v, gr, s0_all, n0_all, m0_all, dh)


def _make_mlstm(n, row_off, reverse):
    def gate_rows(li, lf):
        return jnp.stack([li, lf]).transpose(0, 2, 1)[:, :, None, :]

    @jax.custom_vjp
    def op(q, k, v, li, lf):
        return _mlstm_fwd_call(q, k, v, gate_rows(li, lf), n, row_off, reverse)[0]

    def fwd(q, k, v, li, lf):
        gr = gate_rows(li, lf)
        h, s0, n0, m0 = _mlstm_fwd_call(q, k, v, gr, n, row_off, reverse)
        return h, (q, k, v, gr, s0, n0, m0)

    def bwd(res, dh):
        q, k, v, gr, s0, n0, m0 = res
        dq, dk, dv, dg = _mlstm_bwd_call(q, k, v, gr, s0, n0, m0, dh, n, row_off, reverse)
        rows = ((row_off, q.shape[0] - row_off - n), (0, 0))
        dg = jnp.pad(dg[:, :, 0, :].transpose(0, 2, 1), ((0, 0),) + rows)
        return jnp.pad(dq, rows), jnp.pad(dk, rows), jnp.pad(dv, rows), dg[0], dg[1]

    op.defvjp(fwd, bwd)
    return op


MERGE_ROWS = 128


def _sig(x):
    return jax.nn.sigmoid(x)


def _merge_pre_fwd(h_f, h_b, o_attn, p_oa, p_za, p_zb, mh_w, tc):
    t = o_attn.shape[0]
    tr, off = MERGE_ROWS, tc // MERGE_ROWS

    def body(hf_ref, hb_ref, oat_ref, oa_ref, za_ref, zb_ref, w_ref, a_ref, b_ref):
        for hd in range(NH_A):
            cols = slice(hd * DV_A, (hd + 1) * DV_A)
            h = hf_ref[:, cols] + hb_ref[:, cols]
            hn = h * lax.rsqrt(jnp.mean(h * h, axis=-1, keepdims=True) + EPS) * w_ref[:, cols]
            za = za_ref[:, cols]
            a_ref[:, cols] = (_sig(oa_ref[:, cols]) * hn * (za * _sig(za))).astype(BF16)
        zb = zb_ref[...]
        b_ref[...] = (oat_ref[...] * (zb * _sig(zb))).astype(BF16)

    lat = pl.BlockSpec((tr, V_A), lambda i: (i + off, 0))
    row = pl.BlockSpec((tr, V_A), lambda i: (i, 0))
    return pl.pallas_call(
        body, grid=(t // tr,),
        in_specs=[lat, row, row, lat, lat, lat, pl.BlockSpec((1, V_A), lambda i: (0, 0))],
        out_specs=[row, row],
        out_shape=[jax.ShapeDtypeStruct((t, V_A), BF16), jax.ShapeDtypeStruct((t, V_A), BF16)],
        compiler_params=pltpu.CompilerParams(dimension_semantics=("parallel",), vmem_limit_bytes=VMEM_LIMIT),
        name="merge_pre")(h_f, h_b, o_attn, p_oa, p_za, p_zb, mh_w)


def _ctx_block(i, nb, off):
    k = i - nb
    return jnp.where(i < nb, i + off, jnp.where(k < off, k, k + nb))


def _merge_pre_bwd(da, db, h_f, h_b, o_attn, p_oa, p_za, p_zb, mh_w, tc):
    t = o_attn.shape[0]
    n, r = h_f.shape[0], p_oa.shape[0]
    tr, off = MERGE_ROWS, tc // MERGE_ROWS
    nb = t // tr
    n_ctx = r // tr - nb

    def body(da_ref, db_ref, hf_ref, hb_ref, oat_ref, oa_ref, za_ref, zb_ref, w_ref,
             dhf_ref, dhb_ref, doat_ref, doa_ref, dza_ref, dzb_ref, dw_ref):
        i = pl.program_id(0)

        @pl.when(i == 0)
        def _():
            dw_ref[...] = jnp.zeros(dw_ref.shape, F32)

        @pl.when(i < nb)
        def _():
            for hd in range(NH_A):
                cols = slice(hd * DV_A, (hd + 1) * DV_A)
                h = hf_ref[:, cols] + hb_ref[:, cols]
                rs = lax.rsqrt(jnp.mean(h * h, axis=-1, keepdims=True) + EPS)
                w = w_ref[:, cols]
                hn = h * rs * w
                oa, za, g = oa_ref[:, cols], za_ref[:, cols], da_ref[:, cols]
                so, sz = _sig(oa), _sig(za)
                silu_z = za * sz
                doa_ref[:, cols] = g * hn * silu_z * so * (1.0 - so)
                dza_ref[:, cols] = g * so * hn * (sz * (1.0 + za * (1.0 - sz)))
                dhn = g * so * silu_z
                dw_ref[:, cols] += jnp.sum(dhn * (h * rs), axis=0, keepdims=True)
                dxn = dhn * w
                dh = rs * dxn - h * (rs * rs * rs * jnp.mean(dxn * h, axis=-1, keepdims=True))
                dhf_ref[:, cols] = dh
                dhb_ref[:, cols] = dh
            zb, gb, oat = zb_ref[...], db_ref[...], oat_ref[...]
            sb = _sig(zb)
            doat_ref[...] = gb * (zb * sb)
            dzb_ref[...] = gb * oat * (sb * (1.0 + zb * (1.0 - sb)))

        @pl.when(i >= nb)
        def _():
            for ref in (dhf_ref, dhb_ref, doa_ref, dza_ref, dzb_ref):
                ref[...] = jnp.zeros(ref.shape, F32)

    lati = lambda i: jnp.minimum(i, nb - 1)
    lat = pl.BlockSpec((tr, V_A), lambda i: (lati(i) + off, 0))
    row = pl.BlockSpec((tr, V_A), lambda i: (lati(i), 0))
    vec = pl.BlockSpec((1, V_A), lambda i: (0, 0))
    pout = pl.BlockSpec((tr, V_A), lambda i: (_ctx_block(i, nb, off), 0))
    hf_out = pl.BlockSpec((tr, V_A), lambda i: (jnp.where(i < nb, i + off, jnp.minimum(i - nb, off - 1)), 0))
    hb_out = pl.BlockSpec((tr, V_A), lambda i: (jnp.where(i < nb, i, nb + jnp.minimum(i - nb, off - 1)), 0))
    return pl.pallas_call(
        body, grid=(nb + n_ctx,),
        in_specs=[row, row, lat, row, row, lat, lat, lat, vec],
        out_specs=[hf_out, hb_out, row, pout, pout, pout, vec],
        out_shape=[jax.ShapeDtypeStruct((n, V_A), F32), jax.ShapeDtypeStruct((n, V_A), F32),
                   jax.ShapeDtypeStruct((t, V_A), F32), jax.ShapeDtypeStruct((r, V_A), F32),
                   jax.ShapeDtypeStruct((r, V_A), F32), jax.ShapeDtypeStruct((r, V_A), F32),
                   jax.ShapeDtypeStruct((1, V_A), F32)],
        compiler_params=pltpu.CompilerParams(dimension_semantics=("arbitrary",), vmem_limit_bytes=VMEM_LIMIT),
        name="merge_pre_bwd")(da, db, h_f, h_b, o_attn, p_oa, p_za, p_zb, mh_w)


def _merge_gate_fwd(y_a, y_b, p_ga, p_gb, tc):
    t = y_a.shape[0]
    tr, off = MERGE_ROWS, tc // MERGE_ROWS

    def body(ya_ref, yb_ref, ga_ref, gb_ref, m_ref):
        m_ref[...] = (_sig(ga_ref[...]) * ya_ref[...] + _sig(gb_ref[...]) * yb_ref[...]).astype(BF16)

    lat = pl.BlockSpec((tr, D_MODEL), lambda i: (i + off, 0))
    row = pl.BlockSpec((tr, D_MODEL), lambda i: (i, 0))
    return pl.pallas_call(
        body, grid=(t // tr,), in_specs=[row, row, lat, lat], out_specs=row,
        out_shape=jax.ShapeDtypeStruct((t, D_MODEL), BF16),
        compiler_params=pltpu.CompilerParams(dimension_semantics=("parallel",), vmem_limit_bytes=VMEM_LIMIT),
        name="merge_gate")(y_a, y_b, p_ga, p_gb)


def _merge_gate_bwd(dm, y_a, y_b, p_ga, p_gb, tc):
    t, r = y_a.shape[0], p_ga.shape[0]
    tr, off = MERGE_ROWS, tc // MERGE_ROWS
    nb = t // tr
    n_ctx = r // tr - nb

    def body(dm_ref, ya_ref, yb_ref, ga_ref, gb_ref, dya_ref, dyb_ref, dga_ref, dgb_ref):
        i = pl.program_id(0)

        @pl.when(i < nb)
        def _():
            dm = dm_ref[...]
            sa, sb = _sig(ga_ref[...]), _sig(gb_ref[...])
            dya_ref[...] = (dm * sa).astype(BF16)
            dyb_ref[...] = (dm * sb).astype(BF16)
            dga_ref[...] = dm * ya_ref[...] * sa * (1.0 - sa)
            dgb_ref[...] = dm * yb_ref[...] * sb * (1.0 - sb)

        @pl.when(i >= nb)
        def _():
            dga_ref[...] = jnp.zeros(dga_ref.shape, F32)
            dgb_ref[...] = jnp.zeros(dgb_ref.shape, F32)

    lati = lambda i: jnp.minimum(i, nb - 1)
    lat = pl.BlockSpec((tr, D_MODEL), lambda i: (lati(i) + off, 0))
    row = pl.BlockSpec((tr, D_MODEL), lambda i: (lati(i), 0))
    pout = pl.BlockSpec((tr, D_MODEL), lambda i: (_ctx_block(i, nb, off), 0))
    return pl.pallas_call(
        body, grid=(nb + n_ctx,), in_specs=[row, row, row, lat, lat], out_specs=[row, row, pout, pout],
        out_shape=[jax.ShapeDtypeStruct((t, D_MODEL), BF16), jax.ShapeDtypeStruct((t, D_MODEL), BF16),
                   jax.ShapeDtypeStruct((r, D_MODEL), F32), jax.ShapeDtypeStruct((r, D_MODEL), F32)],
        compiler_params=pltpu.CompilerParams(dimension_semantics=("arbitrary",), vmem_limit_bytes=VMEM_LIMIT),
        name="merge_gate_bwd")(dm, y_a, y_b, p_ga, p_gb)


def _make_merge_block(tc):
    @jax.custom_vjp
    def block(h_f, h_b, o_attn, p_oa, p_za, p_zb, p_ga, p_gb, mh_w, w_ba, w_bb, w_out, pr_ba, pr_bb, pr_out):
        return fwd(h_f, h_b, o_attn, p_oa, p_za, p_zb, p_ga, p_gb, mh_w, w_ba, w_bb, w_out, pr_ba, pr_bb, pr_out)[0]

    def fwd(h_f, h_b, o_attn, p_oa, p_za, p_zb, p_ga, p_gb, mh_w, w_ba, w_bb, w_out, pr_ba, pr_bb, pr_out):
        a_in, b_in = _merge_pre_fwd(h_f, h_b, o_attn, p_oa, p_za, p_zb, mh_w[None], tc)
        y_a, y_b = _mm_nn(a_in, w_ba, "merge_ya"), _mm_nn(b_in, w_bb, "merge_yb")
        m_in = _merge_gate_fwd(y_a, y_b, p_ga, p_gb, tc)
        out = _mm_nn(m_in, w_out, "merge_out")
        return out, (h_f, h_b, o_attn, p_oa, p_za, p_zb, p_ga, p_gb, mh_w, w_ba, w_bb, w_out, a_in, b_in, y_a, y_b, m_in)

    def bwd(res, dout):
        h_f, h_b, o_attn, p_oa, p_za, p_zb, p_ga, p_gb, mh_w, w_ba, w_bb, w_out, a_in, b_in, y_a, y_b, m_in = res
        dm = _mm_nt(dout, w_out, "merge_out_da")
        dw_out = _mm_tn(m_in, dout, "merge_out_dw")
        dy_a, dy_b, dga, dgb = _merge_gate_bwd(dm, y_a, y_b, p_ga, p_gb, tc)
        da, db = _mm_nt(dy_a, w_ba, "merge_ya_da"), _mm_nt(dy_b, w_bb, "merge_yb_da")
        dw_ba, dw_bb = _mm_tn(a_in, dy_a, "merge_ya_dw"), _mm_tn(b_in, dy_b, "merge_yb_dw")
        dhf, dhb, doat, doa, dza, dzb, dmh = _merge_pre_bwd(da, db, h_f, h_b, o_attn, p_oa, p_za, p_zb, mh_w[None], tc)
        z = jnp.zeros_like
        return (dhf, dhb, doat, doa, dza, dzb, dga, dgb, dmh[0], z(w_ba), z(w_bb), z(w_out), dw_ba, dw_bb, dw_out)

    block.defvjp(fwd, bwd)
    return block


def _silu(x):
    return x * jax.nn.sigmoid(x)


CONV_ROWS, CONV_HALO = 256, 8


def _make_conv(t, tc):
    r = t + 2 * tc
    width = 2 * QK_A
    nblk = r // CONV_ROWS
    cb, nb = tc // CONV_ROWS, t // CONV_ROWS
    k_scale = DK_A ** -0.5
    per = CONV_ROWS // CONV_HALO

    def taps(x_ref, prev_ref, next_ref):
        i = pl.program_id(0)
        seg_first = (i == 0) | (i == cb) | (i == cb + nb)
        seg_last = (i == cb - 1) | (i == cb + nb - 1) | (i == nblk - 1)
        x = x_ref[...]
        rows = lax.broadcasted_iota(jnp.int32, (CONV_ROWS, 1), 0)
        before = jnp.where(seg_first, 0.0, prev_ref[CONV_HALO - 1:CONV_HALO, :])
        after = jnp.where(seg_last, 0.0, next_ref[0:1, :])
        xm1 = jnp.where(rows == 0, before, pltpu.roll(x, 1, 0))
        xp1 = jnp.where(rows == CONV_ROWS - 1, after, pltpu.roll(x, CONV_ROWS - 1, 0))
        return xm1, x, xp1

    row = pl.BlockSpec((CONV_ROWS, width), lambda i: (i, 0))
    prev = pl.BlockSpec((CONV_HALO, width), lambda i: (jnp.maximum(i * per - 1, 0), 0))
    nxt = pl.BlockSpec((CONV_HALO, width), lambda i: (jnp.minimum((i + 1) * per, r // CONV_HALO - 1), 0))
    half = pl.BlockSpec((CONV_ROWS, QK_A), lambda i: (i, 0))
    wspec = pl.BlockSpec((3, width), lambda i: (0, 0))
    bspec = pl.BlockSpec((1, width), lambda i: (0, 0))
    par = pltpu.CompilerParams(dimension_semantics=("parallel",), vmem_limit_bytes=VMEM_LIMIT)
    seq = pltpu.CompilerParams(dimension_semantics=("arbitrary",), vmem_limit_bytes=VMEM_LIMIT)

    def fwd_call(x, cw, cb_):
        def body(x_ref, p_ref, n_ref, w_ref, b_ref, q_ref, k_ref):
            xm1, x0, xp1 = taps(x_ref, p_ref, n_ref)
            c = b_ref[...] + xm1 * w_ref[0:1, :] + x0 * w_ref[1:2, :] + xp1 * w_ref[2:3, :]
            y = c * jax.nn.sigmoid(c)
            q_ref[...] = y[:, :QK_A]
            k_ref[...] = y[:, QK_A:] * k_scale

        return pl.pallas_call(
            body, grid=(nblk,), in_specs=[row, prev, nxt, wspec, bspec], out_specs=[half, half],
            out_shape=[jax.ShapeDtypeStruct((r, QK_A), F32), jax.ShapeDtypeStruct((r, QK_A), F32)],
            compiler_params=par, name="conv_silu")(x, x, x, cw, cb_)

    def bwd_pre_call(dq, dk, x, cw, cb_):
        def body(dq_ref, dk_ref, x_ref, p_ref, n_ref, w_ref, b_ref, dc_ref, dw_ref, db_ref):
            @pl.when(pl.program_id(0) == 0)
            def _():
                dw_ref[...] = jnp.zeros(dw_ref.shape, F32)
                db_ref[...] = jnp.zeros(db_ref.shape, F32)

            xm1, x0, xp1 = taps(x_ref, p_ref, n_ref)
            c = b_ref[...] + xm1 * w_ref[0:1, :] + x0 * w_ref[1:2, :] + xp1 * w_ref[2:3, :]
            s = jax.nn.sigmoid(c)
            dy = jnp.concatenate([dq_ref[...], dk_ref[...] * k_scale], axis=1)
            dc = dy * (s * (1.0 + c * (1.0 - s)))
            dc_ref[...] = dc
            db_ref[...] += jnp.sum(dc, axis=0, keepdims=True)
            dw_ref[0:1, :] += jnp.sum(dc * xm1, axis=0, keepdims=True)
            dw_ref[1:2, :] += jnp.sum(dc * x0, axis=0, keepdims=True)
            dw_ref[2:3, :] += jnp.sum(dc * xp1, axis=0, keepdims=True)

        return pl.pallas_call(
            body, grid=(nblk,), in_specs=[half, half, row, prev, nxt, wspec, bspec], out_specs=[row, wspec, bspec],
            out_shape=[jax.ShapeDtypeStruct((r, width), F32), jax.ShapeDtypeStruct((3, width), F32),
                       jax.ShapeDtypeStruct((1, width), F32)],
            compiler_params=seq, name="conv_silu_bwd")(dq, dk, x, x, x, cw, cb_)

    def bwd_x_call(dc, cw):
        def body(d_ref, p_ref, n_ref, w_ref, dx_ref):
            dm1, d0, dp1 = taps(d_ref, p_ref, n_ref)
            dx_ref[...] = dm1 * w_ref[2:3, :] + d0 * w_ref[1:2, :] + dp1 * w_ref[0:1, :]

        return pl.pallas_call(
            body, grid=(nblk,), in_specs=[row, prev, nxt, wspec], out_specs=row,
            out_shape=jax.ShapeDtypeStruct((r, width), F32), compiler_params=par,
            name="conv_silu_bwd_x")(dc, dc, dc, cw)

    @jax.custom_vjp
    def op(x, cw, cb_):
        return tuple(fwd_call(x, cw, cb_[None]))

    def fwd(x, cw, cb_):
        return tuple(fwd_call(x, cw, cb_[None])), (x, cw, cb_)

    def bwd(res, cot):
        x, cw, cb_ = res
        dc, dw, db = bwd_pre_call(cot[0], cot[1], x, cw, cb_[None])
        return bwd_x_call(dc, cw), dw, db[0]

    op.defvjp(fwd, bwd)
    return op


def _local_loss(diff, const):
    x, ctx, target = diff["x"], const["ctx"], const["target"]
    t, tc = x.shape[0], ctx.shape[0]
    n = tc + t

    mod = diff["mod"]
    shift, scale, gate = mod[0, :D_MODEL], mod[0, D_MODEL:2 * D_MODEL], mod[0, 2 * D_MODEL:]
    shift_c, scale_c = mod[1, :D_MODEL], mod[1, D_MODEL:2 * D_MODEL]
    (p_qk, p_va, p_kv, p_oa, p_za, p_qb, p_zb, p_ga, p_gb), p_if = _ln_project(
        x, ctx, jnp.stack([scale, shift, scale_c, shift_c]), const["w_main"], const["w_if"], diff["p_main"], diff["p_if"])
    gt = p_if[:, :N_IF] + diff["b_if"]

    q_a, k_a = _make_conv(t, tc)(p_qk, diff["conv_w"], diff["conv_b"])
    v_a = p_va
    li_f, lf_f, li_b, lf_b = gt[:, 0:8], gt[:, 8:16], gt[:, 16:24], gt[:, 24:32]

    h_f = _make_mlstm(n, 0, False)(q_a, k_a, v_a, li_f, lf_f)
    h_b = _make_mlstm(n, tc, True)(q_a, k_a, v_a, li_b, lf_b)

    o_attn = _make_gqa(t, tc)(p_qb, p_kv, diff["q_norm_w"], diff["k_norm_w"])

    out = _make_merge_block(tc)(h_f, h_b, o_attn, p_oa, p_za, p_zb, p_ga, p_gb, diff["mh_norm_w"],
                                const["w_ba"], const["w_bb"], const["w_out"], diff["p_ba"], diff["p_bb"], diff["p_out"])

    return _loss_head(x, out, target, gate, diff["ln_w"], diff["ln_b"])


OTHER_CHIPS = [(1, 0), (0, 1), (1, 1)]


def _flip(v, bit):
    return 1 - v if bit else v


def _gather_chips(shard, name):
    def body(x_ref, o_ref, send_sems, recv_sems, local_sem):
        x, y, c = lax.axis_index("x"), lax.axis_index("y"), lax.axis_index("c")
        mine = pltpu.make_async_copy(x_ref, o_ref.at[2 * x + y], local_sem)
        mine.start()

        def copy(r, slot):
            dx, dy = OTHER_CHIPS[r]
            return pltpu.make_async_remote_copy(
                src_ref=x_ref, dst_ref=o_ref.at[slot], send_sem=send_sems.at[r], recv_sem=recv_sems.at[r],
                device_id=(_flip(x, dx), _flip(y, dy), c), device_id_type=MESH)

        sends = [copy(r, 2 * x + y) for r in range(3)]
        for cp in sends:
            cp.start()
        for r, (dx, dy) in enumerate(OTHER_CHIPS):
            copy(r, 2 * _flip(x, dx) + _flip(y, dy)).wait_recv()
        for cp in sends:
            cp.wait_send()
        mine.wait()

    return pl.pallas_call(
        body, out_shape=jax.ShapeDtypeStruct((N_CHIPS,) + shard.shape, shard.dtype),
        in_specs=[pl.BlockSpec(memory_space=pl.ANY)], out_specs=pl.BlockSpec(memory_space=pl.ANY),
        scratch_shapes=[pltpu.SemaphoreType.DMA((3,)), pltpu.SemaphoreType.DMA((3,)), pltpu.SemaphoreType.DMA],
        name=name)(shard)


def _gather_chips_halves(shard, name):
    rows, cols = shard.shape
    halves = shard.reshape(2, rows // 2, cols)

    def body(x_ref, o_ref, send_sems, recv_sems, local_sem):
        x, y, c = lax.axis_index("x"), lax.axis_index("y"), lax.axis_index("c")
        my_chip = 2 * x + y
        mine = pltpu.make_async_copy(x_ref, o_ref.at[my_chip], local_sem)
        mine.start()

        def chip_of(r):
            dx, dy = OTHER_CHIPS[r]
            return _flip(x, dx), _flip(y, dy)

        def copy(k, chip_slot, half, to, src=None):
            dst = o_ref.at[chip_slot, half]
            return pltpu.make_async_remote_copy(
                src_ref=dst if src is None else src, dst_ref=dst, send_sem=send_sems.at[k],
                recv_sem=recv_sems.at[k], device_id=to, device_id_type=MESH)

        first = [copy(r, my_chip, c, (*chip_of(r), c), src=x_ref.at[c]) for r in range(3)]
        for cp in first:
            cp.start()
        passed = []
        for r in range(3):
            px, py = chip_of(r)
            copy(r, 2 * px + py, c, (px, py, c)).wait_recv()
            passed.append(copy(3 + r, 2 * px + py, c, (x, y, 1 - c)))
            passed[-1].start()
        for r in range(3):
            px, py = chip_of(r)
            copy(3 + r, 2 * px + py, 1 - c, (x, y, 1 - c)).wait_recv()
        for cp in first + passed:
            cp.wait_send()
        mine.wait()

    out = pl.pallas_call(
        body, out_shape=jax.ShapeDtypeStruct((N_CHIPS, 2, rows // 2, cols), shard.dtype),
        in_specs=[pl.BlockSpec(memory_space=pl.ANY)], out_specs=pl.BlockSpec(memory_space=pl.ANY),
        scratch_shapes=[pltpu.SemaphoreType.DMA((6,)), pltpu.SemaphoreType.DMA((6,)), pltpu.SemaphoreType.DMA],
        name=name)(halves)
    return out.reshape(N_CHIPS, rows, cols)


def _scatter_grads(slots, name):
    def body(g_ref, o_ref, send_sems, recv_sems, local_sem):
        x, y, c = lax.axis_index("x"), lax.axis_index("y"), lax.axis_index("c")
        me, my_chip, sibling = 4 * x + 2 * y + c, 2 * x + y, (x, y, 1 - c)
        mine = pltpu.make_async_copy(g_ref.at[my_chip], o_ref.at[me], local_sem)
        mine.start()

        def chip_of(r):
            dx, dy = OTHER_CHIPS[r]
            return _flip(x, dx), _flip(y, dy)

        def copy(k, slot, to, src=None):
            dst = o_ref.at[slot]
            return pltpu.make_async_remote_copy(
                src_ref=dst if src is None else src, dst_ref=dst, send_sem=send_sems.at[k],
                recv_sem=recv_sems.at[k], device_id=to, device_id_type=MESH)

        first = [copy(0, me, sibling, src=g_ref.at[my_chip])]
        for r in range(3):
            px, py = chip_of(r)
            first.append(copy(1 + r, me, (px, py, c), src=g_ref.at[2 * px + py]))
        for cp in first:
            cp.start()
        passed = []
        for r in range(3):
            px, py = chip_of(r)
            copy(1 + r, 4 * px + 2 * py + c, (px, py, c)).wait_recv()
            passed.append(copy(4 + r, 4 * px + 2 * py + c, sibling))
            passed[-1].start()
        copy(0, 4 * x + 2 * y + 1 - c, sibling).wait_recv()
        for r in range(3):
            px, py = chip_of(r)
            copy(4 + r, 4 * px + 2 * py + 1 - c, sibling).wait_recv()
        for cp in first + passed:
            cp.wait_send()
        mine.wait()

    return pl.pallas_call(
        body, out_shape=jax.ShapeDtypeStruct((N_DEV,) + slots.shape[1:], slots.dtype),
        in_specs=[pl.BlockSpec(memory_space=pl.ANY)], out_specs=pl.BlockSpec(memory_space=pl.ANY),
        scratch_shapes=[pltpu.SemaphoreType.DMA((N_DEV - 1,)), pltpu.SemaphoreType.DMA((N_DEV - 1,)),
                        pltpu.SemaphoreType.DMA],
        name=name)(slots)


def _allreduce_small(v, name):
    def body(v_ref, o_ref, buf, send_sems, recv_sems):
        x, y, c = lax.axis_index("x"), lax.axis_index("y"), lax.axis_index("c")
        me = 4 * x + 2 * y + c
        buf[me] = v_ref[...]

        def peer(r):
            return _flip(x, (r >> 2) & 1), _flip(y, (r >> 1) & 1), _flip(c, r & 1)

        def copy(r, dst_slot):
            return pltpu.make_async_remote_copy(
                src_ref=v_ref, dst_ref=buf.at[dst_slot], send_sem=send_sems.at[r - 1],
                recv_sem=recv_sems.at[r - 1], device_id=peer(r), device_id_type=MESH)

        sends = [copy(r, me) for r in range(1, N_DEV)]
        for cp in sends:
            cp.start()
        for r in range(1, N_DEV):
            px, py, pc = peer(r)
            copy(r, 4 * px + 2 * py + pc).wait_recv()
        for cp in sends:
            cp.wait_send()
        acc = buf[0]
        for d in range(1, N_DEV):
            acc = acc + buf[d]
        o_ref[...] = acc

    return pl.pallas_call(
        body, out_shape=jax.ShapeDtypeStruct(v.shape, v.dtype),
        in_specs=[pl.BlockSpec(memory_space=pltpu.VMEM)], out_specs=pl.BlockSpec(memory_space=pltpu.VMEM),
        scratch_shapes=[pltpu.VMEM((N_DEV,) + v.shape, v.dtype), pltpu.SemaphoreType.DMA((N_DEV - 1,)),
                        pltpu.SemaphoreType.DMA((N_DEV - 1,))],
        name=name)(v)


def _adamw_math(w, g, m, v):
    m = ADAM_B1 * m + (1.0 - ADAM_B1) * g
    v = ADAM_B2 * v + (1.0 - ADAM_B2) * jnp.square(g)
    m_hat = m / (1.0 - ADAM_B1 ** ADAM_STEP)
    v_hat = v / (1.0 - ADAM_B2 ** ADAM_STEP)
    delta = -ADAM_LR * (m_hat / (jnp.sqrt(v_hat) + ADAM_EPS) + ADAM_WD * w)
    return delta, m, v


def _adamw_sum(parts, w, m, v, name):
    npart, rows, cols = parts.shape
    tr = _pick(rows, (64, 32, 16, 8)) if rows >= 8 else rows

    def body(p_ref, w_ref, m_ref, v_ref, g_out, d_out, m_out, v_out):
        g = p_ref[0].astype(F32)
        for k in range(1, npart):
            g = g + p_ref[k].astype(F32)
        d, m2, v2 = _adamw_math(w_ref[...], g, m_ref[...], v_ref[...])
        g_out[...] = g
        d_out[...] = d
        m_out[...] = m2
        v_out[...] = v2

    spec = pl.BlockSpec((tr, cols), lambda i: (i, 0))
    shp = jax.ShapeDtypeStruct((rows, cols), F32)
    return pl.pallas_call(
        body, grid=(rows // tr,),
        in_specs=[pl.BlockSpec((npart, tr, cols), lambda i: (0, i, 0)), spec, spec, spec],
        out_specs=[spec, spec, spec, spec], out_shape=[shp, shp, shp, shp],
        compiler_params=pltpu.CompilerParams(dimension_semantics=("parallel",), vmem_limit_bytes=VMEM_LIMIT),
        name=name)(parts, w, m, v)


SMALL_ROWS = 16


def _pack_small(c_ctx, b_mod, conv_b, mh, ln_w, ln_b, conv_w_rows, b_if, qn, kn):
    last = jnp.concatenate([b_if.reshape(-1), qn.reshape(-1), kn.reshape(-1),
                            jnp.zeros((D_MODEL - N_IF - 2 * HD_B,), F32)])
    rows = [c_ctx.reshape(1, D_MODEL), b_mod.reshape(3, D_MODEL), conv_b.reshape(1, D_MODEL),
            mh.reshape(1, D_MODEL), ln_w.reshape(1, D_MODEL), ln_b.reshape(1, D_MODEL),
            conv_w_rows.reshape(3, D_MODEL), last[None], jnp.zeros((SMALL_ROWS - 12, D_MODEL), F32)]
    return jnp.concatenate(rows, 0)


def _unpack_small(pk, conv_cols):
    return dict(c_ctx=pk[0], b_mod=pk[1:4].reshape(1, 3 * D_MODEL), conv_b=pk[4:5], mh_norm_w=pk[5:6],
                ln_w=pk[6:7], ln_b=pk[7:8], conv_w=pk[8:11, :conv_cols][None], b_if=pk[11:12, :N_IF],
                q_norm_w=pk[11:12, N_IF:N_IF + HD_B], k_norm_w=pk[11:12, N_IF + HD_B:N_IF + 2 * HD_B])


def kernel(x, c, ctx, c_ctx, w_mod, b_mod, w_in, b_if, conv_w, conv_b, mh_norm_w, q_norm_w, k_norm_w, w_branch_a, w_branch_b, w_out, ln_w, ln_b, loss_target, m_c_ctx, m_w_mod, m_b_mod, m_w_in, m_b_if, m_conv_w, m_conv_b, m_mh_norm_w, m_q_norm_w, m_k_norm_w, m_w_branch_a, m_w_branch_b, m_w_out, m_ln_w, m_ln_b, v_c_ctx, v_w_mod, v_b_mod, v_w_in, v_b_if, v_conv_w, v_conv_b, v_mh_norm_w, v_q_norm_w, v_k_norm_w, v_w_branch_a, v_w_branch_b, v_w_out, v_ln_w, v_ln_b):
    core = lax.axis_index("c")
    chip = 2 * lax.axis_index("x") + lax.axis_index("y")
    me = 2 * chip + core
    mod_cols, in_cols, conv_cols = w_mod.shape[2], w_in.shape[2], conv_w.shape[2]
    br_rows = w_out.shape[1]

    def rows_at(block, first):
        return lax.dynamic_update_slice(jnp.zeros((SMALL_ROWS, block.shape[1]), F32), block, (first, 0))

    owner = (core == 0).astype(F32)
    cond = _allreduce_small(rows_at(jnp.stack([_silu(c[0]), _silu(c_ctx)]), 2 * me), "gather_cond").astype(BF16)
    w_mod_b = w_mod[0].astype(BF16)
    mod_part = _mm_nn(cond, w_mod_b, "mod_fwd") * owner
    mod_all = _allreduce_small(
        lax.dynamic_update_slice(jnp.zeros((SMALL_ROWS, 3 * D_MODEL), F32), mod_part, (0, chip * mod_cols)),
        "gather_mod") + b_mod[0]
    mod = lax.dynamic_slice(mod_all, (2 * me, 0), (2, 3 * D_MODEL))

    g_in = _gather_chips_halves(w_in[0].astype(BF16), "gather_w_in")
    g_br = _gather_chips_halves(jnp.concatenate([w_branch_a[0], w_branch_b[0], w_out[0]], 0).astype(BF16),
                                "gather_w_branches").reshape(N_CHIPS, 3, br_rows, D_MODEL)
    g_ba, g_bb, g_out = g_br[:, 0], g_br[:, 1], g_br[:, 2]
    g_conv = _gather_chips(conv_w[0], "gather_conv_w")
    w_in_full = jnp.moveaxis(g_in, 0, 1).reshape(D_MODEL, N_CHIPS * in_cols)
    w_main = jnp.concatenate([w_in_full[:, :IF_START], w_in_full[:, IF_START + N_IF:]], 1)
    w_if = jnp.pad(w_in_full[:, IF_START:IF_START + N_IF], ((0, 0), (0, IF_PAD - N_IF)))
    conv_w_full = jnp.moveaxis(g_conv, 0, 1).reshape(3, N_CHIPS * conv_cols)

    const = dict(ctx=ctx[0], target=loss_target[0], w_main=w_main, w_if=w_if,
                 w_ba=g_ba.reshape(D_MODEL, D_MODEL), w_bb=g_bb.reshape(D_MODEL, D_MODEL),
                 w_out=g_out.reshape(D_MODEL, D_MODEL))
    diff = dict(x=x[0], mod=mod, b_if=b_if[0], conv_w=conv_w_full, conv_b=conv_b[0],
                mh_norm_w=mh_norm_w[0], q_norm_w=q_norm_w[0], k_norm_w=k_norm_w[0], ln_w=ln_w[0], ln_b=ln_b[0],
                p_main=jnp.zeros(w_main.shape, F32),
                p_if=jnp.zeros(w_if.shape, F32), p_ba=jnp.zeros((D_MODEL, D_MODEL), F32),
                p_bb=jnp.zeros((D_MODEL, D_MODEL), F32), p_out=jnp.zeros((D_MODEL, D_MODEL), F32))
    loss_local, g = jax.value_and_grad(_local_loss)(diff, const)
    loss = lax.psum(loss_local, ("x", "y", "c"))

    dmod_all = _allreduce_small(rows_at(g["mod"], 2 * me), "gather_dmod")
    dmod_k = lax.dynamic_slice(dmod_all, (0, chip * mod_cols), (SMALL_ROWS, mod_cols))
    g_w_mod = _mm_tn(cond, dmod_k, "mod_dw")
    g_b_mod = jnp.sum(dmod_all, axis=0) * (me == 0).astype(F32)
    d_cond = _mm_nt(dmod_k, w_mod_b, "mod_da")
    sig_ctx = jax.nn.sigmoid(c_ctx)
    g_c_ctx = owner * (sig_ctx * (1.0 + c_ctx * (1.0 - sig_ctx))) * jnp.sum(d_cond[1::2], axis=0)

    g_small = _allreduce_small(
        _pack_small(g_c_ctx, g_b_mod, g["conv_b"], g["mh_norm_w"], g["ln_w"], g["ln_b"], g["conv_w"],
                    g["b_if"], g["q_norm_w"], g["k_norm_w"]), "allreduce_small")
    conv_g = lax.dynamic_slice(g_small[8:11], (0, chip * conv_cols), (3, conv_cols))
    g_small = g_small.at[8:11].set(jnp.pad(conv_g, ((0, 0), (0, D_MODEL - conv_cols))))
    pad_conv = lambda a: jnp.pad(a[0], ((0, 0), (0, D_MODEL - conv_cols)))
    packed = [_pack_small(cc, bm[0], cb[0], mh[0], lw[0], lb[0], pad_conv(cw), bi[0], qn[0], kn[0])
              for cc, bm, cb, mh, lw, lb, cw, bi, qn, kn in (
                  (c_ctx, b_mod, conv_b, mh_norm_w, ln_w, ln_b, conv_w, b_if, q_norm_w, k_norm_w),
                  (m_c_ctx, m_b_mod, m_conv_b, m_mh_norm_w, m_ln_w, m_ln_b, m_conv_w, m_b_if, m_q_norm_w, m_k_norm_w),
                  (v_c_ctx, v_b_mod, v_conv_b, v_mh_norm_w, v_ln_w, v_ln_b, v_conv_w, v_b_if, v_q_norm_w, v_k_norm_w))]
    small = [_unpack_small(a, conv_cols)
             for a in _adamw_sum(g_small[None], packed[0], packed[1], packed[2], "adamw_small")]

    def col_slots(gfull, cols):
        return jnp.moveaxis(gfull.reshape(D_MODEL, N_CHIPS, cols), 1, 0).astype(BF16)

    g_in_full = jnp.concatenate([g["p_main"][:, :IF_START], g["p_if"][:, :N_IF], g["p_main"][:, IF_START:]], 1)
    big = {"w_mod": [a[None] for a in _adamw_sum(g_w_mod[None], w_mod[0], m_w_mod[0], v_w_mod[0], "adamw_w_mod")]}
    parts_in = _scatter_grads(col_slots(g_in_full, in_cols), "scatter_w_in")
    big["w_in"] = [a[None] for a in _adamw_sum(parts_in, w_in[0], m_w_in[0], v_w_in[0], "adamw_w_in")]
    br_slots = jnp.concatenate([g[p].reshape(N_CHIPS, br_rows, D_MODEL) for p in ("p_ba", "p_bb", "p_out")], 1)
    parts_br = _scatter_grads(br_slots.astype(BF16), "scatter_w_branches")
    for j, (nm, w_, m_, v_) in enumerate((("w_branch_a", w_branch_a, m_w_branch_a, v_w_branch_a),
                                          ("w_branch_b", w_branch_b, m_w_branch_b, v_w_branch_b),
                                          ("w_out", w_out, m_w_out, v_w_out))):
        parts = parts_br[:, j * br_rows:(j + 1) * br_rows]
        big[nm] = [a[None] for a in _adamw_sum(parts, w_[0], m_[0], v_[0], "adamw_" + nm)]

    names = ["c_ctx", "w_mod", "b_mod", "w_in", "b_if", "conv_w", "conv_b", "mh_norm_w", "q_norm_w", "k_norm_w",
             "w_branch_a", "w_branch_b", "w_out", "ln_w", "ln_b"]
    outs = [[big[nm][k] if nm in big else small[k][nm] for nm in names] for k in range(4)]
    return (loss, g["x"][None], *outs[0], *outs[1], *outs[2], *outs[3])
```
